```python
import math
import jax, jax.numpy as jnp
from jax import lax
import numpy as np

D_MODEL = 2048
BATCH = 16
SEQ = 2048
DEPTH = 1
DEC_BATCH = 32
DEC_SEQ = 1
PAST_LEN = 16384
PAGE_SIZE = 128

HEAD_DIM = 128
NSA_HEADS = 8
NSA_KV_GROUPS = 2
NSA_HPG = NSA_HEADS // NSA_KV_GROUPS
CMP_BLOCK = 64
CMP_HIDDEN = 128
SEL_BLOCK = 64
N_SEL_BLOCKS = 16
WINDOW = 512
DSA_HEADS = 8
DSA_KV_GROUPS = 2
DSA_HPG = DSA_HEADS // DSA_KV_GROUPS
IDX_HEADS = 8
IDX_DIM = 64
DSA_TOPK = 256
N_BUCKETS = 32
MAX_EXACT = 16
MAX_DISTANCE = 128
N_EXPERTS = 64
EXPERT_FF = 512
SHARED_FF = 512
TOP_K = 6
N_EXPERT_GROUPS = 8
TOPK_GROUPS = 4
ROUTED_SCALE = 2.5
MOE_BLOCK = 128
Q_BLOCK = 128
SEL_Q_BLOCK = 32
LN_EPS = 1e-5
ATTN_SCALE = HEAD_DIM ** -0.5
DEEPNORM_ALPHA = (2 * DEPTH) ** 0.25
DEEPNORM_BETA = (8 * DEPTH) ** -0.25
IN_COLS = (NSA_HEADS * HEAD_DIM, 6 * NSA_KV_GROUPS * HEAD_DIM, 3 * NSA_HEADS,
           DSA_HEADS * HEAD_DIM, 2 * DSA_KV_GROUPS * HEAD_DIM,
           IDX_HEADS * IDX_DIM, IDX_DIM, IDX_HEADS, 2 * D_MODEL)
D_IN = sum(IN_COLS)

kernel_name = 'nsa_dsa_moe_hybrid_step'


def layer_norm(x, g, b):
    xf = x.astype(jnp.float32)
    mu = xf.mean(-1, keepdims=True)
    var = jnp.square(xf - mu).mean(-1, keepdims=True)
    return ((xf - mu) * lax.rsqrt(var + LN_EPS) * g + b).astype(x.dtype)


def to_blocks(x, n):
    b, t = x.shape[:2]
    return x.reshape((b, t // n, n) + x.shape[2:]).swapaxes(0, 1)


def from_blocks(x):
    nb, b, n = x.shape[:3]
    return x.swapaxes(0, 1).reshape((b, nb * n) + x.shape[3:])


def rel_bucket(dist):
    n = jnp.maximum(dist, 0)
    nf = jnp.maximum(n, 1).astype(jnp.float32)
    large = MAX_EXACT + (jnp.log(nf / MAX_EXACT) / math.log(MAX_DISTANCE / MAX_EXACT)
                         * (N_BUCKETS - MAX_EXACT)).astype(jnp.int32)
    return jnp.where(n < MAX_EXACT, n, jnp.minimum(large, N_BUCKETS - 1))


def rel_bias(dist, tbl):
    bucket = rel_bucket(dist)
    if dist.shape[-2] == 1:
        return jnp.moveaxis(tbl[bucket[..., 0, :]], -3, -1)
    gi = jnp.arange(tbl.shape[1])[:, None]
    return jnp.swapaxes(tbl.transpose(1, 0, 2)[gi, bucket], -1, -2)


def attn_probs(s, dist, valid, tbl):
    s = s.astype(jnp.float32) + rel_bias(dist, tbl).astype(jnp.float32)
    s = jnp.where(valid[..., None, :], s, -jnp.inf)
    m = jnp.max(s, axis=-1, keepdims=True)
    m = jnp.where(jnp.isfinite(m), m, 0.0)
    e = jnp.exp(s - m)
    den = e.sum(-1, keepdims=True)
    return e / jnp.where(den > 0, den, 1.0)


def input_projection(x, w_in):
    b, t, _ = x.shape
    points, acc = [], 0
    for c in IN_COLS[:-1]:
        acc += c
        points.append(acc)
    q_a, kv_a, g_a, q_b, kv_b, iq, ik, iw, g_m = jnp.split(x @ w_in, points, axis=-1)
    return (q_a.reshape(b, t, NSA_KV_GROUPS, NSA_HPG, HEAD_DIM),
            kv_a.reshape(b, t, 6, NSA_KV_GROUPS, HEAD_DIM),
            jax.nn.sigmoid(g_a).reshape(b, t, 3, NSA_KV_GROUPS, NSA_HPG),
            q_b.reshape(b, t, DSA_KV_GROUPS, DSA_HPG, HEAD_DIM),
            kv_b.reshape(b, t, 2, DSA_KV_GROUPS, HEAD_DIM),
            iq.reshape(b, t, IDX_HEADS, IDX_DIM), ik, iw,
            jax.nn.sigmoid(g_m).reshape(b, t, 2, D_MODEL))


def nsa_compress(rows, pe, w1, w2):
    b, l = rows.shape[:2]
    nb = l // CMP_BLOCK
    blk = rows[:, :nb * CMP_BLOCK].reshape(b, nb, CMP_BLOCK, rows.shape[2], HEAD_DIM) + pe[:, None, :]
    h = jax.nn.gelu(jnp.einsum('bnjgd,jde->bnge', blk, w1))
    return jnp.einsum('bnge,ed->bngd', h, w2)


def cmp_attend(q, kc, vc, q_pos, tbl):
    nb = kc.shape[1]
    s = jnp.einsum('btghd,bngd->btghn', q, kc) * ATTN_SCALE
    blk_end = jnp.arange(nb) * CMP_BLOCK + (CMP_BLOCK - 1)
    dist = (q_pos[:, :, None] - blk_end)[:, :, None, :]
    p = attn_probs(s, dist, dist >= 0, tbl)
    out = jnp.einsum('btghn,bngd->btghd', p.astype(vc.dtype), vc)
    return out, p.sum(axis=3)


def nsa_select(imp, q_pos, n_blocks):
    nb = imp.shape[-1]
    imp = jnp.pad(imp, ((0, 0), (0, 0), (0, 0), (0, n_blocks - nb)))
    j = jnp.arange(n_blocks)
    cur = (q_pos // SEL_BLOCK)[:, :, None, None]
    forced = (j == 0) | (j == cur) | (j == cur - 1)
    score = jnp.where(forced, jnp.inf, jnp.where(j <= cur, imp, -jnp.inf))
    return lax.top_k(score, min(N_SEL_BLOCKS, n_blocks))[1]


def sel_positions(idx):
    pos = idx[..., None] * SEL_BLOCK + jnp.arange(SEL_BLOCK)
    return pos.reshape(idx.shape[:-1] + (-1,))


def sparse_attend(q, kg, vg, q_pos, k_pos, tbl):
    s = jnp.einsum('btghd,btgkd->btghk', q, kg) * ATTN_SCALE
    dist = q_pos[:, :, None, None] - k_pos
    p = attn_probs(s, dist, dist >= 0, tbl)
    return jnp.einsum('btghk,btgkd->btghd', p.astype(vg.dtype), vg).astype(q.dtype)


def sel_prompt(q, ks, vs, idx, tbl):
    b, t, g = q.shape[:3]
    bi = jnp.arange(b)[:, None, None, None]
    gi = jnp.arange(g)[None, None, :, None]

    def one_block(args):
        qb, ib, pb = args
        kpos = sel_positions(ib)
        return sparse_attend(qb, ks[bi, kpos, gi], vs[bi, kpos, gi], pb, kpos, tbl)

    pos = jnp.arange(t).reshape(t // SEL_Q_BLOCK, 1, SEL_Q_BLOCK)
    out = lax.map(one_block, (to_blocks(q, SEL_Q_BLOCK), to_blocks(idx, SEL_Q_BLOCK), pos))
    return from_blocks(out)


def window_prompt(q, k, v, tbl):
    b, t, g, h, dh = q.shape
    nb = t // Q_BLOCK
    band = WINDOW + Q_BLOCK
    kp = jnp.pad(k, ((0, 0), (WINDOW, 0), (0, 0), (0, 0)))
    vp = jnp.pad(v, ((0, 0), (WINDOW, 0), (0, 0), (0, 0)))
    gidx = jnp.arange(nb)[:, None] * Q_BLOCK + jnp.arange(band)
    kb, vb = kp[:, gidx], vp[:, gidx]
    qb = q.reshape(b, nb, Q_BLOCK, g, h, dh)
    s = jnp.einsum('bnqghd,bnkgd->bnqghk', qb, kb) * ATTN_SCALE
    q_pos = jnp.arange(nb)[:, None] * Q_BLOCK + jnp.arange(Q_BLOCK)
    k_pos = gidx - WINDOW
    dist = (q_pos[:, :, None] - k_pos[:, None, :])[None, :, :, None, :]
    valid = (dist >= 0) & (dist < WINDOW) & (k_pos[None, :, None, None, :] >= 0)
    p = attn_probs(s, dist, valid, tbl)
    out = jnp.einsum('bnqghk,bnkgd->bnqghd', p.astype(vb.dtype), vb)
    return out.reshape(b, t, g, h, dh).astype(q.dtype)


def window_attend(q, k, v, q_pos, k_pos, tbl):
    s = jnp.einsum('btghd,bkgd->btghk', q, k) * ATTN_SCALE
    dist = q_pos[:, :, None, None] - k_pos
    valid = (dist >= 0) & (dist < WINDOW) & (k_pos >= 0)
    p = attn_probs(s, dist, valid, tbl)
    return jnp.einsum('btghk,bkgd->btghd', p.astype(v.dtype), v).astype(q.dtype)


def dsa_select(iq, iw, ik, q_pos, n_keep):
    logits = jax.nn.relu(jnp.einsum('bthd,bsd->bths', iq, ik).astype(jnp.float32) * IDX_DIM ** -0.5)
    score = jnp.einsum('bths,bth->bts', logits, iw.astype(jnp.float32)) * IDX_HEADS ** -0.5
    k_pos = jnp.arange(ik.shape[1])
    score = jnp.where(k_pos <= q_pos[:, :, None], score, -jnp.inf)
    return lax.top_k(score, n_keep)[1]


def dsa_prompt(q, k, v, iq, iw, ik, tbl):
    b, t = q.shape[:2]
    n_keep = min(DSA_TOPK, t // 4)
    bi = jnp.arange(b)[:, None, None]

    def one_block(args):
        qb, iqb, iwb, pb = args
        pos = dsa_select(iqb, iwb, ik, pb, n_keep)
        kg = k[bi, pos].swapaxes(2, 3)
        vg = v[bi, pos].swapaxes(2, 3)
        return sparse_attend(qb, kg, vg, pb, pos[:, :, None, :], tbl)

    pos = jnp.arange(t).reshape(t // Q_BLOCK, 1, Q_BLOCK)
    out = lax.map(one_block, (to_blocks(q, Q_BLOCK), to_blocks(iq, Q_BLOCK), to_blocks(iw, Q_BLOCK), pos))
    return from_blocks(out)


def gather_past_and_new(pool, page_table, new_rows, pos, *extra):
    past = page_table.shape[1] * PAGE_SIZE
    bi = jnp.arange(pos.shape[0]).reshape((-1,) + (1,) * (pos.ndim - 1))
    page = jnp.minimum(pos // PAGE_SIZE, page_table.shape[1] - 1)
    old = pool[(page_table[bi, page], pos % PAGE_SIZE) + extra]
    new = new_rows[(bi, jnp.clip(pos - past, 0, new_rows.shape[1] - 1)) + extra]
    is_old = (pos < past).reshape(pos.shape + (1,) * (old.ndim - pos.ndim))
    return jnp.where(is_old, old, new)


def merge_branches(o_cmp, o_sel, o_win, g_a, o_b, g_m, w_up_a, w_up_b, w_o):
    b, t = g_m.shape[:2]
    o_a = (g_a[:, :, 0, ..., None] * o_cmp + g_a[:, :, 1, ..., None] * o_sel
           + g_a[:, :, 2, ..., None] * o_win)
    y_a = o_a.reshape(b, t, -1) @ w_up_a
    y_b = o_b.reshape(b, t, -1) @ w_up_b
    return (g_m[:, :, 0] * y_a + g_m[:, :, 1] * y_b) @ w_o


def mixer_prompt(x, w_in, cmp_pe, cmp_w1, cmp_w2, tbl_a, tbl_b, w_up_a, w_up_b, w_o):
    b, t, _ = x.shape
    q_a, kv_a, g_a, q_b, kv_b, iq, ik, iw, g_m = input_projection(x, w_in)
    kc, vc, ks, vs, kw, vw = [kv_a[:, :, i] for i in range(6)]
    kb, vb = kv_b[:, :, 0], kv_b[:, :, 1]
    q_pos = jnp.arange(t)[None]
    o_cmp, imp = cmp_attend(q_a, nsa_compress(kc, cmp_pe[0], cmp_w1[0], cmp_w2[0]),
                            nsa_compress(vc, cmp_pe[1], cmp_w1[1], cmp_w2[1]), q_pos, tbl_a)
    idx = nsa_select(imp, q_pos, -(-t // SEL_BLOCK))
    o_sel = sel_prompt(q_a, ks, vs, idx, tbl_a)
    o_win = window_prompt(q_a, kw, vw, tbl_a)
    o_b = dsa_prompt(q_b, kb, vb, iq, iw, ik, tbl_b)
    y = merge_branches(o_cmp, o_sel, o_win, g_a, o_b, g_m, w_up_a, w_up_b, w_o)
    n_win = min(WINDOW, t)
    return y, (kc, vc, ks, vs, kw[:, -n_win:], vw[:, -n_win:], kb, vb, ik)


def mixer_sample(x, cache_cmp_k, cache_cmp_v, cache_slc_k, cache_slc_v, win_k, win_v,
                 cache_dsa_k, cache_dsa_v, cache_idx_k, page_table,
                 w_in, cmp_pe, cmp_w1, cmp_w2, tbl_a, tbl_b, w_up_a, w_up_b, w_o):
    b, t, _ = x.shape
    past = page_table.shape[1] * PAGE_SIZE
    total = past + t
    q_a, kv_a, g_a, q_b, kv_b, iq, ik, iw, g_m = input_projection(x, w_in)
    kc, vc, ks, vs, kw, vw = [kv_a[:, :, i] for i in range(6)]
    kb, vb = kv_b[:, :, 0], kv_b[:, :, 1]
    q_pos = past + jnp.arange(t)[None]

    def full_rows(pool, new):
        return jnp.concatenate([pool[page_table].reshape((b, past) + pool.shape[2:]), new], axis=1)

    o_cmp, imp = cmp_attend(q_a, nsa_compress(full_rows(cache_cmp_k, kc), cmp_pe[0], cmp_w1[0], cmp_w2[0]),
                            nsa_compress(full_rows(cache_cmp_v, vc), cmp_pe[1], cmp_w1[1], cmp_w2[1]),
                            q_pos, tbl_a)
    idx = nsa_select(imp, q_pos, -(-total // SEL_BLOCK))
    kpos = sel_positions(idx)
    gi = jnp.arange(NSA_KV_GROUPS)[None, None, :, None]
    o_sel = sparse_attend(q_a, gather_past_and_new(cache_slc_k, page_table, ks, kpos, gi),
                          gather_past_and_new(cache_slc_v, page_table, vs, kpos, gi), q_pos, kpos, tbl_a)
    wk = jnp.concatenate([win_k, kw], axis=1)
    wv = jnp.concatenate([win_v, vw], axis=1)
    w_past = win_k.shape[1]
    o_win = window_attend(q_a, wk, wv, q_pos, past - w_past + jnp.arange(w_past + t), tbl_a)
    pos = dsa_select(iq, iw, full_rows(cache_idx_k, ik), q_pos, min(DSA_TOPK, total // 4))
    kg = gather_past_and_new(cache_dsa_k, page_table, kb, pos).swapaxes(2, 3)
    vg = gather_past_and_new(cache_dsa_v, page_table, vb, pos).swapaxes(2, 3)
    o_b = sparse_attend(q_b, kg, vg, q_pos, pos[:, :, None, :], tbl_b)
    y = merge_branches(o_cmp, o_sel, o_win, g_a, o_b, g_m, w_up_a, w_up_b, w_o)
    n_win = min(WINDOW, total)
    return y, (kc, vc, ks, vs, wk[:, -n_win:], wv[:, -n_win:], kb, vb, ik)


def routed_experts(x, eidx, ew, w_gate, w_up, w_down):
    n_tok, d = x.shape
    n_exp = w_gate.shape[0]
    n_asg = eidx.size
    flat_e = eidx.reshape(-1)
    order = jnp.argsort(flat_e, stable=True)
    se = flat_e[order]
    counts = jnp.bincount(flat_e, length=n_exp)
    padded = (counts + MOE_BLOCK - 1) // MOE_BLOCK * MOE_BLOCK
    start = jnp.cumsum(counts) - counts
    pend = jnp.cumsum(padded)
    dest = (pend - padded)[se] + jnp.arange(n_asg) - start[se]
    n_rows = -(-n_asg // MOE_BLOCK) * MOE_BLOCK + n_exp * MOE_BLOCK
    n_blk = n_rows // MOE_BLOCK
    row_tok = jnp.full((n_rows,), n_tok, jnp.int32).at[dest].set((order // eidx.shape[1]).astype(jnp.int32))
    row_w = jnp.zeros((n_rows,), ew.dtype).at[dest].set(ew.reshape(-1)[order])
    blk_e = jnp.minimum(jnp.searchsorted(pend, jnp.arange(n_blk) * MOE_BLOCK, side='right'), n_exp - 1)
    x_pad = jnp.concatenate([x, jnp.zeros((1, d), x.dtype)], axis=0)

    def expert_block(args):
        tok, wt, e = args
        xb = x_pad[tok]
        h = jax.nn.silu(xb @ w_gate[e]) * (xb @ w_up[e])
        return (h @ w_down[e]) * wt[:, None].astype(xb.dtype)

    out = lax.map(expert_block, (row_tok.reshape(n_blk, MOE_BLOCK), row_w.reshape(n_blk, MOE_BLOCK), blk_e))
    y = jnp.zeros((n_tok + 1, d), out.dtype).at[row_tok].add(out.reshape(n_rows, d))
    return y[:n_tok]


def moe_ffn(x, w_router, router_bias, w_gate, w_up, w_down, sh_gate, sh_up, sh_down):
    n = x.shape[0]
    epg = N_EXPERTS // N_EXPERT_GROUPS
    scores = jax.nn.sigmoid(x.astype(jnp.float32) @ w_router.astype(jnp.float32))
    biased = scores + router_bias.astype(jnp.float32)
    grp_score = lax.top_k(biased.reshape(n, N_EXPERT_GROUPS, epg), 2)[0].sum(-1)
    gsel = lax.top_k(grp_score, TOPK_GROUPS)[1]
    gmask = (gsel[:, :, None] == jnp.arange(N_EXPERT_GROUPS)).any(axis=1)
    emask = jnp.repeat(gmask, epg, axis=1)
    eidx = lax.top_k(jnp.where(emask, biased, -jnp.inf), TOP_K)[1]
    wts = jnp.take_along_axis(scores, eidx, axis=1)
    wts = wts / wts.sum(-1, keepdims=True) * ROUTED_SCALE
    routed = routed_experts(x, eidx, wts.astype(x.dtype), w_gate, w_up, w_down)
    shared = (jax.nn.silu(x @ sh_gate) * (x @ sh_up)) @ sh_down
    return routed + shared


def setup_inputs(seed: int = 0) -> dict:
    key = jax.random.key(seed)
    keys = iter(list(jax.random.split(key, 40)))

    def nrm(shape, scale=1.0):
        return jax.random.normal(next(keys), shape, jnp.float32) * scale

    n_pages = PAST_LEN // PAGE_SIZE
    n_pool = (DEC_BATCH * n_pages * 5) // 4
    w_past = min(WINDOW, PAST_LEN)
    pa = (DEPTH, n_pool, PAGE_SIZE, NSA_KV_GROUPS, HEAD_DIM)
    pb = (DEPTH, n_pool, PAGE_SIZE, DSA_KV_GROUPS, HEAD_DIM)
    wa = (DEPTH, DEC_BATCH, w_past, NSA_KV_GROUPS, HEAD_DIM)
    perm = jax.random.permutation(next(keys), n_pool)
    page_table = perm[:DEC_BATCH * n_pages].reshape(DEC_BATCH, n_pages).astype(jnp.int32)
    a_w = NSA_HEADS * HEAD_DIM
    b_w = DSA_HEADS * HEAD_DIM
    return {
        'x_prompt': nrm((BATCH, SEQ, D_MODEL)),
        'x_sample': nrm((DEC_BATCH, DEC_SEQ, D_MODEL)),
        'cache_cmp_k': nrm(pa), 'cache_cmp_v': nrm(pa),
        'cache_slc_k': nrm(pa), 'cache_slc_v': nrm(pa),
        'state_win_k': nrm(wa), 'state_win_v': nrm(wa),
        'cache_dsa_k': nrm(pb), 'cache_dsa_v': nrm(pb),
        'cache_idx_k': nrm((DEPTH, n_pool, PAGE_SIZE, IDX_DIM)),
        'page_table': page_table,
        'rel_bias_table': nrm((N_BUCKETS, NSA_HEADS + DSA_HEADS), 0.5),
        'w_in': nrm((DEPTH, D_MODEL, D_IN), D_MODEL ** -0.5),
        'cmp_pe': nrm((DEPTH, 2, CMP_BLOCK, HEAD_DIM), 0.1),
        'cmp_w1': nrm((DEPTH, 2, CMP_BLOCK, HEAD_DIM, CMP_HIDDEN), (CMP_BLOCK * HEAD_DIM) ** -0.5),
        'cmp_w2': nrm((DEPTH, 2, CMP_HIDDEN, HEAD_DIM), CMP_HIDDEN ** -0.5),
        'w_up_a': nrm((DEPTH, a_w, D_MODEL), a_w ** -0.5),
        'w_up_b': nrm((DEPTH, b_w, D_MODEL), b_w ** -0.5),
        'w_o': nrm((DEPTH, D_MODEL, D_MODEL), D_MODEL ** -0.5 * DEEPNORM_BETA),
        'ln1_g': 1.0 + nrm((DEPTH, D_MODEL), 0.02),
        'ln1_b': nrm((DEPTH, D_MODEL), 0.02),
        'w_router': nrm((DEPTH, D_MODEL, N_EXPERTS), D_MODEL ** -0.5),
        'router_bias': nrm((DEPTH, N_EXPERTS), 0.01),
        'moe_w_gate': nrm((DEPTH, N_EXPERTS, D_MODEL, EXPERT_FF), D_MODEL ** -0.5),
        'moe_w_up': nrm((DEPTH, N_EXPERTS, D_MODEL, EXPERT_FF), D_MODEL ** -0.5),
        'moe_w_down': nrm((DEPTH, N_EXPERTS, EXPERT_FF, D_MODEL), EXPERT_FF ** -0.5 * DEEPNORM_BETA),
        'sh_w_gate': nrm((DEPTH, D_MODEL, SHARED_FF), D_MODEL ** -0.5),
        'sh_w_up': nrm((DEPTH, D_MODEL, SHARED_FF), D_MODEL ** -0.5),
        'sh_w_down': nrm((DEPTH, SHARED_FF, D_MODEL), SHARED_FF ** -0.5 * DEEPNORM_BETA),
        'ln2_g': 1.0 + nrm((DEPTH, D_MODEL), 0.02),
        'ln2_b': nrm((DEPTH, D_MODEL), 0.02),
    }


def reference(x_prompt, x_sample, cache_cmp_k, cache_cmp_v, cache_slc_k, cache_slc_v,
              state_win_k, state_win_v, cache_dsa_k, cache_dsa_v, cache_idx_k, page_table,
              rel_bias_table, w_in, cmp_pe, cmp_w1, cmp_w2, w_up_a, w_up_b, w_o, ln1_g, ln1_b,
              w_router, router_bias, moe_w_gate, moe_w_up, moe_w_down,
              sh_w_gate, sh_w_up, sh_w_down, ln2_g, ln2_b):
    tbl_a = rel_bias_table[:, :NSA_HEADS].reshape(N_BUCKETS, NSA_KV_GROUPS, NSA_HPG)
    tbl_b = rel_bias_table[:, NSA_HEADS:].reshape(N_BUCKETS, DSA_KV_GROUPS, DSA_HPG)
    xp, xs = x_prompt, x_sample
    p_states, s_states = [], []
    for l in range(DEPTH):
        mp, st_p = mixer_prompt(xp, w_in[l], cmp_pe[l], cmp_w1[l], cmp_w2[l], tbl_a, tbl_b,
                                w_up_a[l], w_up_b[l], w_o[l])
        ms, st_s = mixer_sample(xs, cache_cmp_k[l], cache_cmp_v[l], cache_slc_k[l], cache_slc_v[l],
                                state_win_k[l], state_win_v[l], cache_dsa_k[l], cache_dsa_v[l],
                                cache_idx_k[l], page_table, w_in[l], cmp_pe[l], cmp_w1[l], cmp_w2[l],
                                tbl_a, tbl_b, w_up_a[l], w_up_b[l], w_o[l])
        xp = layer_norm(DEEPNORM_ALPHA * xp + mp, ln1_g[l], ln1_b[l])
        xs = layer_norm(DEEPNORM_ALPHA * xs + ms, ln1_g[l], ln1_b[l])
        n_p = xp.shape[0] * xp.shape[1]
        tokens = jnp.concatenate([xp.reshape(-1, D_MODEL), xs.reshape(-1, D_MODEL)], axis=0)
        f = moe_ffn(tokens, w_router[l], router_bias[l], moe_w_gate[l], moe_w_up[l], moe_w_down[l],
                    sh_w_gate[l], sh_w_up[l], sh_w_down[l])
        xp = layer_norm(DEEPNORM_ALPHA * xp + f[:n_p].reshape(xp.shape), ln2_g[l], ln2_b[l])
        xs = layer_norm(DEEPNORM_ALPHA * xs + f[n_p:].reshape(xs.shape), ln2_g[l], ln2_b[l])
        p_states.append(st_p)
        s_states.append(st_s)
    (p_cmp_k, p_cmp_v, p_slc_k, p_slc_v, p_win_k, p_win_v, p_dsa_k, p_dsa_v, p_idx_k) = [
        jnp.stack(a) for a in zip(*p_states)]
    (s_cmp_k, s_cmp_v, s_slc_k, s_slc_v, s_win_k, s_win_v, s_dsa_k, s_dsa_v, s_idx_k) = [
        jnp.stack(a) for a in zip(*s_states)]
    return (xp, xs, p_cmp_k, p_cmp_v, p_slc_k, p_slc_v, p_win_k, p_win_v, p_dsa_k, p_dsa_v, p_idx_k,
            s_cmp_k, s_cmp_v, s_slc_k, s_slc_v, s_win_k, s_win_v, s_dsa_k, s_dsa_v, s_idx_k)
```

```python
import functools
import math

import jax
import jax.numpy as jnp
import numpy as np
from jax import lax
from jax.experimental import pallas as pl
from jax.experimental.pallas import tpu as pltpu

D_MODEL = 2048
PAGE_SIZE = 128
HEAD_DIM = 128
NSA_HEADS = 8
NSA_KV_GROUPS = 2
NSA_HPG = NSA_HEADS // NSA_KV_GROUPS
CMP_BLOCK = 64
CMP_HIDDEN = 128
SEL_BLOCK = 64
N_SEL_BLOCKS = 16
WINDOW = 512
DSA_HEADS = 8
DSA_KV_GROUPS = 2
DSA_HPG = DSA_HEADS // DSA_KV_GROUPS
IDX_HEADS = 8
IDX_DIM = 64
DSA_TOPK = 256
N_BUCKETS = 32
MAX_EXACT = 16
MAX_DISTANCE = 128
N_EXPERTS = 64
EXPERT_FF = 512
SHARED_FF = 512
TOP_K = 6
N_EXPERT_GROUPS = 8
TOPK_GROUPS = 4
ROUTED_SCALE = 2.5
LN_EPS = 1e-5
ATTN_SCALE = HEAD_DIM ** -0.5
DEPTH = 1
DEEPNORM_ALPHA = (2 * DEPTH) ** 0.25
IN_COLS = (NSA_HEADS * HEAD_DIM, 6 * NSA_KV_GROUPS * HEAD_DIM, 3 * NSA_HEADS,
           DSA_HEADS * HEAD_DIM, 2 * DSA_KV_GROUPS * HEAD_DIM,
           IDX_HEADS * IDX_DIM, IDX_DIM, IDX_HEADS, 2 * D_MODEL)

KV_W = NSA_KV_GROUPS * HEAD_DIM
MISC_W = 128
MISC_IW = IDX_DIM
MISC_GA = IDX_DIM + IDX_HEADS
NEG = -1e30
Q_TILE = 256
EXPERT_ROWS = 256
VMEM_LIMIT = 56 * 1024 * 1024

BF16 = jnp.bfloat16
F32 = jnp.float32


def _cparams(*sem):
    return pltpu.CompilerParams(dimension_semantics=sem, vmem_limit_bytes=VMEM_LIMIT)


def _dot(a, b):
    return jnp.dot(a, b, preferred_element_type=F32)


def _dot_nt(a, b):
    return lax.dot_general(a, b, (((1,), (1,)), ((), ())), preferred_element_type=F32)


def _proj_kernel(x_ref, w_ref, *o_refs, widths):
    acc = _dot(x_ref[...], w_ref[...])
    off = 0
    for o_ref, wd in zip(o_refs, widths):
        o_ref[...] = acc[:, off:off + wd]
        off += wd


def _project(x, w, widths, tm):
    m, k = x.shape
    n = w.shape[1]
    assert sum(widths) == n and m % tm == 0
    return pl.pallas_call(
        functools.partial(_proj_kernel, widths=widths),
        grid=(m // tm,),
        in_specs=[pl.BlockSpec((tm, k), lambda i: (i, 0)),
                  pl.BlockSpec((k, n), lambda i: (0, 0))],
        out_specs=[pl.BlockSpec((tm, wd), lambda i: (i, 0)) for wd in widths],
        out_shape=[jax.ShapeDtypeStruct((m, wd), F32) for wd in widths],
        compiler_params=_cparams("parallel"),
        name="project",
    )(x, w)


CMP_JCHUNK = 8


def _compress_kernel(x_ref, pe_ref, w1_ref, w2_ref, o_ref, acc_ref):
    jc = pl.program_id(1)

    @pl.when(jc == 0)
    def _():
        acc_ref[...] = jnp.zeros_like(acc_ref)

    acc = acc_ref[...]
    for jj in range(CMP_JCHUNK):
        lhs = (x_ref[:, jj, :] + pe_ref[jj:jj + 1, :]).astype(BF16)
        acc = acc + _dot(lhs, w1_ref[jj])
    acc_ref[...] = acc

    @pl.when(jc == pl.num_programs(1) - 1)
    def _():
        h = jax.nn.gelu(acc_ref[...])
        o_ref[...] = _dot(h.astype(BF16), w2_ref[...])


def _compress(rows, pe2, w1big, w2big):
    r = rows.shape[0]
    tr = math.gcd(r, 1024)
    assert tr % 8 == 0
    return pl.pallas_call(
        _compress_kernel,
        grid=(r // tr, CMP_BLOCK // CMP_JCHUNK),
        in_specs=[pl.BlockSpec((tr, CMP_JCHUNK, KV_W), lambda i, j: (i, j, 0)),
                  pl.BlockSpec((CMP_JCHUNK, KV_W), lambda i, j: (j, 0)),
                  pl.BlockSpec((CMP_JCHUNK, KV_W, KV_W), lambda i, j: (j, 0, 0)),
                  pl.BlockSpec((KV_W, KV_W), lambda i, j: (0, 0))],
        out_specs=pl.BlockSpec((tr, KV_W), lambda i, j: (i, 0)),
        out_shape=jax.ShapeDtypeStruct((r, KV_W), F32),
        scratch_shapes=[pltpu.VMEM((tr, KV_W), F32)],
        compiler_params=_cparams("parallel", "arbitrary"),
        name="compress",
    )(rows, pe2, w1big, w2big)


def _compress_weights(pe, w1, w2):
    pe2 = jnp.concatenate([pe, pe], axis=-1)
    z1 = jnp.zeros_like(w1)
    w1big = jnp.concatenate([jnp.concatenate([w1, z1], axis=2), jnp.concatenate([z1, w1], axis=2)], axis=1)
    z2 = jnp.zeros_like(w2)
    w2big = jnp.concatenate([jnp.concatenate([w2, z2], axis=1), jnp.concatenate([z2, w2], axis=1)], axis=0)
    return pe2, w1big.astype(BF16), w2big.astype(BF16)


def _cmp_select_kernel(q_ref, kc_ref, vc_ref, cb_ref, o_ref, sel_ref, *, tq, nb, n_sel):
    qi = pl.program_id(1)
    t = qi * tq + lax.broadcasted_iota(jnp.int32, (tq, nb), 0)
    j = lax.broadcasted_iota(jnp.int32, (tq, nb), 1)
    cur = t // SEL_BLOCK
    for g in range(NSA_KV_GROUPS):
        kc = kc_ref[0, :, g * HEAD_DIM:(g + 1) * HEAD_DIM].astype(BF16)
        vc = vc_ref[0, :, g * HEAD_DIM:(g + 1) * HEAD_DIM].astype(BF16)
        imp = jnp.zeros((tq, nb), F32)
        for h in range(NSA_HPG):
            hh = g * NSA_HPG + h
            q = q_ref[0, :, hh * HEAD_DIM:(hh + 1) * HEAD_DIM].astype(BF16)
            cb = cb_ref[hh]
            valid = cb > 0.5 * NEG
            s = _dot_nt(q, kc) * ATTN_SCALE + cb
            m = jnp.max(s, axis=-1, keepdims=True)
            e = jnp.where(valid, jnp.exp(s - m), 0.0)
            den = jnp.sum(e, axis=-1, keepdims=True)
            p = e / jnp.where(den > 0, den, 1.0)
            o_ref[0, :, hh * HEAD_DIM:(hh + 1) * HEAD_DIM] = _dot(p.astype(BF16), vc)
            imp = imp + p
        forced = (j == 0) | (j == cur) | (j == cur - 1)
        score = jnp.where(forced, jnp.inf, jnp.where(j <= cur, imp, -jnp.inf))
        rank = jnp.zeros((tq, nb), jnp.int32)
        for k in range(nb):
            col = score[:, k:k + 1]
            ahead = (col > score) | ((col == score) & (k < j))
            rank = rank + jnp.where(ahead, 1, 0)
        sel_ref[0, g] = jnp.where(rank < n_sel, 1.0, 0.0).astype(F32)


def _cmp_select(qa, kcomp, vcomp, cbias, tq):
    b, t, _ = qa.shape
    nb = kcomp.shape[1]
    n_sel = min(N_SEL_BLOCKS, nb)
    return pl.pallas_call(
        functools.partial(_cmp_select_kernel, tq=tq, nb=nb, n_sel=n_sel),
        grid=(b, t // tq),
        in_specs=[pl.BlockSpec((1, tq, NSA_HEADS * HEAD_DIM), lambda bi, qi: (bi, qi, 0)),
                  pl.BlockSpec((1, nb, KV_W), lambda bi, qi: (bi, 0, 0)),
                  pl.BlockSpec((1, nb, KV_W), lambda bi, qi: (bi, 0, 0)),
                  pl.BlockSpec((NSA_HEADS, tq, nb), lambda bi, qi: (0, qi, 0))],
        out_specs=[pl.BlockSpec((1, tq, NSA_HEADS * HEAD_DIM), lambda bi, qi: (bi, qi, 0)),
                   pl.BlockSpec((1, NSA_KV_GROUPS, tq, nb), lambda bi, qi: (bi, 0, qi, 0))],
        out_shape=[jax.ShapeDtypeStruct((b, t, NSA_HEADS * HEAD_DIM), F32),
                   jax.ShapeDtypeStruct((b, NSA_KV_GROUPS, t, nb), F32)],
        compiler_params=_cparams("parallel", "parallel"),
        name="cmp_select",
    )(qa, kcomp, vcomp, cbias)


def _dense_attn_kernel(c_ref, q_ref, k_ref, v_ref, mask_ref, expand_ref, d_ref, o_ref, s_ref,
                       *, tq, s_len, n_heads, hpg, mode):
    qi = pl.program_id(1)
    t = qi * tq + lax.broadcasted_iota(jnp.int32, (tq, s_len), 0)
    col = lax.broadcasted_iota(jnp.int32, (tq, s_len), 1)
    causal = col <= t
    diag = pl.multiple_of(qi * tq, tq)
    prev = pl.multiple_of(jnp.maximum(qi - 1, 0) * tq, tq)
    keep = None
    for hh in range(n_heads):
        g = hh // hpg
        if hh % hpg == 0:
            if mode == "sel":
                keep = (_dot(mask_ref[0, g].astype(BF16), expand_ref[...]) > 0.5) & causal
            else:
                if g == 0:
                    keep = (mask_ref[0] > 0.5) & causal
            k = k_ref[0, :, g * HEAD_DIM:(g + 1) * HEAD_DIM].astype(BF16)
            v = v_ref[0, :, g * HEAD_DIM:(g + 1) * HEAD_DIM].astype(BF16)
        q = q_ref[0, :, hh * HEAD_DIM:(hh + 1) * HEAD_DIM].astype(BF16)
        s_ref[...] = _dot_nt(q, k) * ATTN_SCALE + c_ref[hh]
        s_ref[:, pl.ds(diag, tq)] += d_ref[hh, 0]

        @pl.when(qi > 0)
        def _():
            s_ref[:, pl.ds(prev, tq)] += d_ref[hh, 1]

        s = jnp.where(keep, s_ref[...], NEG)
        m = jnp.max(s, axis=-1, keepdims=True)
        e = jnp.where(keep, jnp.exp(s - m), 0.0)
        den = jnp.sum(e, axis=-1, keepdims=True)
        o = _dot(e.astype(BF16), v)
        o_ref[0, :, hh * HEAD_DIM:(hh + 1) * HEAD_DIM] = o / jnp.where(den > 0, den, 1.0)


def _dense_attn(q, k, v, mask, expand, cvec, dtiles, tq, mode):
    b, t, qw = q.shape
    n_heads = qw // HEAD_DIM
    hpg = n_heads // (k.shape[2] // HEAD_DIM)
    if mode == "sel":
        nb = mask.shape[-1]
        mask_spec = pl.BlockSpec((1, mask.shape[1], tq, nb), lambda bi, qi: (bi, 0, qi, 0))
    else:
        mask_spec = pl.BlockSpec((1, tq, t), lambda bi, qi: (bi, qi, 0))
    return pl.pallas_call(
        functools.partial(_dense_attn_kernel, tq=tq, s_len=t, n_heads=n_heads, hpg=hpg, mode=mode),
        grid=(b, t // tq),
        in_specs=[pl.BlockSpec(memory_space=pltpu.SMEM),
                  pl.BlockSpec((1, tq, qw), lambda bi, qi: (bi, qi, 0)),
                  pl.BlockSpec((1, t, k.shape[2]), lambda bi, qi: (bi, 0, 0)),
                  pl.BlockSpec((1, t, v.shape[2]), lambda bi, qi: (bi, 0, 0)),
                  mask_spec,
                  pl.BlockSpec(expand.shape, lambda bi, qi: (0, 0)),
                  pl.BlockSpec(dtiles.shape, lambda bi, qi: (0, 0, 0, 0))],
        out_specs=pl.BlockSpec((1, tq, qw), lambda bi, qi: (bi, qi, 0)),
        out_shape=jax.ShapeDtypeStruct((b, t, qw), F32),
        scratch_shapes=[pltpu.VMEM((tq, t), F32)],
        compiler_params=_cparams("parallel", "parallel"),
        name="dense_attn_" + mode,
    )(cvec, q, k, v, mask, expand, dtiles)


def _window_attn_kernel(q_ref, k_ref, v_ref, w_ref, o_ref, *, tq, n_chunks, n_heads, hpg):
    qi = pl.program_id(1)
    starts, pens = [], []
    for r in range(n_chunks):
        cj = qi - (n_chunks - 1) + r
        starts.append(pl.multiple_of(jnp.maximum(cj, 0) * tq, tq))
        pens.append(jnp.where(cj < 0, NEG, 0.0).astype(F32))
    for hh in range(n_heads):
        g = hh // hpg
        q = q_ref[0, :, hh * HEAD_DIM:(hh + 1) * HEAD_DIM].astype(BF16)
        ss = []
        for r in range(n_chunks):
            k = k_ref[0, pl.ds(starts[r], tq), g * HEAD_DIM:(g + 1) * HEAD_DIM].astype(BF16)
            ss.append(_dot_nt(q, k) * ATTN_SCALE + (w_ref[hh, n_chunks - 1 - r] + pens[r]))
        m = ss[0].max(axis=-1, keepdims=True)
        for r in range(1, n_chunks):
            m = jnp.maximum(m, ss[r].max(axis=-1, keepdims=True))
        den = jnp.zeros((tq, 1), F32)
        o = jnp.zeros((tq, HEAD_DIM), F32)
        for r in range(n_chunks):
            e = jnp.where(ss[r] > 0.5 * NEG, jnp.exp(ss[r] - m), 0.0)
            den = den + e.sum(axis=-1, keepdims=True)
            v = v_ref[0, pl.ds(starts[r], tq), g * HEAD_DIM:(g + 1) * HEAD_DIM].astype(BF16)
            o = o + _dot(e.astype(BF16), v)
        o_ref[0, :, hh * HEAD_DIM:(hh + 1) * HEAD_DIM] = o / jnp.where(den > 0, den, 1.0)


def _window_attn(q, k, v, wtiles, tq):
    b, t, qw = q.shape
    n_heads = qw // HEAD_DIM
    hpg = n_heads // (k.shape[2] // HEAD_DIM)
    n_chunks = wtiles.shape[1]
    return pl.pallas_call(
        functools.partial(_window_attn_kernel, tq=tq, n_chunks=n_chunks, n_heads=n_heads, hpg=hpg),
        grid=(b, t // tq),
        in_specs=[pl.BlockSpec((1, tq, qw), lambda bi, qi: (bi, qi, 0)),
                  pl.BlockSpec((1, t, k.shape[2]), lambda bi, qi: (bi, 0, 0)),
                  pl.BlockSpec((1, t, v.shape[2]), lambda bi, qi: (bi, 0, 0)),
                  pl.BlockSpec(wtiles.shape, lambda bi, qi: (0, 0, 0, 0))],
        out_specs=pl.BlockSpec((1, tq, qw), lambda bi, qi: (bi, qi, 0)),
        out_shape=jax.ShapeDtypeStruct((b, t, qw), F32),
        compiler_params=_cparams("parallel", "parallel"),
        name="window_attn",
    )(q, k, v, wtiles)


INT_MIN = -2 ** 31


def _topk_mask(key_ref, n_keep, tq, s_len):
    def body(i, thr_u):
        cand_u = thr_u | jnp.left_shift(jnp.int32(1), 31 - i)
        cnt = jnp.sum(jnp.where(key_ref[...] >= (cand_u ^ INT_MIN), 1, 0), axis=-1, keepdims=True)
        return jnp.where(cnt >= n_keep, cand_u, thr_u)

    thr_u = lax.fori_loop(0, 32, body, jnp.zeros((tq, 1), jnp.int32))
    thr = thr_u ^ INT_MIN
    return thr


def _index_select_kernel(iq_ref, mq_ref, mk_ref, o_ref, key_ref, *, tq, s_len, n_keep):
    qi = pl.program_id(1)
    t = qi * tq + lax.broadcasted_iota(jnp.int32, (tq, s_len), 0)
    col = lax.broadcasted_iota(jnp.int32, (tq, s_len), 1)
    ik = mk_ref[0, :, 0:IDX_DIM].astype(BF16)
    score = jnp.zeros((tq, s_len), F32)
    for h in range(IDX_HEADS):
        iq = iq_ref[0, :, h * IDX_DIM:(h + 1) * IDX_DIM].astype(BF16)
        lg = jnp.maximum(_dot_nt(iq, ik) * IDX_DIM ** -0.5, 0.0)
        score = score + lg * mq_ref[0, :, MISC_IW + h:MISC_IW + h + 1]
    score = score * IDX_HEADS ** -0.5 + 0.0
    score = jnp.where(col <= t, score, -jnp.inf)
    bits = pltpu.bitcast(score, jnp.int32)
    key_ref[...] = jnp.where(bits < 0, bits ^ 0x7FFFFFFF, bits)
    thr = _topk_mask(key_ref, n_keep, tq, s_len)
    key = key_ref[...]
    gt = key > thr
    eq = key == thr
    need = n_keep - jnp.sum(jnp.where(gt, 1, 0), axis=-1, keepdims=True)
    n_eq = jnp.sum(jnp.where(eq, 1, 0), axis=-1, keepdims=True)
    o_ref[0] = jnp.where(gt | eq, 1.0, 0.0).astype(o_ref.dtype)

    @pl.when(jnp.max(n_eq - need) > 0)
    def _():
        r_i = lax.broadcasted_iota(jnp.int32, (128, 128), 0)
        c_i = lax.broadcasted_iota(jnp.int32, (128, 128), 1)
        tri = jnp.where(r_i <= c_i, 1.0, 0.0).astype(BF16)
        before = jnp.zeros((tq, 1), F32)
        need_f = need.astype(F32)
        for c in range(s_len // 128):
            sl = slice(c * 128, (c + 1) * 128)
            eq_c = eq[:, sl]
            eq_f = jnp.where(eq_c, 1.0, 0.0)
            pref = _dot(eq_f.astype(BF16), tri) + before
            keep = gt[:, sl] | (eq_c & (pref <= need_f))
            o_ref[0, :, sl] = jnp.where(keep, 1.0, 0.0).astype(o_ref.dtype)
            before = before + jnp.sum(eq_f, axis=-1, keepdims=True)


def _index_select(iq, misc, tq, n_keep):
    b, t, _ = iq.shape
    return pl.pallas_call(
        functools.partial(_index_select_kernel, tq=tq, s_len=t, n_keep=n_keep),
        grid=(b, t // tq),
        in_specs=[pl.BlockSpec((1, tq, IDX_HEADS * IDX_DIM), lambda bi, qi: (bi, qi, 0)),
                  pl.BlockSpec((1, tq, MISC_W), lambda bi, qi: (bi, qi, 0)),
                  pl.BlockSpec((1, t, MISC_W), lambda bi, qi: (bi, 0, 0))],
        out_specs=pl.BlockSpec((1, tq, t), lambda bi, qi: (bi, qi, 0)),
        out_shape=jax.ShapeDtypeStruct((b, t, t), BF16),
        scratch_shapes=[pltpu.VMEM((tq, t), jnp.int32)],
        compiler_params=_cparams("parallel", "parallel"),
        name="index_select",
    )(iq, misc, misc)


def _merge_up_kernel(x_ref, oc_ref, os_ref, ow_ref, ob_ref, misc_ref, wga_ref, wgb_ref, wua_ref, wub_ref,
                     y_ref, oa_ref):
    ga = jax.nn.sigmoid(misc_ref[:, MISC_GA:MISC_GA + 3 * NSA_HEADS])
    for hh in range(NSA_HEADS):
        sl = slice(hh * HEAD_DIM, (hh + 1) * HEAD_DIM)
        oa = (ga[:, hh:hh + 1] * oc_ref[:, sl] + ga[:, NSA_HEADS + hh:NSA_HEADS + hh + 1] * os_ref[:, sl]
              + ga[:, 2 * NSA_HEADS + hh:2 * NSA_HEADS + hh + 1] * ow_ref[:, sl])
        oa_ref[:, sl] = oa.astype(BF16)
    x = x_ref[...]
    ya = _dot(oa_ref[...], wua_ref[...])
    yb = _dot(ob_ref[...].astype(BF16), wub_ref[...])
    g_a = jax.nn.sigmoid(_dot(x, wga_ref[...]))
    g_b = jax.nn.sigmoid(_dot(x, wgb_ref[...]))
    y_ref[...] = (g_a * ya + g_b * yb).astype(BF16)


def _merge_up(xb, o_cmp, o_sel, o_win, o_b, misc, wga, wgb, wua, wub, tm, tn):
    m = xb.shape[0]
    aw = NSA_HEADS * HEAD_DIM
    bw = DSA_HEADS * HEAD_DIM
    row = lambda w: pl.BlockSpec((tm, w), lambda j, i: (i, 0))
    wcol = lambda k: pl.BlockSpec((k, tn), lambda j, i: (0, j))
    return pl.pallas_call(
        _merge_up_kernel,
        grid=(D_MODEL // tn, m // tm),
        in_specs=[row(D_MODEL), row(aw), row(aw), row(aw), row(bw), row(MISC_W),
                  wcol(D_MODEL), wcol(D_MODEL), wcol(aw), wcol(bw)],
        out_specs=pl.BlockSpec((tm, tn), lambda j, i: (i, j)),
        out_shape=jax.ShapeDtypeStruct((m, D_MODEL), BF16),
        scratch_shapes=[pltpu.VMEM((tm, aw), BF16)],
        compiler_params=_cparams("parallel", "parallel"),
        name="merge_up",
    )(xb, o_cmp, o_sel, o_win, o_b, misc, wga, wgb, wua, wub)


def _layer_norm(z, g, b):
    mu = jnp.mean(z, axis=-1, keepdims=True)
    zc = z - mu
    var = jnp.mean(zc * zc, axis=-1, keepdims=True)
    return zc * lax.rsqrt(var + LN_EPS) * g + b


def _merge_out_kernel(x_ref, y_ref, wo_ref, g_ref, b_ref, o_ref, ob_ref):
    z = DEEPNORM_ALPHA * x_ref[...] + _dot(y_ref[...], wo_ref[...])
    x1 = _layer_norm(z, g_ref[...], b_ref[...])
    o_ref[...] = x1
    ob_ref[...] = x1.astype(BF16)


def _merge_out(x, y, wo, g, b, tm):
    m = x.shape[0]
    row = lambda: pl.BlockSpec((tm, D_MODEL), lambda i: (i, 0))
    return pl.pallas_call(
        _merge_out_kernel,
        grid=(m // tm,),
        in_specs=[row(), row(), pl.BlockSpec((D_MODEL, D_MODEL), lambda i: (0, 0)),
                  pl.BlockSpec((1, D_MODEL), lambda i: (0, 0)), pl.BlockSpec((1, D_MODEL), lambda i: (0, 0))],
        out_specs=[row(), row()],
        out_shape=[jax.ShapeDtypeStruct((m, D_MODEL), F32), jax.ShapeDtypeStruct((m, D_MODEL), BF16)],
        compiler_params=_cparams("parallel"),
        name="merge_out",
    )(x, y, wo, g, b)


def _router_kernel(x_ref, wr_ref, rb_ref, idx_ref, wt_ref, *, tm):
    epg = N_EXPERTS // N_EXPERT_GROUPS
    logits = lax.dot_general(wr_ref[...], x_ref[...], (((1,), (1,)), ((), ())),
                             preferred_element_type=F32, precision=lax.Precision.HIGHEST)
    scores = jax.nn.sigmoid(logits)
    biased = scores + rb_ref[...]
    sub = lax.broadcasted_iota(jnp.int32, (epg, tm), 0)
    gs_rows = []
    for r in range(N_EXPERT_GROUPS):
        bg = biased[r * epg:(r + 1) * epg, :]
        m1 = jnp.max(bg, axis=0, keepdims=True)
        i1 = jnp.min(jnp.where(bg == m1, sub, epg), axis=0, keepdims=True)
        m2 = jnp.max(jnp.where(sub == i1, -jnp.inf, bg), axis=0, keepdims=True)
        gs_rows.append(m1 + m2)
    gs = jnp.concatenate(gs_rows, axis=0)
    grow = lax.broadcasted_iota(jnp.int32, (N_EXPERT_GROUPS, tm), 0)
    rank = jnp.zeros((N_EXPERT_GROUPS, tm), jnp.int32)
    for k in range(N_EXPERT_GROUPS):
        rk = gs[k:k + 1, :]
        rank = rank + jnp.where((rk > gs) | ((rk == gs) & (k < grow)), 1, 0)
    gkeep = rank < TOPK_GROUPS
    masked = jnp.concatenate(
        [jnp.where(gkeep[r:r + 1, :], biased[r * epg:(r + 1) * epg, :], -jnp.inf) for r in range(N_EXPERT_GROUPS)],
        axis=0)
    erow = lax.broadcasted_iota(jnp.int32, (N_EXPERTS, tm), 0)
    idx_rows, w_rows = [], []
    for _ in range(TOP_K):
        m = jnp.max(masked, axis=0, keepdims=True)
        ix = jnp.min(jnp.where(masked == m, erow, N_EXPERTS), axis=0, keepdims=True)
        hit = erow == ix
        w_rows.append(jnp.sum(jnp.where(hit, scores, 0.0), axis=0, keepdims=True))
        idx_rows.append(ix)
        masked = jnp.where(hit, -jnp.inf, masked)
    wsum = w_rows[0]
    for w in w_rows[1:]:
        wsum = wsum + w
    pad = 8 - TOP_K
    idx_ref[...] = jnp.concatenate(idx_rows + [jnp.zeros((pad, tm), jnp.int32)], axis=0)
    wt_ref[...] = jnp.concatenate([w / wsum * ROUTED_SCALE for w in w_rows] + [jnp.zeros((pad, tm), F32)], axis=0)


def _router(x1, wr_t, rb, tm):
    m = x1.shape[0]
    return pl.pallas_call(
        functools.partial(_router_kernel, tm=tm),
        grid=(m // tm,),
        in_specs=[pl.BlockSpec((tm, D_MODEL), lambda i: (i, 0)),
                  pl.BlockSpec((N_EXPERTS, D_MODEL), lambda i: (0, 0)),
                  pl.BlockSpec((N_EXPERTS, 1), lambda i: (0, 0))],
        out_specs=[pl.BlockSpec((8, tm), lambda i: (0, i)), pl.BlockSpec((8, tm), lambda i: (0, i))],
        out_shape=[jax.ShapeDtypeStruct((8, m), jnp.int32), jax.ShapeDtypeStruct((8, m), F32)],
        compiler_params=_cparams("parallel"),
        name="router",
    )(x1, wr_t, rb)


def _expert_kernel(be_ref, nv_ref, x_ref, rw_ref, wg_ref, wu_ref, wd_ref, o_ref):
    i = pl.program_id(0)

    @pl.when(i < nv_ref[0])
    def _():
        x = x_ref[...]
        h = jax.nn.silu(_dot(x, wg_ref[0])) * _dot(x, wu_ref[0])
        o_ref[...] = _dot(h.astype(BF16), wd_ref[0]) * rw_ref[...]

    @pl.when(i >= nv_ref[0])
    def _():
        o_ref[...] = jnp.zeros_like(o_ref)


def _experts(blk_e, n_valid, xg, row_w, wg, wu, wd, tr):
    n_rows = xg.shape[0]
    n_blk = n_rows // tr
    grid_spec = pltpu.PrefetchScalarGridSpec(
        num_scalar_prefetch=2,
        grid=(n_blk,),
        in_specs=[pl.BlockSpec((tr, D_MODEL), lambda i, be, nv: (i, 0)),
                  pl.BlockSpec((tr, 1), lambda i, be, nv: (i, 0)),
                  pl.BlockSpec((1, D_MODEL, EXPERT_FF), lambda i, be, nv: (be[i], 0, 0)),
                  pl.BlockSpec((1, D_MODEL, EXPERT_FF), lambda i, be, nv: (be[i], 0, 0)),
                  pl.BlockSpec((1, EXPERT_FF, D_MODEL), lambda i, be, nv: (be[i], 0, 0))],
        out_specs=pl.BlockSpec((tr, D_MODEL), lambda i, be, nv: (i, 0)),
    )
    return pl.pallas_call(
        _expert_kernel,
        grid_spec=grid_spec,
        out_shape=jax.ShapeDtypeStruct((n_rows, D_MODEL), F32),
        compiler_params=_cparams("arbitrary"),
        name="experts",
    )(blk_e, n_valid, xg, row_w, wg, wu, wd)


def _ffn_out_kernel(x_ref, r_ref, sg_ref, su_ref, sd_ref, g_ref, b_ref, o_ref):
    x = x_ref[...]
    xb = x.astype(BF16)
    h = jax.nn.silu(_dot(xb, sg_ref[...])) * _dot(xb, su_ref[...])
    f = r_ref[...] + _dot(h.astype(BF16), sd_ref[...])
    o_ref[...] = _layer_norm(DEEPNORM_ALPHA * x + f, g_ref[...], b_ref[...])


def _ffn_out(x1, routed, sg, su, sd, g, b, tm):
    m = x1.shape[0]
    row = lambda: pl.BlockSpec((tm, D_MODEL), lambda i: (i, 0))
    full = lambda s: pl.BlockSpec(s, lambda i: (0, 0))
    return pl.pallas_call(
        _ffn_out_kernel,
        grid=(m // tm,),
        in_specs=[row(), row(), full((D_MODEL, SHARED_FF)), full((D_MODEL, SHARED_FF)), full((SHARED_FF, D_MODEL)),
                  full((1, D_MODEL)), full((1, D_MODEL))],
        out_specs=row(),
        out_shape=jax.ShapeDtypeStruct((m, D_MODEL), F32),
        compiler_params=_cparams("parallel"),
        name="ffn_out",
    )(x1, routed, sg, su, sd, g, b)


def _rel_bucket(dist):
    n = jnp.maximum(dist, 0)
    nf = jnp.maximum(n, 1).astype(F32)
    large = MAX_EXACT + (jnp.log(nf / MAX_EXACT) / math.log(MAX_DISTANCE / MAX_EXACT)
                         * (N_BUCKETS - MAX_EXACT)).astype(jnp.int32)
    return jnp.where(n < MAX_EXACT, n, jnp.minimum(large, N_BUCKETS - 1))


def _toeplitz_bias(tbl, tq, n_tiles):
    i = jnp.arange(tq)[:, None]
    j = jnp.arange(tq)[None, :]
    dist = jnp.arange(n_tiles)[:, None, None] * tq + (i - j)[None]
    bias = jnp.moveaxis(tbl[_rel_bucket(dist)], -1, 0)
    return bias, dist


def _rel_bias(dist, tbl):
    bucket = _rel_bucket(dist)
    if dist.shape[-2] == 1:
        return jnp.moveaxis(tbl[bucket[..., 0, :]], -3, -1)
    gi = jnp.arange(tbl.shape[1])[:, None]
    return jnp.swapaxes(tbl.transpose(1, 0, 2)[gi, bucket], -1, -2)


def _attn_probs(s, dist, valid, tbl):
    s = s.astype(F32) + _rel_bias(dist, tbl).astype(F32)
    s = jnp.where(valid[..., None, :], s, -jnp.inf)
    m = jnp.max(s, axis=-1, keepdims=True)
    m = jnp.where(jnp.isfinite(m), m, 0.0)
    e = jnp.exp(s - m)
    den = e.sum(-1, keepdims=True)
    return e / jnp.where(den > 0, den, 1.0)


def _sparse_attend(q, kg, vg, q_pos, k_pos, tbl):
    s = jnp.einsum('btghd,btgkd->btghk', q, kg) * ATTN_SCALE
    dist = q_pos[:, :, None, None] - k_pos
    p = _attn_probs(s, dist, dist >= 0, tbl)
    return jnp.einsum('btghk,btgkd->btghd', p, vg)


def _gather_past_and_new(pool, page_table, new_rows, pos, *extra):
    past = page_table.shape[1] * PAGE_SIZE
    bi = jnp.arange(pos.shape[0]).reshape((-1,) + (1,) * (pos.ndim - 1))
    page = jnp.minimum(pos // PAGE_SIZE, page_table.shape[1] - 1)
    old = pool[(page_table[bi, page], pos % PAGE_SIZE) + extra]
    new = new_rows[(bi, jnp.clip(pos - past, 0, new_rows.shape[1] - 1)) + extra]
    is_old = (pos < past).reshape(pos.shape + (1,) * (old.ndim - pos.ndim))
    return jnp.where(is_old, old, new)


def _sample_mixer_branches(q_a, kcomp, vcomp, ks, vs, kw, vw, q_b, kb, vb, iq, ik, iw,
                           cache_slc_k, cache_slc_v, win_k, win_v, cache_dsa_k, cache_dsa_v, cache_idx_k,
                           page_table, tbl_a, tbl_b):
    b, t = q_a.shape[:2]
    past = page_table.shape[1] * PAGE_SIZE
    total = past + t
    q_pos = past + jnp.arange(t)[None]
    nb = kcomp.shape[1]
    s = jnp.einsum('btghd,bngd->btghn', q_a, kcomp) * ATTN_SCALE
    blk_end = jnp.arange(nb) * CMP_BLOCK + (CMP_BLOCK - 1)
    dist = (q_pos[:, :, None] - blk_end)[:, :, None, :]
    p = _attn_probs(s, dist, dist >= 0, tbl_a)
    o_cmp = jnp.einsum('btghn,bngd->btghd', p, vcomp)
    imp = p.sum(axis=3)
    n_blocks = -(-total // SEL_BLOCK)
    imp = jnp.pad(imp, ((0, 0), (0, 0), (0, 0), (0, n_blocks - nb)))
    j = jnp.arange(n_blocks)
    cur = (q_pos // SEL_BLOCK)[:, :, None, None]
    forced = (j == 0) | (j == cur) | (j == cur - 1)
    score = jnp.where(forced, jnp.inf, jnp.where(j <= cur, imp, -jnp.inf))
    idx = lax.top_k(score, min(N_SEL_BLOCKS, n_blocks))[1]
    kpos = (idx[..., None] * SEL_BLOCK + jnp.arange(SEL_BLOCK)).reshape(idx.shape[:-1] + (-1,))
    gi = jnp.arange(NSA_KV_GROUPS)[None, None, :, None]
    o_sel = _sparse_attend(q_a, _gather_past_and_new(cache_slc_k, page_table, ks, kpos, gi),
                           _gather_past_and_new(cache_slc_v, page_table, vs, kpos, gi), q_pos, kpos, tbl_a)
    wk = jnp.concatenate([win_k, kw], axis=1)
    wv = jnp.concatenate([win_v, vw], axis=1)
    w_past = win_k.shape[1]
    k_pos = past - w_past + jnp.arange(w_past + t)
    s = jnp.einsum('btghd,bkgd->btghk', q_a, wk) * ATTN_SCALE
    dist = q_pos[:, :, None, None] - k_pos
    valid = (dist >= 0) & (dist < WINDOW) & (k_pos >= 0)
    p = _attn_probs(s, dist, valid, tbl_a)
    o_win = jnp.einsum('btghk,bkgd->btghd', p, wv)
    ik_full = jnp.concatenate([cache_idx_k[page_table].reshape((b, past, IDX_DIM)), ik], axis=1)
    logits = jax.nn.relu(jnp.einsum('bthd,bsd->bths', iq, ik_full).astype(F32) * IDX_DIM ** -0.5)
    sc = jnp.einsum('bths,bth->bts', logits, iw.astype(F32)) * IDX_HEADS ** -0.5
    sc = jnp.where(jnp.arange(total) <= q_pos[:, :, None], sc, -jnp.inf)
    pos = lax.top_k(sc, min(DSA_TOPK, total // 4))[1]
    kg = _gather_past_and_new(cache_dsa_k, page_table, kb, pos).swapaxes(2, 3)
    vg = _gather_past_and_new(cache_dsa_v, page_table, vb, pos).swapaxes(2, 3)
    o_b = _sparse_attend(q_b, kg, vg, q_pos, pos[:, :, None, :], tbl_b)
    n_win = min(WINDOW, total)
    return o_cmp, o_sel, o_win, o_b, wk[:, -n_win:], wv[:, -n_win:]


def _split_w_in(w_in):
    points = np.cumsum(IN_COLS)[:-1].tolist()
    q_a, kv_a, g_a, q_b, kv_b, iq, ik, iw, g_m = jnp.split(w_in, points, axis=-1)
    pad = jnp.zeros((D_MODEL, MISC_W - IDX_DIM - IDX_HEADS - 3 * NSA_HEADS), w_in.dtype)
    w_kv = jnp.concatenate([kv_a, kv_b, ik, iw, g_a, pad], axis=-1).astype(BF16)
    w_q = jnp.concatenate([q_a, q_b, iq], axis=-1).astype(BF16)
    return w_kv, w_q, g_m[:, :D_MODEL].astype(BF16), g_m[:, D_MODEL:].astype(BF16)


KV_WIDTHS = (KV_W,) * 8 + (MISC_W,)
Q_WIDTHS = (NSA_HEADS * HEAD_DIM, DSA_HEADS * HEAD_DIM, IDX_HEADS * IDX_DIM)


def _row_tile(m, cap):
    tm = math.gcd(m, cap)
    assert tm % 8 == 0 or tm == m
    return tm


def kernel(x_prompt, x_sample, cache_cmp_k, cache_cmp_v, cache_slc_k, cache_slc_v, state_win_k, state_win_v,
           cache_dsa_k, cache_dsa_v, cache_idx_k, page_table, rel_bias_table, w_in, cmp_pe, cmp_w1, cmp_w2,
           w_up_a, w_up_b, w_o, ln1_g, ln1_b, w_router, router_bias, moe_w_gate, moe_w_up, moe_w_down,
           sh_w_gate, sh_w_up, sh_w_down, ln2_g, ln2_b):
    assert w_in.shape[0] == DEPTH == 1
    b, t, _ = x_prompt.shape
    db, dt, _ = x_sample.shape
    assert dt == 1
    n_pool = cache_cmp_k.shape[1]
    n_pages = page_table.shape[1]
    tq = Q_TILE
    assert t % tq == 0 and t % CMP_BLOCK == 0
    n_p, n_s = b * t, db * dt

    tbl_a = rel_bias_table[:, :NSA_HEADS]
    tbl_b = rel_bias_table[:, NSA_HEADS:]
    w_kv, w_q, w_ga, w_gb = _split_w_in(w_in[0])
    w_ua, w_ub, w_ob = w_up_a[0].astype(BF16), w_up_b[0].astype(BF16), w_o[0].astype(BF16)
    cw = [_compress_weights(cmp_pe[0, i], cmp_w1[0, i], cmp_w2[0, i]) for i in range(2)]

    xp = x_prompt.reshape(n_p, D_MODEL)
    xs = x_sample.reshape(n_s, D_MODEL)
    xpb, xsb = xp.astype(BF16), xs.astype(BF16)
    tm_p = _row_tile(n_p, 512)
    kc, vc, ks, vs, kw, vw, kb, vb, misc = _project(xpb, w_kv, KV_WIDTHS, tm_p)
    qa, qb, iq = _project(xpb, w_q, Q_WIDTHS, tm_p)
    s_kc, s_vc, s_ks, s_vs, s_kw, s_vw, s_kb, s_vb, s_misc = _project(xsb, w_kv, KV_WIDTHS, n_s)
    s_qa, s_qb, s_iq = _project(xsb, w_q, Q_WIDTHS, n_s)

    nb = t // CMP_BLOCK
    r3 = lambda a: a.reshape(b, t, -1)
    kcomp = _compress(kc.reshape(b * nb, CMP_BLOCK, KV_W), *cw[0]).reshape(b, nb, KV_W)
    vcomp = _compress(vc.reshape(b * nb, CMP_BLOCK, KV_W), *cw[1]).reshape(b, nb, KV_W)
    cdist = jnp.arange(t)[:, None] - (jnp.arange(nb) * CMP_BLOCK + CMP_BLOCK - 1)[None, :]
    cbias = jnp.where(cdist >= 0, jnp.moveaxis(tbl_a[_rel_bucket(cdist)], -1, 0), NEG)
    o_cmp, selmask = _cmp_select(r3(qa), kcomp, vcomp, cbias, tq)

    assert MAX_DISTANCE <= tq
    bias2_a, dist2 = _toeplitz_bias(tbl_a, tq, 2)
    bias2_b, _ = _toeplitz_bias(tbl_b, tq, 2)
    c_a, c_b = tbl_a[N_BUCKETS - 1], tbl_b[N_BUCKETS - 1]
    d_a = jnp.where(dist2 >= 0, bias2_a - c_a[:, None, None, None], NEG)
    d_b = jnp.where(dist2 >= 0, bias2_b - c_b[:, None, None, None], NEG)
    expand = (jnp.arange(t)[None, :] // SEL_BLOCK == jnp.arange(nb)[:, None]).astype(BF16)
    o_sel = _dense_attn(r3(qa), r3(ks), r3(vs), selmask, expand, c_a, d_a, tq, "sel")

    n_wchunks = -(-(WINDOW - 1) // tq) + 1
    bias_w, dist_w = _toeplitz_bias(tbl_a, tq, n_wchunks)
    wtiles = jnp.where((dist_w >= 0) & (dist_w < WINDOW), bias_w, NEG)
    o_win = _window_attn(r3(qa), r3(kw), r3(vw), wtiles, tq)

    n_keep = min(DSA_TOPK, t // 4)
    keepmask = _index_select(r3(iq), r3(misc), tq, n_keep)
    o_b = _dense_attn(r3(qb), r3(kb), r3(vb), keepmask, jnp.zeros((8, 128), BF16), c_b, d_b, tq, "dsa")

    tm_m = _row_tile(n_p, 256)
    y_p = _merge_up(xpb, o_cmp.reshape(n_p, -1), o_sel.reshape(n_p, -1), o_win.reshape(n_p, -1),
                    o_b.reshape(n_p, -1), misc, w_ga, w_gb, w_ua, w_ub, tm_m, 1024)
    x1p, x1pb = _merge_out(xp, y_p, w_ob, ln1_g, ln1_b, tm_m)

    past = n_pages * PAGE_SIZE
    halves = PAGE_SIZE // CMP_BLOCK

    def comp_pool(cache, w):
        c = _compress(cache.reshape(n_pool * halves, CMP_BLOCK, KV_W), *w)
        return c.reshape(n_pool, halves, NSA_KV_GROUPS, HEAD_DIM)[page_table].reshape(
            db, n_pages * halves, NSA_KV_GROUPS, HEAD_DIM)

    s_kcomp = comp_pool(cache_cmp_k[0], cw[0])
    s_vcomp = comp_pool(cache_cmp_v[0], cw[1])
    g4 = lambda a: a.reshape(db, dt, NSA_KV_GROUPS, HEAD_DIM)
    so_cmp, so_sel, so_win, so_b, s_wk, s_wv = _sample_mixer_branches(
        s_qa.reshape(db, dt, NSA_KV_GROUPS, NSA_HPG, HEAD_DIM), s_kcomp, s_vcomp, g4(s_ks), g4(s_vs),
        g4(s_kw), g4(s_vw), s_qb.reshape(db, dt, DSA_KV_GROUPS, DSA_HPG, HEAD_DIM), g4(s_kb), g4(s_vb),
        s_iq.reshape(db, dt, IDX_HEADS, IDX_DIM), s_misc[:, :IDX_DIM].reshape(db, dt, IDX_DIM),
        s_misc[:, MISC_IW:MISC_IW + IDX_HEADS].reshape(db, dt, IDX_HEADS),
        cache_slc_k[0], cache_slc_v[0], state_win_k[0], state_win_v[0], cache_dsa_k[0], cache_dsa_v[0],
        cache_idx_k[0], page_table,
        tbl_a.reshape(N_BUCKETS, NSA_KV_GROUPS, NSA_HPG), tbl_b.reshape(N_BUCKETS, DSA_KV_GROUPS, DSA_HPG))
    y_s = _merge_up(xsb, so_cmp.reshape(n_s, -1), so_sel.reshape(n_s, -1), so_win.reshape(n_s, -1),
                    so_b.reshape(n_s, -1), s_misc, w_ga, w_gb, w_ua, w_ub, n_s, 1024)
    x1s, x1sb = _merge_out(xs, y_s, w_ob, ln1_g, ln1_b, n_s)

    wr_t = w_router[0].T
    rb = router_bias[0].reshape(N_EXPERTS, 1)
    eidx_p, wts_p = _router(x1p, wr_t, rb, tm_m)
    eidx_s, wts_s = _router(x1s, wr_t, rb, n_s)
    n_tok = n_p + n_s
    eidx = jnp.concatenate([eidx_p[:TOP_K], eidx_s[:TOP_K]], axis=1).T
    wts = jnp.concatenate([wts_p[:TOP_K], wts_s[:TOP_K]], axis=1).T
    tr = EXPERT_ROWS
    n_asg = n_tok * TOP_K
    flat_e = eidx.reshape(-1)
    order = jnp.argsort(flat_e, stable=True)
    se = flat_e[order]
    counts = jnp.bincount(flat_e, length=N_EXPERTS)
    padded = (counts + tr - 1) // tr * tr
    start = jnp.cumsum(counts) - counts
    pend = jnp.cumsum(padded)
    dest_sorted = (pend - padded)[se] + jnp.arange(n_asg) - start[se]
    n_rows = -(-n_asg // tr) * tr + N_EXPERTS * tr
    n_blk = n_rows // tr
    row_tok = jnp.full((n_rows,), n_tok, jnp.int32).at[dest_sorted].set((order // TOP_K).astype(jnp.int32))
    row_w = jnp.zeros((n_rows,), F32).at[dest_sorted].set(wts.reshape(-1)[order])
    blk_e = jnp.minimum(jnp.searchsorted(pend, jnp.arange(n_blk) * tr, side='right'), N_EXPERTS - 1).astype(jnp.int32)
    n_valid = (pend[-1] // tr).astype(jnp.int32).reshape(1)
    dest = jnp.zeros((n_asg,), jnp.int32).at[order].set(dest_sorted.astype(jnp.int32)).reshape(n_tok, TOP_K)
    x1b_all = jnp.concatenate([x1pb, x1sb, jnp.zeros((1, D_MODEL), BF16)], axis=0)
    xg = x1b_all[row_tok]
    out_rows = _experts(blk_e, n_valid, xg, row_w.reshape(n_rows, 1), moe_w_gate[0].astype(BF16),
                        moe_w_up[0].astype(BF16), moe_w_down[0].astype(BF16), tr)
    routed = out_rows[dest].sum(axis=1)

    sg, su, sd = sh_w_gate[0].astype(BF16), sh_w_up[0].astype(BF16), sh_w_down[0].astype(BF16)
    y_prompt = _ffn_out(x1p, routed[:n_p], sg, su, sd, ln2_g, ln2_b, tm_m).reshape(b, t, D_MODEL)
    y_sample = _ffn_out(x1s, routed[n_p:], sg, su, sd, ln2_g, ln2_b, n_s).reshape(db, dt, D_MODEL)

    n_win = min(WINDOW, t)
    st = lambda a: a.reshape(1, b, t, NSA_KV_GROUPS, HEAD_DIM)
    ss = lambda a: a.reshape(1, db, dt, NSA_KV_GROUPS, HEAD_DIM)
    return (y_prompt, y_sample,
            st(kc), st(vc), st(ks), st(vs), st(kw)[:, :, -n_win:], st(vw)[:, :, -n_win:], st(kb), st(vb),
            misc[:, :IDX_DIM].reshape(1, b, t, IDX_DIM),
            ss(s_kc), ss(s_vc), ss(s_ks), ss(s_vs), s_wk[None], s_wv[None], ss(s_kb), ss(s_vb),
            s_misc[:, :IDX_DIM].reshape(1, db, dt, IDX_DIM))
```

```python
import functools
import math

import jax
import jax.numpy as jnp
import numpy as np
from jax import lax
from jax.experimental import pallas as pl
from jax.experimental.pallas import tpu as pltpu

D_MODEL = 2048
PAGE_SIZE = 128
HEAD_DIM = 128
NSA_HEADS = 8
NSA_KV_GROUPS = 2
NSA_HPG = NSA_HEADS // NSA_KV_GROUPS
CMP_BLOCK = 64
CMP_HIDDEN = 128
SEL_BLOCK = 64
N_SEL_BLOCKS = 16
WINDOW = 512
DSA_HEADS = 8
DSA_KV_GROUPS = 2
DSA_HPG = DSA_HEADS // DSA_KV_GROUPS
IDX_HEADS = 8
IDX_DIM = 64
DSA_TOPK = 256
N_BUCKETS = 32
MAX_EXACT = 16
MAX_DISTANCE = 128
N_EXPERTS = 64
EXPERT_FF = 512
SHARED_FF = 512
TOP_K = 6
N_EXPERT_GROUPS = 8
TOPK_GROUPS = 4
ROUTED_SCALE = 2.5
LN_EPS = 1e-5
ATTN_SCALE = HEAD_DIM ** -0.5
DEPTH = 1
DEEPNORM_ALPHA = (2 * DEPTH) ** 0.25
IN_COLS = (NSA_HEADS * HEAD_DIM, 6 * NSA_KV_GROUPS * HEAD_DIM, 3 * NSA_HEADS,
           DSA_HEADS * HEAD_DIM, 2 * DSA_KV_GROUPS * HEAD_DIM,
           IDX_HEADS * IDX_DIM, IDX_DIM, IDX_HEADS, 2 * D_MODEL)

KV_W = NSA_KV_GROUPS * HEAD_DIM
MISC_W = 128
MISC_IW = IDX_DIM
MISC_GA = IDX_DIM + IDX_HEADS
NEG = -1e30
Q_TILE = 256
EXPERT_ROWS = 256
VMEM_LIMIT = 56 * 1024 * 1024

BF16 = jnp.bfloat16
F32 = jnp.float32


def _cparams(*sem):
    return pltpu.CompilerParams(dimension_semantics=sem, vmem_limit_bytes=VMEM_LIMIT)


def _dot(a, b):
    return jnp.dot(a, b, preferred_element_type=F32)


def _dot_nt(a, b):
    return lax.dot_general(a, b, (((1,), (1,)), ((), ())), preferred_element_type=F32)


def _proj_kernel(x_ref, w_ref, *o_refs, widths):
    acc = _dot(x_ref[...], w_ref[...])
    off = 0
    for o_ref, wd in zip(o_refs, widths):
        o_ref[...] = acc[:, off:off + wd]
        off += wd


def _project(x, w, widths, tm):
    m, k = x.shape
    n = w.shape[1]
    assert sum(widths) == n and m % tm == 0
    return pl.pallas_call(
        functools.partial(_proj_kernel, widths=widths),
        grid=(m // tm,),
        in_specs=[pl.BlockSpec((tm, k), lambda i: (i, 0)),
                  pl.BlockSpec((k, n), lambda i: (0, 0))],
        out_specs=[pl.BlockSpec((tm, wd), lambda i: (i, 0)) for wd in widths],
        out_shape=[jax.ShapeDtypeStruct((m, wd), F32) for wd in widths],
        compiler_params=_cparams("parallel"),
        name="project",
    )(x, w)


CMP_JCHUNK = 8


def _compress_kernel(x_ref, pe_ref, w1_ref, w2_ref, o_ref, acc_ref):
    jc = pl.program_id(1)

    @pl.when(jc == 0)
    def _():
        acc_ref[...] = jnp.zeros_like(acc_ref)

    acc = acc_ref[...]
    for jj in range(CMP_JCHUNK):
        lhs = (x_ref[:, jj, :] + pe_ref[jj:jj + 1, :]).astype(BF16)
        acc = acc + _dot(lhs, w1_ref[jj])
    acc_ref[...] = acc

    @pl.when(jc == pl.num_programs(1) - 1)
    def _():
        h = jax.nn.gelu(acc_ref[...])
        o_ref[...] = _dot(h.astype(BF16), w2_ref[...])


def _compress(rows, pe2, w1big, w2big):
    r = rows.shape[0]
    tr = math.gcd(r, 1024)
    assert tr % 8 == 0
    return pl.pallas_call(
        _compress_kernel,
        grid=(r // tr, CMP_BLOCK // CMP_JCHUNK),
        in_specs=[pl.BlockSpec((tr, CMP_JCHUNK, KV_W), lambda i, j: (i, j, 0)),
                  pl.BlockSpec((CMP_JCHUNK, KV_W), lambda i, j: (j, 0)),
                  pl.BlockSpec((CMP_JCHUNK, KV_W, KV_W), lambda i, j: (j, 0, 0)),
                  pl.BlockSpec((KV_W, KV_W), lambda i, j: (0, 0))],
        out_specs=pl.BlockSpec((tr, KV_W), lambda i, j: (i, 0)),
        out_shape=jax.ShapeDtypeStruct((r, KV_W), F32),
        scratch_shapes=[pltpu.VMEM((tr, KV_W), F32)],
        compiler_params=_cparams("parallel", "arbitrary"),
        name="compress",
    )(rows, pe2, w1big, w2big)


def _rows_at(x_ref, r):
    n, rows, w = x_ref.shape
    return x_ref.reshape(n * rows, w)[pl.ds(r, n, stride=rows), :]


def _compress_pool_kernel(x_ref, pe_ref, w1_ref, w2_ref, o_ref, acc_ref, *, tr):
    jc = pl.program_id(1)

    @pl.when(jc == 0)
    def _():
        acc_ref[...] = jnp.zeros_like(acc_ref)

    for g in range(NSA_KV_GROUPS):
        acc = acc_ref[g]
        for jp in range(CMP_JCHUNK // 2):
            parts = []
            for u in range(2):
                jj = 2 * jp + u
                xs = _rows_at(x_ref, 2 * jj + g) + pe_ref[jj:jj + 1, :]
                parts.append(xs.astype(BF16))
            acc = acc + _dot(jnp.concatenate(parts, axis=-1), w1_ref[jp])
        acc_ref[g] = acc

    @pl.when(jc == pl.num_programs(1) - 1)
    def _():
        for g in range(NSA_KV_GROUPS):
            h = jax.nn.gelu(acc_ref[g])
            o_ref[:, g * HEAD_DIM:(g + 1) * HEAD_DIM] = _dot(h.astype(BF16), w2_ref[...])


def _compress_pool(cache, pe, w1, w2):
    n_pool = cache.shape[0]
    r = n_pool * (PAGE_SIZE // CMP_BLOCK)
    rows = cache.reshape(r, CMP_BLOCK * NSA_KV_GROUPS, HEAD_DIM)
    tr = math.gcd(r, 1024)
    assert tr % 8 == 0
    rows_per = 2 * CMP_JCHUNK
    w1p = w1.reshape(CMP_BLOCK // 2, 2 * HEAD_DIM, CMP_HIDDEN).astype(BF16)
    return pl.pallas_call(
        functools.partial(_compress_pool_kernel, tr=tr),
        grid=(r // tr, CMP_BLOCK // CMP_JCHUNK),
        in_specs=[pl.BlockSpec((tr, rows_per, HEAD_DIM), lambda i, j: (i, j, 0)),
                  pl.BlockSpec((CMP_JCHUNK, HEAD_DIM), lambda i, j: (j, 0)),
                  pl.BlockSpec((CMP_JCHUNK // 2, 2 * HEAD_DIM, CMP_HIDDEN), lambda i, j: (j, 0, 0)),
                  pl.BlockSpec((CMP_HIDDEN, HEAD_DIM), lambda i, j: (0, 0))],
        out_specs=pl.BlockSpec((tr, KV_W), lambda i, j: (i, 0)),
        out_shape=jax.ShapeDtypeStruct((r, KV_W), F32),
        scratch_shapes=[pltpu.VMEM((NSA_KV_GROUPS, tr, CMP_HIDDEN), F32)],
        compiler_params=_cparams("parallel", "arbitrary"),
        name="compress_pool",
    )(rows, pe, w1p, w2.astype(BF16))


def _compress_weights(pe, w1, w2):
    pe2 = jnp.concatenate([pe, pe], axis=-1)
    z1 = jnp.zeros_like(w1)
    w1big = jnp.concatenate([jnp.concatenate([w1, z1], axis=2), jnp.concatenate([z1, w1], axis=2)], axis=1)
    z2 = jnp.zeros_like(w2)
    w2big = jnp.concatenate([jnp.concatenate([w2, z2], axis=1), jnp.concatenate([z2, w2], axis=1)], axis=0)
    return pe2, w1big.astype(BF16), w2big.astype(BF16)


def _cmp_select_kernel(q_ref, kc_ref, vc_ref, cb_ref, o_ref, sel_ref, *, tq, nb, n_sel):
    qi = pl.program_id(1)
    t = qi * tq + lax.broadcasted_iota(jnp.int32, (tq, nb), 0)
    j = lax.broadcasted_iota(jnp.int32, (tq, nb), 1)
    cur = t // SEL_BLOCK
    for g in range(NSA_KV_GROUPS):
        kc = kc_ref[0, :, g * HEAD_DIM:(g + 1) * HEAD_DIM].astype(BF16)
        vc = vc_ref[0, :, g * HEAD_DIM:(g + 1) * HEAD_DIM].astype(BF16)
        imp = jnp.zeros((tq, nb), F32)
        for h in range(NSA_HPG):
            hh = g * NSA_HPG + h
            q = q_ref[0, :, hh * HEAD_DIM:(hh + 1) * HEAD_DIM].astype(BF16)
            cb = cb_ref[hh]
            valid = cb > 0.5 * NEG
            s = _dot_nt(q, kc) * ATTN_SCALE + cb
            m = jnp.max(s, axis=-1, keepdims=True)
            e = jnp.where(valid, jnp.exp(s - m), 0.0)
            den = jnp.sum(e, axis=-1, keepdims=True)
            p = e / jnp.where(den > 0, den, 1.0)
            o_ref[0, :, hh * HEAD_DIM:(hh + 1) * HEAD_DIM] = _dot(p.astype(BF16), vc)
            imp = imp + p
        forced = (j == 0) | (j == cur) | (j == cur - 1)
        score = jnp.where(forced, jnp.inf, jnp.where(j <= cur, imp, -jnp.inf))
        rank = jnp.zeros((tq, nb), jnp.int32)
        for k in range(nb):
            col = score[:, k:k + 1]
            ahead = (col > score) | ((col == score) & (k < j))
            rank = rank + jnp.where(ahead, 1, 0)
        sel_ref[0, g] = jnp.where(rank < n_sel, 1.0, 0.0).astype(F32)


def _cmp_select(qa, kcomp, vcomp, cbias, tq):
    b, t, _ = qa.shape
    nb = kcomp.shape[1]
    n_sel = min(N_SEL_BLOCKS, nb)
    return pl.pallas_call(
        functools.partial(_cmp_select_kernel, tq=tq, nb=nb, n_sel=n_sel),
        grid=(b, t // tq),
        in_specs=[pl.BlockSpec((1, tq, NSA_HEADS * HEAD_DIM), lambda bi, qi: (bi, qi, 0)),
                  pl.BlockSpec((1, nb, KV_W), lambda bi, qi: (bi, 0, 0)),
                  pl.BlockSpec((1, nb, KV_W), lambda bi, qi: (bi, 0, 0)),
                  pl.BlockSpec((NSA_HEADS, tq, nb), lambda bi, qi: (0, qi, 0))],
        out_specs=[pl.BlockSpec((1, tq, NSA_HEADS * HEAD_DIM), lambda bi, qi: (bi, qi, 0)),
                   pl.BlockSpec((1, NSA_KV_GROUPS, tq, nb), lambda bi, qi: (bi, 0, qi, 0))],
        out_shape=[jax.ShapeDtypeStruct((b, t, NSA_HEADS * HEAD_DIM), F32),
                   jax.ShapeDtypeStruct((b, NSA_KV_GROUPS, t, nb), F32)],
        compiler_params=_cparams("parallel", "parallel"),
        name="cmp_select",
    )(qa, kcomp, vcomp, cbias)


def _dense_attn_kernel(c_ref, q_ref, k_ref, v_ref, mask_ref, expand_ref, d_ref, o_ref, s_ref,
                       *, tq, s_len, n_heads, hpg, mode):
    qi = pl.program_id(1)
    t = qi * tq + lax.broadcasted_iota(jnp.int32, (tq, s_len), 0)
    col = lax.broadcasted_iota(jnp.int32, (tq, s_len), 1)
    causal = col <= t
    diag = pl.multiple_of(qi * tq, tq)
    prev = pl.multiple_of(jnp.maximum(qi - 1, 0) * tq, tq)
    keep = None
    for hh in range(n_heads):
        g = hh // hpg
        if hh % hpg == 0:
            if mode == "sel":
                keep = (_dot(mask_ref[0, g].astype(BF16), expand_ref[...]) > 0.5) & causal
            else:
                if g == 0:
                    keep = (mask_ref[0] > 0.5) & causal
            k = k_ref[0, :, g * HEAD_DIM:(g + 1) * HEAD_DIM].astype(BF16)
            v = v_ref[0, :, g * HEAD_DIM:(g + 1) * HEAD_DIM].astype(BF16)
        q = q_ref[0, :, hh * HEAD_DIM:(hh + 1) * HEAD_DIM].astype(BF16)
        s_ref[...] = _dot_nt(q, k) * ATTN_SCALE + c_ref[hh]
        s_ref[:, pl.ds(diag, tq)] += d_ref[hh, 0]

        @pl.when(qi > 0)
        def _():
            s_ref[:, pl.ds(prev, tq)] += d_ref[hh, 1]

        s = jnp.where(keep, s_ref[...], NEG)
        m = jnp.max(s, axis=-1, keepdims=True)
        e = jnp.where(keep, jnp.exp(s - m), 0.0)
        den = jnp.sum(e, axis=-1, keepdims=True)
        o = _dot(e.astype(BF16), v)
        o_ref[0, :, hh * HEAD_DIM:(hh + 1) * HEAD_DIM] = o / jnp.where(den > 0, den, 1.0)


def _dense_attn(q, k, v, mask, expand, cvec, dtiles, tq, mode):
    b, t, qw = q.shape
    n_heads = qw // HEAD_DIM
    hpg = n_heads // (k.shape[2] // HEAD_DIM)
    if mode == "sel":
        nb = mask.shape[-1]
        mask_spec = pl.BlockSpec((1, mask.shape[1], tq, nb), lambda bi, qi: (bi, 0, qi, 0))
    else:
        mask_spec = pl.BlockSpec((1, tq, t), lambda bi, qi: (bi, qi, 0))
    return pl.pallas_call(
        functools.partial(_dense_attn_kernel, tq=tq, s_len=t, n_heads=n_heads, hpg=hpg, mode=mode),
        grid=(b, t // tq),
        in_specs=[pl.BlockSpec(memory_space=pltpu.SMEM),
                  pl.BlockSpec((1, tq, qw), lambda bi, qi: (bi, qi, 0)),
                  pl.BlockSpec((1, t, k.shape[2]), lambda bi, qi: (bi, 0, 0)),
                  pl.BlockSpec((1, t, v.shape[2]), lambda bi, qi: (bi, 0, 0)),
                  mask_spec,
                  pl.BlockSpec(expand.shape, lambda bi, qi: (0, 0)),
                  pl.BlockSpec(dtiles.shape, lambda bi, qi: (0, 0, 0, 0))],
        out_specs=pl.BlockSpec((1, tq, qw), lambda bi, qi: (bi, qi, 0)),
        out_shape=jax.ShapeDtypeStruct((b, t, qw), F32),
        scratch_shapes=[pltpu.VMEM((tq, t), F32)],
        compiler_params=_cparams("parallel", "parallel"),
        name="dense_attn_" + mode,
    )(cvec, q, k, v, mask, expand, dtiles)


def _window_attn_kernel(q_ref, k_ref, v_ref, w_ref, o_ref, *, tq, n_chunks, n_heads, hpg):
    qi = pl.program_id(1)
    starts, pens = [], []
    for r in range(n_chunks):
        cj = qi - (n_chunks - 1) + r
        starts.append(pl.multiple_of(jnp.maximum(cj, 0) * tq, tq))
        pens.append(jnp.where(cj < 0, NEG, 0.0).astype(F32))
    for hh in range(n_heads):
        g = hh // hpg
        q = q_ref[0, :, hh * HEAD_DIM:(hh + 1) * HEAD_DIM].astype(BF16)
        ss = []
        for r in range(n_chunks):
            k = k_ref[0, pl.ds(starts[r], tq), g * HEAD_DIM:(g + 1) * HEAD_DIM].astype(BF16)
            ss.append(_dot_nt(q, k) * ATTN_SCALE + (w_ref[hh, n_chunks - 1 - r] + pens[r]))
        m = ss[0].max(axis=-1, keepdims=True)
        for r in range(1, n_chunks):
            m = jnp.maximum(m, ss[r].max(axis=-1, keepdims=True))
        den = jnp.zeros((tq, 1), F32)
        o = jnp.zeros((tq, HEAD_DIM), F32)
        for r in range(n_chunks):
            e = jnp.where(ss[r] > 0.5 * NEG, jnp.exp(ss[r] - m), 0.0)
            den = den + e.sum(axis=-1, keepdims=True)
            v = v_ref[0, pl.ds(starts[r], tq), g * HEAD_DIM:(g + 1) * HEAD_DIM].astype(BF16)
            o = o + _dot(e.astype(BF16), v)
        o_ref[0, :, hh * HEAD_DIM:(hh + 1) * HEAD_DIM] = o / jnp.where(den > 0, den, 1.0)


def _window_attn(q, k, v, wtiles, tq):
    b, t, qw = q.shape
    n_heads = qw // HEAD_DIM
    hpg = n_heads // (k.shape[2] // HEAD_DIM)
    n_chunks = wtiles.shape[1]
    return pl.pallas_call(
        functools.partial(_window_attn_kernel, tq=tq, n_chunks=n_chunks, n_heads=n_heads, hpg=hpg),
        grid=(b, t // tq),
        in_specs=[pl.BlockSpec((1, tq, qw), lambda bi, qi: (bi, qi, 0)),
                  pl.BlockSpec((1, t, k.shape[2]), lambda bi, qi: (bi, 0, 0)),
                  pl.BlockSpec((1, t, v.shape[2]), lambda bi, qi: (bi, 0, 0)),
                  pl.BlockSpec(wtiles.shape, lambda bi, qi: (0, 0, 0, 0))],
        out_specs=pl.BlockSpec((1, tq, qw), lambda bi, qi: (bi, qi, 0)),
        out_shape=jax.ShapeDtypeStruct((b, t, qw), F32),
        compiler_params=_cparams("parallel", "parallel"),
        name="window_attn",
    )(q, k, v, wtiles)


INT_MIN = -2 ** 31


def _topk_mask(key_ref, n_keep, tq, s_len):
    def body(i, thr_u):
        cand_u = thr_u | jnp.left_shift(jnp.int32(1), 31 - i)
        cnt = jnp.sum(jnp.where(key_ref[...] >= (cand_u ^ INT_MIN), 1, 0), axis=-1, keepdims=True)
        return jnp.where(cnt >= n_keep, cand_u, thr_u)

    thr_u = lax.fori_loop(0, 32, body, jnp.zeros((tq, 1), jnp.int32))
    thr = thr_u ^ INT_MIN
    return thr


def _index_select_kernel(iq_ref, mq_ref, mk_ref, o_ref, key_ref, *, tq, s_len, n_keep):
    qi = pl.program_id(1)
    t = qi * tq + lax.broadcasted_iota(jnp.int32, (tq, s_len), 0)
    col = lax.broadcasted_iota(jnp.int32, (tq, s_len), 1)
    ik = mk_ref[0, :, 0:IDX_DIM].astype(BF16)
    score = jnp.zeros((tq, s_len), F32)
    for h in range(IDX_HEADS):
        iq = iq_ref[0, :, h * IDX_DIM:(h + 1) * IDX_DIM].astype(BF16)
        lg = jnp.maximum(_dot_nt(iq, ik) * IDX_DIM ** -0.5, 0.0)
        score = score + lg * mq_ref[0, :, MISC_IW + h:MISC_IW + h + 1]
    score = score * IDX_HEADS ** -0.5 + 0.0
    score = jnp.where(col <= t, score, -jnp.inf)
    bits = pltpu.bitcast(score, jnp.int32)
    key_ref[...] = jnp.where(bits < 0, bits ^ 0x7FFFFFFF, bits)
    thr = _topk_mask(key_ref, n_keep, tq, s_len)
    key = key_ref[...]
    gt = key > thr
    eq = key == thr
    need = n_keep - jnp.sum(jnp.where(gt, 1, 0), axis=-1, keepdims=True)
    n_eq = jnp.sum(jnp.where(eq, 1, 0), axis=-1, keepdims=True)
    o_ref[0] = jnp.where(gt | eq, 1.0, 0.0).astype(o_ref.dtype)

    @pl.when(jnp.max(n_eq - need) > 0)
    def _():
        r_i = lax.broadcasted_iota(jnp.int32, (128, 128), 0)
        c_i = lax.broadcasted_iota(jnp.int32, (128, 128), 1)
        tri = jnp.where(r_i <= c_i, 1.0, 0.0).astype(BF16)
        before = jnp.zeros((tq, 1), F32)
        need_f = need.astype(F32)
        for c in range(s_len // 128):
            sl = slice(c * 128, (c + 1) * 128)
            eq_c = eq[:, sl]
            eq_f = jnp.where(eq_c, 1.0, 0.0)
            pref = _dot(eq_f.astype(BF16), tri) + before
            keep = gt[:, sl] | (eq_c & (pref <= need_f))
            o_ref[0, :, sl] = jnp.where(keep, 1.0, 0.0).astype(o_ref.dtype)
            before = before + jnp.sum(eq_f, axis=-1, keepdims=True)


def _index_select(iq, misc, tq, n_keep):
    b, t, _ = iq.shape
    return pl.pallas_call(
        functools.partial(_index_select_kernel, tq=tq, s_len=t, n_keep=n_keep),
        grid=(b, t // tq),
        in_specs=[pl.BlockSpec((1, tq, IDX_HEADS * IDX_DIM), lambda bi, qi: (bi, qi, 0)),
                  pl.BlockSpec((1, tq, MISC_W), lambda bi, qi: (bi, qi, 0)),
                  pl.BlockSpec((1, t, MISC_W), lambda bi, qi: (bi, 0, 0))],
        out_specs=pl.BlockSpec((1, tq, t), lambda bi, qi: (bi, qi, 0)),
        out_shape=jax.ShapeDtypeStruct((b, t, t), BF16),
        scratch_shapes=[pltpu.VMEM((tq, t), jnp.int32)],
        compiler_params=_cparams("parallel", "parallel"),
        name="index_select",
    )(iq, misc, misc)


def _merge_up_kernel(x_ref, oc_ref, os_ref, ow_ref, ob_ref, misc_ref, wga_ref, wgb_ref, wua_ref, wub_ref,
                     y_ref, oa_ref):
    ga = jax.nn.sigmoid(misc_ref[:, MISC_GA:MISC_GA + 3 * NSA_HEADS])
    for hh in range(NSA_HEADS):
        sl = slice(hh * HEAD_DIM, (hh + 1) * HEAD_DIM)
        oa = (ga[:, hh:hh + 1] * oc_ref[:, sl] + ga[:, NSA_HEADS + hh:NSA_HEADS + hh + 1] * os_ref[:, sl]
              + ga[:, 2 * NSA_HEADS + hh:2 * NSA_HEADS + hh + 1] * ow_ref[:, sl])
        oa_ref[:, sl] = oa.astype(BF16)
    x = x_ref[...]
    ya = _dot(oa_ref[...], wua_ref[...])
    yb = _dot(ob_ref[...].astype(BF16), wub_ref[...])
    g_a = jax.nn.sigmoid(_dot(x, wga_ref[...]))
    g_b = jax.nn.sigmoid(_dot(x, wgb_ref[...]))
    y_ref[...] = (g_a * ya + g_b * yb).astype(BF16)


def _merge_up(xb, o_cmp, o_sel, o_win, o_b, misc, wga, wgb, wua, wub, tm, tn):
    m = xb.shape[0]
    aw = NSA_HEADS * HEAD_DIM
    bw = DSA_HEADS * HEAD_DIM
    row = lambda w: pl.BlockSpec((tm, w), lambda j, i: (i, 0))
    wcol = lambda k: pl.BlockSpec((k, tn), lambda j, i: (0, j))
    return pl.pallas_call(
        _merge_up_kernel,
        grid=(D_MODEL // tn, m // tm),
        in_specs=[row(D_MODEL), row(aw), row(aw), row(aw), row(bw), row(MISC_W),
                  wcol(D_MODEL), wcol(D_MODEL), wcol(aw), wcol(bw)],
        out_specs=pl.BlockSpec((tm, tn), lambda j, i: (i, j)),
        out_shape=jax.ShapeDtypeStruct((m, D_MODEL), BF16),
        scratch_shapes=[pltpu.VMEM((tm, aw), BF16)],
        compiler_params=_cparams("parallel", "parallel"),
        name="merge_up",
    )(xb, o_cmp, o_sel, o_win, o_b, misc, wga, wgb, wua, wub)


def _layer_norm(z, g, b):
    mu = jnp.mean(z, axis=-1, keepdims=True)
    zc = z - mu
    var = jnp.mean(zc * zc, axis=-1, keepdims=True)
    return zc * lax.rsqrt(var + LN_EPS) * g + b


def _merge_out_kernel(x_ref, y_ref, wo_ref, g_ref, b_ref, o_ref, ob_ref):
    z = DEEPNORM_ALPHA * x_ref[...] + _dot(y_ref[...], wo_ref[...])
    x1 = _layer_norm(z, g_ref[...], b_ref[...])
    o_ref[...] = x1
    ob_ref[...] = x1.astype(BF16)


def _merge_out(x, y, wo, g, b, tm):
    m = x.shape[0]
    row = lambda: pl.BlockSpec((tm, D_MODEL), lambda i: (i, 0))
    return pl.pallas_call(
        _merge_out_kernel,
        grid=(m // tm,),
        in_specs=[row(), row(), pl.BlockSpec((D_MODEL, D_MODEL), lambda i: (0, 0)),
                  pl.BlockSpec((1, D_MODEL), lambda i: (0, 0)), pl.BlockSpec((1, D_MODEL), lambda i: (0, 0))],
        out_specs=[row(), row()],
        out_shape=[jax.ShapeDtypeStruct((m, D_MODEL), F32), jax.ShapeDtypeStruct((m, D_MODEL), BF16)],
        compiler_params=_cparams("parallel"),
        name="merge_out",
    )(x, y, wo, g, b)


def _router_kernel(x_ref, wr_ref, rb_ref, c0_ref, idx_ref, wt_ref, pos_ref, cnt_ref, run_ref, *, tm):
    epg = N_EXPERTS // N_EXPERT_GROUPS
    logits = lax.dot_general(wr_ref[...], x_ref[...], (((1,), (1,)), ((), ())),
                             preferred_element_type=F32, precision=lax.Precision.HIGHEST)
    scores = jax.nn.sigmoid(logits)
    biased = scores + rb_ref[...]
    sub = lax.broadcasted_iota(jnp.int32, (epg, tm), 0)
    gs_rows = []
    for r in range(N_EXPERT_GROUPS):
        bg = biased[r * epg:(r + 1) * epg, :]
        m1 = jnp.max(bg, axis=0, keepdims=True)
        i1 = jnp.min(jnp.where(bg == m1, sub, epg), axis=0, keepdims=True)
        m2 = jnp.max(jnp.where(sub == i1, -jnp.inf, bg), axis=0, keepdims=True)
        gs_rows.append(m1 + m2)
    gs = jnp.concatenate(gs_rows, axis=0)
    grow = lax.broadcasted_iota(jnp.int32, (N_EXPERT_GROUPS, tm), 0)
    rank = jnp.zeros((N_EXPERT_GROUPS, tm), jnp.int32)
    for k in range(N_EXPERT_GROUPS):
        rk = gs[k:k + 1, :]
        rank = rank + jnp.where((rk > gs) | ((rk == gs) & (k < grow)), 1, 0)
    gkeep = rank < TOPK_GROUPS
    masked = jnp.concatenate(
        [jnp.where(gkeep[r:r + 1, :], biased[r * epg:(r + 1) * epg, :], -jnp.inf) for r in range(N_EXPERT_GROUPS)],
        axis=0)
    erow = lax.broadcasted_iota(jnp.int32, (N_EXPERTS, tm), 0)
    idx_rows, w_rows, hits = [], [], []
    for _ in range(TOP_K):
        m = jnp.max(masked, axis=0, keepdims=True)
        ix = jnp.min(jnp.where(masked == m, erow, N_EXPERTS), axis=0, keepdims=True)
        hit = erow == ix
        w_rows.append(jnp.sum(jnp.where(hit, scores, 0.0), axis=0, keepdims=True))
        idx_rows.append(ix)
        hits.append(hit)
        masked = jnp.where(hit, -jnp.inf, masked)
    wsum = w_rows[0]
    for w in w_rows[1:]:
        wsum = wsum + w
    pad = 8 - TOP_K
    idx_ref[...] = jnp.concatenate(idx_rows + [jnp.zeros((pad, tm), jnp.int32)], axis=0)
    wt_ref[...] = jnp.concatenate([w / wsum * ROUTED_SCALE for w in w_rows] + [jnp.zeros((pad, tm), F32)], axis=0)

    @pl.when(pl.program_id(0) == 0)
    def _():
        run_ref[...] = c0_ref[...]

    earlier = (lax.broadcasted_iota(jnp.int32, (tm, tm), 0) < lax.broadcasted_iota(jnp.int32, (tm, tm), 1))
    earlier = jnp.where(earlier, 1.0, 0.0).astype(BF16)
    run = run_ref[...]
    pos_rows = []
    for hit in hits:
        onehot = jnp.where(hit, 1.0, 0.0)
        before = _dot(onehot.astype(BF16), earlier).astype(jnp.int32)
        pos_rows.append(jnp.sum(jnp.where(hit, run + before, 0), axis=0, keepdims=True))
        run = run + jnp.sum(onehot, axis=1, keepdims=True).astype(jnp.int32)
    run_ref[...] = run
    cnt_ref[...] = run
    pos_ref[...] = jnp.concatenate(pos_rows + [jnp.zeros((pad, tm), jnp.int32)], axis=0)


def _router(x1, wr_t, rb, counts0, tm):
    m = x1.shape[0]
    tok = lambda: pl.BlockSpec((8, tm), lambda i: (0, i))
    return pl.pallas_call(
        functools.partial(_router_kernel, tm=tm),
        grid=(m // tm,),
        in_specs=[pl.BlockSpec((tm, D_MODEL), lambda i: (i, 0)),
                  pl.BlockSpec((N_EXPERTS, D_MODEL), lambda i: (0, 0)),
                  pl.BlockSpec((N_EXPERTS, 1), lambda i: (0, 0)),
                  pl.BlockSpec((N_EXPERTS, 1), lambda i: (0, 0))],
        out_specs=[tok(), tok(), tok(), pl.BlockSpec((N_EXPERTS, 1), lambda i: (0, 0))],
        out_shape=[jax.ShapeDtypeStruct((8, m), jnp.int32), jax.ShapeDtypeStruct((8, m), F32),
                   jax.ShapeDtypeStruct((8, m), jnp.int32), jax.ShapeDtypeStruct((N_EXPERTS, 1), jnp.int32)],
        scratch_shapes=[pltpu.VMEM((N_EXPERTS, 1), jnp.int32)],
        compiler_params=_cparams("arbitrary"),
        name="router",
    )(x1, wr_t, rb, counts0)


def _expert_kernel(be_ref, nv_ref, x_ref, wg_ref, wu_ref, wd_ref, o_ref, wgb_ref, wub_ref, wdb_ref):
    i = pl.program_id(0)
    valid = i < nv_ref[0]
    new_expert = (i == 0) | (be_ref[i] != be_ref[jnp.maximum(i - 1, 0)])

    @pl.when(valid & new_expert)
    def _():
        wgb_ref[...] = wg_ref[0].astype(BF16)
        wub_ref[...] = wu_ref[0].astype(BF16)
        wdb_ref[...] = wd_ref[0].astype(BF16)

    @pl.when(valid)
    def _():
        x = x_ref[...]
        h = jax.nn.silu(_dot(x, wgb_ref[...])) * _dot(x, wub_ref[...])
        o_ref[...] = _dot(h.astype(BF16), wdb_ref[...]).astype(o_ref.dtype)

    @pl.when(jnp.logical_not(valid))
    def _():
        o_ref[...] = jnp.zeros_like(o_ref)


def _experts(blk_e, n_valid, xg, wg, wu, wd, tr):
    n_rows = xg.shape[0]
    n_blk = n_rows // tr
    grid_spec = pltpu.PrefetchScalarGridSpec(
        num_scalar_prefetch=2,
        grid=(n_blk,),
        in_specs=[pl.BlockSpec((tr, D_MODEL), lambda i, be, nv: (i, 0)),
                  pl.BlockSpec((1, D_MODEL, EXPERT_FF), lambda i, be, nv: (be[i], 0, 0)),
                  pl.BlockSpec((1, D_MODEL, EXPERT_FF), lambda i, be, nv: (be[i], 0, 0)),
                  pl.BlockSpec((1, EXPERT_FF, D_MODEL), lambda i, be, nv: (be[i], 0, 0))],
        out_specs=pl.BlockSpec((tr, D_MODEL), lambda i, be, nv: (i, 0)),
        scratch_shapes=[pltpu.VMEM((D_MODEL, EXPERT_FF), BF16), pltpu.VMEM((D_MODEL, EXPERT_FF), BF16),
                        pltpu.VMEM((EXPERT_FF, D_MODEL), BF16)],
    )
    return pl.pallas_call(
        _expert_kernel,
        grid_spec=grid_spec,
        out_shape=jax.ShapeDtypeStruct((n_rows, D_MODEL), BF16),
        compiler_params=_cparams("arbitrary"),
        name="experts",
    )(blk_e, n_valid, xg, wg, wu, wd)


def _ffn_out_kernel(x_ref, r_ref, w_ref, sg_ref, su_ref, sd_ref, g_ref, b_ref, o_ref):
    x = x_ref[...]
    xb = x.astype(BF16)
    h = jax.nn.silu(_dot(xb, sg_ref[...])) * _dot(xb, su_ref[...])
    f = _dot(h.astype(BF16), sd_ref[...])
    for k in range(TOP_K):
        f = f + r_ref[k].astype(F32) * w_ref[:, k:k + 1]
    o_ref[...] = _layer_norm(DEEPNORM_ALPHA * x + f, g_ref[...], b_ref[...])


def _ffn_out(x1, rows6, wts, sg, su, sd, g, b, tm):
    m = x1.shape[0]
    row = lambda: pl.BlockSpec((tm, D_MODEL), lambda i: (i, 0))
    full = lambda s: pl.BlockSpec(s, lambda i: (0, 0))
    return pl.pallas_call(
        _ffn_out_kernel,
        grid=(m // tm,),
        in_specs=[row(), pl.BlockSpec((TOP_K, tm, D_MODEL), lambda i: (0, i, 0)), pl.BlockSpec((tm, 8), lambda i: (i, 0)),
                  full((D_MODEL, SHARED_FF)), full((D_MODEL, SHARED_FF)), full((SHARED_FF, D_MODEL)),
                  full((1, D_MODEL)), full((1, D_MODEL))],
        out_specs=row(),
        out_shape=jax.ShapeDtypeStruct((m, D_MODEL), F32),
        compiler_params=_cparams("parallel"),
        name="ffn_out",
    )(x1, rows6, wts, sg, su, sd, g, b)


def _rel_bucket(dist):
    n = jnp.maximum(dist, 0)
    nf = jnp.maximum(n, 1).astype(F32)
    large = MAX_EXACT + (jnp.log(nf / MAX_EXACT) / math.log(MAX_DISTANCE / MAX_EXACT)
                         * (N_BUCKETS - MAX_EXACT)).astype(jnp.int32)
    return jnp.where(n < MAX_EXACT, n, jnp.minimum(large, N_BUCKETS - 1))


def _toeplitz_bias(tbl, tq, n_tiles):
    i = jnp.arange(tq)[:, None]
    j = jnp.arange(tq)[None, :]
    dist = jnp.arange(n_tiles)[:, None, None] * tq + (i - j)[None]
    bias = jnp.moveaxis(tbl[_rel_bucket(dist)], -1, 0)
    return bias, dist


def _rel_bias(dist, tbl):
    bucket = _rel_bucket(dist)
    if dist.shape[-2] == 1:
        return jnp.moveaxis(tbl[bucket[..., 0, :]], -3, -1)
    gi = jnp.arange(tbl.shape[1])[:, None]
    return jnp.swapaxes(tbl.transpose(1, 0, 2)[gi, bucket], -1, -2)


def _attn_probs(s, dist, valid, tbl):
    s = s.astype(F32) + _rel_bias(dist, tbl).astype(F32)
    s = jnp.where(valid[..., None, :], s, -jnp.inf)
    m = jnp.max(s, axis=-1, keepdims=True)
    m = jnp.where(jnp.isfinite(m), m, 0.0)
    e = jnp.exp(s - m)
    den = e.sum(-1, keepdims=True)
    return e / jnp.where(den > 0, den, 1.0)


def _sparse_attend(q, kg, vg, q_pos, k_pos, tbl):
    s = jnp.einsum('btghd,btgkd->btghk', q, kg) * ATTN_SCALE
    dist = q_pos[:, :, None, None] - k_pos
    p = _attn_probs(s, dist, dist >= 0, tbl)
    return jnp.einsum('btghk,btgkd->btghd', p, vg)


def _gather_past_and_new(pool, page_table, new_rows, pos, *extra):
    past = page_table.shape[1] * PAGE_SIZE
    bi = jnp.arange(pos.shape[0]).reshape((-1,) + (1,) * (pos.ndim - 1))
    page = jnp.minimum(pos // PAGE_SIZE, page_table.shape[1] - 1)
    old = pool[(page_table[bi, page], pos % PAGE_SIZE) + extra]
    new = new_rows[(bi, jnp.clip(pos - past, 0, new_rows.shape[1] - 1)) + extra]
    is_old = (pos < past).reshape(pos.shape + (1,) * (old.ndim - pos.ndim))
    return jnp.where(is_old, old, new)


def _sample_mixer_branches(q_a, kcomp, vcomp, ks, vs, kw, vw, q_b, kb, vb, iq, ik, iw,
                           cache_slc_k, cache_slc_v, win_k, win_v, cache_dsa_k, cache_dsa_v, cache_idx_k,
                           page_table, tbl_a, tbl_b):
    b, t = q_a.shape[:2]
    past = page_table.shape[1] * PAGE_SIZE
    total = past + t
    q_pos = past + jnp.arange(t)[None]
    nb = kcomp.shape[1]
    s = jnp.einsum('btghd,bngd->btghn', q_a, kcomp) * ATTN_SCALE
    blk_end = jnp.arange(nb) * CMP_BLOCK + (CMP_BLOCK - 1)
    dist = (q_pos[:, :, None] - blk_end)[:, :, None, :]
    p = _attn_probs(s, dist, dist >= 0, tbl_a)
    o_cmp = jnp.einsum('btghn,bngd->btghd', p, vcomp)
    imp = p.sum(axis=3)
    n_blocks = -(-total // SEL_BLOCK)
    imp = jnp.pad(imp, ((0, 0), (0, 0), (0, 0), (0, n_blocks - nb)))
    j = jnp.arange(n_blocks)
    cur = (q_pos // SEL_BLOCK)[:, :, None, None]
    forced = (j == 0) | (j == cur) | (j == cur - 1)
    score = jnp.where(forced, jnp.inf, jnp.where(j <= cur, imp, -jnp.inf))
    idx = lax.top_k(score, min(N_SEL_BLOCKS, n_blocks))[1]
    kpos = (idx[..., None] * SEL_BLOCK + jnp.arange(SEL_BLOCK)).reshape(idx.shape[:-1] + (-1,))
    gi = jnp.arange(NSA_KV_GROUPS)[None, None, :, None]
    o_sel = _sparse_attend(q_a, _gather_past_and_new(cache_slc_k, page_table, ks, kpos, gi),
                           _gather_past_and_new(cache_slc_v, page_table, vs, kpos, gi), q_pos, kpos, tbl_a)
    wk = jnp.concatenate([win_k, kw], axis=1)
    wv = jnp.concatenate([win_v, vw], axis=1)
    w_past = win_k.shape[1]
    k_pos = past - w_past + jnp.arange(w_past + t)
    s = jnp.einsum('btghd,bkgd->btghk', q_a, wk) * ATTN_SCALE
    dist = q_pos[:, :, None, None] - k_pos
    valid = (dist >= 0) & (dist < WINDOW) & (k_pos >= 0)
    p = _attn_probs(s, dist, valid, tbl_a)
    o_win = jnp.einsum('btghk,bkgd->btghd', p, wv)
    ik_full = jnp.concatenate([cache_idx_k[page_table].reshape((b, past, IDX_DIM)), ik], axis=1)
    logits = jax.nn.relu(jnp.einsum('bthd,bsd->bths', iq, ik_full).astype(F32) * IDX_DIM ** -0.5)
    sc = jnp.einsum('bths,bth->bts', logits, iw.astype(F32)) * IDX_HEADS ** -0.5
    sc = jnp.where(jnp.arange(total) <= q_pos[:, :, None], sc, -jnp.inf)
    pos = lax.top_k(sc, min(DSA_TOPK, total // 4))[1]
    kg = _gather_past_and_new(cache_dsa_k, page_table, kb, pos).swapaxes(2, 3)
    vg = _gather_past_and_new(cache_dsa_v, page_table, vb, pos).swapaxes(2, 3)
    o_b = _sparse_attend(q_b, kg, vg, q_pos, pos[:, :, None, :], tbl_b)
    n_win = min(WINDOW, total)
    return o_cmp, o_sel, o_win, o_b, wk[:, -n_win:], wv[:, -n_win:]


def _split_w_in(w_in):
    points = np.cumsum(IN_COLS)[:-1].tolist()
    q_a, kv_a, g_a, q_b, kv_b, iq, ik, iw, g_m = jnp.split(w_in, points, axis=-1)
    pad = jnp.zeros((D_MODEL, MISC_W - IDX_DIM - IDX_HEADS - 3 * NSA_HEADS), w_in.dtype)
    w_kv = jnp.concatenate([kv_a, kv_b, ik, iw, g_a, pad], axis=-1).astype(BF16)
    w_q = jnp.concatenate([q_a, q_b, iq], axis=-1).astype(BF16)
    return w_kv, w_q, g_m[:, :D_MODEL].astype(BF16), g_m[:, D_MODEL:].astype(BF16)


KV_WIDTHS = (KV_W,) * 8 + (MISC_W,)
Q_WIDTHS = (NSA_HEADS * HEAD_DIM, DSA_HEADS * HEAD_DIM, IDX_HEADS * IDX_DIM)


def _row_tile(m, cap):
    tm = math.gcd(m, cap)
    assert tm % 8 == 0 or tm == m
    return tm


def kernel(x_prompt, x_sample, cache_cmp_k, cache_cmp_v, cache_slc_k, cache_slc_v, state_win_k, state_win_v,
           cache_dsa_k, cache_dsa_v, cache_idx_k, page_table, rel_bias_table, w_in, cmp_pe, cmp_w1, cmp_w2,
           w_up_a, w_up_b, w_o, ln1_g, ln1_b, w_router, router_bias, moe_w_gate, moe_w_up, moe_w_down,
           sh_w_gate, sh_w_up, sh_w_down, ln2_g, ln2_b):
    assert w_in.shape[0] == DEPTH == 1
    b, t, _ = x_prompt.shape
    db, dt, _ = x_sample.shape
    assert dt == 1
    n_pool = cache_cmp_k.shape[1]
    n_pages = page_table.shape[1]
    tq = Q_TILE
    assert t % tq == 0 and t % CMP_BLOCK == 0
    n_p, n_s = b * t, db * dt

    tbl_a = rel_bias_table[:, :NSA_HEADS]
    tbl_b = rel_bias_table[:, NSA_HEADS:]
    w_kv, w_q, w_ga, w_gb = _split_w_in(w_in[0])
    w_ua, w_ub, w_ob = w_up_a[0].astype(BF16), w_up_b[0].astype(BF16), w_o[0].astype(BF16)
    cw = [_compress_weights(cmp_pe[0, i], cmp_w1[0, i], cmp_w2[0, i]) for i in range(2)]

    xp = x_prompt.reshape(n_p, D_MODEL)
    xs = x_sample.reshape(n_s, D_MODEL)
    xpb, xsb = xp.astype(BF16), xs.astype(BF16)
    tm_p = _row_tile(n_p, 512)
    kc, vc, ks, vs, kw, vw, kb, vb, misc = _project(xpb, w_kv, KV_WIDTHS, tm_p)
    qa, qb, iq = _project(xpb, w_q, Q_WIDTHS, tm_p)
    s_kc, s_vc, s_ks, s_vs, s_kw, s_vw, s_kb, s_vb, s_misc = _project(xsb, w_kv, KV_WIDTHS, n_s)
    s_qa, s_qb, s_iq = _project(xsb, w_q, Q_WIDTHS, n_s)

    nb = t // CMP_BLOCK
    r3 = lambda a: a.reshape(b, t, -1)
    kcomp = _compress(kc.reshape(b * nb, CMP_BLOCK, KV_W), *cw[0]).reshape(b, nb, KV_W)
    vcomp = _compress(vc.reshape(b * nb, CMP_BLOCK, KV_W), *cw[1]).reshape(b, nb, KV_W)
    cdist = jnp.arange(t)[:, None] - (jnp.arange(nb) * CMP_BLOCK + CMP_BLOCK - 1)[None, :]
    cbias = jnp.where(cdist >= 0, jnp.moveaxis(tbl_a[_rel_bucket(cdist)], -1, 0), NEG)
    o_cmp, selmask = _cmp_select(r3(qa), kcomp, vcomp, cbias, tq)

    assert MAX_DISTANCE <= tq
    bias2_a, dist2 = _toeplitz_bias(tbl_a, tq, 2)
    bias2_b, _ = _toeplitz_bias(tbl_b, tq, 2)
    c_a, c_b = tbl_a[N_BUCKETS - 1], tbl_b[N_BUCKETS - 1]
    d_a = jnp.where(dist2 >= 0, bias2_a - c_a[:, None, None, None], NEG)
    d_b = jnp.where(dist2 >= 0, bias2_b - c_b[:, None, None, None], NEG)
    expand = (jnp.arange(t)[None, :] // SEL_BLOCK == jnp.arange(nb)[:, None]).astype(BF16)
    o_sel = _dense_attn(r3(qa), r3(ks), r3(vs), selmask, expand, c_a, d_a, tq, "sel")

    n_wchunks = -(-(WINDOW - 1) // tq) + 1
    bias_w, dist_w = _toeplitz_bias(tbl_a, tq, n_wchunks)
    wtiles = jnp.where((dist_w >= 0) & (dist_w < WINDOW), bias_w, NEG)
    o_win = _window_attn(r3(qa), r3(kw), r3(vw), wtiles, tq)

    n_keep = min(DSA_TOPK, t // 4)
    keepmask = _index_select(r3(iq), r3(misc), tq, n_keep)
    o_b = _dense_attn(r3(qb), r3(kb), r3(vb), keepmask, jnp.zeros((8, 128), BF16), c_b, d_b, tq, "dsa")

    tm_m = _row_tile(n_p, 256)
    y_p = _merge_up(xpb, o_cmp.reshape(n_p, -1), o_sel.reshape(n_p, -1), o_win.reshape(n_p, -1),
                    o_b.reshape(n_p, -1), misc, w_ga, w_gb, w_ua, w_ub, tm_m, 1024)
    x1p, x1pb = _merge_out(xp, y_p, w_ob, ln1_g, ln1_b, tm_m)

    past = n_pages * PAGE_SIZE
    halves = PAGE_SIZE // CMP_BLOCK

    def comp_pool(cache, i):
        c = _compress_pool(cache, cmp_pe[0, i], cmp_w1[0, i], cmp_w2[0, i])
        return c.reshape(n_pool, halves, NSA_KV_GROUPS, HEAD_DIM)[page_table].reshape(
            db, n_pages * halves, NSA_KV_GROUPS, HEAD_DIM)

    s_kcomp = comp_pool(cache_cmp_k[0], 0)
    s_vcomp = comp_pool(cache_cmp_v[0], 1)
    g4 = lambda a: a.reshape(db, dt, NSA_KV_GROUPS, HEAD_DIM)
    so_cmp, so_sel, so_win, so_b, s_wk, s_wv = _sample_mixer_branches(
        s_qa.reshape(db, dt, NSA_KV_GROUPS, NSA_HPG, HEAD_DIM), s_kcomp, s_vcomp, g4(s_ks), g4(s_vs),
        g4(s_kw), g4(s_vw), s_qb.reshape(db, dt, DSA_KV_GROUPS, DSA_HPG, HEAD_DIM), g4(s_kb), g4(s_vb),
        s_iq.reshape(db, dt, IDX_HEADS, IDX_DIM), s_misc[:, :IDX_DIM].reshape(db, dt, IDX_DIM),
        s_misc[:, MISC_IW:MISC_IW + IDX_HEADS].reshape(db, dt, IDX_HEADS),
        cache_slc_k[0], cache_slc_v[0], state_win_k[0], state_win_v[0], cache_dsa_k[0], cache_dsa_v[0],
        cache_idx_k[0], page_table,
        tbl_a.reshape(N_BUCKETS, NSA_KV_GROUPS, NSA_HPG), tbl_b.reshape(N_BUCKETS, DSA_KV_GROUPS, DSA_HPG))
    y_s = _merge_up(xsb, so_cmp.reshape(n_s, -1), so_sel.reshape(n_s, -1), so_win.reshape(n_s, -1),
                    so_b.reshape(n_s, -1), s_misc, w_ga, w_gb, w_ua, w_ub, n_s, 1024)
    x1s, x1sb = _merge_out(xs, y_s, w_ob, ln1_g, ln1_b, n_s)

    wr_t = w_router[0].T
    rb = router_bias[0].reshape(N_EXPERTS, 1)
    eidx_p, wts_p, pos_p, cnt_p = _router(x1p, wr_t, rb, jnp.zeros((N_EXPERTS, 1), jnp.int32), tm_m)
    eidx_s, wts_s, pos_s, cnt = _router(x1s, wr_t, rb, cnt_p, n_s)
    n_tok = n_p + n_s
    eidx = jnp.concatenate([eidx_p[:TOP_K], eidx_s[:TOP_K]], axis=1)
    pos = jnp.concatenate([pos_p[:TOP_K], pos_s[:TOP_K]], axis=1)
    tr = EXPERT_ROWS
    n_asg = n_tok * TOP_K
    counts = cnt[:, 0]
    padded = (counts + tr - 1) // tr * tr
    pend = jnp.cumsum(padded)
    pad_start = pend - padded
    n_rows = -(-n_asg // tr) * tr + N_EXPERTS * tr
    n_blk = n_rows // tr
    dest = pos + jnp.sum(jnp.where(eidx[..., None] == jnp.arange(N_EXPERTS), pad_start, 0), axis=-1)
    blk_e = jnp.minimum(jnp.sum(pend[None, :] <= (jnp.arange(n_blk) * tr)[:, None], axis=1), N_EXPERTS - 1)
    n_valid = (pend[-1] // tr).astype(jnp.int32).reshape(1)
    row_tok = jnp.full((n_rows,), n_tok, jnp.int32).at[dest.reshape(-1)].set(
        jnp.tile(jnp.arange(n_tok, dtype=jnp.int32), TOP_K), unique_indices=True)
    x1b_all = jnp.concatenate([x1pb, x1sb, jnp.zeros((1, D_MODEL), BF16)], axis=0)
    xg = x1b_all[row_tok]
    out_rows = _experts(blk_e.astype(jnp.int32), n_valid, xg, moe_w_gate[0], moe_w_up[0], moe_w_down[0], tr)
    rows6 = out_rows[dest]

    sg, su, sd = sh_w_gate[0].astype(BF16), sh_w_up[0].astype(BF16), sh_w_down[0].astype(BF16)
    y_prompt = _ffn_out(x1p, rows6, wts_p.T, sg, su, sd, ln2_g, ln2_b, tm_m).reshape(b, t, D_MODEL)
    y_sample = _ffn_out(x1s, rows6[:, n_p:], wts_s.T, sg, su, sd, ln2_g, ln2_b, n_s).reshape(db, dt, D_MODEL)

    n_win = min(WINDOW, t)
    st = lambda a: a.reshape(1, b, t, NSA_KV_GROUPS, HEAD_DIM)
    ss = lambda a: a.reshape(1, db, dt, NSA_KV_GROUPS, HEAD_DIM)
    return (y_prompt, y_sample,
            st(kc), st(vc), st(ks), st(vs), st(kw)[:, :, -n_win:], st(vw)[:, :, -n_win:], st(kb), st(vb),
            misc[:, :IDX_DIM].reshape(1, b, t, IDX_DIM),
            ss(s_kc), ss(s_vc), ss(s_ks), ss(s_vs), s_wk[None], s_wv[None], ss(s_kb), ss(s_vb),
            s_misc[:, :IDX_DIM].reshape(1, db, dt, IDX_DIM))
```

```python
import functools
import math

import jax
import jax.numpy as jnp
import numpy as np
from jax import lax
from jax.experimental import pallas as pl
from jax.experimental.pallas import tpu as pltpu

D_MODEL = 2048
PAGE_SIZE = 128
HEAD_DIM = 128
NSA_HEADS = 8
NSA_KV_GROUPS = 2
NSA_HPG = NSA_HEADS // NSA_KV_GROUPS
CMP_BLOCK = 64
CMP_HIDDEN = 128
SEL_BLOCK = 64
N_SEL_BLOCKS = 16
WINDOW = 512
DSA_HEADS = 8
DSA_KV_GROUPS = 2
DSA_HPG = DSA_HEADS // DSA_KV_GROUPS
IDX_HEADS = 8
IDX_DIM = 64
DSA_TOPK = 256
N_BUCKETS = 32
MAX_EXACT = 16
MAX_DISTANCE = 128
N_EXPERTS = 64
EXPERT_FF = 512
SHARED_FF = 512
TOP_K = 6
N_EXPERT_GROUPS = 8
TOPK_GROUPS = 4
ROUTED_SCALE = 2.5
LN_EPS = 1e-5
ATTN_SCALE = HEAD_DIM ** -0.5
DEPTH = 1
DEEPNORM_ALPHA = (2 * DEPTH) ** 0.25
IN_COLS = (NSA_HEADS * HEAD_DIM, 6 * NSA_KV_GROUPS * HEAD_DIM, 3 * NSA_HEADS,
           DSA_HEADS * HEAD_DIM, 2 * DSA_KV_GROUPS * HEAD_DIM,
           IDX_HEADS * IDX_DIM, IDX_DIM, IDX_HEADS, 2 * D_MODEL)

KV_W = NSA_KV_GROUPS * HEAD_DIM
MISC_W = 128
MISC_IW = IDX_DIM
MISC_GA = IDX_DIM + IDX_HEADS
NEG = -1e30
Q_TILE = 256
EXPERT_ROWS = 256
VMEM_LIMIT = 56 * 1024 * 1024

BF16 = jnp.bfloat16
F32 = jnp.float32


def _cparams(*sem):
    return pltpu.CompilerParams(dimension_semantics=sem, vmem_limit_bytes=VMEM_LIMIT)


def _dot(a, b):
    return jnp.dot(a, b, preferred_element_type=F32)


def _dot_nt(a, b):
    return lax.dot_general(a, b, (((1,), (1,)), ((), ())), preferred_element_type=F32)


def _proj_kernel(x_ref, w_ref, *o_refs, widths):
    acc = _dot(x_ref[...], w_ref[...])
    off = 0
    for o_ref, wd in zip(o_refs, widths):
        o_ref[...] = acc[:, off:off + wd]
        off += wd


def _project(x, w, widths, tm):
    m, k = x.shape
    n = w.shape[1]
    assert sum(widths) == n and m % tm == 0
    return pl.pallas_call(
        functools.partial(_proj_kernel, widths=widths),
        grid=(m // tm,),
        in_specs=[pl.BlockSpec((tm, k), lambda i: (i, 0)),
                  pl.BlockSpec((k, n), lambda i: (0, 0))],
        out_specs=[pl.BlockSpec((tm, wd), lambda i: (i, 0)) for wd in widths],
        out_shape=[jax.ShapeDtypeStruct((m, wd), F32) for wd in widths],
        compiler_params=_cparams("parallel"),
        name="project",
    )(x, w)


CMP_JCHUNK = 8


def _compress_kernel(x_ref, pe_ref, w1_ref, w2_ref, o_ref, acc_ref):
    jc = pl.program_id(1)

    @pl.when(jc == 0)
    def _():
        acc_ref[...] = jnp.zeros_like(acc_ref)

    acc = acc_ref[...]
    for jj in range(CMP_JCHUNK):
        lhs = (x_ref[:, jj, :] + pe_ref[jj:jj + 1, :]).astype(BF16)
        acc = acc + _dot(lhs, w1_ref[jj])
    acc_ref[...] = acc

    @pl.when(jc == pl.num_programs(1) - 1)
    def _():
        h = jax.nn.gelu(acc_ref[...])
        o_ref[...] = _dot(h.astype(BF16), w2_ref[...])


def _compress(rows, pe2, w1big, w2big):
    r = rows.shape[0]
    tr = math.gcd(r, 1024)
    assert tr % 8 == 0
    return pl.pallas_call(
        _compress_kernel,
        grid=(r // tr, CMP_BLOCK // CMP_JCHUNK),
        in_specs=[pl.BlockSpec((tr, CMP_JCHUNK, KV_W), lambda i, j: (i, j, 0)),
                  pl.BlockSpec((CMP_JCHUNK, KV_W), lambda i, j: (j, 0)),
                  pl.BlockSpec((CMP_JCHUNK, KV_W, KV_W), lambda i, j: (j, 0, 0)),
                  pl.BlockSpec((KV_W, KV_W), lambda i, j: (0, 0))],
        out_specs=pl.BlockSpec((tr, KV_W), lambda i, j: (i, 0)),
        out_shape=jax.ShapeDtypeStruct((r, KV_W), F32),
        scratch_shapes=[pltpu.VMEM((tr, KV_W), F32)],
        compiler_params=_cparams("parallel", "arbitrary"),
        name="compress",
    )(rows, pe2, w1big, w2big)


def _rows_at(x_ref, r):
    n, rows, w = x_ref.shape
    return x_ref.reshape(n * rows, w)[pl.ds(r, n, stride=rows), :]


def _compress_pool_kernel(x_ref, pe_ref, w1_ref, w2_ref, o_ref, acc_ref, *, tr):
    jc = pl.program_id(1)

    @pl.when(jc == 0)
    def _():
        acc_ref[...] = jnp.zeros_like(acc_ref)

    for g in range(NSA_KV_GROUPS):
        acc = acc_ref[g]
        for jp in range(CMP_JCHUNK // 2):
            parts = []
            for u in range(2):
                jj = 2 * jp + u
                xs = _rows_at(x_ref, 2 * jj + g) + pe_ref[jj:jj + 1, :]
                parts.append(xs.astype(BF16))
            acc = acc + _dot(jnp.concatenate(parts, axis=-1), w1_ref[jp])
        acc_ref[g] = acc

    @pl.when(jc == pl.num_programs(1) - 1)
    def _():
        for g in range(NSA_KV_GROUPS):
            h = jax.nn.gelu(acc_ref[g])
            o_ref[:, g * HEAD_DIM:(g + 1) * HEAD_DIM] = _dot(h.astype(BF16), w2_ref[...])


def _compress_pool(cache, pe, w1, w2):
    n_pool = cache.shape[0]
    r = n_pool * (PAGE_SIZE // CMP_BLOCK)
    rows = cache.reshape(r, CMP_BLOCK * NSA_KV_GROUPS, HEAD_DIM)
    tr = math.gcd(r, 1024)
    assert tr % 8 == 0
    rows_per = 2 * CMP_JCHUNK
    w1p = w1.reshape(CMP_BLOCK // 2, 2 * HEAD_DIM, CMP_HIDDEN).astype(BF16)
    return pl.pallas_call(
        functools.partial(_compress_pool_kernel, tr=tr),
        grid=(r // tr, CMP_BLOCK // CMP_JCHUNK),
        in_specs=[pl.BlockSpec((tr, rows_per, HEAD_DIM), lambda i, j: (i, j, 0)),
                  pl.BlockSpec((CMP_JCHUNK, HEAD_DIM), lambda i, j: (j, 0)),
                  pl.BlockSpec((CMP_JCHUNK // 2, 2 * HEAD_DIM, CMP_HIDDEN), lambda i, j: (j, 0, 0)),
                  pl.BlockSpec((CMP_HIDDEN, HEAD_DIM), lambda i, j: (0, 0))],
        out_specs=pl.BlockSpec((tr, KV_W), lambda i, j: (i, 0)),
        out_shape=jax.ShapeDtypeStruct((r, KV_W), F32),
        scratch_shapes=[pltpu.VMEM((NSA_KV_GROUPS, tr, CMP_HIDDEN), F32)],
        compiler_params=_cparams("parallel", "arbitrary"),
        name="compress_pool",
    )(rows, pe, w1p, w2.astype(BF16))


def _compress_weights(pe, w1, w2):
    pe2 = jnp.concatenate([pe, pe], axis=-1)
    z1 = jnp.zeros_like(w1)
    w1big = jnp.concatenate([jnp.concatenate([w1, z1], axis=2), jnp.concatenate([z1, w1], axis=2)], axis=1)
    z2 = jnp.zeros_like(w2)
    w2big = jnp.concatenate([jnp.concatenate([w2, z2], axis=1), jnp.concatenate([z2, w2], axis=1)], axis=0)
    return pe2, w1big.astype(BF16), w2big.astype(BF16)


def _cmp_select_kernel(q_ref, kc_ref, vc_ref, cb_ref, o_ref, sel_ref, *, tq, nb, n_sel):
    qi = pl.program_id(1)
    t = qi * tq + lax.broadcasted_iota(jnp.int32, (tq, nb), 0)
    j = lax.broadcasted_iota(jnp.int32, (tq, nb), 1)
    cur = t // SEL_BLOCK
    for g in range(NSA_KV_GROUPS):
        kc = kc_ref[0, :, g * HEAD_DIM:(g + 1) * HEAD_DIM].astype(BF16)
        vc = vc_ref[0, :, g * HEAD_DIM:(g + 1) * HEAD_DIM].astype(BF16)
        imp = jnp.zeros((tq, nb), F32)
        for h in range(NSA_HPG):
            hh = g * NSA_HPG + h
            q = q_ref[0, :, hh * HEAD_DIM:(hh + 1) * HEAD_DIM].astype(BF16)
            cb = cb_ref[hh]
            valid = cb > 0.5 * NEG
            s = _dot_nt(q, kc) * ATTN_SCALE + cb
            m = jnp.max(s, axis=-1, keepdims=True)
            e = jnp.where(valid, jnp.exp(s - m), 0.0)
            den = jnp.sum(e, axis=-1, keepdims=True)
            p = e / jnp.where(den > 0, den, 1.0)
            o_ref[0, :, hh * HEAD_DIM:(hh + 1) * HEAD_DIM] = _dot(p.astype(BF16), vc)
            imp = imp + p
        forced = (j == 0) | (j == cur) | (j == cur - 1)
        score = jnp.where(forced, jnp.inf, jnp.where(j <= cur, imp, -jnp.inf))
        rank = jnp.zeros((tq, nb), jnp.int32)
        for k in range(nb):
            col = score[:, k:k + 1]
            ahead = (col > score) | ((col == score) & (k < j))
            rank = rank + jnp.where(ahead, 1, 0)
        sel_ref[0, g] = jnp.where(rank < n_sel, 1.0, 0.0).astype(F32)


def _cmp_select(qa, kcomp, vcomp, cbias, tq):
    b, t, _ = qa.shape
    nb = kcomp.shape[1]
    n_sel = min(N_SEL_BLOCKS, nb)
    return pl.pallas_call(
        functools.partial(_cmp_select_kernel, tq=tq, nb=nb, n_sel=n_sel),
        grid=(b, t // tq),
        in_specs=[pl.BlockSpec((1, tq, NSA_HEADS * HEAD_DIM), lambda bi, qi: (bi, qi, 0)),
                  pl.BlockSpec((1, nb, KV_W), lambda bi, qi: (bi, 0, 0)),
                  pl.BlockSpec((1, nb, KV_W), lambda bi, qi: (bi, 0, 0)),
                  pl.BlockSpec((NSA_HEADS, tq, nb), lambda bi, qi: (0, qi, 0))],
        out_specs=[pl.BlockSpec((1, tq, NSA_HEADS * HEAD_DIM), lambda bi, qi: (bi, qi, 0)),
                   pl.BlockSpec((1, NSA_KV_GROUPS, tq, nb), lambda bi, qi: (bi, 0, qi, 0))],
        out_shape=[jax.ShapeDtypeStruct((b, t, NSA_HEADS * HEAD_DIM), F32),
                   jax.ShapeDtypeStruct((b, NSA_KV_GROUPS, t, nb), F32)],
        compiler_params=_cparams("parallel", "parallel"),
        name="cmp_select",
    )(qa, kcomp, vcomp, cbias)


def _dense_attn_kernel(c_ref, q_ref, k_ref, v_ref, mask_ref, expand_ref, d_ref, o_ref,
                       *, tq, n_heads, hpg, mode):
    qi = pl.program_id(1)

    def update(carry, q, g, chunk, bias):
        m, l, acc = carry
        off = pl.multiple_of(chunk * tq, tq)
        if mode == "sel":
            keep = _dot(mask_ref[0, g].astype(BF16), expand_ref[:, pl.ds(off, tq)]) > 0.5
        else:
            keep = mask_ref[0, :, pl.ds(off, tq)].astype(F32) > 0.5
        k = k_ref[0, pl.ds(off, tq), g * HEAD_DIM:(g + 1) * HEAD_DIM].astype(BF16)
        v = v_ref[0, pl.ds(off, tq), g * HEAD_DIM:(g + 1) * HEAD_DIM].astype(BF16)
        s = jnp.where(keep, _dot_nt(q, k) * ATTN_SCALE + bias, NEG)
        m_new = jnp.maximum(m, jnp.max(s, axis=-1, keepdims=True))
        alpha = jnp.exp(m - m_new)
        e = jnp.where(keep, jnp.exp(s - m_new), 0.0)
        return (m_new, alpha * l + jnp.sum(e, axis=-1, keepdims=True), alpha * acc + _dot(e.astype(BF16), v))

    for hh in range(n_heads):
        g = hh // hpg
        q = q_ref[0, :, hh * HEAD_DIM:(hh + 1) * HEAD_DIM].astype(BF16)
        c = c_ref[hh]
        carry = (jnp.full((tq, 1), NEG, F32), jnp.zeros((tq, 1), F32), jnp.zeros((tq, HEAD_DIM), F32))
        carry = lax.fori_loop(0, jnp.maximum(qi - 1, 0), lambda kj, cr: update(cr, q, g, kj, c), carry)
        carry = lax.cond(qi > 0, lambda cr: update(cr, q, g, qi - 1, d_ref[hh, 1] + c), lambda cr: cr, carry)
        _, l, acc = update(carry, q, g, qi, d_ref[hh, 0] + c)
        o_ref[0, :, hh * HEAD_DIM:(hh + 1) * HEAD_DIM] = acc / jnp.where(l > 0, l, 1.0)


def _dense_attn(q, k, v, mask, expand, cvec, dtiles, tq, mode):
    b, t, qw = q.shape
    n_heads = qw // HEAD_DIM
    hpg = n_heads // (k.shape[2] // HEAD_DIM)
    if mode == "sel":
        nb = mask.shape[-1]
        mask_spec = pl.BlockSpec((1, mask.shape[1], tq, nb), lambda bi, qi: (bi, 0, qi, 0))
    else:
        mask_spec = pl.BlockSpec((1, tq, t), lambda bi, qi: (bi, qi, 0))
    return pl.pallas_call(
        functools.partial(_dense_attn_kernel, tq=tq, n_heads=n_heads, hpg=hpg, mode=mode),
        grid=(b, t // tq),
        in_specs=[pl.BlockSpec(memory_space=pltpu.SMEM),
                  pl.BlockSpec((1, tq, qw), lambda bi, qi: (bi, qi, 0)),
                  pl.BlockSpec((1, t, k.shape[2]), lambda bi, qi: (bi, 0, 0)),
                  pl.BlockSpec((1, t, v.shape[2]), lambda bi, qi: (bi, 0, 0)),
                  mask_spec,
                  pl.BlockSpec(expand.shape, lambda bi, qi: (0, 0)),
                  pl.BlockSpec(dtiles.shape, lambda bi, qi: (0, 0, 0, 0))],
        out_specs=pl.BlockSpec((1, tq, qw), lambda bi, qi: (bi, qi, 0)),
        out_shape=jax.ShapeDtypeStruct((b, t, qw), F32),
        compiler_params=_cparams("parallel", "parallel"),
        name="dense_attn_" + mode,
    )(cvec, q, k, v, mask, expand, dtiles)


def _window_attn_kernel(q_ref, k_ref, v_ref, w_ref, o_ref, *, tq, n_chunks, n_heads, hpg):
    qi = pl.program_id(1)
    starts, pens = [], []
    for r in range(n_chunks):
        cj = qi - (n_chunks - 1) + r
        starts.append(pl.multiple_of(jnp.maximum(cj, 0) * tq, tq))
        pens.append(jnp.where(cj < 0, NEG, 0.0).astype(F32))
    for hh in range(n_heads):
        g = hh // hpg
        q = q_ref[0, :, hh * HEAD_DIM:(hh + 1) * HEAD_DIM].astype(BF16)
        ss = []
        for r in range(n_chunks):
            k = k_ref[0, pl.ds(starts[r], tq), g * HEAD_DIM:(g + 1) * HEAD_DIM].astype(BF16)
            ss.append(_dot_nt(q, k) * ATTN_SCALE + (w_ref[hh, n_chunks - 1 - r] + pens[r]))
        m = ss[0].max(axis=-1, keepdims=True)
        for r in range(1, n_chunks):
            m = jnp.maximum(m, ss[r].max(axis=-1, keepdims=True))
        den = jnp.zeros((tq, 1), F32)
        o = jnp.zeros((tq, HEAD_DIM), F32)
        for r in range(n_chunks):
            e = jnp.where(ss[r] > 0.5 * NEG, jnp.exp(ss[r] - m), 0.0)
            den = den + e.sum(axis=-1, keepdims=True)
            v = v_ref[0, pl.ds(starts[r], tq), g * HEAD_DIM:(g + 1) * HEAD_DIM].astype(BF16)
            o = o + _dot(e.astype(BF16), v)
        o_ref[0, :, hh * HEAD_DIM:(hh + 1) * HEAD_DIM] = o / jnp.where(den > 0, den, 1.0)


def _window_attn(q, k, v, wtiles, tq):
    b, t, qw = q.shape
    n_heads = qw // HEAD_DIM
    hpg = n_heads // (k.shape[2] // HEAD_DIM)
    n_chunks = wtiles.shape[1]
    return pl.pallas_call(
        functools.partial(_window_attn_kernel, tq=tq, n_chunks=n_chunks, n_heads=n_heads, hpg=hpg),
        grid=(b, t // tq),
        in_specs=[pl.BlockSpec((1, tq, qw), lambda bi, qi: (bi, qi, 0)),
                  pl.BlockSpec((1, t, k.shape[2]), lambda bi, qi: (bi, 0, 0)),
                  pl.BlockSpec((1, t, v.shape[2]), lambda bi, qi: (bi, 0, 0)),
                  pl.BlockSpec(wtiles.shape, lambda bi, qi: (0, 0, 0, 0))],
        out_specs=pl.BlockSpec((1, tq, qw), lambda bi, qi: (bi, qi, 0)),
        out_shape=jax.ShapeDtypeStruct((b, t, qw), F32),
        compiler_params=_cparams("parallel", "parallel"),
        name="window_attn",
    )(q, k, v, wtiles)


INT_MIN = -2 ** 31


def _topk_mask(key_ref, n_keep, tq, s_len):
    def body(i, thr_u):
        cand_u = thr_u | jnp.left_shift(jnp.int32(1), 31 - i)
        cnt = jnp.sum(jnp.where(key_ref[...] >= (cand_u ^ INT_MIN), 1, 0), axis=-1, keepdims=True)
        return jnp.where(cnt >= n_keep, cand_u, thr_u)

    thr_u = lax.fori_loop(0, 32, body, jnp.zeros((tq, 1), jnp.int32))
    thr = thr_u ^ INT_MIN
    return thr


def _index_select_kernel(iq_ref, mq_ref, mk_ref, o_ref, key_ref, *, tq, s_len, n_keep):
    qi = pl.program_id(1)
    n_ch = qi + 1
    t = qi * tq + lax.broadcasted_iota(jnp.int32, (tq, tq), 0)
    col = lax.broadcasted_iota(jnp.int32, (tq, tq), 1)
    wts = mq_ref[0, :, MISC_IW:MISC_IW + IDX_HEADS] * IDX_DIM ** -0.5
    iqs = [iq_ref[0, :, h * IDX_DIM:(h + 1) * IDX_DIM].astype(BF16) for h in range(IDX_HEADS)]
    chunk = lambda c: pl.ds(pl.multiple_of(c * tq, tq), tq)

    def score_chunk(c, carry):
        ik = mk_ref[0, chunk(c), 0:IDX_DIM].astype(BF16)
        score = jnp.zeros((tq, tq), F32)
        for h in range(IDX_HEADS):
            score = score + jnp.maximum(_dot_nt(iqs[h], ik), 0.0) * wts[:, h:h + 1]
        score = score * IDX_HEADS ** -0.5 + 0.0
        score = jnp.where(c * tq + col <= t, score, -jnp.inf)
        bits = pltpu.bitcast(score, jnp.int32)
        key_ref[:, chunk(c)] = jnp.where(bits < 0, bits ^ 0x7FFFFFFF, bits)
        return carry

    lax.fori_loop(0, n_ch, score_chunk, 0)

    def bit_step(i, thr_u):
        cand_u = thr_u | jnp.left_shift(jnp.int32(1), 31 - i)
        cand = cand_u ^ INT_MIN
        count = lambda c, n: n + jnp.sum(jnp.where(key_ref[:, chunk(c)] >= cand, 1, 0), axis=-1, keepdims=True)
        cnt = lax.fori_loop(0, n_ch, count, jnp.zeros((tq, 1), jnp.int32))
        return jnp.where(cnt >= n_keep, cand_u, thr_u)

    thr = lax.fori_loop(0, 32, bit_step, jnp.zeros((tq, 1), jnp.int32)) ^ INT_MIN
    o_ref[0] = jnp.zeros((tq, s_len), o_ref.dtype)

    def write_chunk(c, carry):
        n_gt, n_eq = carry
        key = key_ref[:, chunk(c)]
        gt = key > thr
        eq = key == thr
        o_ref[0, :, chunk(c)] = jnp.where(gt | eq, 1.0, 0.0).astype(o_ref.dtype)
        return (n_gt + jnp.sum(jnp.where(gt, 1, 0), axis=-1, keepdims=True),
                n_eq + jnp.sum(jnp.where(eq, 1, 0), axis=-1, keepdims=True))

    zero = jnp.zeros((tq, 1), jnp.int32)
    n_gt, n_eq = lax.fori_loop(0, n_ch, write_chunk, (zero, zero))
    need = n_keep - n_gt

    @pl.when(jnp.max(n_eq - need) > 0)
    def _():
        r_i = lax.broadcasted_iota(jnp.int32, (128, 128), 0)
        c_i = lax.broadcasted_iota(jnp.int32, (128, 128), 1)
        tri = jnp.where(r_i <= c_i, 1.0, 0.0).astype(BF16)
        need_f = need.astype(F32)

        def tie_chunk(c, before):
            key = key_ref[:, chunk(c)]
            keeps = []
            for u in range(tq // 128):
                sl = slice(u * 128, (u + 1) * 128)
                eq_c = key[:, sl] == thr
                eq_f = jnp.where(eq_c, 1.0, 0.0)
                pref = _dot(eq_f.astype(BF16), tri) + before
                keeps.append(jnp.where((key[:, sl] > thr) | (eq_c & (pref <= need_f)), 1.0, 0.0))
                before = before + jnp.sum(eq_f, axis=-1, keepdims=True)
            o_ref[0, :, chunk(c)] = jnp.concatenate(keeps, axis=-1).astype(o_ref.dtype)
            return before

        lax.fori_loop(0, n_ch, tie_chunk, jnp.zeros((tq, 1), F32))


def _index_select(iq, misc, tq, n_keep):
    b, t, _ = iq.shape
    return pl.pallas_call(
        functools.partial(_index_select_kernel, tq=tq, s_len=t, n_keep=n_keep),
        grid=(b, t // tq),
        in_specs=[pl.BlockSpec((1, tq, IDX_HEADS * IDX_DIM), lambda bi, qi: (bi, qi, 0)),
                  pl.BlockSpec((1, tq, MISC_W), lambda bi, qi: (bi, qi, 0)),
                  pl.BlockSpec((1, t, MISC_W), lambda bi, qi: (bi, 0, 0))],
        out_specs=pl.BlockSpec((1, tq, t), lambda bi, qi: (bi, qi, 0)),
        out_shape=jax.ShapeDtypeStruct((b, t, t), BF16),
        scratch_shapes=[pltpu.VMEM((tq, t), jnp.int32)],
        compiler_params=_cparams("parallel", "parallel"),
        name="index_select",
    )(iq, misc, misc)


def _merge_up_kernel(x_ref, oc_ref, os_ref, ow_ref, ob_ref, misc_ref, wga_ref, wgb_ref, wua_ref, wub_ref,
                     y_ref, oa_ref):
    ga = jax.nn.sigmoid(misc_ref[:, MISC_GA:MISC_GA + 3 * NSA_HEADS])
    for hh in range(NSA_HEADS):
        sl = slice(hh * HEAD_DIM, (hh + 1) * HEAD_DIM)
        oa = (ga[:, hh:hh + 1] * oc_ref[:, sl] + ga[:, NSA_HEADS + hh:NSA_HEADS + hh + 1] * os_ref[:, sl]
              + ga[:, 2 * NSA_HEADS + hh:2 * NSA_HEADS + hh + 1] * ow_ref[:, sl])
        oa_ref[:, sl] = oa.astype(BF16)
    x = x_ref[...]
    ya = _dot(oa_ref[...], wua_ref[...])
    yb = _dot(ob_ref[...].astype(BF16), wub_ref[...])
    g_a = jax.nn.sigmoid(_dot(x, wga_ref[...]))
    g_b = jax.nn.sigmoid(_dot(x, wgb_ref[...]))
    y_ref[...] = (g_a * ya + g_b * yb).astype(BF16)


def _merge_up(xb, o_cmp, o_sel, o_win, o_b, misc, wga, wgb, wua, wub, tm, tn):
    m = xb.shape[0]
    aw = NSA_HEADS * HEAD_DIM
    bw = DSA_HEADS * HEAD_DIM
    row = lambda w: pl.BlockSpec((tm, w), lambda j, i: (i, 0))
    wcol = lambda k: pl.BlockSpec((k, tn), lambda j, i: (0, j))
    return pl.pallas_call(
        _merge_up_kernel,
        grid=(D_MODEL // tn, m // tm),
        in_specs=[row(D_MODEL), row(aw), row(aw), row(aw), row(bw), row(MISC_W),
                  wcol(D_MODEL), wcol(D_MODEL), wcol(aw), wcol(bw)],
        out_specs=pl.BlockSpec((tm, tn), lambda j, i: (i, j)),
        out_shape=jax.ShapeDtypeStruct((m, D_MODEL), BF16),
        scratch_shapes=[pltpu.VMEM((tm, aw), BF16)],
        compiler_params=_cparams("parallel", "parallel"),
        name="merge_up",
    )(xb, o_cmp, o_sel, o_win, o_b, misc, wga, wgb, wua, wub)


def _layer_norm(z, g, b):
    mu = jnp.mean(z, axis=-1, keepdims=True)
    zc = z - mu
    var = jnp.mean(zc * zc, axis=-1, keepdims=True)
    return zc * lax.rsqrt(var + LN_EPS) * g + b


def _merge_out_kernel(x_ref, y_ref, wo_ref, g_ref, b_ref, o_ref, ob_ref):
    z = DEEPNORM_ALPHA * x_ref[...] + _dot(y_ref[...], wo_ref[...])
    x1 = _layer_norm(z, g_ref[...], b_ref[...])
    o_ref[...] = x1
    ob_ref[...] = x1.astype(BF16)


def _merge_out(x, y, wo, g, b, tm):
    m = x.shape[0]
    row = lambda: pl.BlockSpec((tm, D_MODEL), lambda i: (i, 0))
    return pl.pallas_call(
        _merge_out_kernel,
        grid=(m // tm,),
        in_specs=[row(), row(), pl.BlockSpec((D_MODEL, D_MODEL), lambda i: (0, 0)),
                  pl.BlockSpec((1, D_MODEL), lambda i: (0, 0)), pl.BlockSpec((1, D_MODEL), lambda i: (0, 0))],
        out_specs=[row(), row()],
        out_shape=[jax.ShapeDtypeStruct((m, D_MODEL), F32), jax.ShapeDtypeStruct((m, D_MODEL), BF16)],
        compiler_params=_cparams("parallel"),
        name="merge_out",
    )(x, y, wo, g, b)


def _router_kernel(x_ref, wr_ref, rb_ref, c0_ref, idx_ref, wt_ref, pos_ref, cnt_ref, run_ref, *, tm):
    epg = N_EXPERTS // N_EXPERT_GROUPS
    logits = lax.dot_general(wr_ref[...], x_ref[...], (((1,), (1,)), ((), ())),
                             preferred_element_type=F32, precision=lax.Precision.HIGHEST)
    scores = jax.nn.sigmoid(logits)
    biased = scores + rb_ref[...]
    sub = lax.broadcasted_iota(jnp.int32, (epg, tm), 0)
    gs_rows = []
    for r in range(N_EXPERT_GROUPS):
        bg = biased[r * epg:(r + 1) * epg, :]
        m1 = jnp.max(bg, axis=0, keepdims=True)
        i1 = jnp.min(jnp.where(bg == m1, sub, epg), axis=0, keepdims=True)
        m2 = jnp.max(jnp.where(sub == i1, -jnp.inf, bg), axis=0, keepdims=True)
        gs_rows.append(m1 + m2)
    gs = jnp.concatenate(gs_rows, axis=0)
    grow = lax.broadcasted_iota(jnp.int32, (N_EXPERT_GROUPS, tm), 0)
    rank = jnp.zeros((N_EXPERT_GROUPS, tm), jnp.int32)
    for k in range(N_EXPERT_GROUPS):
        rk = gs[k:k + 1, :]
        rank = rank + jnp.where((rk > gs) | ((rk == gs) & (k < grow)), 1, 0)
    gkeep = rank < TOPK_GROUPS
    masked = jnp.concatenate(
        [jnp.where(gkeep[r:r + 1, :], biased[r * epg:(r + 1) * epg, :], -jnp.inf) for r in range(N_EXPERT_GROUPS)],
        axis=0)
    erow = lax.broadcasted_iota(jnp.int32, (N_EXPERTS, tm), 0)
    idx_rows, w_rows, hits = [], [], []
    for _ in range(TOP_K):
        m = jnp.max(masked, axis=0, keepdims=True)
        ix = jnp.min(jnp.where(masked == m, erow, N_EXPERTS), axis=0, keepdims=True)
        hit = erow == ix
        w_rows.append(jnp.sum(jnp.where(hit, scores, 0.0), axis=0, keepdims=True))
        idx_rows.append(ix)
        hits.append(hit)
        masked = jnp.where(hit, -jnp.inf, masked)
    wsum = w_rows[0]
    for w in w_rows[1:]:
        wsum = wsum + w
    pad = 8 - TOP_K
    idx_ref[...] = jnp.concatenate(idx_rows + [jnp.zeros((pad, tm), jnp.int32)], axis=0)
    wt_ref[...] = jnp.concatenate([w / wsum * ROUTED_SCALE for w in w_rows] + [jnp.zeros((pad, tm), F32)], axis=0)

    @pl.when(pl.program_id(0) == 0)
    def _():
        run_ref[...] = c0_ref[...]

    earlier = (lax.broadcasted_iota(jnp.int32, (tm, tm), 0) < lax.broadcasted_iota(jnp.int32, (tm, tm), 1))
    earlier = jnp.where(earlier, 1.0, 0.0).astype(BF16)
    run = run_ref[...]
    pos_rows = []
    for hit in hits:
        onehot = jnp.where(hit, 1.0, 0.0)
        before = _dot(onehot.astype(BF16), earlier).astype(jnp.int32)
        pos_rows.append(jnp.sum(jnp.where(hit, run + before, 0), axis=0, keepdims=True))
        run = run + jnp.sum(onehot, axis=1, keepdims=True).astype(jnp.int32)
    run_ref[...] = run
    cnt_ref[...] = run
    pos_ref[...] = jnp.concatenate(pos_rows + [jnp.zeros((pad, tm), jnp.int32)], axis=0)


def _router(x1, wr_t, rb, counts0, tm):
    m = x1.shape[0]
    tok = lambda: pl.BlockSpec((8, tm), lambda i: (0, i))
    return pl.pallas_call(
        functools.partial(_router_kernel, tm=tm),
        grid=(m // tm,),
        in_specs=[pl.BlockSpec((tm, D_MODEL), lambda i: (i, 0)),
                  pl.BlockSpec((N_EXPERTS, D_MODEL), lambda i: (0, 0)),
                  pl.BlockSpec((N_EXPERTS, 1), lambda i: (0, 0)),
                  pl.BlockSpec((N_EXPERTS, 1), lambda i: (0, 0))],
        out_specs=[tok(), tok(), tok(), pl.BlockSpec((N_EXPERTS, 1), lambda i: (0, 0))],
        out_shape=[jax.ShapeDtypeStruct((8, m), jnp.int32), jax.ShapeDtypeStruct((8, m), F32),
                   jax.ShapeDtypeStruct((8, m), jnp.int32), jax.ShapeDtypeStruct((N_EXPERTS, 1), jnp.int32)],
        scratch_shapes=[pltpu.VMEM((N_EXPERTS, 1), jnp.int32)],
        compiler_params=_cparams("arbitrary"),
        name="router",
    )(x1, wr_t, rb, counts0)


def _expert_kernel(be_ref, nv_ref, x_ref, wg_ref, wu_ref, wd_ref, o_ref, wgb_ref, wub_ref, wdb_ref):
    i = pl.program_id(0)
    valid = i < nv_ref[0]
    new_expert = (i == 0) | (be_ref[i] != be_ref[jnp.maximum(i - 1, 0)])

    @pl.when(valid & new_expert)
    def _():
        wgb_ref[...] = wg_ref[0].astype(BF16)
        wub_ref[...] = wu_ref[0].astype(BF16)
        wdb_ref[...] = wd_ref[0].astype(BF16)

    @pl.when(valid)
    def _():
        x = x_ref[...]
        h = jax.nn.silu(_dot(x, wgb_ref[...])) * _dot(x, wub_ref[...])
        o_ref[...] = _dot(h.astype(BF16), wdb_ref[...]).astype(o_ref.dtype)

    @pl.when(jnp.logical_not(valid))
    def _():
        o_ref[...] = jnp.zeros_like(o_ref)


def _experts(blk_e, n_valid, xg, wg, wu, wd, tr):
    n_rows = xg.shape[0]
    n_blk = n_rows // tr
    grid_spec = pltpu.PrefetchScalarGridSpec(
        num_scalar_prefetch=2,
        grid=(n_blk,),
        in_specs=[pl.BlockSpec((tr, D_MODEL), lambda i, be, nv: (i, 0)),
                  pl.BlockSpec((1, D_MODEL, EXPERT_FF), lambda i, be, nv: (be[i], 0, 0)),
                  pl.BlockSpec((1, D_MODEL, EXPERT_FF), lambda i, be, nv: (be[i], 0, 0)),
                  pl.BlockSpec((1, EXPERT_FF, D_MODEL), lambda i, be, nv: (be[i], 0, 0))],
        out_specs=pl.BlockSpec((tr, D_MODEL), lambda i, be, nv: (i, 0)),
        scratch_shapes=[pltpu.VMEM((D_MODEL, EXPERT_FF), BF16), pltpu.VMEM((D_MODEL, EXPERT_FF), BF16),
                        pltpu.VMEM((EXPERT_FF, D_MODEL), BF16)],
    )
    return pl.pallas_call(
        _expert_kernel,
        grid_spec=grid_spec,
        out_shape=jax.ShapeDtypeStruct((n_rows, D_MODEL), BF16),
        compiler_params=_cparams("arbitrary"),
        name="experts",
    )(blk_e, n_valid, xg, wg, wu, wd)


def _ffn_out_kernel(x_ref, r_ref, w_ref, sg_ref, su_ref, sd_ref, g_ref, b_ref, o_ref):
    x = x_ref[...]
    xb = x.astype(BF16)
    h = jax.nn.silu(_dot(xb, sg_ref[...])) * _dot(xb, su_ref[...])
    f = _dot(h.astype(BF16), sd_ref[...])
    for k in range(TOP_K):
        f = f + r_ref[k].astype(F32) * w_ref[:, k:k + 1]
    o_ref[...] = _layer_norm(DEEPNORM_ALPHA * x + f, g_ref[...], b_ref[...])


def _ffn_out(x1, rows6, wts, sg, su, sd, g, b, tm):
    m = x1.shape[0]
    row = lambda: pl.BlockSpec((tm, D_MODEL), lambda i: (i, 0))
    full = lambda s: pl.BlockSpec(s, lambda i: (0, 0))
    return pl.pallas_call(
        _ffn_out_kernel,
        grid=(m // tm,),
        in_specs=[row(), pl.BlockSpec((TOP_K, tm, D_MODEL), lambda i: (0, i, 0)), pl.BlockSpec((tm, 8), lambda i: (i, 0)),
                  full((D_MODEL, SHARED_FF)), full((D_MODEL, SHARED_FF)), full((SHARED_FF, D_MODEL)),
                  full((1, D_MODEL)), full((1, D_MODEL))],
        out_specs=row(),
        out_shape=jax.ShapeDtypeStruct((m, D_MODEL), F32),
        compiler_params=_cparams("parallel"),
        name="ffn_out",
    )(x1, rows6, wts, sg, su, sd, g, b)


def _rel_bucket(dist):
    n = jnp.maximum(dist, 0)
    nf = jnp.maximum(n, 1).astype(F32)
    large = MAX_EXACT + (jnp.log(nf / MAX_EXACT) / math.log(MAX_DISTANCE / MAX_EXACT)
                         * (N_BUCKETS - MAX_EXACT)).astype(jnp.int32)
    return jnp.where(n < MAX_EXACT, n, jnp.minimum(large, N_BUCKETS - 1))


def _toeplitz_bias(tbl, tq, n_tiles):
    i = jnp.arange(tq)[:, None]
    j = jnp.arange(tq)[None, :]
    dist = jnp.arange(n_tiles)[:, None, None] * tq + (i - j)[None]
    bias = jnp.moveaxis(tbl[_rel_bucket(dist)], -1, 0)
    return bias, dist


PAGE_GROUP = 16


def _by_group(hpg, a0, a1):
    row = lax.broadcasted_iota(jnp.int32, (2 * hpg, 1), 0)
    return jnp.where(row < hpg, a0, a1)


def _order_key(x):
    bits = pltpu.bitcast(x, jnp.int32)
    return jnp.where(bits < 0, bits ^ 0x7FFFFFFF, bits)


def _row_topk(key_ref, keep_ref, n_keep, width):
    thr = _topk_mask(key_ref, n_keep, 1, width)
    key = key_ref[...]
    gt = key > thr
    eq = key == thr
    need = n_keep - jnp.sum(jnp.where(gt, 1, 0), axis=-1, keepdims=True)
    n_eq = jnp.sum(jnp.where(eq, 1, 0), axis=-1, keepdims=True)
    keep_ref[...] = jnp.where(gt | eq, 1.0, 0.0)

    @pl.when(jnp.max(n_eq - need) > 0)
    def _():
        r_i = lax.broadcasted_iota(jnp.int32, (128, 128), 0)
        c_i = lax.broadcasted_iota(jnp.int32, (128, 128), 1)
        tri = jnp.where(r_i <= c_i, 1.0, 0.0).astype(BF16)
        need_f = need.astype(F32)

        def body(c, before):
            off = pl.multiple_of(c * 128, 128)
            kc = key_ref[:, pl.ds(off, 128)]
            eq_c = kc == thr
            eq_f = jnp.where(eq_c, 1.0, 0.0)
            pref = _dot(jnp.broadcast_to(eq_f, (8, 128)).astype(BF16), tri)[0:1] + before
            keep_ref[:, pl.ds(off, 128)] = jnp.where((kc > thr) | (eq_c & (pref <= need_f)), 1.0, 0.0)
            return before + jnp.sum(eq_f, axis=-1, keepdims=True)

        lax.fori_loop(0, width // 128, body, jnp.zeros((1, 1), F32))


def _s_cmp_win_kernel(q_ref, kc_ref, vc_ref, cb_ref, wk_ref, wv_ref, nk_ref, nv_ref, wb_ref, b0_ref,
                      ocmp_ref, owin_ref, sel_ref, key_ref, keep_ref, *, nbc, n_blocks, width, n_sel, wlen):
    hpg = NSA_HPG
    qf = q_ref[0]
    q = qf.astype(BF16)
    row = lax.broadcasted_iota(jnp.int32, (NSA_HEADS, 1), 0)
    g0 = row < hpg
    kc = kc_ref[0]
    vc = vc_ref[0]
    cb = cb_ref[...]
    valid = cb > 0.5 * NEG
    s = _by_group(hpg, _dot_nt(q, kc[:, :HEAD_DIM].astype(BF16)), _dot_nt(q, kc[:, HEAD_DIM:].astype(BF16)))
    s = s * ATTN_SCALE + cb
    m = jnp.max(s, axis=-1, keepdims=True)
    e = jnp.where(valid, jnp.exp(s - m), 0.0)
    den = jnp.sum(e, axis=-1, keepdims=True)
    p = e / jnp.where(den > 0, den, 1.0)
    ocmp_ref[0] = (_dot(jnp.where(g0, p, 0.0).astype(BF16), vc[:, :HEAD_DIM].astype(BF16))
                   + _dot(jnp.where(g0, 0.0, p).astype(BF16), vc[:, HEAD_DIM:].astype(BF16)))
    lane = lax.broadcasted_iota(jnp.int32, (1, width), 1)
    cur = n_blocks - 1
    forced = (lane == 0) | (lane == cur) | (lane == cur - 1)
    for g in range(NSA_KV_GROUPS):
        imp = jnp.sum(jnp.where(g0 if g == 0 else jnp.logical_not(g0), p, 0.0), axis=0, keepdims=True)
        impw = jnp.concatenate([imp, jnp.zeros((1, width - nbc), F32)], axis=1)
        score = jnp.where(forced, jnp.inf, jnp.where(lane <= cur, impw, -jnp.inf))
        key_ref[...] = _order_key(score)
        _row_topk(key_ref, keep_ref, n_sel, width)
        sel_ref[0, g:g + 1, :] = keep_ref[...]
    wk0 = wk_ref[pl.ds(0, wlen, stride=2), :].astype(BF16)
    wk1 = wk_ref[pl.ds(1, wlen, stride=2), :].astype(BF16)
    wv0 = wv_ref[pl.ds(0, wlen, stride=2), :].astype(BF16)
    wv1 = wv_ref[pl.ds(1, wlen, stride=2), :].astype(BF16)
    wb = wb_ref[...]
    s = _by_group(hpg, _dot_nt(q, wk0), _dot_nt(q, wk1)) * ATTN_SCALE + wb
    nk = _by_group(hpg, nk_ref[0][:, :HEAD_DIM], nk_ref[0][:, HEAD_DIM:])
    nv = _by_group(hpg, nv_ref[0][:, :HEAD_DIM], nv_ref[0][:, HEAD_DIM:])
    s_new = jnp.sum(qf * nk, axis=-1, keepdims=True) * ATTN_SCALE + b0_ref[...]
    m = jnp.maximum(jnp.max(s, axis=-1, keepdims=True), s_new)
    e = jnp.where(wb > 0.5 * NEG, jnp.exp(s - m), 0.0)
    e_new = jnp.exp(s_new - m)
    den = jnp.sum(e, axis=-1, keepdims=True) + e_new
    o = (_dot(jnp.where(g0, e, 0.0).astype(BF16), wv0) + _dot(jnp.where(g0, 0.0, e).astype(BF16), wv1)
         + e_new * nv)
    owin_ref[0] = o / den


def _s_cmp_win(q, kcomp, vcomp, cbias, wk, wv, nk, nv, wbias, b0, n_blocks, n_sel):
    db = q.shape[0]
    nbc = kcomp.shape[1]
    wlen = wk.shape[1] // 2
    width = -(-n_blocks // 128) * 128
    one = lambda *s: pl.BlockSpec((1,) + s, lambda bi: (bi,) + (0,) * len(s))
    const = lambda a: pl.BlockSpec(a.shape, lambda bi: (0,) * a.ndim)
    return pl.pallas_call(
        functools.partial(_s_cmp_win_kernel, nbc=nbc, n_blocks=n_blocks, width=width, n_sel=n_sel, wlen=wlen),
        grid=(db,),
        in_specs=[one(NSA_HEADS, HEAD_DIM), one(nbc, KV_W), one(nbc, KV_W), const(cbias),
                  pl.BlockSpec((None, 2 * wlen, HEAD_DIM), lambda bi: (bi, 0, 0)),
                  pl.BlockSpec((None, 2 * wlen, HEAD_DIM), lambda bi: (bi, 0, 0)),
                  one(1, KV_W), one(1, KV_W), const(wbias), const(b0)],
        out_specs=[one(NSA_HEADS, HEAD_DIM), one(NSA_HEADS, HEAD_DIM), one(NSA_KV_GROUPS, width)],
        out_shape=[jax.ShapeDtypeStruct((db, NSA_HEADS, HEAD_DIM), F32),
                   jax.ShapeDtypeStruct((db, NSA_HEADS, HEAD_DIM), F32),
                   jax.ShapeDtypeStruct((db, NSA_KV_GROUPS, width), F32)],
        scratch_shapes=[pltpu.VMEM((1, width), jnp.int32), pltpu.VMEM((1, width), F32)],
        compiler_params=_cparams("parallel"),
        name="sample_cmp_win",
    )(q, kcomp, vcomp, cbias, wk, wv, nk, nv, wbias, b0)


def _s_sel_kernel(idx_ref, hp_ref, q_ref, nk_ref, nv_ref, tb_ref, c_ref, *rest, n_sel, cur, n_near):
    del hp_ref
    k_refs, v_refs, o_ref = rest[:n_sel], rest[n_sel:2 * n_sel], rest[2 * n_sel]
    bi = pl.program_id(0)
    g = pl.program_id(1)
    q = q_ref[0].astype(BF16)
    nk = jnp.where(g == 0, nk_ref[0][:, :HEAD_DIM], nk_ref[0][:, HEAD_DIM:])
    nv = jnp.where(g == 0, nv_ref[0][:, :HEAD_DIM], nv_ref[0][:, HEAD_DIM:])
    rowi = lax.broadcasted_iota(jnp.int32, (SEL_BLOCK, 1), 0)
    ss, vs = [], []
    for r in range(n_sel):
        idx = idx_ref[(bi * NSA_KV_GROUPS + g) * n_sel + r]
        first = (rowi == 0) & (idx >= cur)
        kr = jnp.where(first, nk, k_refs[r][pl.ds(g, SEL_BLOCK, stride=2), :]).astype(BF16)
        vs.append(jnp.where(first, nv, v_refs[r][pl.ds(g, SEL_BLOCK, stride=2), :]).astype(BF16))
        u = jnp.clip(idx - (cur - (n_near - 1)), 0, n_near - 1)
        bias = jnp.where(idx >= cur - (n_near - 1), tb_ref[u], c_ref[...])
        ss.append(_dot_nt(q, kr) * ATTN_SCALE + bias)
    m = ss[0].max(axis=-1, keepdims=True)
    for s in ss[1:]:
        m = jnp.maximum(m, s.max(axis=-1, keepdims=True))
    den = jnp.zeros((NSA_HEADS, 1), F32)
    o = jnp.zeros((NSA_HEADS, HEAD_DIM), F32)
    for s, v in zip(ss, vs):
        e = jnp.where(s > 0.5 * NEG, jnp.exp(s - m), 0.0)
        den = den + e.sum(axis=-1, keepdims=True)
        o = o + _dot(e.astype(BF16), v)
    o_ref[0, 0] = o / jnp.where(den > 0, den, 1.0)


def _s_sel(idx_flat, hp_flat, q, nk, nv, tb, c, cache_k, cache_v, n_sel, cur):
    db = q.shape[0]
    rows = SEL_BLOCK * NSA_KV_GROUPS

    def page(r):
        return pl.BlockSpec((None, rows, HEAD_DIM),
                            lambda bi, g, idx, hp, r=r: (hp[(bi * NSA_KV_GROUPS + g) * n_sel + r], 0, 0))

    grid_spec = pltpu.PrefetchScalarGridSpec(
        num_scalar_prefetch=2,
        grid=(db, NSA_KV_GROUPS),
        in_specs=[pl.BlockSpec((1, NSA_HEADS, HEAD_DIM), lambda bi, g, idx, hp: (bi, 0, 0)),
                  pl.BlockSpec((1, 1, KV_W), lambda bi, g, idx, hp: (bi, 0, 0)),
                  pl.BlockSpec((1, 1, KV_W), lambda bi, g, idx, hp: (bi, 0, 0)),
                  pl.BlockSpec(tb.shape, lambda bi, g, idx, hp: (0, 0, 0)),
                  pl.BlockSpec(c.shape, lambda bi, g, idx, hp: (0, 0))]
        + [page(r) for r in range(n_sel)] + [page(r) for r in range(n_sel)],
        out_specs=pl.BlockSpec((1, 1, NSA_HEADS, HEAD_DIM), lambda bi, g, idx, hp: (bi, g, 0, 0)),
    )
    return pl.pallas_call(
        functools.partial(_s_sel_kernel, n_sel=n_sel, cur=cur, n_near=tb.shape[0]),
        grid_spec=grid_spec,
        out_shape=jax.ShapeDtypeStruct((db, NSA_KV_GROUPS, NSA_HEADS, HEAD_DIM), F32),
        compiler_params=_cparams("parallel", "parallel"),
        name="sample_sel",
    )(idx_flat, hp_flat, q, nk, nv, tb, c, *([cache_k] * n_sel), *([cache_v] * n_sel))


def _s_index_kernel(pt_ref, iq_ref, iw_ref, nik_ref, *rest, pg, past, n_keep, width):
    del pt_ref
    pages, keep_ref, sc_ref, key_ref = rest[:pg], rest[pg], rest[pg + 1], rest[pg + 2]
    j = pl.program_id(1)
    span = pg * PAGE_SIZE
    iqf = iq_ref[0]
    iw = iw_ref[0]
    ik = jnp.concatenate([p[...] for p in pages], axis=0).astype(BF16)
    lg = jnp.maximum(_dot_nt(iqf.astype(BF16), ik) * IDX_DIM ** -0.5, 0.0)
    sc = jnp.sum(lg * iw, axis=0, keepdims=True) * IDX_HEADS ** -0.5 + 0.0
    sc_ref[:, pl.ds(pl.multiple_of(j * span, span), span)] = sc

    @pl.when(j == pl.num_programs(1) - 1)
    def _():
        lg_new = jnp.maximum(jnp.sum(iqf * nik_ref[0], axis=-1, keepdims=True) * IDX_DIM ** -0.5, 0.0)
        sc_new = jnp.sum(lg_new * iw, axis=0, keepdims=True) * IDX_HEADS ** -0.5 + 0.0
        lane = lax.broadcasted_iota(jnp.int32, (1, width - past), 1)
        sc_ref[:, past:] = jnp.where(lane == 0, sc_new, -jnp.inf)
        key_ref[...] = _order_key(sc_ref[...])
        _row_topk(key_ref, keep_ref.at[0], n_keep, width)


def _s_index(page_table, iq, iw, nik, cache_idx, n_keep):
    db, n_pages = page_table.shape
    pg = math.gcd(n_pages, PAGE_GROUP)
    past = n_pages * PAGE_SIZE
    width = past + 128

    def page(r):
        return pl.BlockSpec((None, PAGE_SIZE, IDX_DIM), lambda bi, j, pt, r=r: (pt[bi, j * pg + r], 0, 0))

    grid_spec = pltpu.PrefetchScalarGridSpec(
        num_scalar_prefetch=1,
        grid=(db, n_pages // pg),
        in_specs=[pl.BlockSpec((1, IDX_HEADS, IDX_DIM), lambda bi, j, pt: (bi, 0, 0)),
                  pl.BlockSpec((1, IDX_HEADS, 1), lambda bi, j, pt: (bi, 0, 0)),
                  pl.BlockSpec((1, 1, IDX_DIM), lambda bi, j, pt: (bi, 0, 0))]
        + [page(r) for r in range(pg)],
        out_specs=pl.BlockSpec((1, 1, width), lambda bi, j, pt: (bi, 0, 0)),
        scratch_shapes=[pltpu.VMEM((1, width), F32), pltpu.VMEM((1, width), jnp.int32)],
    )
    return pl.pallas_call(
        functools.partial(_s_index_kernel, pg=pg, past=past, n_keep=n_keep, width=width),
        grid_spec=grid_spec,
        out_shape=jax.ShapeDtypeStruct((db, 1, width), F32),
        compiler_params=_cparams("parallel", "arbitrary"),
        name="sample_index",
    )(page_table, iq, iw, nik, *([cache_idx] * pg))


def _s_dsa_kernel(pt_ref, q_ref, keep_ref, bt_ref, c_ref, b0_ref, nk_ref, nv_ref, *rest, pg, past):
    del pt_ref
    kp, vp, o_ref = rest[:pg], rest[pg:2 * pg], rest[2 * pg]
    m_ref, l_ref, acc_ref = rest[2 * pg + 1:]
    hpg = DSA_HPG
    j = pl.program_id(1)
    last = pl.num_programs(1) - 1
    span = pg * PAGE_SIZE

    @pl.when(j == 0)
    def _():
        m_ref[...] = jnp.full_like(m_ref, NEG)
        l_ref[...] = jnp.zeros_like(l_ref)
        acc_ref[...] = jnp.zeros_like(acc_ref)

    qf = q_ref[0]
    q = qf.astype(BF16)
    row = lax.broadcasted_iota(jnp.int32, (DSA_HEADS, 1), 0)
    g0 = row < hpg
    rows = lambda refs, g: jnp.concatenate([p[pl.ds(g, PAGE_SIZE, stride=2), :] for p in refs], axis=0).astype(BF16)
    s = _by_group(hpg, _dot_nt(q, rows(kp, 0)), _dot_nt(q, rows(kp, 1))) * ATTN_SCALE
    keep = keep_ref[0, :, pl.ds(pl.multiple_of(j * span, span), span)] > 0.5
    s = jnp.where(keep, s + jnp.where(j == last, bt_ref[...], c_ref[...]), NEG)
    m_old = m_ref[...]
    m_new = jnp.maximum(m_old, jnp.max(s, axis=-1, keepdims=True))
    alpha = jnp.exp(m_old - m_new)
    e = jnp.where(keep, jnp.exp(s - m_new), 0.0)
    l_ref[...] = alpha * l_ref[...] + jnp.sum(e, axis=-1, keepdims=True)
    acc_ref[...] = (alpha * acc_ref[...] + _dot(jnp.where(g0, e, 0.0).astype(BF16), rows(vp, 0))
                    + _dot(jnp.where(g0, 0.0, e).astype(BF16), rows(vp, 1)))
    m_ref[...] = m_new

    @pl.when(j == last)
    def _():
        keep_new = keep_ref[0, :, past:past + 1] > 0.5
        nk = _by_group(hpg, nk_ref[0][:, :HEAD_DIM], nk_ref[0][:, HEAD_DIM:])
        nv = _by_group(hpg, nv_ref[0][:, :HEAD_DIM], nv_ref[0][:, HEAD_DIM:])
        s_new = jnp.sum(qf * nk, axis=-1, keepdims=True) * ATTN_SCALE + b0_ref[...]
        s_new = jnp.where(keep_new, s_new, NEG)
        m_old2 = m_ref[...]
        m2 = jnp.maximum(m_old2, s_new)
        a2 = jnp.exp(m_old2 - m2)
        e_new = jnp.where(keep_new, jnp.exp(s_new - m2), 0.0)
        l = a2 * l_ref[...] + e_new
        o_ref[0] = (a2 * acc_ref[...] + e_new * nv) / jnp.where(l > 0, l, 1.0)


def _s_dsa(page_table, q, keep, btail, c, b0, nk, nv, cache_k, cache_v):
    db, n_pages = page_table.shape
    pg = math.gcd(n_pages, PAGE_GROUP)
    past = n_pages * PAGE_SIZE
    width = keep.shape[-1]
    rows = PAGE_SIZE * DSA_KV_GROUPS

    def page(r):
        return pl.BlockSpec((None, rows, HEAD_DIM), lambda bi, j, pt, r=r: (pt[bi, j * pg + r], 0, 0))

    grid_spec = pltpu.PrefetchScalarGridSpec(
        num_scalar_prefetch=1,
        grid=(db, n_pages // pg),
        in_specs=[pl.BlockSpec((1, DSA_HEADS, HEAD_DIM), lambda bi, j, pt: (bi, 0, 0)),
                  pl.BlockSpec((1, 1, width), lambda bi, j, pt: (bi, 0, 0)),
                  pl.BlockSpec(btail.shape, lambda bi, j, pt: (0, 0)),
                  pl.BlockSpec(c.shape, lambda bi, j, pt: (0, 0)),
                  pl.BlockSpec(b0.shape, lambda bi, j, pt: (0, 0)),
                  pl.BlockSpec((1, 1, KV_W), lambda bi, j, pt: (bi, 0, 0)),
                  pl.BlockSpec((1, 1, KV_W), lambda bi, j, pt: (bi, 0, 0))]
        + [page(r) for r in range(pg)] + [page(r) for r in range(pg)],
        out_specs=pl.BlockSpec((1, DSA_HEADS, HEAD_DIM), lambda bi, j, pt: (bi, 0, 0)),
        scratch_shapes=[pltpu.VMEM((DSA_HEADS, 1), F32), pltpu.VMEM((DSA_HEADS, 1), F32),
                        pltpu.VMEM((DSA_HEADS, HEAD_DIM), F32)],
    )
    return pl.pallas_call(
        functools.partial(_s_dsa_kernel, pg=pg, past=past),
        grid_spec=grid_spec,
        out_shape=jax.ShapeDtypeStruct((db, DSA_HEADS, HEAD_DIM), F32),
        compiler_params=_cparams("parallel", "arbitrary"),
        name="sample_dsa",
    )(page_table, q, keep, btail, c, b0, nk, nv, *([cache_k] * pg), *([cache_v] * pg))


def _split_w_in(w_in):
    points = np.cumsum(IN_COLS)[:-1].tolist()
    q_a, kv_a, g_a, q_b, kv_b, iq, ik, iw, g_m = jnp.split(w_in, points, axis=-1)
    pad = jnp.zeros((D_MODEL, MISC_W - IDX_DIM - IDX_HEADS - 3 * NSA_HEADS), w_in.dtype)
    w_kv = jnp.concatenate([kv_a, kv_b, ik, iw, g_a, pad], axis=-1).astype(BF16)
    w_q = jnp.concatenate([q_a, q_b, iq], axis=-1).astype(BF16)
    return w_kv, w_q, g_m[:, :D_MODEL].astype(BF16), g_m[:, D_MODEL:].astype(BF16)


KV_WIDTHS = (KV_W,) * 8 + (MISC_W,)
Q_WIDTHS = (NSA_HEADS * HEAD_DIM, DSA_HEADS * HEAD_DIM, IDX_HEADS * IDX_DIM)


def _row_tile(m, cap):
    tm = math.gcd(m, cap)
    assert tm % 8 == 0 or tm == m
    return tm


def kernel(x_prompt, x_sample, cache_cmp_k, cache_cmp_v, cache_slc_k, cache_slc_v, state_win_k, state_win_v,
           cache_dsa_k, cache_dsa_v, cache_idx_k, page_table, rel_bias_table, w_in, cmp_pe, cmp_w1, cmp_w2,
           w_up_a, w_up_b, w_o, ln1_g, ln1_b, w_router, router_bias, moe_w_gate, moe_w_up, moe_w_down,
           sh_w_gate, sh_w_up, sh_w_down, ln2_g, ln2_b):
    assert w_in.shape[0] == DEPTH == 1
    b, t, _ = x_prompt.shape
    db, dt, _ = x_sample.shape
    assert dt == 1
    n_pool = cache_cmp_k.shape[1]
    n_pages = page_table.shape[1]
    tq = Q_TILE
    assert t % tq == 0 and t % CMP_BLOCK == 0
    n_p, n_s = b * t, db * dt

    tbl_a = rel_bias_table[:, :NSA_HEADS]
    tbl_b = rel_bias_table[:, NSA_HEADS:]
    w_kv, w_q, w_ga, w_gb = _split_w_in(w_in[0])
    w_ua, w_ub, w_ob = w_up_a[0].astype(BF16), w_up_b[0].astype(BF16), w_o[0].astype(BF16)
    cw = [_compress_weights(cmp_pe[0, i], cmp_w1[0, i], cmp_w2[0, i]) for i in range(2)]

    xp = x_prompt.reshape(n_p, D_MODEL)
    xs = x_sample.reshape(n_s, D_MODEL)
    xpb, xsb = xp.astype(BF16), xs.astype(BF16)
    tm_p = _row_tile(n_p, 512)
    kc, vc, ks, vs, kw, vw, kb, vb, misc = _project(xpb, w_kv, KV_WIDTHS, tm_p)
    qa, qb, iq = _project(xpb, w_q, Q_WIDTHS, tm_p)
    s_kc, s_vc, s_ks, s_vs, s_kw, s_vw, s_kb, s_vb, s_misc = _project(xsb, w_kv, KV_WIDTHS, n_s)
    s_qa, s_qb, s_iq = _project(xsb, w_q, Q_WIDTHS, n_s)

    nb = t // CMP_BLOCK
    r3 = lambda a: a.reshape(b, t, -1)
    kcomp = _compress(kc.reshape(b * nb, CMP_BLOCK, KV_W), *cw[0]).reshape(b, nb, KV_W)
    vcomp = _compress(vc.reshape(b * nb, CMP_BLOCK, KV_W), *cw[1]).reshape(b, nb, KV_W)
    cdist = jnp.arange(t)[:, None] - (jnp.arange(nb) * CMP_BLOCK + CMP_BLOCK - 1)[None, :]
    cbias = jnp.where(cdist >= 0, jnp.moveaxis(tbl_a[_rel_bucket(cdist)], -1, 0), NEG)
    o_cmp, selmask = _cmp_select(r3(qa), kcomp, vcomp, cbias, tq)

    assert MAX_DISTANCE <= tq
    bias2_a, dist2 = _toeplitz_bias(tbl_a, tq, 2)
    bias2_b, _ = _toeplitz_bias(tbl_b, tq, 2)
    c_a, c_b = tbl_a[N_BUCKETS - 1], tbl_b[N_BUCKETS - 1]
    d_a = jnp.where(dist2 >= 0, bias2_a - c_a[:, None, None, None], NEG)
    d_b = jnp.where(dist2 >= 0, bias2_b - c_b[:, None, None, None], NEG)
    expand = (jnp.arange(t)[None, :] // SEL_BLOCK == jnp.arange(nb)[:, None]).astype(BF16)
    o_sel = _dense_attn(r3(qa), r3(ks), r3(vs), selmask, expand, c_a, d_a, tq, "sel")

    n_wchunks = -(-(WINDOW - 1) // tq) + 1
    bias_w, dist_w = _toeplitz_bias(tbl_a, tq, n_wchunks)
    wtiles = jnp.where((dist_w >= 0) & (dist_w < WINDOW), bias_w, NEG)
    o_win = _window_attn(r3(qa), r3(kw), r3(vw), wtiles, tq)

    n_keep = min(DSA_TOPK, t // 4)
    keepmask = _index_select(r3(iq), r3(misc), tq, n_keep)
    o_b = _dense_attn(r3(qb), r3(kb), r3(vb), keepmask, jnp.zeros((8, 128), BF16), c_b, d_b, tq, "dsa")

    tm_m = _row_tile(n_p, 256)
    y_p = _merge_up(xpb, o_cmp.reshape(n_p, -1), o_sel.reshape(n_p, -1), o_win.reshape(n_p, -1),
                    o_b.reshape(n_p, -1), misc, w_ga, w_gb, w_ua, w_ub, tm_m, 1024)
    x1p, x1pb = _merge_out(xp, y_p, w_ob, ln1_g, ln1_b, tm_m)

    past = n_pages * PAGE_SIZE
    halves = PAGE_SIZE // CMP_BLOCK

    def comp_pool(cache, i):
        c = _compress_pool(cache, cmp_pe[0, i], cmp_w1[0, i], cmp_w2[0, i])
        return c.reshape(n_pool, halves * KV_W)[page_table].reshape(db, n_pages * halves, KV_W)

    s_kcomp = comp_pool(cache_cmp_k[0], 0)
    s_vcomp = comp_pool(cache_cmp_v[0], 1)
    assert past % SEL_BLOCK == 0 and past >= 4 * SEL_BLOCK and PAGE_SIZE == 2 * SEL_BLOCK
    total = past + dt
    nbc = past // CMP_BLOCK
    n_blocks = -(-total // SEL_BLOCK)
    cur = past // SEL_BLOCK
    n_sel = min(N_SEL_BLOCKS, n_blocks)
    col = lambda v: v.reshape(-1, 1)
    new = lambda a: a.reshape(db, 1, KV_W)
    cb_s = tbl_a[_rel_bucket(past - (jnp.arange(nbc) * CMP_BLOCK + CMP_BLOCK - 1))].T
    w_past = state_win_k.shape[2]
    wdist = w_past - jnp.arange(w_past)
    wb_s = jnp.where(wdist < WINDOW, tbl_a[_rel_bucket(wdist)].T, NEG)
    rows2 = lambda a, n: a.reshape(a.shape[0], n * NSA_KV_GROUPS, HEAD_DIM)
    s_q8 = s_qa.reshape(db, NSA_HEADS, HEAD_DIM)
    so_cmp, so_win, selmask = _s_cmp_win(
        s_q8, s_kcomp, s_vcomp, cb_s, rows2(state_win_k[0], w_past), rows2(state_win_v[0], w_past),
        new(s_kw), new(s_vw), wb_s, col(tbl_a[0]), n_blocks, n_sel)
    sel_idx = lax.top_k(selmask[:, :, :n_blocks], n_sel)[1].astype(jnp.int32)
    sel_page = jnp.take_along_axis(page_table, jnp.minimum(sel_idx // 2, n_pages - 1).reshape(db, -1), axis=1)
    sel_hp = sel_page.reshape(sel_idx.shape) * 2 + sel_idx % 2
    n_near = 4
    ndist = (n_near - 1 - jnp.arange(n_near))[:, None] * SEL_BLOCK - jnp.arange(SEL_BLOCK)[None, :]
    tb_s = jnp.where(ndist[:, None, :] >= 0, jnp.moveaxis(tbl_a[_rel_bucket(ndist)], -1, 1), NEG)
    half_pages = lambda c: c.reshape(n_pool * halves, SEL_BLOCK * NSA_KV_GROUPS, HEAD_DIM)
    so_sel2 = _s_sel(sel_idx.reshape(-1), sel_hp.reshape(-1).astype(jnp.int32), s_q8, new(s_ks), new(s_vs), tb_s,
                     col(tbl_a[N_BUCKETS - 1]), half_pages(cache_slc_k[0]), half_pages(cache_slc_v[0]), n_sel, cur)
    so_sel = jnp.concatenate([so_sel2[:, 0, :NSA_HPG], so_sel2[:, 1, NSA_HPG:]], axis=1)
    keep_s = _s_index(page_table, s_iq.reshape(db, IDX_HEADS, IDX_DIM),
                      s_misc[:, MISC_IW:MISC_IW + IDX_HEADS].reshape(db, IDX_HEADS, 1),
                      s_misc[:, :IDX_DIM].reshape(db, 1, IDX_DIM), cache_idx_k[0], min(DSA_TOPK, total // 4))
    span = math.gcd(n_pages, PAGE_GROUP) * PAGE_SIZE
    assert span >= MAX_DISTANCE
    bt_s = tbl_b[_rel_bucket(span - jnp.arange(span))].T
    pages2 = lambda c: c.reshape(n_pool, PAGE_SIZE * DSA_KV_GROUPS, HEAD_DIM)
    so_b = _s_dsa(page_table, s_qb.reshape(db, DSA_HEADS, HEAD_DIM), keep_s, bt_s, col(tbl_b[N_BUCKETS - 1]),
                  col(tbl_b[0]), new(s_kb), new(s_vb), pages2(cache_dsa_k[0]), pages2(cache_dsa_v[0]))
    n_win = min(WINDOW, total)
    g4 = lambda a: a.reshape(db, dt, NSA_KV_GROUPS, HEAD_DIM)
    s_wk = jnp.concatenate([state_win_k[0], g4(s_kw)], axis=1)[:, -n_win:]
    s_wv = jnp.concatenate([state_win_v[0], g4(s_vw)], axis=1)[:, -n_win:]
    y_s = _merge_up(xsb, so_cmp.reshape(n_s, -1), so_sel.reshape(n_s, -1), so_win.reshape(n_s, -1),
                    so_b.reshape(n_s, -1), s_misc, w_ga, w_gb, w_ua, w_ub, n_s, 1024)
    x1s, x1sb = _merge_out(xs, y_s, w_ob, ln1_g, ln1_b, n_s)

    wr_t = w_router[0].T
    rb = router_bias[0].reshape(N_EXPERTS, 1)
    eidx_p, wts_p, pos_p, cnt_p = _router(x1p, wr_t, rb, jnp.zeros((N_EXPERTS, 1), jnp.int32), tm_m)
    eidx_s, wts_s, pos_s, cnt = _router(x1s, wr_t, rb, cnt_p, n_s)
    n_tok = n_p + n_s
    eidx = jnp.concatenate([eidx_p[:TOP_K], eidx_s[:TOP_K]], axis=1)
    pos = jnp.concatenate([pos_p[:TOP_K], pos_s[:TOP_K]], axis=1)
    tr = EXPERT_ROWS
    n_asg = n_tok * TOP_K
    counts = cnt[:, 0]
    padded = (counts + tr - 1) // tr * tr
    pend = jnp.cumsum(padded)
    pad_start = pend - padded
    n_rows = -(-n_asg // tr) * tr + N_EXPERTS * tr
    n_blk = n_rows // tr
    dest = pos + jnp.sum(jnp.where(eidx[..., None] == jnp.arange(N_EXPERTS), pad_start, 0), axis=-1)
    blk_e = jnp.minimum(jnp.sum(pend[None, :] <= (jnp.arange(n_blk) * tr)[:, None], axis=1), N_EXPERTS - 1)
    n_valid = (pend[-1] // tr).astype(jnp.int32).reshape(1)
    row_tok = jnp.full((n_rows,), n_tok, jnp.int32).at[dest.reshape(-1)].set(
        jnp.tile(jnp.arange(n_tok, dtype=jnp.int32), TOP_K), unique_indices=True)
    x1b_all = jnp.concatenate([x1pb, x1sb, jnp.zeros((1, D_MODEL), BF16)], axis=0)
    xg = x1b_all[row_tok]
    out_rows = _experts(blk_e.astype(jnp.int32), n_valid, xg, moe_w_gate[0], moe_w_up[0], moe_w_down[0], tr)
    rows6 = out_rows[dest]

    sg, su, sd = sh_w_gate[0].astype(BF16), sh_w_up[0].astype(BF16), sh_w_down[0].astype(BF16)
    y_prompt = _ffn_out(x1p, rows6, wts_p.T, sg, su, sd, ln2_g, ln2_b, tm_m).reshape(b, t, D_MODEL)
    y_sample = _ffn_out(x1s, rows6[:, n_p:], wts_s.T, sg, su, sd, ln2_g, ln2_b, n_s).reshape(db, dt, D_MODEL)

    n_win = min(WINDOW, t)
    st = lambda a: a.reshape(1, b, t, NSA_KV_GROUPS, HEAD_DIM)
    ss = lambda a: a.reshape(1, db, dt, NSA_KV_GROUPS, HEAD_DIM)
    return (y_prompt, y_sample,
            st(kc), st(vc), st(ks), st(vs), st(kw)[:, :, -n_win:], st(vw)[:, :, -n_win:], st(kb), st(vb),
            misc[:, :IDX_DIM].reshape(1, b, t, IDX_DIM),
            ss(s_kc), ss(s_vc), ss(s_ks), ss(s_vs), s_wk[None], s_wv[None], ss(s_kb), ss(s_vb),
            s_misc[:, :IDX_DIM].reshape(1, db, dt, IDX_DIM))
```

```python
import functools
import math

import jax
import jax.numpy as jnp
import numpy as np
from jax import lax
from jax.experimental import pallas as pl
from jax.experimental.pallas import tpu as pltpu

D_MODEL = 2048
PAGE_SIZE = 128
HEAD_DIM = 128
NSA_HEADS = 8
NSA_KV_GROUPS = 2
NSA_HPG = NSA_HEADS // NSA_KV_GROUPS
CMP_BLOCK = 64
CMP_HIDDEN = 128
SEL_BLOCK = 64
N_SEL_BLOCKS = 16
WINDOW = 512
DSA_HEADS = 8
DSA_KV_GROUPS = 2
DSA_HPG = DSA_HEADS // DSA_KV_GROUPS
IDX_HEADS = 8
IDX_DIM = 64
DSA_TOPK = 256
N_BUCKETS = 32
MAX_EXACT = 16
MAX_DISTANCE = 128
N_EXPERTS = 64
EXPERT_FF = 512
SHARED_FF = 512
TOP_K = 6
N_EXPERT_GROUPS = 8
TOPK_GROUPS = 4
ROUTED_SCALE = 2.5
LN_EPS = 1e-5
ATTN_SCALE = HEAD_DIM ** -0.5
DEPTH = 1
DEEPNORM_ALPHA = (2 * DEPTH) ** 0.25
IN_COLS = (NSA_HEADS * HEAD_DIM, 6 * NSA_KV_GROUPS * HEAD_DIM, 3 * NSA_HEADS,
           DSA_HEADS * HEAD_DIM, 2 * DSA_KV_GROUPS * HEAD_DIM,
           IDX_HEADS * IDX_DIM, IDX_DIM, IDX_HEADS, 2 * D_MODEL)

KV_W = NSA_KV_GROUPS * HEAD_DIM
MISC_W = 128
MISC_IW = IDX_DIM
MISC_GA = IDX_DIM + IDX_HEADS
NEG = -1e30
Q_TILE = 256
EXPERT_ROWS = 256
VMEM_LIMIT = 56 * 1024 * 1024

BF16 = jnp.bfloat16
F32 = jnp.float32


def _cparams(*sem):
    return pltpu.CompilerParams(dimension_semantics=sem, vmem_limit_bytes=VMEM_LIMIT)


def _dot(a, b):
    return jnp.dot(a, b, preferred_element_type=F32)


def _dot_nt(a, b):
    return lax.dot_general(a, b, (((1,), (1,)), ((), ())), preferred_element_type=F32)


def _dot_nt_full(a, b):
    return lax.dot_general(a.astype(F32), b.astype(F32), (((1,), (1,)), ((), ())),
                           preferred_element_type=F32, precision=lax.Precision.HIGHEST)


def _mm(a, w):
    if w.dtype == F32:
        return jnp.dot(a.astype(F32), w, preferred_element_type=F32, precision=lax.Precision.HIGHEST)
    return jnp.dot(a.astype(w.dtype), w, preferred_element_type=F32)


def _proj_kernel(x_ref, w_ref, *o_refs, widths):
    acc = _mm(x_ref[...], w_ref[...])
    off = 0
    for o_ref, wd in zip(o_refs, widths):
        o_ref[...] = acc[:, off:off + wd]
        off += wd


def _project(x, w, widths, tm):
    m, k = x.shape
    n = w.shape[1]
    assert sum(widths) == n and m % tm == 0
    return pl.pallas_call(
        functools.partial(_proj_kernel, widths=widths),
        grid=(m // tm,),
        in_specs=[pl.BlockSpec((tm, k), lambda i: (i, 0)),
                  pl.BlockSpec((k, n), lambda i: (0, 0))],
        out_specs=[pl.BlockSpec((tm, wd), lambda i: (i, 0)) for wd in widths],
        out_shape=[jax.ShapeDtypeStruct((m, wd), F32) for wd in widths],
        compiler_params=_cparams("parallel"),
        name="project",
    )(x, w)


CMP_JCHUNK = 8


def _compress_kernel(x_ref, pe_ref, w1_ref, w2_ref, o_ref, acc_ref):
    jc = pl.program_id(1)

    @pl.when(jc == 0)
    def _():
        acc_ref[...] = jnp.zeros_like(acc_ref)

    acc = acc_ref[...]
    for jj in range(CMP_JCHUNK):
        lhs = (x_ref[:, jj, :] + pe_ref[jj:jj + 1, :]).astype(BF16)
        acc = acc + _dot(lhs, w1_ref[jj])
    acc_ref[...] = acc

    @pl.when(jc == pl.num_programs(1) - 1)
    def _():
        h = jax.nn.gelu(acc_ref[...])
        o_ref[...] = _dot(h.astype(BF16), w2_ref[...])


def _compress(rows, pe2, w1big, w2big):
    r = rows.shape[0]
    tr = math.gcd(r, 1024)
    assert tr % 8 == 0
    return pl.pallas_call(
        _compress_kernel,
        grid=(r // tr, CMP_BLOCK // CMP_JCHUNK),
        in_specs=[pl.BlockSpec((tr, CMP_JCHUNK, KV_W), lambda i, j: (i, j, 0)),
                  pl.BlockSpec((CMP_JCHUNK, KV_W), lambda i, j: (j, 0)),
                  pl.BlockSpec((CMP_JCHUNK, KV_W, KV_W), lambda i, j: (j, 0, 0)),
                  pl.BlockSpec((KV_W, KV_W), lambda i, j: (0, 0))],
        out_specs=pl.BlockSpec((tr, KV_W), lambda i, j: (i, 0)),
        out_shape=jax.ShapeDtypeStruct((r, KV_W), F32),
        scratch_shapes=[pltpu.VMEM((tr, KV_W), F32)],
        compiler_params=_cparams("parallel", "arbitrary"),
        name="compress",
    )(rows, pe2, w1big, w2big)


def _rows_at(x_ref, r):
    n, rows, w = x_ref.shape
    return x_ref.reshape(n * rows, w)[pl.ds(r, n, stride=rows), :]


def _compress_pool_kernel(x_ref, pe_ref, w1_ref, w2_ref, o_ref, acc_ref, *, tr):
    jc = pl.program_id(1)

    @pl.when(jc == 0)
    def _():
        acc_ref[...] = jnp.zeros_like(acc_ref)

    for g in range(NSA_KV_GROUPS):
        acc = acc_ref[g]
        for jp in range(CMP_JCHUNK // 2):
            parts = []
            for u in range(2):
                jj = 2 * jp + u
                xs = _rows_at(x_ref, 2 * jj + g) + pe_ref[jj:jj + 1, :]
                parts.append(xs.astype(BF16))
            acc = acc + _dot(jnp.concatenate(parts, axis=-1), w1_ref[jp])
        acc_ref[g] = acc

    @pl.when(jc == pl.num_programs(1) - 1)
    def _():
        for g in range(NSA_KV_GROUPS):
            h = jax.nn.gelu(acc_ref[g])
            o_ref[:, g * HEAD_DIM:(g + 1) * HEAD_DIM] = _dot(h.astype(BF16), w2_ref[...])


def _compress_pool(cache, pe, w1, w2):
    n_pool = cache.shape[0]
    r = n_pool * (PAGE_SIZE // CMP_BLOCK)
    rows = cache.reshape(r, CMP_BLOCK * NSA_KV_GROUPS, HEAD_DIM)
    tr = math.gcd(r, 1024)
    assert tr % 8 == 0
    rows_per = 2 * CMP_JCHUNK
    w1p = w1.reshape(CMP_BLOCK // 2, 2 * HEAD_DIM, CMP_HIDDEN).astype(BF16)
    return pl.pallas_call(
        functools.partial(_compress_pool_kernel, tr=tr),
        grid=(r // tr, CMP_BLOCK // CMP_JCHUNK),
        in_specs=[pl.BlockSpec((tr, rows_per, HEAD_DIM), lambda i, j: (i, j, 0)),
                  pl.BlockSpec((CMP_JCHUNK, HEAD_DIM), lambda i, j: (j, 0)),
                  pl.BlockSpec((CMP_JCHUNK // 2, 2 * HEAD_DIM, CMP_HIDDEN), lambda i, j: (j, 0, 0)),
                  pl.BlockSpec((CMP_HIDDEN, HEAD_DIM), lambda i, j: (0, 0))],
        out_specs=pl.BlockSpec((tr, KV_W), lambda i, j: (i, 0)),
        out_shape=jax.ShapeDtypeStruct((r, KV_W), F32),
        scratch_shapes=[pltpu.VMEM((NSA_KV_GROUPS, tr, CMP_HIDDEN), F32)],
        compiler_params=_cparams("parallel", "arbitrary"),
        name="compress_pool",
    )(rows, pe, w1p, w2.astype(BF16))


def _compress_weights(pe, w1, w2):
    pe2 = jnp.concatenate([pe, pe], axis=-1)
    z1 = jnp.zeros_like(w1)
    w1big = jnp.concatenate([jnp.concatenate([w1, z1], axis=2), jnp.concatenate([z1, w1], axis=2)], axis=1)
    z2 = jnp.zeros_like(w2)
    w2big = jnp.concatenate([jnp.concatenate([w2, z2], axis=1), jnp.concatenate([z2, w2], axis=1)], axis=0)
    return pe2, w1big.astype(BF16), w2big.astype(BF16)


def _cmp_select_kernel(q_ref, kc_ref, vc_ref, cb_ref, o_ref, sel_ref, *, tq, nb, n_sel):
    qi = pl.program_id(1)
    t = qi * tq + lax.broadcasted_iota(jnp.int32, (tq, nb), 0)
    j = lax.broadcasted_iota(jnp.int32, (tq, nb), 1)
    cur = t // SEL_BLOCK
    for g in range(NSA_KV_GROUPS):
        kc = kc_ref[0, :, g * HEAD_DIM:(g + 1) * HEAD_DIM].astype(BF16)
        vc = vc_ref[0, :, g * HEAD_DIM:(g + 1) * HEAD_DIM].astype(BF16)
        imp = jnp.zeros((tq, nb), F32)
        for h in range(NSA_HPG):
            hh = g * NSA_HPG + h
            q = q_ref[0, :, hh * HEAD_DIM:(hh + 1) * HEAD_DIM].astype(BF16)
            cb = cb_ref[hh]
            valid = cb > 0.5 * NEG
            s = _dot_nt(q, kc) * ATTN_SCALE + cb
            m = jnp.max(s, axis=-1, keepdims=True)
            e = jnp.where(valid, jnp.exp(s - m), 0.0)
            den = jnp.sum(e, axis=-1, keepdims=True)
            p = e / jnp.where(den > 0, den, 1.0)
            o_ref[0, :, hh * HEAD_DIM:(hh + 1) * HEAD_DIM] = _dot(p.astype(BF16), vc)
            imp = imp + p
        forced = (j == 0) | (j == cur) | (j == cur - 1)
        score = jnp.where(forced, jnp.inf, jnp.where(j <= cur, imp, -jnp.inf))
        rank = jnp.zeros((tq, nb), jnp.int32)
        for k in range(nb):
            col = score[:, k:k + 1]
            ahead = (col > score) | ((col == score) & (k < j))
            rank = rank + jnp.where(ahead, 1, 0)
        sel_ref[0, g] = jnp.where(rank < n_sel, 1.0, 0.0).astype(F32)


def _cmp_select(qa, kcomp, vcomp, cbias, tq):
    b, t, _ = qa.shape
    nb = kcomp.shape[1]
    n_sel = min(N_SEL_BLOCKS, nb)
    return pl.pallas_call(
        functools.partial(_cmp_select_kernel, tq=tq, nb=nb, n_sel=n_sel),
        grid=(b, t // tq),
        in_specs=[pl.BlockSpec((1, tq, NSA_HEADS * HEAD_DIM), lambda bi, qi: (bi, qi, 0)),
                  pl.BlockSpec((1, nb, KV_W), lambda bi, qi: (bi, 0, 0)),
                  pl.BlockSpec((1, nb, KV_W), lambda bi, qi: (bi, 0, 0)),
                  pl.BlockSpec((NSA_HEADS, tq, nb), lambda bi, qi: (0, qi, 0))],
        out_specs=[pl.BlockSpec((1, tq, NSA_HEADS * HEAD_DIM), lambda bi, qi: (bi, qi, 0)),
                   pl.BlockSpec((1, NSA_KV_GROUPS, tq, nb), lambda bi, qi: (bi, 0, qi, 0))],
        out_shape=[jax.ShapeDtypeStruct((b, t, NSA_HEADS * HEAD_DIM), F32),
                   jax.ShapeDtypeStruct((b, NSA_KV_GROUPS, t, nb), F32)],
        compiler_params=_cparams("parallel", "parallel"),
        name="cmp_select",
    )(qa, kcomp, vcomp, cbias)


LOG2E = math.log2(math.e)


def _dense_attn_kernel(q_ref, k_ref, v_ref, mask_ref, expand_ref, d_ref, o_ref, *, tq, n_tiles, n_heads, hpg, mode):
    qi = pl.program_id(1)

    def attend(n_ch):
        chunk = lambda c: slice(c * tq, (c + 1) * tq)
        mbs, ks, vs = [], [], []
        for hh in range(n_heads):
            g = hh // hpg
            if hh % hpg == 0:
                if mode == "sel":
                    mbs = [jnp.where(_dot(mask_ref[0, g].astype(BF16), expand_ref[:, chunk(c)]) > 0.5, 0.0, NEG)
                           for c in range(n_ch)]
                elif g == 0:
                    mbs = [jnp.where(mask_ref[0, :, chunk(c)].astype(F32) > 0.5, 0.0, NEG) for c in range(n_ch)]
                ks = [k_ref[0, chunk(c), g * HEAD_DIM:(g + 1) * HEAD_DIM].astype(BF16) for c in range(n_ch)]
                vs = [v_ref[0, chunk(c), g * HEAD_DIM:(g + 1) * HEAD_DIM].astype(BF16) for c in range(n_ch)]
            q = (q_ref[0, :, hh * HEAD_DIM:(hh + 1) * HEAD_DIM] * (ATTN_SCALE * LOG2E)).astype(BF16)
            ss = []
            for c in range(n_ch):
                s = _dot_nt(q, ks[c]) + mbs[c]
                if n_ch - 1 - c < 2:
                    s = s + d_ref[hh, n_ch - 1 - c]
                ss.append(s)
            mx = ss[0]
            for s in ss[1:]:
                mx = jnp.maximum(mx, s)
            m = jnp.max(mx, axis=-1, keepdims=True)
            es = [jnp.exp2(s - m) for s in ss]
            tot = es[0]
            for e in es[1:]:
                tot = tot + e
            den = jnp.sum(tot, axis=-1, keepdims=True)
            o = _dot(es[0].astype(BF16), vs[0])
            for c in range(1, n_ch):
                o = o + _dot(es[c].astype(BF16), vs[c])
            o_ref[0, :, hh * HEAD_DIM:(hh + 1) * HEAD_DIM] = jnp.where(m > 0.5 * NEG, o / den, 0.0)

    for tile in range(n_tiles):
        @pl.when(qi == tile)
        def _(tile=tile):
            attend(tile + 1)


def _dense_attn(q, k, v, mask, expand, dtiles, tq, mode):
    b, t, qw = q.shape
    n_heads = qw // HEAD_DIM
    hpg = n_heads // (k.shape[2] // HEAD_DIM)
    if mode == "sel":
        nb = mask.shape[-1]
        mask_spec = pl.BlockSpec((1, mask.shape[1], tq, nb), lambda bi, qi: (bi, 0, qi, 0))
    else:
        mask_spec = pl.BlockSpec((1, tq, t), lambda bi, qi: (bi, qi, 0))
    return pl.pallas_call(
        functools.partial(_dense_attn_kernel, tq=tq, n_tiles=t // tq, n_heads=n_heads, hpg=hpg, mode=mode),
        grid=(b, t // tq),
        in_specs=[pl.BlockSpec((1, tq, qw), lambda bi, qi: (bi, qi, 0)),
                  pl.BlockSpec((1, t, k.shape[2]), lambda bi, qi: (bi, 0, 0)),
                  pl.BlockSpec((1, t, v.shape[2]), lambda bi, qi: (bi, 0, 0)),
                  mask_spec,
                  pl.BlockSpec(expand.shape, lambda bi, qi: (0, 0)),
                  pl.BlockSpec(dtiles.shape, lambda bi, qi: (0, 0, 0, 0))],
        out_specs=pl.BlockSpec((1, tq, qw), lambda bi, qi: (bi, qi, 0)),
        out_shape=jax.ShapeDtypeStruct((b, t, qw), F32),
        compiler_params=_cparams("parallel", "parallel"),
        name="dense_attn_" + mode,
    )(q, k, v, mask, expand, dtiles)


def _window_attn_kernel(q_ref, k_ref, v_ref, w_ref, o_ref, *, tq, n_chunks, n_heads, hpg):
    qi = pl.program_id(1)
    starts, pens = [], []
    for r in range(n_chunks):
        cj = qi - (n_chunks - 1) + r
        starts.append(pl.multiple_of(jnp.maximum(cj, 0) * tq, tq))
        pens.append(jnp.where(cj < 0, NEG, 0.0).astype(F32))
    for hh in range(n_heads):
        g = hh // hpg
        q = q_ref[0, :, hh * HEAD_DIM:(hh + 1) * HEAD_DIM].astype(BF16)
        ss = []
        for r in range(n_chunks):
            k = k_ref[0, pl.ds(starts[r], tq), g * HEAD_DIM:(g + 1) * HEAD_DIM].astype(BF16)
            ss.append(_dot_nt(q, k) * ATTN_SCALE + (w_ref[hh, n_chunks - 1 - r] + pens[r]))
        m = ss[0].max(axis=-1, keepdims=True)
        for r in range(1, n_chunks):
            m = jnp.maximum(m, ss[r].max(axis=-1, keepdims=True))
        den = jnp.zeros((tq, 1), F32)
        o = jnp.zeros((tq, HEAD_DIM), F32)
        for r in range(n_chunks):
            e = jnp.where(ss[r] > 0.5 * NEG, jnp.exp(ss[r] - m), 0.0)
            den = den + e.sum(axis=-1, keepdims=True)
            v = v_ref[0, pl.ds(starts[r], tq), g * HEAD_DIM:(g + 1) * HEAD_DIM].astype(BF16)
            o = o + _dot(e.astype(BF16), v)
        o_ref[0, :, hh * HEAD_DIM:(hh + 1) * HEAD_DIM] = o / jnp.where(den > 0, den, 1.0)


def _window_attn(q, k, v, wtiles, tq):
    b, t, qw = q.shape
    n_heads = qw // HEAD_DIM
    hpg = n_heads // (k.shape[2] // HEAD_DIM)
    n_chunks = wtiles.shape[1]
    return pl.pallas_call(
        functools.partial(_window_attn_kernel, tq=tq, n_chunks=n_chunks, n_heads=n_heads, hpg=hpg),
        grid=(b, t // tq),
        in_specs=[pl.BlockSpec((1, tq, qw), lambda bi, qi: (bi, qi, 0)),
                  pl.BlockSpec((1, t, k.shape[2]), lambda bi, qi: (bi, 0, 0)),
                  pl.BlockSpec((1, t, v.shape[2]), lambda bi, qi: (bi, 0, 0)),
                  pl.BlockSpec(wtiles.shape, lambda bi, qi: (0, 0, 0, 0))],
        out_specs=pl.BlockSpec((1, tq, qw), lambda bi, qi: (bi, qi, 0)),
        out_shape=jax.ShapeDtypeStruct((b, t, qw), F32),
        compiler_params=_cparams("parallel", "parallel"),
        name="window_attn",
    )(q, k, v, wtiles)


INT_MIN = -2 ** 31


def _topk_mask(key_ref, n_keep, tq, s_len):
    def body(i, thr_u):
        cand_u = thr_u | jnp.left_shift(jnp.int32(1), 31 - i)
        below = (cand_u ^ INT_MIN) - 1
        cnt = jnp.sum(jnp.where(key_ref[...] > below, 1, 0), axis=-1, keepdims=True)
        return jnp.where(cnt >= n_keep, cand_u, thr_u)

    thr_u = lax.fori_loop(0, 32, body, jnp.zeros((tq, 1), jnp.int32))
    thr = thr_u ^ INT_MIN
    return thr


def _index_select_kernel(iq_ref, mq_ref, mk_ref, o_ref, key_ref, *, tq, s_len, n_keep, n_bands):
    qi = pl.program_id(1)
    tiles = (s_len // tq) // n_bands
    wts = mq_ref[0, :, MISC_IW:MISC_IW + IDX_HEADS] * IDX_DIM ** -0.5
    iqs = [iq_ref[0, :, h * IDX_DIM:(h + 1) * IDX_DIM].astype(BF16) for h in range(IDX_HEADS)]
    t = qi * tq + lax.broadcasted_iota(jnp.int32, (tq, tq), 0)
    col = lax.broadcasted_iota(jnp.int32, (tq, tq), 1)

    def select(s_b):
        for c in range(s_b // tq):
            ik = mk_ref[0, c * tq:(c + 1) * tq, 0:IDX_DIM].astype(BF16)
            score = jnp.zeros((tq, tq), F32)
            for h in range(IDX_HEADS):
                score = score + jnp.maximum(_dot_nt(iqs[h], ik), 0.0) * wts[:, h:h + 1]
            score = score * IDX_HEADS ** -0.5 + 0.0
            score = jnp.where(c * tq + col <= t, score, -jnp.inf)
            bits = pltpu.bitcast(score, jnp.int32)
            key_ref[:, c * tq:(c + 1) * tq] = jnp.where(bits < 0, bits ^ 0x7FFFFFFF, bits)

        def bit_step(i, thr_u):
            cand_u = thr_u | jnp.left_shift(jnp.int32(1), 31 - i)
            below = (cand_u ^ INT_MIN) - 1
            cnt = jnp.sum(jnp.where(key_ref[:, :s_b] > below, 1, 0), axis=-1, keepdims=True)
            return jnp.where(cnt >= n_keep, cand_u, thr_u)

        thr = lax.fori_loop(0, 32, bit_step, jnp.zeros((tq, 1), jnp.int32)) ^ INT_MIN
        key = key_ref[:, :s_b]
        gt = key > thr
        eq = key == thr
        need = n_keep - jnp.sum(jnp.where(gt, 1, 0), axis=-1, keepdims=True)
        n_eq = jnp.sum(jnp.where(eq, 1, 0), axis=-1, keepdims=True)
        o_ref[0, :, :s_b] = jnp.where(gt | eq, 1.0, 0.0).astype(o_ref.dtype)
        if s_b < s_len:
            o_ref[0, :, s_b:] = jnp.zeros((tq, s_len - s_b), o_ref.dtype)

        @pl.when(jnp.max(n_eq - need) > 0)
        def _():
            r_i = lax.broadcasted_iota(jnp.int32, (128, 128), 0)
            c_i = lax.broadcasted_iota(jnp.int32, (128, 128), 1)
            tri = jnp.where(r_i <= c_i, 1.0, 0.0).astype(BF16)
            before = jnp.zeros((tq, 1), F32)
            need_f = need.astype(F32)
            for c in range(s_b // 128):
                sl = slice(c * 128, (c + 1) * 128)
                eq_c = eq[:, sl]
                eq_f = jnp.where(eq_c, 1.0, 0.0)
                pref = _dot(eq_f.astype(BF16), tri) + before
                keep = gt[:, sl] | (eq_c & (pref <= need_f))
                o_ref[0, :, sl] = jnp.where(keep, 1.0, 0.0).astype(o_ref.dtype)
                before = before + jnp.sum(eq_f, axis=-1, keepdims=True)

    for band in range(n_bands):
        @pl.when(qi // tiles == band)
        def _(band=band):
            select((band + 1) * tiles * tq)


def _index_select(iq, misc, tq, n_keep):
    b, t, _ = iq.shape
    return pl.pallas_call(
        functools.partial(_index_select_kernel, tq=tq, s_len=t, n_keep=n_keep, n_bands=t // tq),
        grid=(b, t // tq),
        in_specs=[pl.BlockSpec((1, tq, IDX_HEADS * IDX_DIM), lambda bi, qi: (bi, qi, 0)),
                  pl.BlockSpec((1, tq, MISC_W), lambda bi, qi: (bi, qi, 0)),
                  pl.BlockSpec((1, t, MISC_W), lambda bi, qi: (bi, 0, 0))],
        out_specs=pl.BlockSpec((1, tq, t), lambda bi, qi: (bi, qi, 0)),
        out_shape=jax.ShapeDtypeStruct((b, t, t), BF16),
        scratch_shapes=[pltpu.VMEM((tq, t), jnp.int32)],
        compiler_params=_cparams("parallel", "parallel"),
        name="index_select",
    )(iq, misc, misc)


def _merge_up_kernel(x_ref, oc_ref, os_ref, ow_ref, ob_ref, misc_ref, wga_ref, wgb_ref, wua_ref, wub_ref,
                     y_ref, oa_ref):
    ga = jax.nn.sigmoid(misc_ref[:, MISC_GA:MISC_GA + 3 * NSA_HEADS])
    for hh in range(NSA_HEADS):
        sl = slice(hh * HEAD_DIM, (hh + 1) * HEAD_DIM)
        oa = (ga[:, hh:hh + 1] * oc_ref[:, sl] + ga[:, NSA_HEADS + hh:NSA_HEADS + hh + 1] * os_ref[:, sl]
              + ga[:, 2 * NSA_HEADS + hh:2 * NSA_HEADS + hh + 1] * ow_ref[:, sl])
        oa_ref[:, sl] = oa.astype(oa_ref.dtype)
    x = x_ref[...]
    ya = _mm(oa_ref[...], wua_ref[...])
    yb = _mm(ob_ref[...], wub_ref[...])
    g_a = jax.nn.sigmoid(_mm(x, wga_ref[...]))
    g_b = jax.nn.sigmoid(_mm(x, wgb_ref[...]))
    y_ref[...] = (g_a * ya + g_b * yb).astype(y_ref.dtype)


def _merge_up(xb, o_cmp, o_sel, o_win, o_b, misc, wga, wgb, wua, wub, tm, tn):
    m = xb.shape[0]
    mx = wua.dtype
    aw = NSA_HEADS * HEAD_DIM
    bw = DSA_HEADS * HEAD_DIM
    row = lambda w: pl.BlockSpec((tm, w), lambda j, i: (i, 0))
    wcol = lambda k: pl.BlockSpec((k, tn), lambda j, i: (0, j))
    return pl.pallas_call(
        _merge_up_kernel,
        grid=(D_MODEL // tn, m // tm),
        in_specs=[row(D_MODEL), row(aw), row(aw), row(aw), row(bw), row(MISC_W),
                  wcol(D_MODEL), wcol(D_MODEL), wcol(aw), wcol(bw)],
        out_specs=pl.BlockSpec((tm, tn), lambda j, i: (i, j)),
        out_shape=jax.ShapeDtypeStruct((m, D_MODEL), mx),
        scratch_shapes=[pltpu.VMEM((tm, aw), mx)],
        compiler_params=_cparams("parallel", "parallel"),
        name="merge_up",
    )(xb, o_cmp, o_sel, o_win, o_b, misc, wga, wgb, wua, wub)


def _layer_norm(z, g, b):
    mu = jnp.mean(z, axis=-1, keepdims=True)
    zc = z - mu
    var = jnp.mean(zc * zc, axis=-1, keepdims=True)
    return zc * lax.rsqrt(var + LN_EPS) * g + b


def _merge_out_kernel(x_ref, y_ref, wo_ref, g_ref, b_ref, o_ref):
    z = DEEPNORM_ALPHA * x_ref[...] + _mm(y_ref[...], wo_ref[...])
    o_ref[...] = _layer_norm(z, g_ref[...], b_ref[...])


def _merge_out(x, y, wo, g, b, tm):
    m = x.shape[0]
    row = lambda: pl.BlockSpec((tm, D_MODEL), lambda i: (i, 0))
    return pl.pallas_call(
        _merge_out_kernel,
        grid=(m // tm,),
        in_specs=[row(), row(), pl.BlockSpec((D_MODEL, D_MODEL), lambda i: (0, 0)),
                  pl.BlockSpec((1, D_MODEL), lambda i: (0, 0)), pl.BlockSpec((1, D_MODEL), lambda i: (0, 0))],
        out_specs=row(),
        out_shape=jax.ShapeDtypeStruct((m, D_MODEL), F32),
        compiler_params=_cparams("parallel"),
        name="merge_out",
    )(x, y, wo, g, b)


def _router_kernel(x_ref, wr_ref, rb_ref, c0_ref, idx_ref, wt_ref, pos_ref, cnt_ref, run_ref, *, tm):
    epg = N_EXPERTS // N_EXPERT_GROUPS
    logits = lax.dot_general(wr_ref[...], x_ref[...], (((1,), (1,)), ((), ())),
                             preferred_element_type=F32, precision=lax.Precision.HIGHEST)
    scores = jax.nn.sigmoid(logits)
    biased = scores + rb_ref[...]
    sub = lax.broadcasted_iota(jnp.int32, (epg, tm), 0)
    gs_rows = []
    for r in range(N_EXPERT_GROUPS):
        bg = biased[r * epg:(r + 1) * epg, :]
        m1 = jnp.max(bg, axis=0, keepdims=True)
        i1 = jnp.min(jnp.where(bg == m1, sub, epg), axis=0, keepdims=True)
        m2 = jnp.max(jnp.where(sub == i1, -jnp.inf, bg), axis=0, keepdims=True)
        gs_rows.append(m1 + m2)
    gs = jnp.concatenate(gs_rows, axis=0)
    grow = lax.broadcasted_iota(jnp.int32, (N_EXPERT_GROUPS, tm), 0)
    rank = jnp.zeros((N_EXPERT_GROUPS, tm), jnp.int32)
    for k in range(N_EXPERT_GROUPS):
        rk = gs[k:k + 1, :]
        rank = rank + jnp.where((rk > gs) | ((rk == gs) & (k < grow)), 1, 0)
    gkeep = rank < TOPK_GROUPS
    masked = jnp.concatenate(
        [jnp.where(gkeep[r:r + 1, :], biased[r * epg:(r + 1) * epg, :], -jnp.inf) for r in range(N_EXPERT_GROUPS)],
        axis=0)
    erow = lax.broadcasted_iota(jnp.int32, (N_EXPERTS, tm), 0)
    idx_rows, w_rows, hits = [], [], []
    for _ in range(TOP_K):
        m = jnp.max(masked, axis=0, keepdims=True)
        ix = jnp.min(jnp.where(masked == m, erow, N_EXPERTS), axis=0, keepdims=True)
        hit = erow == ix
        w_rows.append(jnp.sum(jnp.where(hit, scores, 0.0), axis=0, keepdims=True))
        idx_rows.append(ix)
        hits.append(hit)
        masked = jnp.where(hit, -jnp.inf, masked)
    wsum = w_rows[0]
    for w in w_rows[1:]:
        wsum = wsum + w
    pad = 8 - TOP_K
    idx_ref[...] = jnp.concatenate(idx_rows + [jnp.zeros((pad, tm), jnp.int32)], axis=0)
    wt_ref[...] = jnp.concatenate([w / wsum * ROUTED_SCALE for w in w_rows] + [jnp.zeros((pad, tm), F32)], axis=0)

    @pl.when(pl.program_id(0) == 0)
    def _():
        run_ref[...] = c0_ref[...]

    earlier = (lax.broadcasted_iota(jnp.int32, (tm, tm), 0) < lax.broadcasted_iota(jnp.int32, (tm, tm), 1))
    earlier = jnp.where(earlier, 1.0, 0.0).astype(BF16)
    run = run_ref[...]
    pos_rows = []
    for hit in hits:
        onehot = jnp.where(hit, 1.0, 0.0)
        before = _dot(onehot.astype(BF16), earlier).astype(jnp.int32)
        pos_rows.append(jnp.sum(jnp.where(hit, run + before, 0), axis=0, keepdims=True))
        run = run + jnp.sum(onehot, axis=1, keepdims=True).astype(jnp.int32)
    run_ref[...] = run
    cnt_ref[...] = run
    pos_ref[...] = jnp.concatenate(pos_rows + [jnp.zeros((pad, tm), jnp.int32)], axis=0)


def _router(x1, wr_t, rb, counts0, tm):
    m = x1.shape[0]
    tok = lambda: pl.BlockSpec((8, tm), lambda i: (0, i))
    return pl.pallas_call(
        functools.partial(_router_kernel, tm=tm),
        grid=(m // tm,),
        in_specs=[pl.BlockSpec((tm, D_MODEL), lambda i: (i, 0)),
                  pl.BlockSpec((N_EXPERTS, D_MODEL), lambda i: (0, 0)),
                  pl.BlockSpec((N_EXPERTS, 1), lambda i: (0, 0)),
                  pl.BlockSpec((N_EXPERTS, 1), lambda i: (0, 0))],
        out_specs=[tok(), tok(), tok(), pl.BlockSpec((N_EXPERTS, 1), lambda i: (0, 0))],
        out_shape=[jax.ShapeDtypeStruct((8, m), jnp.int32), jax.ShapeDtypeStruct((8, m), F32),
                   jax.ShapeDtypeStruct((8, m), jnp.int32), jax.ShapeDtypeStruct((N_EXPERTS, 1), jnp.int32)],
        scratch_shapes=[pltpu.VMEM((N_EXPERTS, 1), jnp.int32)],
        compiler_params=_cparams("arbitrary"),
        name="router",
    )(x1, wr_t, rb, counts0)


def _expert_kernel(be_ref, nv_ref, x_ref, wg_ref, wu_ref, wd_ref, o_ref, wgb_ref, wub_ref, wdb_ref):
    i = pl.program_id(0)
    valid = i < nv_ref[0]
    new_expert = (i == 0) | (be_ref[i] != be_ref[jnp.maximum(i - 1, 0)])

    @pl.when(valid & new_expert)
    def _():
        wgb_ref[...] = wg_ref[0].astype(BF16)
        wub_ref[...] = wu_ref[0].astype(BF16)
        wdb_ref[...] = wd_ref[0].astype(BF16)

    @pl.when(valid)
    def _():
        x = x_ref[...].astype(BF16)
        h = jax.nn.silu(_dot(x, wgb_ref[...])) * _dot(x, wub_ref[...])
        o_ref[...] = _dot(h.astype(BF16), wdb_ref[...]).astype(o_ref.dtype)

    @pl.when(jnp.logical_not(valid))
    def _():
        o_ref[...] = jnp.zeros_like(o_ref)


def _experts(blk_e, n_valid, xg, wg, wu, wd, tr):
    n_rows = xg.shape[0]
    n_blk = n_rows // tr
    grid_spec = pltpu.PrefetchScalarGridSpec(
        num_scalar_prefetch=2,
        grid=(n_blk,),
        in_specs=[pl.BlockSpec((tr, D_MODEL), lambda i, be, nv: (i, 0)),
                  pl.BlockSpec((1, D_MODEL, EXPERT_FF), lambda i, be, nv: (be[i], 0, 0)),
                  pl.BlockSpec((1, D_MODEL, EXPERT_FF), lambda i, be, nv: (be[i], 0, 0)),
                  pl.BlockSpec((1, EXPERT_FF, D_MODEL), lambda i, be, nv: (be[i], 0, 0))],
        out_specs=pl.BlockSpec((tr, D_MODEL), lambda i, be, nv: (i, 0)),
        scratch_shapes=[pltpu.VMEM((D_MODEL, EXPERT_FF), BF16), pltpu.VMEM((D_MODEL, EXPERT_FF), BF16),
                        pltpu.VMEM((EXPERT_FF, D_MODEL), BF16)],
    )
    return pl.pallas_call(
        _expert_kernel,
        grid_spec=grid_spec,
        out_shape=jax.ShapeDtypeStruct((n_rows, D_MODEL), BF16),
        compiler_params=_cparams("arbitrary"),
        name="experts",
    )(blk_e, n_valid, xg, wg, wu, wd)


def _ffn_out_kernel(x_ref, r_ref, w_ref, sg_ref, su_ref, sd_ref, g_ref, b_ref, o_ref):
    x = x_ref[...]
    xb = x.astype(BF16)
    h = jax.nn.silu(_dot(xb, sg_ref[...])) * _dot(xb, su_ref[...])
    f = _dot(h.astype(BF16), sd_ref[...])
    for k in range(TOP_K):
        f = f + r_ref[k].astype(F32) * w_ref[:, k:k + 1]
    o_ref[...] = _layer_norm(DEEPNORM_ALPHA * x + f, g_ref[...], b_ref[...])


def _ffn_out(x1, rows6, wts, sg, su, sd, g, b, tm):
    m = x1.shape[0]
    row = lambda: pl.BlockSpec((tm, D_MODEL), lambda i: (i, 0))
    full = lambda s: pl.BlockSpec(s, lambda i: (0, 0))
    return pl.pallas_call(
        _ffn_out_kernel,
        grid=(m // tm,),
        in_specs=[row(), pl.BlockSpec((TOP_K, tm, D_MODEL), lambda i: (0, i, 0)), pl.BlockSpec((tm, 8), lambda i: (i, 0)),
                  full((D_MODEL, SHARED_FF)), full((D_MODEL, SHARED_FF)), full((SHARED_FF, D_MODEL)),
                  full((1, D_MODEL)), full((1, D_MODEL))],
        out_specs=row(),
        out_shape=jax.ShapeDtypeStruct((m, D_MODEL), F32),
        compiler_params=_cparams("parallel"),
        name="ffn_out",
    )(x1, rows6, wts, sg, su, sd, g, b)


def _rel_bucket(dist):
    n = jnp.maximum(dist, 0)
    nf = jnp.maximum(n, 1).astype(F32)
    large = MAX_EXACT + (jnp.log(nf / MAX_EXACT) / math.log(MAX_DISTANCE / MAX_EXACT)
                         * (N_BUCKETS - MAX_EXACT)).astype(jnp.int32)
    return jnp.where(n < MAX_EXACT, n, jnp.minimum(large, N_BUCKETS - 1))


def _bias_of(tbl, dist):
    onehot = (_rel_bucket(dist)[..., None] == jnp.arange(N_BUCKETS)).astype(F32)
    return jnp.einsum('...k,kh->...h', onehot, tbl, precision=lax.Precision.HIGHEST)


def _toeplitz_bias(tbl, tq, n_tiles):
    i = jnp.arange(tq)[:, None]
    j = jnp.arange(tq)[None, :]
    dist = jnp.arange(n_tiles)[:, None, None] * tq + (i - j)[None]
    bias = jnp.moveaxis(_bias_of(tbl, dist), -1, 0)
    return bias, dist


PAGE_GROUP = 16


def _by_group(hpg, a0, a1):
    row = lax.broadcasted_iota(jnp.int32, (2 * hpg, 1), 0)
    return jnp.where(row < hpg, a0, a1)


def _order_key(x):
    bits = pltpu.bitcast(x, jnp.int32)
    return jnp.where(bits < 0, bits ^ 0x7FFFFFFF, bits)


def _row_topk(key_ref, keep_ref, n_keep, width):
    thr = _topk_mask(key_ref, n_keep, 1, width)
    key = key_ref[...]
    gt = key > thr
    eq = key == thr
    need = n_keep - jnp.sum(jnp.where(gt, 1, 0), axis=-1, keepdims=True)
    n_eq = jnp.sum(jnp.where(eq, 1, 0), axis=-1, keepdims=True)
    keep_ref[...] = jnp.where(gt | eq, 1.0, 0.0)

    @pl.when(jnp.max(n_eq - need) > 0)
    def _():
        r_i = lax.broadcasted_iota(jnp.int32, (128, 128), 0)
        c_i = lax.broadcasted_iota(jnp.int32, (128, 128), 1)
        tri = jnp.where(r_i <= c_i, 1.0, 0.0).astype(BF16)
        need_f = need.astype(F32)

        def body(c, before):
            off = pl.multiple_of(c * 128, 128)
            kc = key_ref[:, pl.ds(off, 128)]
            eq_c = kc == thr
            eq_f = jnp.where(eq_c, 1.0, 0.0)
            pref = _dot(jnp.broadcast_to(eq_f, (8, 128)).astype(BF16), tri)[0:1] + before
            keep_ref[:, pl.ds(off, 128)] = jnp.where((kc > thr) | (eq_c & (pref <= need_f)), 1.0, 0.0)
            return before + jnp.sum(eq_f, axis=-1, keepdims=True)

        lax.fori_loop(0, width // 128, body, jnp.zeros((1, 1), F32))


def _s_cmp_win_kernel(q_ref, kc_ref, vc_ref, cb_ref, wk_ref, wv_ref, nk_ref, nv_ref, wb_ref, b0_ref,
                      ocmp_ref, owin_ref, sel_ref, key_ref, keep_ref, *, nbc, n_blocks, width, n_sel, wlen):
    hpg = NSA_HPG
    qf = q_ref[0]
    q = qf.astype(BF16)
    row = lax.broadcasted_iota(jnp.int32, (NSA_HEADS, 1), 0)
    g0 = row < hpg
    kc = kc_ref[0]
    vc = vc_ref[0]
    cb = cb_ref[...]
    valid = cb > 0.5 * NEG
    s = _by_group(hpg, _dot_nt_full(qf, kc[:, :HEAD_DIM]), _dot_nt_full(qf, kc[:, HEAD_DIM:]))
    s = s * ATTN_SCALE + cb
    m = jnp.max(s, axis=-1, keepdims=True)
    e = jnp.where(valid, jnp.exp(s - m), 0.0)
    den = jnp.sum(e, axis=-1, keepdims=True)
    p = e / jnp.where(den > 0, den, 1.0)
    ocmp_ref[0] = (_dot(jnp.where(g0, p, 0.0).astype(BF16), vc[:, :HEAD_DIM].astype(BF16))
                   + _dot(jnp.where(g0, 0.0, p).astype(BF16), vc[:, HEAD_DIM:].astype(BF16)))
    lane = lax.broadcasted_iota(jnp.int32, (1, width), 1)
    cur = n_blocks - 1
    forced = (lane == 0) | (lane == cur) | (lane == cur - 1)
    for g in range(NSA_KV_GROUPS):
        imp = jnp.sum(jnp.where(g0 if g == 0 else jnp.logical_not(g0), p, 0.0), axis=0, keepdims=True)
        impw = jnp.concatenate([imp, jnp.zeros((1, width - nbc), F32)], axis=1)
        score = jnp.where(forced, jnp.inf, jnp.where(lane <= cur, impw, -jnp.inf))
        key_ref[...] = _order_key(score)
        _row_topk(key_ref, keep_ref, n_sel, width)
        sel_ref[0, g:g + 1, :] = keep_ref[...]
    wk0 = wk_ref[pl.ds(0, wlen, stride=2), :].astype(BF16)
    wk1 = wk_ref[pl.ds(1, wlen, stride=2), :].astype(BF16)
    wv0 = wv_ref[pl.ds(0, wlen, stride=2), :].astype(BF16)
    wv1 = wv_ref[pl.ds(1, wlen, stride=2), :].astype(BF16)
    wb = wb_ref[...]
    s = _by_group(hpg, _dot_nt(q, wk0), _dot_nt(q, wk1)) * ATTN_SCALE + wb
    nk = _by_group(hpg, nk_ref[0][:, :HEAD_DIM], nk_ref[0][:, HEAD_DIM:])
    nv = _by_group(hpg, nv_ref[0][:, :HEAD_DIM], nv_ref[0][:, HEAD_DIM:])
    s_new = jnp.sum(qf * nk, axis=-1, keepdims=True) * ATTN_SCALE + b0_ref[...]
    m = jnp.maximum(jnp.max(s, axis=-1, keepdims=True), s_new)
    e = jnp.where(wb > 0.5 * NEG, jnp.exp(s - m), 0.0)
    e_new = jnp.exp(s_new - m)
    den = jnp.sum(e, axis=-1, keepdims=True) + e_new
    o = (_dot(jnp.where(g0, e, 0.0).astype(BF16), wv0) + _dot(jnp.where(g0, 0.0, e).astype(BF16), wv1)
         + e_new * nv)
    owin_ref[0] = o / den


def _s_cmp_win(q, kcomp, vcomp, cbias, wk, wv, nk, nv, wbias, b0, n_blocks, n_sel):
    db = q.shape[0]
    nbc = kcomp.shape[1]
    wlen = wk.shape[1] // 2
    width = -(-n_blocks // 128) * 128
    one = lambda *s: pl.BlockSpec((1,) + s, lambda bi: (bi,) + (0,) * len(s))
    const = lambda a: pl.BlockSpec(a.shape, lambda bi: (0,) * a.ndim)
    return pl.pallas_call(
        functools.partial(_s_cmp_win_kernel, nbc=nbc, n_blocks=n_blocks, width=width, n_sel=n_sel, wlen=wlen),
        grid=(db,),
        in_specs=[one(NSA_HEADS, HEAD_DIM), one(nbc, KV_W), one(nbc, KV_W), const(cbias),
                  pl.BlockSpec((None, 2 * wlen, HEAD_DIM), lambda bi: (bi, 0, 0)),
                  pl.BlockSpec((None, 2 * wlen, HEAD_DIM), lambda bi: (bi, 0, 0)),
                  one(1, KV_W), one(1, KV_W), const(wbias), const(b0)],
        out_specs=[one(NSA_HEADS, HEAD_DIM), one(NSA_HEADS, HEAD_DIM), one(NSA_KV_GROUPS, width)],
        out_shape=[jax.ShapeDtypeStruct((db, NSA_HEADS, HEAD_DIM), F32),
                   jax.ShapeDtypeStruct((db, NSA_HEADS, HEAD_DIM), F32),
                   jax.ShapeDtypeStruct((db, NSA_KV_GROUPS, width), F32)],
        scratch_shapes=[pltpu.VMEM((1, width), jnp.int32), pltpu.VMEM((1, width), F32)],
        compiler_params=_cparams("parallel"),
        name="sample_cmp_win",
    )(q, kcomp, vcomp, cbias, wk, wv, nk, nv, wbias, b0)


def _s_sel_kernel(idx_ref, hp_ref, q_ref, nk_ref, nv_ref, tb_ref, c_ref, *rest, n_sel, cur, n_near):
    del hp_ref
    k_refs, v_refs, o_ref = rest[:n_sel], rest[n_sel:2 * n_sel], rest[2 * n_sel]
    bi = pl.program_id(0)
    g = pl.program_id(1)
    q = q_ref[0].astype(BF16)
    nk = jnp.where(g == 0, nk_ref[0][:, :HEAD_DIM], nk_ref[0][:, HEAD_DIM:])
    nv = jnp.where(g == 0, nv_ref[0][:, :HEAD_DIM], nv_ref[0][:, HEAD_DIM:])
    rowi = lax.broadcasted_iota(jnp.int32, (SEL_BLOCK, 1), 0)
    ss, vs = [], []
    for r in range(n_sel):
        idx = idx_ref[(bi * NSA_KV_GROUPS + g) * n_sel + r]
        first = (rowi == 0) & (idx >= cur)
        kr = jnp.where(first, nk, k_refs[r][pl.ds(g, SEL_BLOCK, stride=2), :]).astype(BF16)
        vs.append(jnp.where(first, nv, v_refs[r][pl.ds(g, SEL_BLOCK, stride=2), :]).astype(BF16))
        u = jnp.clip(idx - (cur - (n_near - 1)), 0, n_near - 1)
        bias = jnp.where(idx >= cur - (n_near - 1), tb_ref[u], c_ref[...])
        ss.append(_dot_nt(q, kr) * ATTN_SCALE + bias)
    m = ss[0].max(axis=-1, keepdims=True)
    for s in ss[1:]:
        m = jnp.maximum(m, s.max(axis=-1, keepdims=True))
    den = jnp.zeros((NSA_HEADS, 1), F32)
    o = jnp.zeros((NSA_HEADS, HEAD_DIM), F32)
    for s, v in zip(ss, vs):
        e = jnp.where(s > 0.5 * NEG, jnp.exp(s - m), 0.0)
        den = den + e.sum(axis=-1, keepdims=True)
        o = o + _dot(e.astype(BF16), v)
    o_ref[0, 0] = o / jnp.where(den > 0, den, 1.0)


def _s_sel(idx_flat, hp_flat, q, nk, nv, tb, c, cache_k, cache_v, n_sel, cur):
    db = q.shape[0]
    rows = SEL_BLOCK * NSA_KV_GROUPS

    def page(r):
        return pl.BlockSpec((None, rows, HEAD_DIM),
                            lambda bi, g, idx, hp, r=r: (hp[(bi * NSA_KV_GROUPS + g) * n_sel + r], 0, 0))

    grid_spec = pltpu.PrefetchScalarGridSpec(
        num_scalar_prefetch=2,
        grid=(db, NSA_KV_GROUPS),
        in_specs=[pl.BlockSpec((1, NSA_HEADS, HEAD_DIM), lambda bi, g, idx, hp: (bi, 0, 0)),
                  pl.BlockSpec((1, 1, KV_W), lambda bi, g, idx, hp: (bi, 0, 0)),
                  pl.BlockSpec((1, 1, KV_W), lambda bi, g, idx, hp: (bi, 0, 0)),
                  pl.BlockSpec(tb.shape, lambda bi, g, idx, hp: (0, 0, 0)),
                  pl.BlockSpec(c.shape, lambda bi, g, idx, hp: (0, 0))]
        + [page(r) for r in range(n_sel)] + [page(r) for r in range(n_sel)],
        out_specs=pl.BlockSpec((1, 1, NSA_HEADS, HEAD_DIM), lambda bi, g, idx, hp: (bi, g, 0, 0)),
    )
    return pl.pallas_call(
        functools.partial(_s_sel_kernel, n_sel=n_sel, cur=cur, n_near=tb.shape[0]),
        grid_spec=grid_spec,
        out_shape=jax.ShapeDtypeStruct((db, NSA_KV_GROUPS, NSA_HEADS, HEAD_DIM), F32),
        compiler_params=_cparams("parallel", "parallel"),
        name="sample_sel",
    )(idx_flat, hp_flat, q, nk, nv, tb, c, *([cache_k] * n_sel), *([cache_v] * n_sel))


def _s_index_kernel(pt_ref, iq_ref, iw_ref, nik_ref, *rest, pg, past, n_keep, width):
    del pt_ref
    pages, keep_ref, sc_ref, key_ref = rest[:pg], rest[pg], rest[pg + 1], rest[pg + 2]
    j = pl.program_id(1)
    span = pg * PAGE_SIZE
    iqf = iq_ref[0]
    iw = iw_ref[0]
    ik = jnp.concatenate([p[...] for p in pages], axis=0)
    lg = jnp.maximum(_dot_nt_full(iqf, ik) * IDX_DIM ** -0.5, 0.0)
    sc = jnp.sum(lg * iw, axis=0, keepdims=True) * IDX_HEADS ** -0.5 + 0.0
    sc_ref[:, pl.ds(pl.multiple_of(j * span, span), span)] = sc

    @pl.when(j == pl.num_programs(1) - 1)
    def _():
        lg_new = jnp.maximum(jnp.sum(iqf * nik_ref[0], axis=-1, keepdims=True) * IDX_DIM ** -0.5, 0.0)
        sc_new = jnp.sum(lg_new * iw, axis=0, keepdims=True) * IDX_HEADS ** -0.5 + 0.0
        lane = lax.broadcasted_iota(jnp.int32, (1, width - past), 1)
        sc_ref[:, past:] = jnp.where(lane == 0, sc_new, -jnp.inf)
        key_ref[...] = _order_key(sc_ref[...])
        _row_topk(key_ref, keep_ref.at[0], n_keep, width)


def _s_index(page_table, iq, iw, nik, cache_idx, n_keep):
    db, n_pages = page_table.shape
    pg = math.gcd(n_pages, PAGE_GROUP)
    past = n_pages * PAGE_SIZE
    width = past + 128

    def page(r):
        return pl.BlockSpec((None, PAGE_SIZE, IDX_DIM), lambda bi, j, pt, r=r: (pt[bi, j * pg + r], 0, 0))

    grid_spec = pltpu.PrefetchScalarGridSpec(
        num_scalar_prefetch=1,
        grid=(db, n_pages // pg),
        in_specs=[pl.BlockSpec((1, IDX_HEADS, IDX_DIM), lambda bi, j, pt: (bi, 0, 0)),
                  pl.BlockSpec((1, IDX_HEADS, 1), lambda bi, j, pt: (bi, 0, 0)),
                  pl.BlockSpec((1, 1, IDX_DIM), lambda bi, j, pt: (bi, 0, 0))]
        + [page(r) for r in range(pg)],
        out_specs=pl.BlockSpec((1, 1, width), lambda bi, j, pt: (bi, 0, 0)),
        scratch_shapes=[pltpu.VMEM((1, width), F32), pltpu.VMEM((1, width), jnp.int32)],
    )
    return pl.pallas_call(
        functools.partial(_s_index_kernel, pg=pg, past=past, n_keep=n_keep, width=width),
        grid_spec=grid_spec,
        out_shape=jax.ShapeDtypeStruct((db, 1, width), F32),
        compiler_params=_cparams("parallel", "arbitrary"),
        name="sample_index",
    )(page_table, iq, iw, nik, *([cache_idx] * pg))


def _s_dsa_kernel(pt_ref, q_ref, keep_ref, bt_ref, c_ref, b0_ref, nk_ref, nv_ref, *rest, pg, past):
    del pt_ref
    kp, vp, o_ref = rest[:pg], rest[pg:2 * pg], rest[2 * pg]
    m_ref, l_ref, acc_ref = rest[2 * pg + 1:]
    hpg = DSA_HPG
    j = pl.program_id(1)
    last = pl.num_programs(1) - 1
    span = pg * PAGE_SIZE

    @pl.when(j == 0)
    def _():
        m_ref[...] = jnp.full_like(m_ref, NEG)
        l_ref[...] = jnp.zeros_like(l_ref)
        acc_ref[...] = jnp.zeros_like(acc_ref)

    qf = q_ref[0]
    q = qf.astype(BF16)
    row = lax.broadcasted_iota(jnp.int32, (DSA_HEADS, 1), 0)
    g0 = row < hpg
    rows = lambda refs, g: jnp.concatenate([p[pl.ds(g, PAGE_SIZE, stride=2), :] for p in refs], axis=0).astype(BF16)
    s = _by_group(hpg, _dot_nt(q, rows(kp, 0)), _dot_nt(q, rows(kp, 1))) * ATTN_SCALE
    keep = keep_ref[0, :, pl.ds(pl.multiple_of(j * span, span), span)] > 0.5
    s = jnp.where(keep, s + jnp.where(j == last, bt_ref[...], c_ref[...]), NEG)
    m_old = m_ref[...]
    m_new = jnp.maximum(m_old, jnp.max(s, axis=-1, keepdims=True))
    alpha = jnp.exp(m_old - m_new)
    e = jnp.where(keep, jnp.exp(s - m_new), 0.0)
    l_ref[...] = alpha * l_ref[...] + jnp.sum(e, axis=-1, keepdims=True)
    acc_ref[...] = (alpha * acc_ref[...] + _dot(jnp.where(g0, e, 0.0).astype(BF16), rows(vp, 0))
                    + _dot(jnp.where(g0, 0.0, e).astype(BF16), rows(vp, 1)))
    m_ref[...] = m_new

    @pl.when(j == last)
    def _():
        keep_new = keep_ref[0, :, past:past + 1] > 0.5
        nk = _by_group(hpg, nk_ref[0][:, :HEAD_DIM], nk_ref[0][:, HEAD_DIM:])
        nv = _by_group(hpg, nv_ref[0][:, :HEAD_DIM], nv_ref[0][:, HEAD_DIM:])
        s_new = jnp.sum(qf * nk, axis=-1, keepdims=True) * ATTN_SCALE + b0_ref[...]
        s_new = jnp.where(keep_new, s_new, NEG)
        m_old2 = m_ref[...]
        m2 = jnp.maximum(m_old2, s_new)
        a2 = jnp.exp(m_old2 - m2)
        e_new = jnp.where(keep_new, jnp.exp(s_new - m2), 0.0)
        l = a2 * l_ref[...] + e_new
        o_ref[0] = (a2 * acc_ref[...] + e_new * nv) / jnp.where(l > 0, l, 1.0)


def _s_dsa(page_table, q, keep, btail, c, b0, nk, nv, cache_k, cache_v):
    db, n_pages = page_table.shape
    pg = math.gcd(n_pages, PAGE_GROUP)
    past = n_pages * PAGE_SIZE
    width = keep.shape[-1]
    rows = PAGE_SIZE * DSA_KV_GROUPS

    def page(r):
        return pl.BlockSpec((None, rows, HEAD_DIM), lambda bi, j, pt, r=r: (pt[bi, j * pg + r], 0, 0))

    grid_spec = pltpu.PrefetchScalarGridSpec(
        num_scalar_prefetch=1,
        grid=(db, n_pages // pg),
        in_specs=[pl.BlockSpec((1, DSA_HEADS, HEAD_DIM), lambda bi, j, pt: (bi, 0, 0)),
                  pl.BlockSpec((1, 1, width), lambda bi, j, pt: (bi, 0, 0)),
                  pl.BlockSpec(btail.shape, lambda bi, j, pt: (0, 0)),
                  pl.BlockSpec(c.shape, lambda bi, j, pt: (0, 0)),
                  pl.BlockSpec(b0.shape, lambda bi, j, pt: (0, 0)),
                  pl.BlockSpec((1, 1, KV_W), lambda bi, j, pt: (bi, 0, 0)),
                  pl.BlockSpec((1, 1, KV_W), lambda bi, j, pt: (bi, 0, 0))]
        + [page(r) for r in range(pg)] + [page(r) for r in range(pg)],
        out_specs=pl.BlockSpec((1, DSA_HEADS, HEAD_DIM), lambda bi, j, pt: (bi, 0, 0)),
        scratch_shapes=[pltpu.VMEM((DSA_HEADS, 1), F32), pltpu.VMEM((DSA_HEADS, 1), F32),
                        pltpu.VMEM((DSA_HEADS, HEAD_DIM), F32)],
    )
    return pl.pallas_call(
        functools.partial(_s_dsa_kernel, pg=pg, past=past),
        grid_spec=grid_spec,
        out_shape=jax.ShapeDtypeStruct((db, DSA_HEADS, HEAD_DIM), F32),
        compiler_params=_cparams("parallel", "arbitrary"),
        name="sample_dsa",
    )(page_table, q, keep, btail, c, b0, nk, nv, *([cache_k] * pg), *([cache_v] * pg))


def _split_w_in(w_in):
    points = np.cumsum(IN_COLS)[:-1].tolist()
    q_a, kv_a, g_a, q_b, kv_b, iq, ik, iw, g_m = jnp.split(w_in, points, axis=-1)
    pad = jnp.zeros((D_MODEL, MISC_W - IDX_DIM - IDX_HEADS - 3 * NSA_HEADS), w_in.dtype)
    w_kv = jnp.concatenate([kv_a, kv_b, ik, iw, g_a, pad], axis=-1)
    w_q = jnp.concatenate([q_a, q_b, iq], axis=-1)
    return w_kv, w_q, g_m[:, :D_MODEL], g_m[:, D_MODEL:]


KV_WIDTHS = (KV_W,) * 8 + (MISC_W,)
Q_WIDTHS = (NSA_HEADS * HEAD_DIM, DSA_HEADS * HEAD_DIM, IDX_HEADS * IDX_DIM)


def _row_tile(m, cap):
    tm = math.gcd(m, cap)
    assert tm % 8 == 0 or tm == m
    return tm


def kernel(x_prompt, x_sample, cache_cmp_k, cache_cmp_v, cache_slc_k, cache_slc_v, state_win_k, state_win_v,
           cache_dsa_k, cache_dsa_v, cache_idx_k, page_table, rel_bias_table, w_in, cmp_pe, cmp_w1, cmp_w2,
           w_up_a, w_up_b, w_o, ln1_g, ln1_b, w_router, router_bias, moe_w_gate, moe_w_up, moe_w_down,
           sh_w_gate, sh_w_up, sh_w_down, ln2_g, ln2_b):
    assert w_in.shape[0] == DEPTH == 1
    b, t, _ = x_prompt.shape
    db, dt, _ = x_sample.shape
    assert dt == 1
    n_pool = cache_cmp_k.shape[1]
    n_pages = page_table.shape[1]
    tq = Q_TILE
    assert t % tq == 0 and t % CMP_BLOCK == 0
    n_p, n_s = b * t, db * dt

    tbl_a = rel_bias_table[:, :NSA_HEADS]
    tbl_b = rel_bias_table[:, NSA_HEADS:]
    mixer_w32 = _split_w_in(w_in[0]) + (w_up_a[0], w_up_b[0], w_o[0])
    w_kv32, w_q32, w_ga32, w_gb32, w_ua32, w_ub32, w_o32 = mixer_w32
    w_kv, w_q, w_ga, w_gb, w_ua, w_ub, w_ob = [w.astype(BF16) for w in mixer_w32]
    cw = [_compress_weights(cmp_pe[0, i], cmp_w1[0, i], cmp_w2[0, i]) for i in range(2)]

    xp = x_prompt.reshape(n_p, D_MODEL)
    xs = x_sample.reshape(n_s, D_MODEL)
    xpb = xp.astype(BF16)
    tm_p = _row_tile(n_p, 512)
    kc, vc, ks, vs, kw, vw, kb, vb, misc = _project(xpb, w_kv, KV_WIDTHS, tm_p)
    qa, qb, iq = _project(xpb, w_q, Q_WIDTHS, tm_p)
    s_kc, s_vc, s_ks, s_vs, s_kw, s_vw, s_kb, s_vb, s_misc = _project(xs, w_kv32, KV_WIDTHS, n_s)
    s_qa, s_qb, s_iq = _project(xs, w_q32, Q_WIDTHS, n_s)

    nb = t // CMP_BLOCK
    r3 = lambda a: a.reshape(b, t, -1)
    kcomp = _compress(kc.reshape(b * nb, CMP_BLOCK, KV_W), *cw[0]).reshape(b, nb, KV_W)
    vcomp = _compress(vc.reshape(b * nb, CMP_BLOCK, KV_W), *cw[1]).reshape(b, nb, KV_W)
    cdist = jnp.arange(t)[:, None] - (jnp.arange(nb) * CMP_BLOCK + CMP_BLOCK - 1)[None, :]
    cbias = jnp.where(cdist >= 0, jnp.moveaxis(_bias_of(tbl_a, cdist), -1, 0), NEG)
    o_cmp, selmask = _cmp_select(r3(qa), kcomp, vcomp, cbias, tq)

    assert MAX_DISTANCE <= tq
    bias2_a, dist2 = _toeplitz_bias(tbl_a, tq, 2)
    bias2_b, _ = _toeplitz_bias(tbl_b, tq, 2)
    c_a, c_b = tbl_a[N_BUCKETS - 1], tbl_b[N_BUCKETS - 1]
    d_a = jnp.where(dist2 >= 0, (bias2_a - c_a[:, None, None, None]) * LOG2E, NEG)
    d_b = jnp.where(dist2 >= 0, (bias2_b - c_b[:, None, None, None]) * LOG2E, NEG)
    expand = (jnp.arange(t)[None, :] // SEL_BLOCK == jnp.arange(nb)[:, None]).astype(BF16)
    o_sel = _dense_attn(r3(qa), r3(ks), r3(vs), selmask, expand, d_a, tq, "sel")

    n_wchunks = -(-(WINDOW - 1) // tq) + 1
    bias_w, dist_w = _toeplitz_bias(tbl_a, tq, n_wchunks)
    wtiles = jnp.where((dist_w >= 0) & (dist_w < WINDOW), bias_w, NEG)
    o_win = _window_attn(r3(qa), r3(kw), r3(vw), wtiles, tq)

    n_keep = min(DSA_TOPK, t // 4)
    keepmask = _index_select(r3(iq), r3(misc), tq, n_keep)
    o_b = _dense_attn(r3(qb), r3(kb), r3(vb), keepmask, jnp.zeros((8, 128), BF16), d_b, tq, "dsa")

    tm_m = _row_tile(n_p, 256)
    y_p = _merge_up(xpb, o_cmp.reshape(n_p, -1), o_sel.reshape(n_p, -1), o_win.reshape(n_p, -1),
                    o_b.reshape(n_p, -1), misc, w_ga, w_gb, w_ua, w_ub, tm_m, 1024)
    x1p = _merge_out(xp, y_p, w_ob, ln1_g, ln1_b, tm_m)

    past = n_pages * PAGE_SIZE
    halves = PAGE_SIZE // CMP_BLOCK

    def comp_pool(cache, i):
        c = _compress_pool(cache, cmp_pe[0, i], cmp_w1[0, i], cmp_w2[0, i])
        return c.reshape(n_pool, halves * KV_W)[page_table].reshape(db, n_pages * halves, KV_W)

    s_kcomp = comp_pool(cache_cmp_k[0], 0)
    s_vcomp = comp_pool(cache_cmp_v[0], 1)
    assert past % SEL_BLOCK == 0 and past >= 4 * SEL_BLOCK and PAGE_SIZE == 2 * SEL_BLOCK
    total = past + dt
    nbc = past // CMP_BLOCK
    n_blocks = -(-total // SEL_BLOCK)
    cur = past // SEL_BLOCK
    n_sel = min(N_SEL_BLOCKS, n_blocks)
    col = lambda v: v.reshape(-1, 1)
    new = lambda a: a.reshape(db, 1, KV_W)
    cb_s = _bias_of(tbl_a, past - (jnp.arange(nbc) * CMP_BLOCK + CMP_BLOCK - 1)).T
    w_past = state_win_k.shape[2]
    wdist = w_past - jnp.arange(w_past)
    wb_s = jnp.where(wdist < WINDOW, _bias_of(tbl_a, wdist).T, NEG)
    rows2 = lambda a, n: a.reshape(a.shape[0], n * NSA_KV_GROUPS, HEAD_DIM)
    s_q8 = s_qa.reshape(db, NSA_HEADS, HEAD_DIM)
    so_cmp, so_win, selmask = _s_cmp_win(
        s_q8, s_kcomp, s_vcomp, cb_s, rows2(state_win_k[0], w_past), rows2(state_win_v[0], w_past),
        new(s_kw), new(s_vw), wb_s, col(tbl_a[0]), n_blocks, n_sel)
    sel_idx = lax.top_k(selmask[:, :, :n_blocks], n_sel)[1].astype(jnp.int32)
    sel_page = jnp.take_along_axis(page_table, jnp.minimum(sel_idx // 2, n_pages - 1).reshape(db, -1), axis=1)
    sel_hp = sel_page.reshape(sel_idx.shape) * 2 + sel_idx % 2
    n_near = 4
    ndist = (n_near - 1 - jnp.arange(n_near))[:, None] * SEL_BLOCK - jnp.arange(SEL_BLOCK)[None, :]
    tb_s = jnp.where(ndist[:, None, :] >= 0, jnp.moveaxis(_bias_of(tbl_a, ndist), -1, 1), NEG)
    half_pages = lambda c: c.reshape(n_pool * halves, SEL_BLOCK * NSA_KV_GROUPS, HEAD_DIM)
    so_sel2 = _s_sel(sel_idx.reshape(-1), sel_hp.reshape(-1).astype(jnp.int32), s_q8, new(s_ks), new(s_vs), tb_s,
                     col(tbl_a[N_BUCKETS - 1]), half_pages(cache_slc_k[0]), half_pages(cache_slc_v[0]), n_sel, cur)
    so_sel = jnp.concatenate([so_sel2[:, 0, :NSA_HPG], so_sel2[:, 1, NSA_HPG:]], axis=1)
    keep_s = _s_index(page_table, s_iq.reshape(db, IDX_HEADS, IDX_DIM),
                      s_misc[:, MISC_IW:MISC_IW + IDX_HEADS].reshape(db, IDX_HEADS, 1),
                      s_misc[:, :IDX_DIM].reshape(db, 1, IDX_DIM), cache_idx_k[0], min(DSA_TOPK, total // 4))
    span = math.gcd(n_pages, PAGE_GROUP) * PAGE_SIZE
    assert span >= MAX_DISTANCE
    bt_s = _bias_of(tbl_b, span - jnp.arange(span)).T
    pages2 = lambda c: c.reshape(n_pool, PAGE_SIZE * DSA_KV_GROUPS, HEAD_DIM)
    so_b = _s_dsa(page_table, s_qb.reshape(db, DSA_HEADS, HEAD_DIM), keep_s, bt_s, col(tbl_b[N_BUCKETS - 1]),
                  col(tbl_b[0]), new(s_kb), new(s_vb), pages2(cache_dsa_k[0]), pages2(cache_dsa_v[0]))
    n_win = min(WINDOW, total)
    g4 = lambda a: a.reshape(db, dt, NSA_KV_GROUPS, HEAD_DIM)
    s_wk = jnp.concatenate([state_win_k[0], g4(s_kw)], axis=1)[:, -n_win:]
    s_wv = jnp.concatenate([state_win_v[0], g4(s_vw)], axis=1)[:, -n_win:]
    y_s = _merge_up(xs, so_cmp.reshape(n_s, -1), so_sel.reshape(n_s, -1), so_win.reshape(n_s, -1),
                    so_b.reshape(n_s, -1), s_misc, w_ga32, w_gb32, w_ua32, w_ub32, n_s, 512)
    x1s = _merge_out(xs, y_s, w_o32, ln1_g, ln1_b, n_s)

    wr_t = w_router[0].T
    rb = router_bias[0].reshape(N_EXPERTS, 1)
    eidx_p, wts_p, pos_p, cnt_p = _router(x1p, wr_t, rb, jnp.zeros((N_EXPERTS, 1), jnp.int32), tm_m)
    eidx_s, wts_s, pos_s, cnt = _router(x1s, wr_t, rb, cnt_p, n_s)
    n_tok = n_p + n_s
    eidx = jnp.concatenate([eidx_p[:TOP_K], eidx_s[:TOP_K]], axis=1)
    pos = jnp.concatenate([pos_p[:TOP_K], pos_s[:TOP_K]], axis=1)
    tr = EXPERT_ROWS
    n_asg = n_tok * TOP_K
    counts = cnt[:, 0]
    padded = (counts + tr - 1) // tr * tr
    pend = jnp.cumsum(padded)
    pad_start = pend - padded
    n_rows = -(-n_asg // tr) * tr + N_EXPERTS * tr
    n_blk = n_rows // tr
    dest = pos + jnp.sum(jnp.where(eidx[..., None] == jnp.arange(N_EXPERTS), pad_start, 0), axis=-1)
    blk_e = jnp.minimum(jnp.sum(pend[None, :] <= (jnp.arange(n_blk) * tr)[:, None], axis=1), N_EXPERTS - 1)
    n_valid = (pend[-1] // tr).astype(jnp.int32).reshape(1)
    row_tok = jnp.full((n_rows,), n_tok, jnp.int32).at[dest.reshape(-1)].set(
        jnp.tile(jnp.arange(n_tok, dtype=jnp.int32), TOP_K), unique_indices=True)
    xg = jnp.concatenate([x1p, x1s, jnp.zeros((1, D_MODEL), F32)], axis=0)[row_tok]
    out_rows = _experts(blk_e.astype(jnp.int32), n_valid, xg, moe_w_gate[0], moe_w_up[0], moe_w_down[0], tr)
    rows6 = out_rows[dest]

    sg, su, sd = sh_w_gate[0].astype(BF16), sh_w_up[0].astype(BF16), sh_w_down[0].astype(BF16)
    y_prompt = _ffn_out(x1p, rows6, wts_p.T, sg, su, sd, ln2_g, ln2_b, tm_m).reshape(b, t, D_MODEL)
    y_sample = _ffn_out(x1s, rows6[:, n_p:], wts_s.T, sg, su, sd, ln2_g, ln2_b, n_s).reshape(db, dt, D_MODEL)

    n_win = min(WINDOW, t)
    st = lambda a: a.reshape(1, b, t, NSA_KV_GROUPS, HEAD_DIM)
    ss = lambda a: a.reshape(1, db, dt, NSA_KV_GROUPS, HEAD_DIM)
    return (y_prompt, y_sample,
            st(kc), st(vc), st(ks), st(vs), st(kw)[:, :, -n_win:], st(vw)[:, :, -n_win:], st(kb), st(vb),
            misc[:, :IDX_DIM].reshape(1, b, t, IDX_DIM),
            ss(s_kc), ss(s_vc), ss(s_ks), ss(s_vs), s_wk[None], s_wv[None], ss(s_kb), ss(s_vb),
            s_misc[:, :IDX_DIM].reshape(1, db, dt, IDX_DIM))
```

```python
import functools
import math

import jax
import jax.numpy as jnp
import numpy as np
from jax import lax
from jax.experimental import pallas as pl
from jax.experimental.pallas import tpu as pltpu

D_MODEL = 2048
PAGE_SIZE = 128
HEAD_DIM = 128
NSA_HEADS = 8
NSA_KV_GROUPS = 2
NSA_HPG = NSA_HEADS // NSA_KV_GROUPS
CMP_BLOCK = 64
CMP_HIDDEN = 128
SEL_BLOCK = 64
N_SEL_BLOCKS = 16
WINDOW = 512
DSA_HEADS = 8
DSA_KV_GROUPS = 2
DSA_HPG = DSA_HEADS // DSA_KV_GROUPS
IDX_HEADS = 8
IDX_DIM = 64
DSA_TOPK = 256
N_BUCKETS = 32
MAX_EXACT = 16
MAX_DISTANCE = 128
N_EXPERTS = 64
EXPERT_FF = 512
SHARED_FF = 512
TOP_K = 6
N_EXPERT_GROUPS = 8
TOPK_GROUPS = 4
ROUTED_SCALE = 2.5
LN_EPS = 1e-5
ATTN_SCALE = HEAD_DIM ** -0.5
DEPTH = 1
DEEPNORM_ALPHA = (2 * DEPTH) ** 0.25
IN_COLS = (NSA_HEADS * HEAD_DIM, 6 * NSA_KV_GROUPS * HEAD_DIM, 3 * NSA_HEADS,
           DSA_HEADS * HEAD_DIM, 2 * DSA_KV_GROUPS * HEAD_DIM,
           IDX_HEADS * IDX_DIM, IDX_DIM, IDX_HEADS, 2 * D_MODEL)

KV_W = NSA_KV_GROUPS * HEAD_DIM
MISC_W = 128
MISC_IW = IDX_DIM
MISC_GA = IDX_DIM + IDX_HEADS
NEG = -1e30
Q_TILE = 256
EXPERT_ROWS = 256
VMEM_LIMIT = 56 * 1024 * 1024

BF16 = jnp.bfloat16
F32 = jnp.float32


def _cparams(*sem):
    return pltpu.CompilerParams(dimension_semantics=sem, vmem_limit_bytes=VMEM_LIMIT)


def _dot(a, b):
    return jnp.dot(a, b, preferred_element_type=F32)


def _dot_nt(a, b):
    return lax.dot_general(a, b, (((1,), (1,)), ((), ())), preferred_element_type=F32)


def _dot_nt_full(a, b):
    return lax.dot_general(a.astype(F32), b.astype(F32), (((1,), (1,)), ((), ())),
                           preferred_element_type=F32, precision=lax.Precision.HIGHEST)


def _mm(a, w):
    if w.dtype == F32:
        return jnp.dot(a.astype(F32), w, preferred_element_type=F32, precision=lax.Precision.HIGHEST)
    return jnp.dot(a.astype(w.dtype), w, preferred_element_type=F32)


def _proj_kernel(x_ref, w_ref, *o_refs, widths):
    acc = _mm(x_ref[...], w_ref[...])
    off = 0
    for o_ref, wd in zip(o_refs, widths):
        o_ref[...] = acc[:, off:off + wd]
        off += wd


def _project(x, w, widths, tm):
    m, k = x.shape
    n = w.shape[1]
    assert sum(widths) == n and m % tm == 0
    return pl.pallas_call(
        functools.partial(_proj_kernel, widths=widths),
        grid=(m // tm,),
        in_specs=[pl.BlockSpec((tm, k), lambda i: (i, 0)),
                  pl.BlockSpec((k, n), lambda i: (0, 0))],
        out_specs=[pl.BlockSpec((tm, wd), lambda i: (i, 0)) for wd in widths],
        out_shape=[jax.ShapeDtypeStruct((m, wd), F32) for wd in widths],
        compiler_params=_cparams("parallel"),
        name="project",
    )(x, w)


CMP_JCHUNK = 8


def _compress_kernel(x_ref, pe_ref, w1_ref, w2_ref, o_ref, acc_ref):
    jc = pl.program_id(1)

    @pl.when(jc == 0)
    def _():
        acc_ref[...] = jnp.zeros_like(acc_ref)

    acc = acc_ref[...]
    for jj in range(CMP_JCHUNK):
        lhs = (x_ref[:, jj, :] + pe_ref[jj:jj + 1, :]).astype(BF16)
        acc = acc + _dot(lhs, w1_ref[jj])
    acc_ref[...] = acc

    @pl.when(jc == pl.num_programs(1) - 1)
    def _():
        h = jax.nn.gelu(acc_ref[...])
        o_ref[...] = _dot(h.astype(BF16), w2_ref[...])


def _compress(rows, pe2, w1big, w2big):
    r = rows.shape[0]
    tr = math.gcd(r, 1024)
    assert tr % 8 == 0
    return pl.pallas_call(
        _compress_kernel,
        grid=(r // tr, CMP_BLOCK // CMP_JCHUNK),
        in_specs=[pl.BlockSpec((tr, CMP_JCHUNK, KV_W), lambda i, j: (i, j, 0)),
                  pl.BlockSpec((CMP_JCHUNK, KV_W), lambda i, j: (j, 0)),
                  pl.BlockSpec((CMP_JCHUNK, KV_W, KV_W), lambda i, j: (j, 0, 0)),
                  pl.BlockSpec((KV_W, KV_W), lambda i, j: (0, 0))],
        out_specs=pl.BlockSpec((tr, KV_W), lambda i, j: (i, 0)),
        out_shape=jax.ShapeDtypeStruct((r, KV_W), F32),
        scratch_shapes=[pltpu.VMEM((tr, KV_W), F32)],
        compiler_params=_cparams("parallel", "arbitrary"),
        name="compress",
    )(rows, pe2, w1big, w2big)


def _rows_at(x_ref, r):
    n, rows, w = x_ref.shape
    return x_ref.reshape(n * rows, w)[pl.ds(r, n, stride=rows), :]


def _compress_pool_kernel(x_ref, pe_ref, w1_ref, w2_ref, o_ref, acc_ref, *, tr):
    jc = pl.program_id(1)

    @pl.when(jc == 0)
    def _():
        acc_ref[...] = jnp.zeros_like(acc_ref)

    for g in range(NSA_KV_GROUPS):
        acc = acc_ref[g]
        for jp in range(CMP_JCHUNK // 2):
            parts = []
            for u in range(2):
                jj = 2 * jp + u
                xs = _rows_at(x_ref, 2 * jj + g) + pe_ref[jj:jj + 1, :]
                parts.append(xs.astype(BF16))
            acc = acc + _dot(jnp.concatenate(parts, axis=-1), w1_ref[jp])
        acc_ref[g] = acc

    @pl.when(jc == pl.num_programs(1) - 1)
    def _():
        for g in range(NSA_KV_GROUPS):
            h = jax.nn.gelu(acc_ref[g])
            o_ref[:, g * HEAD_DIM:(g + 1) * HEAD_DIM] = _dot(h.astype(BF16), w2_ref[...])


def _compress_pool(cache, pe, w1, w2):
    n_pool = cache.shape[0]
    r = n_pool * (PAGE_SIZE // CMP_BLOCK)
    rows = cache.reshape(r, CMP_BLOCK * NSA_KV_GROUPS, HEAD_DIM)
    tr = math.gcd(r, 1024)
    assert tr % 8 == 0
    rows_per = 2 * CMP_JCHUNK
    w1p = w1.reshape(CMP_BLOCK // 2, 2 * HEAD_DIM, CMP_HIDDEN).astype(BF16)
    return pl.pallas_call(
        functools.partial(_compress_pool_kernel, tr=tr),
        grid=(r // tr, CMP_BLOCK // CMP_JCHUNK),
        in_specs=[pl.BlockSpec((tr, rows_per, HEAD_DIM), lambda i, j: (i, j, 0)),
                  pl.BlockSpec((CMP_JCHUNK, HEAD_DIM), lambda i, j: (j, 0)),
                  pl.BlockSpec((CMP_JCHUNK // 2, 2 * HEAD_DIM, CMP_HIDDEN), lambda i, j: (j, 0, 0)),
                  pl.BlockSpec((CMP_HIDDEN, HEAD_DIM), lambda i, j: (0, 0))],
        out_specs=pl.BlockSpec((tr, KV_W), lambda i, j: (i, 0)),
        out_shape=jax.ShapeDtypeStruct((r, KV_W), F32),
        scratch_shapes=[pltpu.VMEM((NSA_KV_GROUPS, tr, CMP_HIDDEN), F32)],
        compiler_params=_cparams("parallel", "arbitrary"),
        name="compress_pool",
    )(rows, pe, w1p, w2.astype(BF16))


def _compress_weights(pe, w1, w2):
    pe2 = jnp.concatenate([pe, pe], axis=-1)
    z1 = jnp.zeros_like(w1)
    w1big = jnp.concatenate([jnp.concatenate([w1, z1], axis=2), jnp.concatenate([z1, w1], axis=2)], axis=1)
    z2 = jnp.zeros_like(w2)
    w2big = jnp.concatenate([jnp.concatenate([w2, z2], axis=1), jnp.concatenate([z2, w2], axis=1)], axis=0)
    return pe2, w1big.astype(BF16), w2big.astype(BF16)


def _dot_tn(a, b):
    return lax.dot_general(a, b, (((0,), (0,)), ((), ())), preferred_element_type=F32)


def _cmp_select_kernel(q_ref, kc_ref, vc_ref, cb_ref, o_ref, sel_ref, *, tq, nb, n_sel):
    qi = pl.program_id(1)
    t = qi * tq + lax.broadcasted_iota(jnp.int32, (nb, tq), 1)
    j = lax.broadcasted_iota(jnp.int32, (nb, tq), 0)
    cur = t // SEL_BLOCK
    for g in range(NSA_KV_GROUPS):
        kc = kc_ref[0, :, g * HEAD_DIM:(g + 1) * HEAD_DIM].astype(BF16)
        vc = vc_ref[0, :, g * HEAD_DIM:(g + 1) * HEAD_DIM].astype(BF16)
        imp = jnp.zeros((nb, tq), F32)
        for h in range(NSA_HPG):
            hh = g * NSA_HPG + h
            q = q_ref[0, :, hh * HEAD_DIM:(hh + 1) * HEAD_DIM].astype(BF16)
            cb = cb_ref[hh]
            valid = cb > 0.5 * NEG
            s = _dot_nt(kc, q) * ATTN_SCALE + cb
            m = jnp.max(s, axis=0, keepdims=True)
            e = jnp.where(valid, jnp.exp(s - m), 0.0)
            den = jnp.sum(e, axis=0, keepdims=True)
            p = e / jnp.where(den > 0, den, 1.0)
            o_ref[0, :, hh * HEAD_DIM:(hh + 1) * HEAD_DIM] = _dot_tn(p.astype(BF16), vc)
            imp = imp + p
        forced = (j == 0) | (j == cur) | (j == cur - 1)
        score = jnp.where(forced, jnp.inf, jnp.where(j <= cur, imp, -jnp.inf))
        rank = jnp.zeros((nb, tq), jnp.int32)
        for k in range(nb):
            row = score[k:k + 1, :]
            rank = rank + jnp.where(row > score, 1, 0) + jnp.where(row == score, jnp.where(k < j, 1, 0), 0)
        sel_ref[0, g] = jnp.where(rank < n_sel, 1.0, 0.0).astype(F32)


def _cmp_select(qa, kcomp, vcomp, cbias, tq):
    b, t, _ = qa.shape
    nb = kcomp.shape[1]
    n_sel = min(N_SEL_BLOCKS, nb)
    return pl.pallas_call(
        functools.partial(_cmp_select_kernel, tq=tq, nb=nb, n_sel=n_sel),
        grid=(b, t // tq),
        in_specs=[pl.BlockSpec((1, tq, NSA_HEADS * HEAD_DIM), lambda bi, qi: (bi, qi, 0)),
                  pl.BlockSpec((1, nb, KV_W), lambda bi, qi: (bi, 0, 0)),
                  pl.BlockSpec((1, nb, KV_W), lambda bi, qi: (bi, 0, 0)),
                  pl.BlockSpec((NSA_HEADS, nb, tq), lambda bi, qi: (0, 0, qi))],
        out_specs=[pl.BlockSpec((1, tq, NSA_HEADS * HEAD_DIM), lambda bi, qi: (bi, qi, 0)),
                   pl.BlockSpec((1, NSA_KV_GROUPS, nb, tq), lambda bi, qi: (bi, 0, 0, qi))],
        out_shape=[jax.ShapeDtypeStruct((b, t, NSA_HEADS * HEAD_DIM), F32),
                   jax.ShapeDtypeStruct((b, NSA_KV_GROUPS, nb, t), F32)],
        compiler_params=_cparams("parallel", "parallel"),
        name="cmp_select",
    )(qa, kcomp, vcomp, cbias)


LOG2E = math.log2(math.e)


def _dense_attn_kernel(q_ref, k_ref, v_ref, mask_ref, expand_ref, d_ref, o_ref, *, tq, n_tiles, hpg, mode):
    qi = pl.program_id(1)

    def attend(n_ch):
        chunk = lambda c: slice(c * tq, (c + 1) * tq)
        if mode == "sel":
            mbs = [jnp.where(_dot_tn(mask_ref[0, 0].astype(BF16), expand_ref[:, chunk(c)]) > 0.5, 0.0, NEG)
                   for c in range(n_ch)]
        else:
            mbs = [jnp.where(mask_ref[0, :, chunk(c)].astype(F32) > 0.5, 0.0, NEG) for c in range(n_ch)]
        ks = [k_ref[0, chunk(c), :].astype(BF16) for c in range(n_ch)]
        vs = [v_ref[0, chunk(c), :].astype(BF16) for c in range(n_ch)]
        for hh in range(hpg):
            q = (q_ref[0, :, hh * HEAD_DIM:(hh + 1) * HEAD_DIM] * (ATTN_SCALE * LOG2E)).astype(BF16)
            ss = []
            for c in range(n_ch):
                s = _dot_nt(q, ks[c]) + mbs[c]
                if n_ch - 1 - c < 2:
                    s = s + d_ref[hh, n_ch - 1 - c]
                ss.append(s)
            mx = ss[0]
            for s in ss[1:]:
                mx = jnp.maximum(mx, s)
            m = jnp.max(mx, axis=-1, keepdims=True)
            es = [jnp.exp2(s - m) for s in ss]
            tot = es[0]
            for e in es[1:]:
                tot = tot + e
            den = jnp.sum(tot, axis=-1, keepdims=True)
            o = _dot(es[0].astype(BF16), vs[0])
            for c in range(1, n_ch):
                o = o + _dot(es[c].astype(BF16), vs[c])
            o_ref[0, :, hh * HEAD_DIM:(hh + 1) * HEAD_DIM] = jnp.where(m > 0.5 * NEG, o / den, 0.0)

    for tile in range(n_tiles):
        @pl.when(qi == tile)
        def _(tile=tile):
            attend(tile + 1)


def _dense_attn(q, k, v, mask, expand, dtiles, tq, mode):
    b, t, qw = q.shape
    n_heads = qw // HEAD_DIM
    hpg = n_heads // (k.shape[2] // HEAD_DIM)
    n_groups = n_heads // hpg
    gw = hpg * HEAD_DIM
    if mode == "sel":
        nb = mask.shape[2]
        mask_spec = pl.BlockSpec((1, 1, nb, tq), lambda bi, qi, g: (bi, g, 0, qi))
    else:
        mask_spec = pl.BlockSpec((1, tq, t), lambda bi, qi, g: (bi, qi, 0))
    return pl.pallas_call(
        functools.partial(_dense_attn_kernel, tq=tq, n_tiles=t // tq, hpg=hpg, mode=mode),
        grid=(b, t // tq, n_groups),
        in_specs=[pl.BlockSpec((1, tq, gw), lambda bi, qi, g: (bi, qi, g)),
                  pl.BlockSpec((1, t, HEAD_DIM), lambda bi, qi, g: (bi, 0, g)),
                  pl.BlockSpec((1, t, HEAD_DIM), lambda bi, qi, g: (bi, 0, g)),
                  mask_spec,
                  pl.BlockSpec(expand.shape, lambda bi, qi, g: (0, 0)),
                  pl.BlockSpec((hpg,) + dtiles.shape[1:], lambda bi, qi, g: (g, 0, 0, 0))],
        out_specs=pl.BlockSpec((1, tq, gw), lambda bi, qi, g: (bi, qi, g)),
        out_shape=jax.ShapeDtypeStruct((b, t, qw), F32),
        compiler_params=_cparams("parallel", "parallel", "parallel"),
        name="dense_attn_" + mode,
    )(q, k, v, mask, expand, dtiles)


def _window_attn_kernel(q_ref, k_ref, v_ref, w_ref, o_ref, *, tq, n_chunks, n_heads, hpg):
    qi = pl.program_id(1)
    starts, pens = [], []
    for r in range(n_chunks):
        cj = qi - (n_chunks - 1) + r
        starts.append(pl.multiple_of(jnp.maximum(cj, 0) * tq, tq))
        pens.append(jnp.where(cj < 0, NEG, 0.0).astype(F32))
    for hh in range(n_heads):
        g = hh // hpg
        q = q_ref[0, :, hh * HEAD_DIM:(hh + 1) * HEAD_DIM].astype(BF16)
        ss = []
        for r in range(n_chunks):
            k = k_ref[0, pl.ds(starts[r], tq), g * HEAD_DIM:(g + 1) * HEAD_DIM].astype(BF16)
            ss.append(_dot_nt(q, k) * ATTN_SCALE + (w_ref[hh, n_chunks - 1 - r] + pens[r]))
        m = ss[0].max(axis=-1, keepdims=True)
        for r in range(1, n_chunks):
            m = jnp.maximum(m, ss[r].max(axis=-1, keepdims=True))
        den = jnp.zeros((tq, 1), F32)
        o = jnp.zeros((tq, HEAD_DIM), F32)
        for r in range(n_chunks):
            e = jnp.where(ss[r] > 0.5 * NEG, jnp.exp(ss[r] - m), 0.0)
            den = den + e.sum(axis=-1, keepdims=True)
            v = v_ref[0, pl.ds(starts[r], tq), g * HEAD_DIM:(g + 1) * HEAD_DIM].astype(BF16)
            o = o + _dot(e.astype(BF16), v)
        o_ref[0, :, hh * HEAD_DIM:(hh + 1) * HEAD_DIM] = o / jnp.where(den > 0, den, 1.0)


def _window_attn(q, k, v, wtiles, tq):
    b, t, qw = q.shape
    n_heads = qw // HEAD_DIM
    hpg = n_heads // (k.shape[2] // HEAD_DIM)
    n_chunks = wtiles.shape[1]
    return pl.pallas_call(
        functools.partial(_window_attn_kernel, tq=tq, n_chunks=n_chunks, n_heads=n_heads, hpg=hpg),
        grid=(b, t // tq),
        in_specs=[pl.BlockSpec((1, tq, qw), lambda bi, qi: (bi, qi, 0)),
                  pl.BlockSpec((1, t, k.shape[2]), lambda bi, qi: (bi, 0, 0)),
                  pl.BlockSpec((1, t, v.shape[2]), lambda bi, qi: (bi, 0, 0)),
                  pl.BlockSpec(wtiles.shape, lambda bi, qi: (0, 0, 0, 0))],
        out_specs=pl.BlockSpec((1, tq, qw), lambda bi, qi: (bi, qi, 0)),
        out_shape=jax.ShapeDtypeStruct((b, t, qw), F32),
        compiler_params=_cparams("parallel", "parallel"),
        name="window_attn",
    )(q, k, v, wtiles)


INT_MIN = -2 ** 31


def _topk_mask(key_ref, n_keep, tq, s_len):
    def body(i, thr_u):
        cand_u = thr_u | jnp.left_shift(jnp.int32(1), 31 - i)
        below = (cand_u ^ INT_MIN) - 1
        cnt = jnp.sum(jnp.where(key_ref[...] > below, 1, 0), axis=-1, keepdims=True)
        return jnp.where(cnt >= n_keep, cand_u, thr_u)

    thr_u = lax.fori_loop(0, 32, body, jnp.zeros((tq, 1), jnp.int32))
    thr = thr_u ^ INT_MIN
    return thr


def _index_select_kernel(iq_ref, mq_ref, mk_ref, o_ref, key_ref, *, tq, s_len, n_keep, n_bands):
    qi = pl.program_id(1)
    tiles = (s_len // tq) // n_bands
    wts = mq_ref[0, :, MISC_IW:MISC_IW + IDX_HEADS] * IDX_DIM ** -0.5
    iqs = [iq_ref[0, :, h * IDX_DIM:(h + 1) * IDX_DIM].astype(BF16) for h in range(IDX_HEADS)]
    t = qi * tq + lax.broadcasted_iota(jnp.int32, (tq, tq), 0)
    col = lax.broadcasted_iota(jnp.int32, (tq, tq), 1)

    def select(s_b):
        for c in range(s_b // tq):
            ik = mk_ref[0, c * tq:(c + 1) * tq, 0:IDX_DIM].astype(BF16)
            score = jnp.zeros((tq, tq), F32)
            for h in range(IDX_HEADS):
                score = score + jnp.maximum(_dot_nt(iqs[h], ik), 0.0) * wts[:, h:h + 1]
            score = score * IDX_HEADS ** -0.5 + 0.0
            score = jnp.where(c * tq + col <= t, score, -jnp.inf)
            bits = pltpu.bitcast(score, jnp.int32)
            key_ref[:, c * tq:(c + 1) * tq] = jnp.where(bits < 0, bits ^ 0x7FFFFFFF, bits)

        def bit_step(i, thr_u):
            cand_u = thr_u | jnp.left_shift(jnp.int32(1), 31 - i)
            below = (cand_u ^ INT_MIN) - 1
            cnt = jnp.sum(jnp.where(key_ref[:, :s_b] > below, 1, 0), axis=-1, keepdims=True)
            return jnp.where(cnt >= n_keep, cand_u, thr_u)

        thr = lax.fori_loop(0, 32, bit_step, jnp.zeros((tq, 1), jnp.int32)) ^ INT_MIN
        key = key_ref[:, :s_b]
        gt = key > thr
        eq = key == thr
        need = n_keep - jnp.sum(jnp.where(gt, 1, 0), axis=-1, keepdims=True)
        n_eq = jnp.sum(jnp.where(eq, 1, 0), axis=-1, keepdims=True)
        o_ref[0, :, :s_b] = jnp.where(gt | eq, 1.0, 0.0).astype(o_ref.dtype)
        if s_b < s_len:
            o_ref[0, :, s_b:] = jnp.zeros((tq, s_len - s_b), o_ref.dtype)

        @pl.when(jnp.max(n_eq - need) > 0)
        def _():
            r_i = lax.broadcasted_iota(jnp.int32, (128, 128), 0)
            c_i = lax.broadcasted_iota(jnp.int32, (128, 128), 1)
            tri = jnp.where(r_i <= c_i, 1.0, 0.0).astype(BF16)
            before = jnp.zeros((tq, 1), F32)
            need_f = need.astype(F32)
            for c in range(s_b // 128):
                sl = slice(c * 128, (c + 1) * 128)
                eq_c = eq[:, sl]
                eq_f = jnp.where(eq_c, 1.0, 0.0)
                pref = _dot(eq_f.astype(BF16), tri) + before
                keep = gt[:, sl] | (eq_c & (pref <= need_f))
                o_ref[0, :, sl] = jnp.where(keep, 1.0, 0.0).astype(o_ref.dtype)
                before = before + jnp.sum(eq_f, axis=-1, keepdims=True)

    for band in range(n_bands):
        @pl.when(qi // tiles == band)
        def _(band=band):
            select((band + 1) * tiles * tq)


def _index_select(iq, misc, tq, n_keep):
    b, t, _ = iq.shape
    return pl.pallas_call(
        functools.partial(_index_select_kernel, tq=tq, s_len=t, n_keep=n_keep, n_bands=t // tq),
        grid=(b, t // tq),
        in_specs=[pl.BlockSpec((1, tq, IDX_HEADS * IDX_DIM), lambda bi, qi: (bi, qi, 0)),
                  pl.BlockSpec((1, tq, MISC_W), lambda bi, qi: (bi, qi, 0)),
                  pl.BlockSpec((1, t, MISC_W), lambda bi, qi: (bi, 0, 0))],
        out_specs=pl.BlockSpec((1, tq, t), lambda bi, qi: (bi, qi, 0)),
        out_shape=jax.ShapeDtypeStruct((b, t, t), BF16),
        scratch_shapes=[pltpu.VMEM((tq, t), jnp.int32)],
        compiler_params=_cparams("parallel", "parallel"),
        name="index_select",
    )(iq, misc, misc)


def _merge_up_kernel(x_ref, oc_ref, os_ref, ow_ref, ob_ref, misc_ref, wga_ref, wgb_ref, wua_ref, wub_ref,
                     y_ref, oa_ref):
    ga = jax.nn.sigmoid(misc_ref[:, MISC_GA:MISC_GA + 3 * NSA_HEADS])
    for hh in range(NSA_HEADS):
        sl = slice(hh * HEAD_DIM, (hh + 1) * HEAD_DIM)
        oa = (ga[:, hh:hh + 1] * oc_ref[:, sl] + ga[:, NSA_HEADS + hh:NSA_HEADS + hh + 1] * os_ref[:, sl]
              + ga[:, 2 * NSA_HEADS + hh:2 * NSA_HEADS + hh + 1] * ow_ref[:, sl])
        oa_ref[:, sl] = oa.astype(oa_ref.dtype)
    x = x_ref[...]
    ya = _mm(oa_ref[...], wua_ref[...])
    yb = _mm(ob_ref[...], wub_ref[...])
    g_a = jax.nn.sigmoid(_mm(x, wga_ref[...]))
    g_b = jax.nn.sigmoid(_mm(x, wgb_ref[...]))
    y_ref[...] = (g_a * ya + g_b * yb).astype(y_ref.dtype)


def _merge_up(xb, o_cmp, o_sel, o_win, o_b, misc, wga, wgb, wua, wub, tm, tn):
    m = xb.shape[0]
    mx = wua.dtype
    aw = NSA_HEADS * HEAD_DIM
    bw = DSA_HEADS * HEAD_DIM
    row = lambda w: pl.BlockSpec((tm, w), lambda j, i: (i, 0))
    wcol = lambda k: pl.BlockSpec((k, tn), lambda j, i: (0, j))
    return pl.pallas_call(
        _merge_up_kernel,
        grid=(D_MODEL // tn, m // tm),
        in_specs=[row(D_MODEL), row(aw), row(aw), row(aw), row(bw), row(MISC_W),
                  wcol(D_MODEL), wcol(D_MODEL), wcol(aw), wcol(bw)],
        out_specs=pl.BlockSpec((tm, tn), lambda j, i: (i, j)),
        out_shape=jax.ShapeDtypeStruct((m, D_MODEL), mx),
        scratch_shapes=[pltpu.VMEM((tm, aw), mx)],
        compiler_params=_cparams("parallel", "parallel"),
        name="merge_up",
    )(xb, o_cmp, o_sel, o_win, o_b, misc, wga, wgb, wua, wub)


def _layer_norm(z, g, b):
    mu = jnp.mean(z, axis=-1, keepdims=True)
    zc = z - mu
    var = jnp.mean(zc * zc, axis=-1, keepdims=True)
    return zc * lax.rsqrt(var + LN_EPS) * g + b


def _merge_out_kernel(x_ref, y_ref, wo_ref, g_ref, b_ref, o_ref):
    z = DEEPNORM_ALPHA * x_ref[...] + _mm(y_ref[...], wo_ref[...])
    o_ref[...] = _layer_norm(z, g_ref[...], b_ref[...])


def _merge_out(x, y, wo, g, b, tm):
    m = x.shape[0]
    row = lambda: pl.BlockSpec((tm, D_MODEL), lambda i: (i, 0))
    return pl.pallas_call(
        _merge_out_kernel,
        grid=(m // tm,),
        in_specs=[row(), row(), pl.BlockSpec((D_MODEL, D_MODEL), lambda i: (0, 0)),
                  pl.BlockSpec((1, D_MODEL), lambda i: (0, 0)), pl.BlockSpec((1, D_MODEL), lambda i: (0, 0))],
        out_specs=row(),
        out_shape=jax.ShapeDtypeStruct((m, D_MODEL), F32),
        compiler_params=_cparams("parallel"),
        name="merge_out",
    )(x, y, wo, g, b)


def _router_kernel(x_ref, wr_ref, rb_ref, c0_ref, idx_ref, wt_ref, pos_ref, cnt_ref, run_ref, *, tm):
    epg = N_EXPERTS // N_EXPERT_GROUPS
    logits = lax.dot_general(wr_ref[...], x_ref[...], (((1,), (1,)), ((), ())),
                             preferred_element_type=F32, precision=lax.Precision.HIGHEST)
    scores = jax.nn.sigmoid(logits)
    biased = scores + rb_ref[...]
    sub = lax.broadcasted_iota(jnp.int32, (epg, tm), 0)
    gs_rows = []
    for r in range(N_EXPERT_GROUPS):
        bg = biased[r * epg:(r + 1) * epg, :]
        m1 = jnp.max(bg, axis=0, keepdims=True)
        i1 = jnp.min(jnp.where(bg == m1, sub, epg), axis=0, keepdims=True)
        m2 = jnp.max(jnp.where(sub == i1, -jnp.inf, bg), axis=0, keepdims=True)
        gs_rows.append(m1 + m2)
    gs = jnp.concatenate(gs_rows, axis=0)
    grow = lax.broadcasted_iota(jnp.int32, (N_EXPERT_GROUPS, tm), 0)
    rank = jnp.zeros((N_EXPERT_GROUPS, tm), jnp.int32)
    for k in range(N_EXPERT_GROUPS):
        rk = gs[k:k + 1, :]
        rank = rank + jnp.where((rk > gs) | ((rk == gs) & (k < grow)), 1, 0)
    gkeep = rank < TOPK_GROUPS
    masked = jnp.concatenate(
        [jnp.where(gkeep[r:r + 1, :], biased[r * epg:(r + 1) * epg, :], -jnp.inf) for r in range(N_EXPERT_GROUPS)],
        axis=0)
    erow = lax.broadcasted_iota(jnp.int32, (N_EXPERTS, tm), 0)
    idx_rows, w_rows, hits = [], [], []
    for _ in range(TOP_K):
        m = jnp.max(masked, axis=0, keepdims=True)
        ix = jnp.min(jnp.where(masked == m, erow, N_EXPERTS), axis=0, keepdims=True)
        hit = erow == ix
        w_rows.append(jnp.sum(jnp.where(hit, scores, 0.0), axis=0, keepdims=True))
        idx_rows.append(ix)
        hits.append(hit)
        masked = jnp.where(hit, -jnp.inf, masked)
    wsum = w_rows[0]
    for w in w_rows[1:]:
        wsum = wsum + w
    pad = 8 - TOP_K
    idx_ref[...] = jnp.concatenate(idx_rows + [jnp.zeros((pad, tm), jnp.int32)], axis=0)
    wt_ref[...] = jnp.concatenate([w / wsum * ROUTED_SCALE for w in w_rows] + [jnp.zeros((pad, tm), F32)], axis=0)

    @pl.when(pl.program_id(0) == 0)
    def _():
        run_ref[...] = c0_ref[...]

    earlier = (lax.broadcasted_iota(jnp.int32, (tm, tm), 0) < lax.broadcasted_iota(jnp.int32, (tm, tm), 1))
    earlier = jnp.where(earlier, 1.0, 0.0).astype(BF16)
    run = run_ref[...]
    pos_rows = []
    for hit in hits:
        onehot = jnp.where(hit, 1.0, 0.0)
        before = _dot(onehot.astype(BF16), earlier).astype(jnp.int32)
        pos_rows.append(jnp.sum(jnp.where(hit, run + before, 0), axis=0, keepdims=True))
        run = run + jnp.sum(onehot, axis=1, keepdims=True).astype(jnp.int32)
    run_ref[...] = run
    cnt_ref[...] = run
    pos_ref[...] = jnp.concatenate(pos_rows + [jnp.zeros((pad, tm), jnp.int32)], axis=0)


def _router(x1, wr_t, rb, counts0, tm):
    m = x1.shape[0]
    tok = lambda: pl.BlockSpec((8, tm), lambda i: (0, i))
    return pl.pallas_call(
        functools.partial(_router_kernel, tm=tm),
        grid=(m // tm,),
        in_specs=[pl.BlockSpec((tm, D_MODEL), lambda i: (i, 0)),
                  pl.BlockSpec((N_EXPERTS, D_MODEL), lambda i: (0, 0)),
                  pl.BlockSpec((N_EXPERTS, 1), lambda i: (0, 0)),
                  pl.BlockSpec((N_EXPERTS, 1), lambda i: (0, 0))],
        out_specs=[tok(), tok(), tok(), pl.BlockSpec((N_EXPERTS, 1), lambda i: (0, 0))],
        out_shape=[jax.ShapeDtypeStruct((8, m), jnp.int32), jax.ShapeDtypeStruct((8, m), F32),
                   jax.ShapeDtypeStruct((8, m), jnp.int32), jax.ShapeDtypeStruct((N_EXPERTS, 1), jnp.int32)],
        scratch_shapes=[pltpu.VMEM((N_EXPERTS, 1), jnp.int32)],
        compiler_params=_cparams("arbitrary"),
        name="router",
    )(x1, wr_t, rb, counts0)


def _expert_kernel(be_ref, nv_ref, x_ref, wg_ref, wu_ref, wd_ref, o_ref, wgb_ref, wub_ref, wdb_ref):
    i = pl.program_id(0)
    valid = i < nv_ref[0]
    new_expert = (i == 0) | (be_ref[i] != be_ref[jnp.maximum(i - 1, 0)])

    @pl.when(valid & new_expert)
    def _():
        wgb_ref[...] = wg_ref[0].astype(BF16)
        wub_ref[...] = wu_ref[0].astype(BF16)
        wdb_ref[...] = wd_ref[0].astype(BF16)

    @pl.when(valid)
    def _():
        x = x_ref[...].astype(BF16)
        h = jax.nn.silu(_dot(x, wgb_ref[...])) * _dot(x, wub_ref[...])
        o_ref[...] = _dot(h.astype(BF16), wdb_ref[...]).astype(o_ref.dtype)

    @pl.when(jnp.logical_not(valid))
    def _():
        o_ref[...] = jnp.zeros_like(o_ref)


def _experts(blk_e, n_valid, xg, wg, wu, wd, tr):
    n_rows = xg.shape[0]
    n_blk = n_rows // tr
    grid_spec = pltpu.PrefetchScalarGridSpec(
        num_scalar_prefetch=2,
        grid=(n_blk,),
        in_specs=[pl.BlockSpec((tr, D_MODEL), lambda i, be, nv: (i, 0)),
                  pl.BlockSpec((1, D_MODEL, EXPERT_FF), lambda i, be, nv: (be[i], 0, 0)),
                  pl.BlockSpec((1, D_MODEL, EXPERT_FF), lambda i, be, nv: (be[i], 0, 0)),
                  pl.BlockSpec((1, EXPERT_FF, D_MODEL), lambda i, be, nv: (be[i], 0, 0))],
        out_specs=pl.BlockSpec((tr, D_MODEL), lambda i, be, nv: (i, 0)),
        scratch_shapes=[pltpu.VMEM((D_MODEL, EXPERT_FF), BF16), pltpu.VMEM((D_MODEL, EXPERT_FF), BF16),
                        pltpu.VMEM((EXPERT_FF, D_MODEL), BF16)],
    )
    return pl.pallas_call(
        _expert_kernel,
        grid_spec=grid_spec,
        out_shape=jax.ShapeDtypeStruct((n_rows, D_MODEL), BF16),
        compiler_params=_cparams("arbitrary"),
        name="experts",
    )(blk_e, n_valid, xg, wg, wu, wd)


def _ffn_out_kernel(x_ref, r_ref, w_ref, sg_ref, su_ref, sd_ref, g_ref, b_ref, o_ref):
    x = x_ref[...]
    xb = x.astype(BF16)
    h = jax.nn.silu(_dot(xb, sg_ref[...])) * _dot(xb, su_ref[...])
    f = _dot(h.astype(BF16), sd_ref[...])
    for k in range(TOP_K):
        f = f + r_ref[k].astype(F32) * w_ref[:, k:k + 1]
    o_ref[...] = _layer_norm(DEEPNORM_ALPHA * x + f, g_ref[...], b_ref[...])


def _ffn_out(x1, rows6, wts, sg, su, sd, g, b, tm):
    m = x1.shape[0]
    row = lambda: pl.BlockSpec((tm, D_MODEL), lambda i: (i, 0))
    full = lambda s: pl.BlockSpec(s, lambda i: (0, 0))
    return pl.pallas_call(
        _ffn_out_kernel,
        grid=(m // tm,),
        in_specs=[row(), pl.BlockSpec((TOP_K, tm, D_MODEL), lambda i: (0, i, 0)), pl.BlockSpec((tm, 8), lambda i: (i, 0)),
                  full((D_MODEL, SHARED_FF)), full((D_MODEL, SHARED_FF)), full((SHARED_FF, D_MODEL)),
                  full((1, D_MODEL)), full((1, D_MODEL))],
        out_specs=row(),
        out_shape=jax.ShapeDtypeStruct((m, D_MODEL), F32),
        compiler_params=_cparams("parallel"),
        name="ffn_out",
    )(x1, rows6, wts, sg, su, sd, g, b)


def _rel_bucket(dist):
    n = jnp.maximum(dist, 0)
    nf = jnp.maximum(n, 1).astype(F32)
    large = MAX_EXACT + (jnp.log(nf / MAX_EXACT) / math.log(MAX_DISTANCE / MAX_EXACT)
                         * (N_BUCKETS - MAX_EXACT)).astype(jnp.int32)
    return jnp.where(n < MAX_EXACT, n, jnp.minimum(large, N_BUCKETS - 1))


def _bias_of(tbl, dist):
    onehot = (_rel_bucket(dist)[..., None] == jnp.arange(N_BUCKETS)).astype(F32)
    return jnp.einsum('...k,kh->...h', onehot, tbl, precision=lax.Precision.HIGHEST)


def _toeplitz_bias(tbl, tq, n_tiles):
    i = jnp.arange(tq)[:, None]
    j = jnp.arange(tq)[None, :]
    dist = jnp.arange(n_tiles)[:, None, None] * tq + (i - j)[None]
    bias = jnp.moveaxis(_bias_of(tbl, dist), -1, 0)
    return bias, dist


PAGE_GROUP = 16
IDX_PAGE_GROUP = 64


def _by_group(hpg, a0, a1):
    row = lax.broadcasted_iota(jnp.int32, (2 * hpg, 1), 0)
    return jnp.where(row < hpg, a0, a1)


def _order_key(x):
    bits = pltpu.bitcast(x, jnp.int32)
    return jnp.where(bits < 0, bits ^ 0x7FFFFFFF, bits)


def _row_topk(key_ref, keep_ref, n_keep, width):
    thr = _topk_mask(key_ref, n_keep, 1, width)
    key = key_ref[...]
    gt = key > thr
    eq = key == thr
    need = n_keep - jnp.sum(jnp.where(gt, 1, 0), axis=-1, keepdims=True)
    n_eq = jnp.sum(jnp.where(eq, 1, 0), axis=-1, keepdims=True)
    keep_ref[...] = jnp.where(gt | eq, 1.0, 0.0)

    @pl.when(jnp.max(n_eq - need) > 0)
    def _():
        r_i = lax.broadcasted_iota(jnp.int32, (128, 128), 0)
        c_i = lax.broadcasted_iota(jnp.int32, (128, 128), 1)
        tri = jnp.where(r_i <= c_i, 1.0, 0.0).astype(BF16)
        need_f = need.astype(F32)

        def body(c, before):
            off = pl.multiple_of(c * 128, 128)
            kc = key_ref[:, pl.ds(off, 128)]
            eq_c = kc == thr
            eq_f = jnp.where(eq_c, 1.0, 0.0)
            pref = _dot(jnp.broadcast_to(eq_f, (8, 128)).astype(BF16), tri)[0:1] + before
            keep_ref[:, pl.ds(off, 128)] = jnp.where((kc > thr) | (eq_c & (pref <= need_f)), 1.0, 0.0)
            return before + jnp.sum(eq_f, axis=-1, keepdims=True)

        lax.fori_loop(0, width // 128, body, jnp.zeros((1, 1), F32))


def _s_cmp_win_kernel(q_ref, kc_ref, vc_ref, cb_ref, wk_ref, wv_ref, nk_ref, nv_ref, wb_ref, b0_ref,
                      ocmp_ref, owin_ref, sel_ref, key_ref, keep_ref, *, nbc, n_blocks, width, n_sel, wlen):
    hpg = NSA_HPG
    qf = q_ref[0]
    q = qf.astype(BF16)
    row = lax.broadcasted_iota(jnp.int32, (NSA_HEADS, 1), 0)
    g0 = row < hpg
    kc = kc_ref[0]
    vc = vc_ref[0]
    cb = cb_ref[...]
    valid = cb > 0.5 * NEG
    s = _by_group(hpg, _dot_nt_full(qf, kc[:, :HEAD_DIM]), _dot_nt_full(qf, kc[:, HEAD_DIM:]))
    s = s * ATTN_SCALE + cb
    m = jnp.max(s, axis=-1, keepdims=True)
    e = jnp.where(valid, jnp.exp(s - m), 0.0)
    den = jnp.sum(e, axis=-1, keepdims=True)
    p = e / jnp.where(den > 0, den, 1.0)
    ocmp_ref[0] = (_dot(jnp.where(g0, p, 0.0).astype(BF16), vc[:, :HEAD_DIM].astype(BF16))
                   + _dot(jnp.where(g0, 0.0, p).astype(BF16), vc[:, HEAD_DIM:].astype(BF16)))
    lane = lax.broadcasted_iota(jnp.int32, (1, width), 1)
    cur = n_blocks - 1
    forced = (lane == 0) | (lane == cur) | (lane == cur - 1)
    for g in range(NSA_KV_GROUPS):
        imp = jnp.sum(jnp.where(g0 if g == 0 else jnp.logical_not(g0), p, 0.0), axis=0, keepdims=True)
        impw = jnp.concatenate([imp, jnp.zeros((1, width - nbc), F32)], axis=1)
        score = jnp.where(forced, jnp.inf, jnp.where(lane <= cur, impw, -jnp.inf))
        key_ref[...] = _order_key(score)
        _row_topk(key_ref, keep_ref, n_sel, width)
        sel_ref[0, g:g + 1, :] = keep_ref[...]
    wk0 = wk_ref[pl.ds(0, wlen, stride=2), :].astype(BF16)
    wk1 = wk_ref[pl.ds(1, wlen, stride=2), :].astype(BF16)
    wv0 = wv_ref[pl.ds(0, wlen, stride=2), :].astype(BF16)
    wv1 = wv_ref[pl.ds(1, wlen, stride=2), :].astype(BF16)
    wb = wb_ref[...]
    s = _by_group(hpg, _dot_nt(q, wk0), _dot_nt(q, wk1)) * ATTN_SCALE + wb
    nk = _by_group(hpg, nk_ref[0][:, :HEAD_DIM], nk_ref[0][:, HEAD_DIM:])
    nv = _by_group(hpg, nv_ref[0][:, :HEAD_DIM], nv_ref[0][:, HEAD_DIM:])
    s_new = jnp.sum(qf * nk, axis=-1, keepdims=True) * ATTN_SCALE + b0_ref[...]
    m = jnp.maximum(jnp.max(s, axis=-1, keepdims=True), s_new)
    e = jnp.where(wb > 0.5 * NEG, jnp.exp(s - m), 0.0)
    e_new = jnp.exp(s_new - m)
    den = jnp.sum(e, axis=-1, keepdims=True) + e_new
    o = (_dot(jnp.where(g0, e, 0.0).astype(BF16), wv0) + _dot(jnp.where(g0, 0.0, e).astype(BF16), wv1)
         + e_new * nv)
    owin_ref[0] = o / den


def _s_cmp_win(q, kcomp, vcomp, cbias, wk, wv, nk, nv, wbias, b0, n_blocks, n_sel):
    db = q.shape[0]
    nbc = kcomp.shape[1]
    wlen = wk.shape[1] // 2
    width = -(-n_blocks // 128) * 128
    one = lambda *s: pl.BlockSpec((1,) + s, lambda bi: (bi,) + (0,) * len(s))
    const = lambda a: pl.BlockSpec(a.shape, lambda bi: (0,) * a.ndim)
    return pl.pallas_call(
        functools.partial(_s_cmp_win_kernel, nbc=nbc, n_blocks=n_blocks, width=width, n_sel=n_sel, wlen=wlen),
        grid=(db,),
        in_specs=[one(NSA_HEADS, HEAD_DIM), one(nbc, KV_W), one(nbc, KV_W), const(cbias),
                  pl.BlockSpec((None, 2 * wlen, HEAD_DIM), lambda bi: (bi, 0, 0)),
                  pl.BlockSpec((None, 2 * wlen, HEAD_DIM), lambda bi: (bi, 0, 0)),
                  one(1, KV_W), one(1, KV_W), const(wbias), const(b0)],
        out_specs=[one(NSA_HEADS, HEAD_DIM), one(NSA_HEADS, HEAD_DIM), one(NSA_KV_GROUPS, width)],
        out_shape=[jax.ShapeDtypeStruct((db, NSA_HEADS, HEAD_DIM), F32),
                   jax.ShapeDtypeStruct((db, NSA_HEADS, HEAD_DIM), F32),
                   jax.ShapeDtypeStruct((db, NSA_KV_GROUPS, width), F32)],
        scratch_shapes=[pltpu.VMEM((1, width), jnp.int32), pltpu.VMEM((1, width), F32)],
        compiler_params=_cparams("parallel"),
        name="sample_cmp_win",
    )(q, kcomp, vcomp, cbias, wk, wv, nk, nv, wbias, b0)


def _s_sel_kernel(idx_ref, hp_ref, q_ref, nk_ref, nv_ref, tb_ref, c_ref, *rest, n_sel, cur, n_near):
    del hp_ref
    k_refs, v_refs, o_ref = rest[:n_sel], rest[n_sel:2 * n_sel], rest[2 * n_sel]
    bi = pl.program_id(0)
    g = pl.program_id(1)
    q = q_ref[0].astype(BF16)
    nk = jnp.where(g == 0, nk_ref[0][:, :HEAD_DIM], nk_ref[0][:, HEAD_DIM:])
    nv = jnp.where(g == 0, nv_ref[0][:, :HEAD_DIM], nv_ref[0][:, HEAD_DIM:])
    rowi = lax.broadcasted_iota(jnp.int32, (SEL_BLOCK, 1), 0)
    ss, vs = [], []
    for r in range(n_sel):
        idx = idx_ref[(bi * NSA_KV_GROUPS + g) * n_sel + r]
        first = (rowi == 0) & (idx >= cur)
        kr = jnp.where(first, nk, k_refs[r][pl.ds(g, SEL_BLOCK, stride=2), :]).astype(BF16)
        vs.append(jnp.where(first, nv, v_refs[r][pl.ds(g, SEL_BLOCK, stride=2), :]).astype(BF16))
        u = jnp.clip(idx - (cur - (n_near - 1)), 0, n_near - 1)
        bias = jnp.where(idx >= cur - (n_near - 1), tb_ref[u], c_ref[...])
        ss.append(_dot_nt(q, kr) * ATTN_SCALE + bias)
    m = ss[0].max(axis=-1, keepdims=True)
    for s in ss[1:]:
        m = jnp.maximum(m, s.max(axis=-1, keepdims=True))
    den = jnp.zeros((NSA_HEADS, 1), F32)
    o = jnp.zeros((NSA_HEADS, HEAD_DIM), F32)
    for s, v in zip(ss, vs):
        e = jnp.where(s > 0.5 * NEG, jnp.exp(s - m), 0.0)
        den = den + e.sum(axis=-1, keepdims=True)
        o = o + _dot(e.astype(BF16), v)
    o_ref[0, 0] = o / jnp.where(den > 0, den, 1.0)


def _s_sel(idx_flat, hp_flat, q, nk, nv, tb, c, cache_k, cache_v, n_sel, cur):
    db = q.shape[0]
    rows = SEL_BLOCK * NSA_KV_GROUPS

    def page(r):
        return pl.BlockSpec((None, rows, HEAD_DIM),
                            lambda bi, g, idx, hp, r=r: (hp[(bi * NSA_KV_GROUPS + g) * n_sel + r], 0, 0))

    grid_spec = pltpu.PrefetchScalarGridSpec(
        num_scalar_prefetch=2,
        grid=(db, NSA_KV_GROUPS),
        in_specs=[pl.BlockSpec((1, NSA_HEADS, HEAD_DIM), lambda bi, g, idx, hp: (bi, 0, 0)),
                  pl.BlockSpec((1, 1, KV_W), lambda bi, g, idx, hp: (bi, 0, 0)),
                  pl.BlockSpec((1, 1, KV_W), lambda bi, g, idx, hp: (bi, 0, 0)),
                  pl.BlockSpec(tb.shape, lambda bi, g, idx, hp: (0, 0, 0)),
                  pl.BlockSpec(c.shape, lambda bi, g, idx, hp: (0, 0))]
        + [page(r) for r in range(n_sel)] + [page(r) for r in range(n_sel)],
        out_specs=pl.BlockSpec((1, 1, NSA_HEADS, HEAD_DIM), lambda bi, g, idx, hp: (bi, g, 0, 0)),
    )
    return pl.pallas_call(
        functools.partial(_s_sel_kernel, n_sel=n_sel, cur=cur, n_near=tb.shape[0]),
        grid_spec=grid_spec,
        out_shape=jax.ShapeDtypeStruct((db, NSA_KV_GROUPS, NSA_HEADS, HEAD_DIM), F32),
        compiler_params=_cparams("parallel", "parallel"),
        name="sample_sel",
    )(idx_flat, hp_flat, q, nk, nv, tb, c, *([cache_k] * n_sel), *([cache_v] * n_sel))


def _s_index_kernel(pt_ref, iq_ref, iw_ref, nik_ref, *rest, pg, past, n_keep, width):
    del pt_ref
    pages, keep_ref, sc_ref, key_ref = rest[:pg], rest[pg], rest[pg + 1], rest[pg + 2]
    j = pl.program_id(1)
    span = pg * PAGE_SIZE
    iqf = iq_ref[0]
    iw = iw_ref[0]
    ik = jnp.concatenate([p[...] for p in pages], axis=0)
    lg = jnp.maximum(_dot_nt_full(iqf, ik) * IDX_DIM ** -0.5, 0.0)
    sc = jnp.sum(lg * iw, axis=0, keepdims=True) * IDX_HEADS ** -0.5 + 0.0
    sc_ref[:, pl.ds(pl.multiple_of(j * span, span), span)] = sc

    @pl.when(j == pl.num_programs(1) - 1)
    def _():
        lg_new = jnp.maximum(jnp.sum(iqf * nik_ref[0], axis=-1, keepdims=True) * IDX_DIM ** -0.5, 0.0)
        sc_new = jnp.sum(lg_new * iw, axis=0, keepdims=True) * IDX_HEADS ** -0.5 + 0.0
        lane = lax.broadcasted_iota(jnp.int32, (1, width - past), 1)
        sc_ref[:, past:] = jnp.where(lane == 0, sc_new, -jnp.inf)
        key_ref[...] = _order_key(sc_ref[...])
        _row_topk(key_ref, keep_ref.at[0], n_keep, width)


def _s_index(page_table, iq, iw, nik, cache_idx, n_keep):
    db, n_pages = page_table.shape
    pg = math.gcd(n_pages, IDX_PAGE_GROUP)
    past = n_pages * PAGE_SIZE
    width = past + 128

    def page(r):
        return pl.BlockSpec((None, PAGE_SIZE, IDX_DIM), lambda bi, j, pt, r=r: (pt[bi, j * pg + r], 0, 0))

    grid_spec = pltpu.PrefetchScalarGridSpec(
        num_scalar_prefetch=1,
        grid=(db, n_pages // pg),
        in_specs=[pl.BlockSpec((1, IDX_HEADS, IDX_DIM), lambda bi, j, pt: (bi, 0, 0)),
                  pl.BlockSpec((1, IDX_HEADS, 1), lambda bi, j, pt: (bi, 0, 0)),
                  pl.BlockSpec((1, 1, IDX_DIM), lambda bi, j, pt: (bi, 0, 0))]
        + [page(r) for r in range(pg)],
        out_specs=pl.BlockSpec((1, 1, width), lambda bi, j, pt: (bi, 0, 0)),
        scratch_shapes=[pltpu.VMEM((1, width), F32), pltpu.VMEM((1, width), jnp.int32)],
    )
    return pl.pallas_call(
        functools.partial(_s_index_kernel, pg=pg, past=past, n_keep=n_keep, width=width),
        grid_spec=grid_spec,
        out_shape=jax.ShapeDtypeStruct((db, 1, width), F32),
        compiler_params=_cparams("parallel", "arbitrary"),
        name="sample_index",
    )(page_table, iq, iw, nik, *([cache_idx] * pg))


def _s_dsa_kernel(pt_ref, q_ref, keep_ref, bt_ref, c_ref, b0_ref, nk_ref, nv_ref, *rest, pg, past):
    del pt_ref
    kp, vp, o_ref = rest[:pg], rest[pg:2 * pg], rest[2 * pg]
    m_ref, l_ref, acc_ref = rest[2 * pg + 1:]
    hpg = DSA_HPG
    j = pl.program_id(1)
    last = pl.num_programs(1) - 1
    span = pg * PAGE_SIZE

    @pl.when(j == 0)
    def _():
        m_ref[...] = jnp.full_like(m_ref, NEG)
        l_ref[...] = jnp.zeros_like(l_ref)
        acc_ref[...] = jnp.zeros_like(acc_ref)

    qf = q_ref[0]
    q = qf.astype(BF16)
    row = lax.broadcasted_iota(jnp.int32, (DSA_HEADS, 1), 0)
    g0 = row < hpg
    rows = lambda refs, g: jnp.concatenate([p[pl.ds(g, PAGE_SIZE, stride=2), :] for p in refs], axis=0).astype(BF16)
    s = _by_group(hpg, _dot_nt(q, rows(kp, 0)), _dot_nt(q, rows(kp, 1))) * ATTN_SCALE
    keep = keep_ref[0, :, pl.ds(pl.multiple_of(j * span, span), span)] > 0.5
    s = jnp.where(keep, s + jnp.where(j == last, bt_ref[...], c_ref[...]), NEG)
    m_old = m_ref[...]
    m_new = jnp.maximum(m_old, jnp.max(s, axis=-1, keepdims=True))
    alpha = jnp.exp(m_old - m_new)
    e = jnp.where(keep, jnp.exp(s - m_new), 0.0)
    l_ref[...] = alpha * l_ref[...] + jnp.sum(e, axis=-1, keepdims=True)
    acc_ref[...] = (alpha * acc_ref[...] + _dot(jnp.where(g0, e, 0.0).astype(BF16), rows(vp, 0))
                    + _dot(jnp.where(g0, 0.0, e).astype(BF16), rows(vp, 1)))
    m_ref[...] = m_new

    @pl.when(j == last)
    def _():
        keep_new = keep_ref[0, :, past:past + 1] > 0.5
        nk = _by_group(hpg, nk_ref[0][:, :HEAD_DIM], nk_ref[0][:, HEAD_DIM:])
        nv = _by_group(hpg, nv_ref[0][:, :HEAD_DIM], nv_ref[0][:, HEAD_DIM:])
        s_new = jnp.sum(qf * nk, axis=-1, keepdims=True) * ATTN_SCALE + b0_ref[...]
        s_new = jnp.where(keep_new, s_new, NEG)
        m_old2 = m_ref[...]
        m2 = jnp.maximum(m_old2, s_new)
        a2 = jnp.exp(m_old2 - m2)
        e_new = jnp.where(keep_new, jnp.exp(s_new - m2), 0.0)
        l = a2 * l_ref[...] + e_new
        o_ref[0] = (a2 * acc_ref[...] + e_new * nv) / jnp.where(l > 0, l, 1.0)


def _s_dsa(page_table, q, keep, btail, c, b0, nk, nv, cache_k, cache_v):
    db, n_pages = page_table.shape
    pg = math.gcd(n_pages, PAGE_GROUP)
    past = n_pages * PAGE_SIZE
    width = keep.shape[-1]
    rows = PAGE_SIZE * DSA_KV_GROUPS

    def page(r):
        return pl.BlockSpec((None, rows, HEAD_DIM), lambda bi, j, pt, r=r: (pt[bi, j * pg + r], 0, 0))

    grid_spec = pltpu.PrefetchScalarGridSpec(
        num_scalar_prefetch=1,
        grid=(db, n_pages // pg),
        in_specs=[pl.BlockSpec((1, DSA_HEADS, HEAD_DIM), lambda bi, j, pt: (bi, 0, 0)),
                  pl.BlockSpec((1, 1, width), lambda bi, j, pt: (bi, 0, 0)),
                  pl.BlockSpec(btail.shape, lambda bi, j, pt: (0, 0)),
                  pl.BlockSpec(c.shape, lambda bi, j, pt: (0, 0)),
                  pl.BlockSpec(b0.shape, lambda bi, j, pt: (0, 0)),
                  pl.BlockSpec((1, 1, KV_W), lambda bi, j, pt: (bi, 0, 0)),
                  pl.BlockSpec((1, 1, KV_W), lambda bi, j, pt: (bi, 0, 0))]
        + [page(r) for r in range(pg)] + [page(r) for r in range(pg)],
        out_specs=pl.BlockSpec((1, DSA_HEADS, HEAD_DIM), lambda bi, j, pt: (bi, 0, 0)),
        scratch_shapes=[pltpu.VMEM((DSA_HEADS, 1), F32), pltpu.VMEM((DSA_HEADS, 1), F32),
                        pltpu.VMEM((DSA_HEADS, HEAD_DIM), F32)],
    )
    return pl.pallas_call(
        functools.partial(_s_dsa_kernel, pg=pg, past=past),
        grid_spec=grid_spec,
        out_shape=jax.ShapeDtypeStruct((db, DSA_HEADS, HEAD_DIM), F32),
        compiler_params=_cparams("parallel", "arbitrary"),
        name="sample_dsa",
    )(page_table, q, keep, btail, c, b0, nk, nv, *([cache_k] * pg), *([cache_v] * pg))


def _split_w_in(w_in):
    points = np.cumsum(IN_COLS)[:-1].tolist()
    q_a, kv_a, g_a, q_b, kv_b, iq, ik, iw, g_m = jnp.split(w_in, points, axis=-1)
    pad = jnp.zeros((D_MODEL, MISC_W - IDX_DIM - IDX_HEADS - 3 * NSA_HEADS), w_in.dtype)
    w_kv = jnp.concatenate([kv_a, kv_b, ik, iw, g_a, pad], axis=-1)
    w_q = jnp.concatenate([q_a, q_b, iq], axis=-1)
    return w_kv, w_q, g_m[:, :D_MODEL], g_m[:, D_MODEL:]


KV_WIDTHS = (KV_W,) * 8 + (MISC_W,)
Q_WIDTHS = (NSA_HEADS * HEAD_DIM, DSA_HEADS * HEAD_DIM, IDX_HEADS * IDX_DIM)


def _row_tile(m, cap):
    tm = math.gcd(m, cap)
    assert tm % 8 == 0 or tm == m
    return tm


def kernel(x_prompt, x_sample, cache_cmp_k, cache_cmp_v, cache_slc_k, cache_slc_v, state_win_k, state_win_v,
           cache_dsa_k, cache_dsa_v, cache_idx_k, page_table, rel_bias_table, w_in, cmp_pe, cmp_w1, cmp_w2,
           w_up_a, w_up_b, w_o, ln1_g, ln1_b, w_router, router_bias, moe_w_gate, moe_w_up, moe_w_down,
           sh_w_gate, sh_w_up, sh_w_down, ln2_g, ln2_b):
    assert w_in.shape[0] == DEPTH == 1
    b, t, _ = x_prompt.shape
    db, dt, _ = x_sample.shape
    assert dt == 1
    n_pool = cache_cmp_k.shape[1]
    n_pages = page_table.shape[1]
    tq = Q_TILE
    assert t % tq == 0 and t % CMP_BLOCK == 0
    n_p, n_s = b * t, db * dt

    tbl_a = rel_bias_table[:, :NSA_HEADS]
    tbl_b = rel_bias_table[:, NSA_HEADS:]
    mixer_w32 = _split_w_in(w_in[0]) + (w_up_a[0], w_up_b[0], w_o[0])
    w_kv32, w_q32, w_ga32, w_gb32, w_ua32, w_ub32, w_o32 = mixer_w32
    w_kv, w_q, w_ga, w_gb, w_ua, w_ub, w_ob = [w.astype(BF16) for w in mixer_w32]
    cw = [_compress_weights(cmp_pe[0, i], cmp_w1[0, i], cmp_w2[0, i]) for i in range(2)]

    xp = x_prompt.reshape(n_p, D_MODEL)
    xs = x_sample.reshape(n_s, D_MODEL)
    xpb = xp.astype(BF16)
    tm_p = _row_tile(n_p, 512)
    kc, vc, ks, vs, kw, vw, kb, vb, misc = _project(xpb, w_kv, KV_WIDTHS, tm_p)
    qa, qb, iq = _project(xpb, w_q, Q_WIDTHS, tm_p)
    s_kc, s_vc, s_ks, s_vs, s_kw, s_vw, s_kb, s_vb, s_misc = _project(xs, w_kv32, KV_WIDTHS, n_s)
    s_qa, s_qb, s_iq = _project(xs, w_q32, Q_WIDTHS, n_s)

    nb = t // CMP_BLOCK
    r3 = lambda a: a.reshape(b, t, -1)
    kcomp = _compress(kc.reshape(b * nb, CMP_BLOCK, KV_W), *cw[0]).reshape(b, nb, KV_W)
    vcomp = _compress(vc.reshape(b * nb, CMP_BLOCK, KV_W), *cw[1]).reshape(b, nb, KV_W)
    cdist = jnp.arange(t)[:, None] - (jnp.arange(nb) * CMP_BLOCK + CMP_BLOCK - 1)[None, :]
    cbias = jnp.where(cdist >= 0, jnp.moveaxis(_bias_of(tbl_a, cdist), -1, 0), NEG).swapaxes(1, 2)
    o_cmp, selmask = _cmp_select(r3(qa), kcomp, vcomp, cbias, tq)

    assert MAX_DISTANCE <= tq
    bias2_a, dist2 = _toeplitz_bias(tbl_a, tq, 2)
    bias2_b, _ = _toeplitz_bias(tbl_b, tq, 2)
    c_a, c_b = tbl_a[N_BUCKETS - 1], tbl_b[N_BUCKETS - 1]
    d_a = jnp.where(dist2 >= 0, (bias2_a - c_a[:, None, None, None]) * LOG2E, NEG)
    d_b = jnp.where(dist2 >= 0, (bias2_b - c_b[:, None, None, None]) * LOG2E, NEG)
    expand = (jnp.arange(t)[None, :] // SEL_BLOCK == jnp.arange(nb)[:, None]).astype(BF16)
    o_sel = _dense_attn(r3(qa), r3(ks), r3(vs), selmask, expand, d_a, tq, "sel")

    n_wchunks = -(-(WINDOW - 1) // tq) + 1
    bias_w, dist_w = _toeplitz_bias(tbl_a, tq, n_wchunks)
    wtiles = jnp.where((dist_w >= 0) & (dist_w < WINDOW), bias_w, NEG)
    o_win = _window_attn(r3(qa), r3(kw), r3(vw), wtiles, tq)

    n_keep = min(DSA_TOPK, t // 4)
    keepmask = _index_select(r3(iq), r3(misc), tq, n_keep)
    o_b = _dense_attn(r3(qb), r3(kb), r3(vb), keepmask, jnp.zeros((8, 128), BF16), d_b, tq, "dsa")

    tm_m = _row_tile(n_p, 256)
    y_p = _merge_up(xpb, o_cmp.reshape(n_p, -1), o_sel.reshape(n_p, -1), o_win.reshape(n_p, -1),
                    o_b.reshape(n_p, -1), misc, w_ga, w_gb, w_ua, w_ub, tm_m, 1024)
    x1p = _merge_out(xp, y_p, w_ob, ln1_g, ln1_b, tm_m)

    past = n_pages * PAGE_SIZE
    halves = PAGE_SIZE // CMP_BLOCK

    def comp_pool(cache, i):
        c = _compress_pool(cache, cmp_pe[0, i], cmp_w1[0, i], cmp_w2[0, i])
        return c.reshape(n_pool, halves * KV_W)[page_table].reshape(db, n_pages * halves, KV_W)

    s_kcomp = comp_pool(cache_cmp_k[0], 0)
    s_vcomp = comp_pool(cache_cmp_v[0], 1)
    assert past % SEL_BLOCK == 0 and past >= 4 * SEL_BLOCK and PAGE_SIZE == 2 * SEL_BLOCK
    total = past + dt
    nbc = past // CMP_BLOCK
    n_blocks = -(-total // SEL_BLOCK)
    cur = past // SEL_BLOCK
    n_sel = min(N_SEL_BLOCKS, n_blocks)
    col = lambda v: v.reshape(-1, 1)
    new = lambda a: a.reshape(db, 1, KV_W)
    cb_s = _bias_of(tbl_a, past - (jnp.arange(nbc) * CMP_BLOCK + CMP_BLOCK - 1)).T
    w_past = state_win_k.shape[2]
    wdist = w_past - jnp.arange(w_past)
    wb_s = jnp.where(wdist < WINDOW, _bias_of(tbl_a, wdist).T, NEG)
    rows2 = lambda a, n: a.reshape(a.shape[0], n * NSA_KV_GROUPS, HEAD_DIM)
    s_q8 = s_qa.reshape(db, NSA_HEADS, HEAD_DIM)
    so_cmp, so_win, selmask = _s_cmp_win(
        s_q8, s_kcomp, s_vcomp, cb_s, rows2(state_win_k[0], w_past), rows2(state_win_v[0], w_past),
        new(s_kw), new(s_vw), wb_s, col(tbl_a[0]), n_blocks, n_sel)
    kept = selmask[:, :, None, :n_blocks] > 0.5
    nth = jnp.cumsum(kept, axis=-1) == (jnp.arange(n_sel) + 1)[:, None]
    sel_idx = jnp.sum(jnp.where(kept & nth, jnp.arange(n_blocks), 0), axis=-1).astype(jnp.int32)
    sel_page = jnp.take_along_axis(page_table, jnp.minimum(sel_idx // 2, n_pages - 1).reshape(db, -1), axis=1)
    sel_hp = sel_page.reshape(sel_idx.shape) * 2 + sel_idx % 2
    n_near = 4
    ndist = (n_near - 1 - jnp.arange(n_near))[:, None] * SEL_BLOCK - jnp.arange(SEL_BLOCK)[None, :]
    tb_s = jnp.where(ndist[:, None, :] >= 0, jnp.moveaxis(_bias_of(tbl_a, ndist), -1, 1), NEG)
    half_pages = lambda c: c.reshape(n_pool * halves, SEL_BLOCK * NSA_KV_GROUPS, HEAD_DIM)
    so_sel2 = _s_sel(sel_idx.reshape(-1), sel_hp.reshape(-1).astype(jnp.int32), s_q8, new(s_ks), new(s_vs), tb_s,
                     col(tbl_a[N_BUCKETS - 1]), half_pages(cache_slc_k[0]), half_pages(cache_slc_v[0]), n_sel, cur)
    so_sel = jnp.concatenate([so_sel2[:, 0, :NSA_HPG], so_sel2[:, 1, NSA_HPG:]], axis=1)
    keep_s = _s_index(page_table, s_iq.reshape(db, IDX_HEADS, IDX_DIM),
                      s_misc[:, MISC_IW:MISC_IW + IDX_HEADS].reshape(db, IDX_HEADS, 1),
                      s_misc[:, :IDX_DIM].reshape(db, 1, IDX_DIM), cache_idx_k[0], min(DSA_TOPK, total // 4))
    span = math.gcd(n_pages, PAGE_GROUP) * PAGE_SIZE
    assert span >= MAX_DISTANCE
    bt_s = _bias_of(tbl_b, span - jnp.arange(span)).T
    pages2 = lambda c: c.reshape(n_pool, PAGE_SIZE * DSA_KV_GROUPS, HEAD_DIM)
    so_b = _s_dsa(page_table, s_qb.reshape(db, DSA_HEADS, HEAD_DIM), keep_s, bt_s, col(tbl_b[N_BUCKETS - 1]),
                  col(tbl_b[0]), new(s_kb), new(s_vb), pages2(cache_dsa_k[0]), pages2(cache_dsa_v[0]))
    n_win = min(WINDOW, total)
    g4 = lambda a: a.reshape(db, dt, NSA_KV_GROUPS, HEAD_DIM)
    s_wk = jnp.concatenate([state_win_k[0], g4(s_kw)], axis=1)[:, -n_win:]
    s_wv = jnp.concatenate([state_win_v[0], g4(s_vw)], axis=1)[:, -n_win:]
    y_s = _merge_up(xs, so_cmp.reshape(n_s, -1), so_sel.reshape(n_s, -1), so_win.reshape(n_s, -1),
                    so_b.reshape(n_s, -1), s_misc, w_ga32, w_gb32, w_ua32, w_ub32, n_s, 512)
    x1s = _merge_out(xs, y_s, w_o32, ln1_g, ln1_b, n_s)

    wr_t = w_router[0].T
    rb = router_bias[0].reshape(N_EXPERTS, 1)
    eidx_p, wts_p, pos_p, cnt_p = _router(x1p, wr_t, rb, jnp.zeros((N_EXPERTS, 1), jnp.int32), tm_m)
    eidx_s, wts_s, pos_s, cnt = _router(x1s, wr_t, rb, cnt_p, n_s)
    n_tok = n_p + n_s
    eidx = jnp.concatenate([eidx_p[:TOP_K], eidx_s[:TOP_K]], axis=1)
    pos = jnp.concatenate([pos_p[:TOP_K], pos_s[:TOP_K]], axis=1)
    tr = EXPERT_ROWS
    n_asg = n_tok * TOP_K
    counts = cnt[:, 0]
    padded = (counts + tr - 1) // tr * tr
    pend = jnp.cumsum(padded)
    pad_start = pend - padded
    n_rows = -(-n_asg // tr) * tr + N_EXPERTS * tr
    n_blk = n_rows // tr
    dest = pos + jnp.sum(jnp.where(eidx[..., None] == jnp.arange(N_EXPERTS), pad_start, 0), axis=-1)
    blk_e = jnp.minimum(jnp.sum(pend[None, :] <= (jnp.arange(n_blk) * tr)[:, None], axis=1), N_EXPERTS - 1)
    n_valid = (pend[-1] // tr).astype(jnp.int32).reshape(1)
    row_tok = jnp.full((n_rows,), n_tok, jnp.int32).at[dest.reshape(-1)].set(
        jnp.tile(jnp.arange(n_tok, dtype=jnp.int32), TOP_K), unique_indices=True)
    xg = jnp.concatenate([x1p, x1s, jnp.zeros((1, D_MODEL), F32)], axis=0)[row_tok]
    out_rows = _experts(blk_e.astype(jnp.int32), n_valid, xg, moe_w_gate[0], moe_w_up[0], moe_w_down[0], tr)
    rows6 = out_rows[dest]

    sg, su, sd = sh_w_gate[0].astype(BF16), sh_w_up[0].astype(BF16), sh_w_down[0].astype(BF16)
    y_prompt = _ffn_out(x1p, rows6, wts_p.T, sg, su, sd, ln2_g, ln2_b, tm_m).reshape(b, t, D_MODEL)
    y_sample = _ffn_out(x1s, rows6[:, n_p:], wts_s.T, sg, su, sd, ln2_g, ln2_b, n_s).reshape(db, dt, D_MODEL)

    n_win = min(WINDOW, t)
    st = lambda a: a.reshape(1, b, t, NSA_KV_GROUPS, HEAD_DIM)
    ss = lambda a: a.reshape(1, db, dt, NSA_KV_GROUPS, HEAD_DIM)
    return (y_prompt, y_sample,
            st(kc), st(vc), st(ks), st(vs), st(kw)[:, :, -n_win:], st(vw)[:, :, -n_win:], st(kb), st(vb),
            misc[:, :IDX_DIM].reshape(1, b, t, IDX_DIM),
            ss(s_kc), ss(s_vc), ss(s_ks), ss(s_vs), s_wk[None], s_wv[None], ss(s_kb), ss(s_vb),
            s_misc[:, :IDX_DIM].reshape(1, db, dt, IDX_DIM))
```

```python
import functools
import math

import jax
import jax.numpy as jnp
import numpy as np
from jax import lax
from jax.experimental import pallas as pl
from jax.experimental.pallas import tpu as pltpu

D_MODEL = 2048
PAGE_SIZE = 128
HEAD_DIM = 128
NSA_HEADS = 8
NSA_KV_GROUPS = 2
NSA_HPG = NSA_HEADS // NSA_KV_GROUPS
CMP_BLOCK = 64
CMP_HIDDEN = 128
SEL_BLOCK = 64
N_SEL_BLOCKS = 16
WINDOW = 512
DSA_HEADS = 8
DSA_KV_GROUPS = 2
DSA_HPG = DSA_HEADS // DSA_KV_GROUPS
IDX_HEADS = 8
IDX_DIM = 64
DSA_TOPK = 256
N_BUCKETS = 32
MAX_EXACT = 16
MAX_DISTANCE = 128
N_EXPERTS = 64
EXPERT_FF = 512
SHARED_FF = 512
TOP_K = 6
N_EXPERT_GROUPS = 8
TOPK_GROUPS = 4
ROUTED_SCALE = 2.5
LN_EPS = 1e-5
ATTN_SCALE = HEAD_DIM ** -0.5
DEPTH = 1
DEEPNORM_ALPHA = (2 * DEPTH) ** 0.25
IN_COLS = (NSA_HEADS * HEAD_DIM, 6 * NSA_KV_GROUPS * HEAD_DIM, 3 * NSA_HEADS,
           DSA_HEADS * HEAD_DIM, 2 * DSA_KV_GROUPS * HEAD_DIM,
           IDX_HEADS * IDX_DIM, IDX_DIM, IDX_HEADS, 2 * D_MODEL)

KV_W = NSA_KV_GROUPS * HEAD_DIM
MISC_W = 128
MISC_IW = IDX_DIM
MISC_GA = IDX_DIM + IDX_HEADS
NEG = -1e30
Q_TILE = 256
EXPERT_ROWS = 256
VMEM_LIMIT = 56 * 1024 * 1024

BF16 = jnp.bfloat16
F32 = jnp.float32


def _cparams(*sem):
    return pltpu.CompilerParams(dimension_semantics=sem, vmem_limit_bytes=VMEM_LIMIT)


def _dot(a, b):
    return jnp.dot(a, b, preferred_element_type=F32)


def _dot_nt(a, b):
    return lax.dot_general(a, b, (((1,), (1,)), ((), ())), preferred_element_type=F32)


def _dot_nt_full(a, b):
    return lax.dot_general(a.astype(F32), b.astype(F32), (((1,), (1,)), ((), ())),
                           preferred_element_type=F32, precision=lax.Precision.HIGHEST)


def _dot_nt_x3(a, b):
    ah, bh = a.astype(BF16), b.astype(BF16)
    al, bl = (a - ah.astype(F32)).astype(BF16), (b - bh.astype(F32)).astype(BF16)
    return _dot_nt(ah, bh) + (_dot_nt(ah, bl) + _dot_nt(al, bh))


def _mm(a, w):
    if w.dtype == F32:
        return jnp.dot(a.astype(F32), w, preferred_element_type=F32, precision=lax.Precision.HIGHEST)
    return jnp.dot(a.astype(w.dtype), w, preferred_element_type=F32)


def _proj_kernel(x_ref, w_ref, *o_refs, widths, by_head):
    acc = _mm(x_ref[...], w_ref[...])
    tm = acc.shape[0]
    off = 0
    for o_ref, wd in zip(o_refs, widths):
        o_ref[...] = acc[:, off:off + wd]
        off += wd
    for r_ref, i in zip(o_refs[len(widths):], by_head):
        off = sum(widths[:i])
        n_h = widths[i] // HEAD_DIM
        for h in range(n_h):
            r_ref[pl.ds(h, tm, stride=n_h), :] = acc[:, off + h * HEAD_DIM:off + (h + 1) * HEAD_DIM]


def _project(x, w, widths, tm, by_head=()):
    m, k = x.shape
    n = w.shape[1]
    assert sum(widths) == n and m % tm == 0
    heads = [widths[i] // HEAD_DIM for i in by_head]
    return pl.pallas_call(
        functools.partial(_proj_kernel, widths=widths, by_head=by_head),
        grid=(m // tm,),
        in_specs=[pl.BlockSpec((tm, k), lambda i: (i, 0)),
                  pl.BlockSpec((k, n), lambda i: (0, 0))],
        out_specs=[pl.BlockSpec((tm, wd), lambda i: (i, 0)) for wd in widths]
        + [pl.BlockSpec((tm * n_h, HEAD_DIM), lambda i: (i, 0)) for n_h in heads],
        out_shape=[jax.ShapeDtypeStruct((m, wd), F32) for wd in widths]
        + [jax.ShapeDtypeStruct((m * n_h, HEAD_DIM), F32) for n_h in heads],
        compiler_params=_cparams("parallel"),
        name="project",
    )(x, w)


CMP_JCHUNK = 8


def _compress_kernel(x_ref, pe_ref, w1_ref, w2_ref, o_ref, acc_ref):
    jc = pl.program_id(1)

    @pl.when(jc == 0)
    def _():
        acc_ref[...] = jnp.zeros_like(acc_ref)

    acc = acc_ref[...]
    for jj in range(CMP_JCHUNK):
        lhs = (x_ref[:, jj, :] + pe_ref[jj:jj + 1, :]).astype(BF16)
        acc = acc + _dot(lhs, w1_ref[jj])
    acc_ref[...] = acc

    @pl.when(jc == pl.num_programs(1) - 1)
    def _():
        h = jax.nn.gelu(acc_ref[...])
        o_ref[...] = _dot(h.astype(BF16), w2_ref[...])


def _compress(rows, pe2, w1big, w2big):
    r = rows.shape[0]
    tr = math.gcd(r, 1024)
    assert tr % 8 == 0
    return pl.pallas_call(
        _compress_kernel,
        grid=(r // tr, CMP_BLOCK // CMP_JCHUNK),
        in_specs=[pl.BlockSpec((tr, CMP_JCHUNK, KV_W), lambda i, j: (i, j, 0)),
                  pl.BlockSpec((CMP_JCHUNK, KV_W), lambda i, j: (j, 0)),
                  pl.BlockSpec((CMP_JCHUNK, KV_W, KV_W), lambda i, j: (j, 0, 0)),
                  pl.BlockSpec((KV_W, KV_W), lambda i, j: (0, 0))],
        out_specs=pl.BlockSpec((tr, KV_W), lambda i, j: (i, 0)),
        out_shape=jax.ShapeDtypeStruct((r, KV_W), F32),
        scratch_shapes=[pltpu.VMEM((tr, KV_W), F32)],
        compiler_params=_cparams("parallel", "arbitrary"),
        name="compress",
    )(rows, pe2, w1big, w2big)


def _rows_at(x_ref, r):
    n, rows, w = x_ref.shape
    return x_ref.reshape(n * rows, w)[pl.ds(r, n, stride=rows), :]


def _compress_pool_kernel(x_ref, pe_ref, w1_ref, w2_ref, o_ref, acc_ref, *, tr):
    jc = pl.program_id(1)

    @pl.when(jc == 0)
    def _():
        acc_ref[...] = jnp.zeros_like(acc_ref)

    for g in range(NSA_KV_GROUPS):
        acc = acc_ref[g]
        for jp in range(CMP_JCHUNK // 2):
            parts = []
            for u in range(2):
                jj = 2 * jp + u
                xs = _rows_at(x_ref, 2 * jj + g) + pe_ref[jj:jj + 1, :]
                parts.append(xs.astype(BF16))
            acc = acc + _dot(jnp.concatenate(parts, axis=-1), w1_ref[jp])
        acc_ref[g] = acc

    @pl.when(jc == pl.num_programs(1) - 1)
    def _():
        for g in range(NSA_KV_GROUPS):
            h = jax.nn.gelu(acc_ref[g])
            o_ref[:, g * HEAD_DIM:(g + 1) * HEAD_DIM] = _dot(h.astype(BF16), w2_ref[...])


def _compress_pool(cache, pe, w1, w2):
    n_pool = cache.shape[0]
    r = n_pool * (PAGE_SIZE // CMP_BLOCK)
    rows = cache.reshape(r, CMP_BLOCK * NSA_KV_GROUPS, HEAD_DIM)
    tr = math.gcd(r, 1024)
    assert tr % 8 == 0
    rows_per = 2 * CMP_JCHUNK
    w1p = w1.reshape(CMP_BLOCK // 2, 2 * HEAD_DIM, CMP_HIDDEN).astype(BF16)
    return pl.pallas_call(
        functools.partial(_compress_pool_kernel, tr=tr),
        grid=(r // tr, CMP_BLOCK // CMP_JCHUNK),
        in_specs=[pl.BlockSpec((tr, rows_per, HEAD_DIM), lambda i, j: (i, j, 0)),
                  pl.BlockSpec((CMP_JCHUNK, HEAD_DIM), lambda i, j: (j, 0)),
                  pl.BlockSpec((CMP_JCHUNK // 2, 2 * HEAD_DIM, CMP_HIDDEN), lambda i, j: (j, 0, 0)),
                  pl.BlockSpec((CMP_HIDDEN, HEAD_DIM), lambda i, j: (0, 0))],
        out_specs=pl.BlockSpec((tr, KV_W), lambda i, j: (i, 0)),
        out_shape=jax.ShapeDtypeStruct((r, KV_W), F32),
        scratch_shapes=[pltpu.VMEM((NSA_KV_GROUPS, tr, CMP_HIDDEN), F32)],
        compiler_params=_cparams("parallel", "arbitrary"),
        name="compress_pool",
    )(rows, pe, w1p, w2.astype(BF16))


def _compress_weights(pe, w1, w2):
    pe2 = jnp.concatenate([pe, pe], axis=-1)
    z1 = jnp.zeros_like(w1)
    w1big = jnp.concatenate([jnp.concatenate([w1, z1], axis=2), jnp.concatenate([z1, w1], axis=2)], axis=1)
    z2 = jnp.zeros_like(w2)
    w2big = jnp.concatenate([jnp.concatenate([w2, z2], axis=1), jnp.concatenate([z2, w2], axis=1)], axis=0)
    return pe2, w1big.astype(BF16), w2big.astype(BF16)


def _dot_tn(a, b):
    return lax.dot_general(a, b, (((0,), (0,)), ((), ())), preferred_element_type=F32)


def _cmp_select_kernel(q_ref, kc_ref, vc_ref, cb_ref, o_ref, sel_ref, *, tq, nb, n_sel):
    qi = pl.program_id(1)
    t = qi * tq + lax.broadcasted_iota(jnp.int32, (nb, tq), 1)
    j = lax.broadcasted_iota(jnp.int32, (nb, tq), 0)
    cur = t // SEL_BLOCK
    for g in range(NSA_KV_GROUPS):
        kc = kc_ref[0, :, g * HEAD_DIM:(g + 1) * HEAD_DIM].astype(BF16)
        vc = vc_ref[0, :, g * HEAD_DIM:(g + 1) * HEAD_DIM].astype(BF16)
        imp = jnp.zeros((nb, tq), F32)
        for h in range(NSA_HPG):
            hh = g * NSA_HPG + h
            q = q_ref[0, :, hh * HEAD_DIM:(hh + 1) * HEAD_DIM].astype(BF16)
            cb = cb_ref[hh]
            valid = cb > 0.5 * NEG
            s = _dot_nt(kc, q) * ATTN_SCALE + cb
            m = jnp.max(s, axis=0, keepdims=True)
            e = jnp.where(valid, jnp.exp(s - m), 0.0)
            den = jnp.sum(e, axis=0, keepdims=True)
            p = e / jnp.where(den > 0, den, 1.0)
            o_ref[0, :, hh * HEAD_DIM:(hh + 1) * HEAD_DIM] = _dot_tn(p.astype(BF16), vc)
            imp = imp + p
        forced = (j == 0) | (j == cur) | (j == cur - 1)
        score = jnp.where(forced, jnp.inf, jnp.where(j <= cur, imp, -jnp.inf))
        rank = jnp.zeros((nb, tq), jnp.int32)
        for k in range(nb):
            row = score[k:k + 1, :]
            rank = rank + jnp.where(row > score, 1, 0) + jnp.where(row == score, jnp.where(k < j, 1, 0), 0)
        sel_ref[0, g] = jnp.where(rank < n_sel, 1.0, 0.0).astype(F32)


def _cmp_select(qa, kcomp, vcomp, cbias, tq):
    b, t, _ = qa.shape
    nb = kcomp.shape[1]
    n_sel = min(N_SEL_BLOCKS, nb)
    return pl.pallas_call(
        functools.partial(_cmp_select_kernel, tq=tq, nb=nb, n_sel=n_sel),
        grid=(b, t // tq),
        in_specs=[pl.BlockSpec((1, tq, NSA_HEADS * HEAD_DIM), lambda bi, qi: (bi, qi, 0)),
                  pl.BlockSpec((1, nb, KV_W), lambda bi, qi: (bi, 0, 0)),
                  pl.BlockSpec((1, nb, KV_W), lambda bi, qi: (bi, 0, 0)),
                  pl.BlockSpec((NSA_HEADS, nb, tq), lambda bi, qi: (0, 0, qi))],
        out_specs=[pl.BlockSpec((1, tq, NSA_HEADS * HEAD_DIM), lambda bi, qi: (bi, qi, 0)),
                   pl.BlockSpec((1, NSA_KV_GROUPS, nb, tq), lambda bi, qi: (bi, 0, 0, qi))],
        out_shape=[jax.ShapeDtypeStruct((b, t, NSA_HEADS * HEAD_DIM), F32),
                   jax.ShapeDtypeStruct((b, NSA_KV_GROUPS, nb, t), F32)],
        compiler_params=_cparams("parallel", "parallel"),
        name="cmp_select",
    )(qa, kcomp, vcomp, cbias)


LOG2E = math.log2(math.e)


def _dense_attn_kernel(q_ref, k_ref, v_ref, mask_ref, expand_ref, d_ref, o_ref, *, tq, n_tiles, hpg, mode):
    qi = pl.program_id(1)

    def attend(n_ch):
        chunk = lambda c: slice(c * tq, (c + 1) * tq)
        if mode == "sel":
            mbs = [jnp.where(_dot_tn(mask_ref[0, 0].astype(BF16), expand_ref[:, chunk(c)]) > 0.5, 0.0, NEG)
                   for c in range(n_ch)]
        else:
            mbs = [jnp.where(mask_ref[0, :, chunk(c)].astype(F32) > 0.5, 0.0, NEG) for c in range(n_ch)]
        ks = [k_ref[0, chunk(c), :].astype(BF16) for c in range(n_ch)]
        vs = [v_ref[0, chunk(c), :].astype(BF16) for c in range(n_ch)]
        for hh in range(hpg):
            q = (q_ref[0, :, hh * HEAD_DIM:(hh + 1) * HEAD_DIM] * (ATTN_SCALE * LOG2E)).astype(BF16)
            ss = []
            for c in range(n_ch):
                s = _dot_nt(q, ks[c]) + mbs[c]
                if n_ch - 1 - c < 2:
                    s = s + d_ref[hh, n_ch - 1 - c]
                ss.append(s)
            mx = ss[0]
            for s in ss[1:]:
                mx = jnp.maximum(mx, s)
            m = jnp.max(mx, axis=-1, keepdims=True)
            es = [jnp.exp2(s - m) for s in ss]
            tot = es[0]
            for e in es[1:]:
                tot = tot + e
            den = jnp.sum(tot, axis=-1, keepdims=True)
            o = _dot(es[0].astype(BF16), vs[0])
            for c in range(1, n_ch):
                o = o + _dot(es[c].astype(BF16), vs[c])
            o_ref[0, :, hh * HEAD_DIM:(hh + 1) * HEAD_DIM] = jnp.where(m > 0.5 * NEG, o / den, 0.0)

    for tile in range(n_tiles):
        @pl.when(qi == tile)
        def _(tile=tile):
            attend(tile + 1)


def _dense_attn(q, k, v, mask, expand, dtiles, tq, mode):
    b, t, qw = q.shape
    n_heads = qw // HEAD_DIM
    hpg = n_heads // (k.shape[2] // HEAD_DIM)
    n_groups = n_heads // hpg
    gw = hpg * HEAD_DIM
    if mode == "sel":
        nb = mask.shape[2]
        mask_spec = pl.BlockSpec((1, 1, nb, tq), lambda bi, qi, g: (bi, g, 0, qi))
    else:
        mask_spec = pl.BlockSpec((1, tq, t), lambda bi, qi, g: (bi, qi, 0))
    return pl.pallas_call(
        functools.partial(_dense_attn_kernel, tq=tq, n_tiles=t // tq, hpg=hpg, mode=mode),
        grid=(b, t // tq, n_groups),
        in_specs=[pl.BlockSpec((1, tq, gw), lambda bi, qi, g: (bi, qi, g)),
                  pl.BlockSpec((1, t, HEAD_DIM), lambda bi, qi, g: (bi, 0, g)),
                  pl.BlockSpec((1, t, HEAD_DIM), lambda bi, qi, g: (bi, 0, g)),
                  mask_spec,
                  pl.BlockSpec(expand.shape, lambda bi, qi, g: (0, 0)),
                  pl.BlockSpec((hpg,) + dtiles.shape[1:], lambda bi, qi, g: (g, 0, 0, 0))],
        out_specs=pl.BlockSpec((1, tq, gw), lambda bi, qi, g: (bi, qi, g)),
        out_shape=jax.ShapeDtypeStruct((b, t, qw), F32),
        compiler_params=_cparams("parallel", "parallel", "parallel"),
        name="dense_attn_" + mode,
    )(q, k, v, mask, expand, dtiles)


def _window_attn_kernel(q_ref, k_ref, v_ref, w_ref, o_ref, *, tq, n_chunks, n_heads, hpg):
    qi = pl.program_id(1)
    starts, pens = [], []
    for r in range(n_chunks):
        cj = qi - (n_chunks - 1) + r
        starts.append(pl.multiple_of(jnp.maximum(cj, 0) * tq, tq))
        pens.append(jnp.where(cj < 0, NEG, 0.0).astype(F32))
    for hh in range(n_heads):
        g = hh // hpg
        q = (q_ref[0, :, hh * HEAD_DIM:(hh + 1) * HEAD_DIM] * (ATTN_SCALE * LOG2E)).astype(BF16)
        ss = []
        for r in range(n_chunks):
            k = k_ref[0, pl.ds(starts[r], tq), g * HEAD_DIM:(g + 1) * HEAD_DIM].astype(BF16)
            ss.append(_dot_nt(q, k) + (w_ref[hh, n_chunks - 1 - r] + pens[r]))
        mx = ss[0]
        for s in ss[1:]:
            mx = jnp.maximum(mx, s)
        m = jnp.max(mx, axis=-1, keepdims=True)
        es = [jnp.exp2(s - m) for s in ss]
        tot = es[0]
        for e in es[1:]:
            tot = tot + e
        den = jnp.sum(tot, axis=-1, keepdims=True)
        o = jnp.zeros((tq, HEAD_DIM), F32)
        for r in range(n_chunks):
            v = v_ref[0, pl.ds(starts[r], tq), g * HEAD_DIM:(g + 1) * HEAD_DIM].astype(BF16)
            o = o + _dot(es[r].astype(BF16), v)
        o_ref[0, :, hh * HEAD_DIM:(hh + 1) * HEAD_DIM] = jnp.where(m > 0.5 * NEG, o / den, 0.0)


def _window_attn(q, k, v, wtiles, tq):
    b, t, qw = q.shape
    n_heads = qw // HEAD_DIM
    hpg = n_heads // (k.shape[2] // HEAD_DIM)
    n_chunks = wtiles.shape[1]
    return pl.pallas_call(
        functools.partial(_window_attn_kernel, tq=tq, n_chunks=n_chunks, n_heads=n_heads, hpg=hpg),
        grid=(b, t // tq),
        in_specs=[pl.BlockSpec((1, tq, qw), lambda bi, qi: (bi, qi, 0)),
                  pl.BlockSpec((1, t, k.shape[2]), lambda bi, qi: (bi, 0, 0)),
                  pl.BlockSpec((1, t, v.shape[2]), lambda bi, qi: (bi, 0, 0)),
                  pl.BlockSpec(wtiles.shape, lambda bi, qi: (0, 0, 0, 0))],
        out_specs=pl.BlockSpec((1, tq, qw), lambda bi, qi: (bi, qi, 0)),
        out_shape=jax.ShapeDtypeStruct((b, t, qw), F32),
        compiler_params=_cparams("parallel", "parallel"),
        name="window_attn",
    )(q, k, v, wtiles)


INT_MIN = -2 ** 31


def _topk_mask(key_ref, n_keep, tq, s_len):
    def body(i, thr_u):
        cand_u = thr_u | jnp.left_shift(jnp.int32(1), 31 - i)
        below = (cand_u ^ INT_MIN) - 1
        cnt = jnp.sum(jnp.where(key_ref[...] > below, 1, 0), axis=-1, keepdims=True)
        return jnp.where(cnt >= n_keep, cand_u, thr_u)

    thr_u = lax.fori_loop(0, 32, body, jnp.zeros((tq, 1), jnp.int32))
    thr = thr_u ^ INT_MIN
    return thr


def _index_select_kernel(iq_ref, mq_ref, mk_ref, o_ref, key_ref, *, tq, s_len, n_keep, n_bands):
    qi = pl.program_id(1)
    tiles = (s_len // tq) // n_bands
    wts = mq_ref[0, :, MISC_IW:MISC_IW + IDX_HEADS] * IDX_DIM ** -0.5
    iqs = [iq_ref[0, :, h * IDX_DIM:(h + 1) * IDX_DIM].astype(BF16) for h in range(IDX_HEADS)]
    t = qi * tq + lax.broadcasted_iota(jnp.int32, (tq, tq), 0)
    col = lax.broadcasted_iota(jnp.int32, (tq, tq), 1)

    def select(s_b):
        for c in range(s_b // tq):
            ik = mk_ref[0, c * tq:(c + 1) * tq, 0:IDX_DIM].astype(BF16)
            score = jnp.zeros((tq, tq), F32)
            for h in range(IDX_HEADS):
                score = score + jnp.maximum(_dot_nt(iqs[h], ik), 0.0) * wts[:, h:h + 1]
            score = score * IDX_HEADS ** -0.5 + 0.0
            score = jnp.where(c * tq + col <= t, score, -jnp.inf)
            bits = pltpu.bitcast(score, jnp.int32)
            key_ref[:, c * tq:(c + 1) * tq] = jnp.where(bits < 0, bits ^ 0x7FFFFFFF, bits)

        def bit_step(i, thr_u):
            cand_u = thr_u | jnp.left_shift(jnp.int32(1), 31 - i)
            below = (cand_u ^ INT_MIN) - 1
            cnt = jnp.sum(jnp.where(key_ref[:, :s_b] > below, 1, 0), axis=-1, keepdims=True)
            return jnp.where(cnt >= n_keep, cand_u, thr_u)

        thr = lax.fori_loop(0, 32, bit_step, jnp.zeros((tq, 1), jnp.int32)) ^ INT_MIN
        key = key_ref[:, :s_b]
        gt = key > thr
        eq = key == thr
        need = n_keep - jnp.sum(jnp.where(gt, 1, 0), axis=-1, keepdims=True)
        n_eq = jnp.sum(jnp.where(eq, 1, 0), axis=-1, keepdims=True)
        o_ref[0, :, :s_b] = jnp.where(gt | eq, 1.0, 0.0).astype(o_ref.dtype)
        if s_b < s_len:
            o_ref[0, :, s_b:] = jnp.zeros((tq, s_len - s_b), o_ref.dtype)

        @pl.when(jnp.max(n_eq - need) > 0)
        def _():
            r_i = lax.broadcasted_iota(jnp.int32, (128, 128), 0)
            c_i = lax.broadcasted_iota(jnp.int32, (128, 128), 1)
            tri = jnp.where(r_i <= c_i, 1.0, 0.0).astype(BF16)
            before = jnp.zeros((tq, 1), F32)
            need_f = need.astype(F32)
            for c in range(s_b // 128):
                sl = slice(c * 128, (c + 1) * 128)
                eq_c = eq[:, sl]
                eq_f = jnp.where(eq_c, 1.0, 0.0)
                pref = _dot(eq_f.astype(BF16), tri) + before
                keep = gt[:, sl] | (eq_c & (pref <= need_f))
                o_ref[0, :, sl] = jnp.where(keep, 1.0, 0.0).astype(o_ref.dtype)
                before = before + jnp.sum(eq_f, axis=-1, keepdims=True)

    for band in range(n_bands):
        @pl.when(qi // tiles == band)
        def _(band=band):
            select((band + 1) * tiles * tq)


def _index_select(iq, misc, tq, n_keep):
    b, t, _ = iq.shape
    return pl.pallas_call(
        functools.partial(_index_select_kernel, tq=tq, s_len=t, n_keep=n_keep, n_bands=t // tq),
        grid=(b, t // tq),
        in_specs=[pl.BlockSpec((1, tq, IDX_HEADS * IDX_DIM), lambda bi, qi: (bi, qi, 0)),
                  pl.BlockSpec((1, tq, MISC_W), lambda bi, qi: (bi, qi, 0)),
                  pl.BlockSpec((1, t, MISC_W), lambda bi, qi: (bi, 0, 0))],
        out_specs=pl.BlockSpec((1, tq, t), lambda bi, qi: (bi, qi, 0)),
        out_shape=jax.ShapeDtypeStruct((b, t, t), BF16),
        scratch_shapes=[pltpu.VMEM((tq, t), jnp.int32)],
        compiler_params=_cparams("parallel", "parallel"),
        name="index_select",
    )(iq, misc, misc)


def _merge_up_kernel(x_ref, oc_ref, os_ref, ow_ref, ob_ref, misc_ref, wga_ref, wgb_ref, wua_ref, wub_ref,
                     y_ref, oa_ref):
    ga = jax.nn.sigmoid(misc_ref[:, MISC_GA:MISC_GA + 3 * NSA_HEADS])
    for hh in range(NSA_HEADS):
        sl = slice(hh * HEAD_DIM, (hh + 1) * HEAD_DIM)
        oa = (ga[:, hh:hh + 1] * oc_ref[:, sl] + ga[:, NSA_HEADS + hh:NSA_HEADS + hh + 1] * os_ref[:, sl]
              + ga[:, 2 * NSA_HEADS + hh:2 * NSA_HEADS + hh + 1] * ow_ref[:, sl])
        oa_ref[:, sl] = oa.astype(oa_ref.dtype)
    x = x_ref[...]
    ya = _mm(oa_ref[...], wua_ref[...])
    yb = _mm(ob_ref[...], wub_ref[...])
    g_a = jax.nn.sigmoid(_mm(x, wga_ref[...]))
    g_b = jax.nn.sigmoid(_mm(x, wgb_ref[...]))
    y_ref[...] = (g_a * ya + g_b * yb).astype(y_ref.dtype)


def _merge_up(xb, o_cmp, o_sel, o_win, o_b, misc, wga, wgb, wua, wub, tm, tn):
    m = xb.shape[0]
    mx = wua.dtype
    aw = NSA_HEADS * HEAD_DIM
    bw = DSA_HEADS * HEAD_DIM
    row = lambda w: pl.BlockSpec((tm, w), lambda j, i: (i, 0))
    wcol = lambda k: pl.BlockSpec((k, tn), lambda j, i: (0, j))
    return pl.pallas_call(
        _merge_up_kernel,
        grid=(D_MODEL // tn, m // tm),
        in_specs=[row(D_MODEL), row(aw), row(aw), row(aw), row(bw), row(MISC_W),
                  wcol(D_MODEL), wcol(D_MODEL), wcol(aw), wcol(bw)],
        out_specs=pl.BlockSpec((tm, tn), lambda j, i: (i, j)),
        out_shape=jax.ShapeDtypeStruct((m, D_MODEL), mx),
        scratch_shapes=[pltpu.VMEM((tm, aw), mx)],
        compiler_params=_cparams("parallel", "parallel"),
        name="merge_up",
    )(xb, o_cmp, o_sel, o_win, o_b, misc, wga, wgb, wua, wub)


def _layer_norm(z, g, b):
    mu = jnp.mean(z, axis=-1, keepdims=True)
    zc = z - mu
    var = jnp.mean(zc * zc, axis=-1, keepdims=True)
    return zc * lax.rsqrt(var + LN_EPS) * g + b


def _merge_out_kernel(x_ref, y_ref, wo_ref, g_ref, b_ref, *rest):
    o_ref = rest[-1]
    z = DEEPNORM_ALPHA * x_ref[...] + _mm(y_ref[...], wo_ref[...])
    o_ref[...] = _layer_norm(z, g_ref[...], b_ref[...])


def _merge_out(x, y, wo, g, b, tm, out_rows=None, into=None):
    m = x.shape[0]
    out_rows = out_rows or m
    row = lambda: pl.BlockSpec((tm, D_MODEL), lambda i: (i, 0))
    args = [x, y, wo, g, b]
    in_specs = [row(), row(), pl.BlockSpec((D_MODEL, D_MODEL), lambda i: (0, 0)),
                pl.BlockSpec((1, D_MODEL), lambda i: (0, 0)), pl.BlockSpec((1, D_MODEL), lambda i: (0, 0))]
    first = 0
    aliases = {}
    if into is not None:
        assert into.shape == (out_rows, D_MODEL) and (out_rows - m) % tm == 0
        first = (out_rows - m) // tm
        args.append(into)
        in_specs.append(pl.BlockSpec(memory_space=pl.ANY))
        aliases = {len(args) - 1: 0}
    return pl.pallas_call(
        _merge_out_kernel,
        grid=(m // tm,),
        in_specs=in_specs,
        out_specs=pl.BlockSpec((tm, D_MODEL), lambda i: (i + first, 0)),
        out_shape=jax.ShapeDtypeStruct((out_rows, D_MODEL), F32),
        input_output_aliases=aliases,
        compiler_params=_cparams("parallel"),
        name="merge_out",
    )(*args)


def _router_kernel(x_ref, wr_ref, rb_ref, c0_ref, idx_ref, wt_ref, pos_ref, cnt_ref, run_ref, *, tm):
    epg = N_EXPERTS // N_EXPERT_GROUPS
    logits = lax.dot_general(wr_ref[...], x_ref[...], (((1,), (1,)), ((), ())),
                             preferred_element_type=F32, precision=lax.Precision.HIGHEST)
    scores = jax.nn.sigmoid(logits)
    biased = scores + rb_ref[...]
    sub = lax.broadcasted_iota(jnp.int32, (epg, tm), 0)
    gs_rows = []
    for r in range(N_EXPERT_GROUPS):
        bg = biased[r * epg:(r + 1) * epg, :]
        m1 = jnp.max(bg, axis=0, keepdims=True)
        i1 = jnp.min(jnp.where(bg == m1, sub, epg), axis=0, keepdims=True)
        m2 = jnp.max(jnp.where(sub == i1, -jnp.inf, bg), axis=0, keepdims=True)
        gs_rows.append(m1 + m2)
    gs = jnp.concatenate(gs_rows, axis=0)
    grow = lax.broadcasted_iota(jnp.int32, (N_EXPERT_GROUPS, tm), 0)
    rank = jnp.zeros((N_EXPERT_GROUPS, tm), jnp.int32)
    for k in range(N_EXPERT_GROUPS):
        rk = gs[k:k + 1, :]
        rank = rank + jnp.where((rk > gs) | ((rk == gs) & (k < grow)), 1, 0)
    gkeep = rank < TOPK_GROUPS
    masked = jnp.concatenate(
        [jnp.where(gkeep[r:r + 1, :], biased[r * epg:(r + 1) * epg, :], -jnp.inf) for r in range(N_EXPERT_GROUPS)],
        axis=0)
    erow = lax.broadcasted_iota(jnp.int32, (N_EXPERTS, tm), 0)
    idx_rows, w_rows, hits = [], [], []
    for _ in range(TOP_K):
        m = jnp.max(masked, axis=0, keepdims=True)
        ix = jnp.min(jnp.where(masked == m, erow, N_EXPERTS), axis=0, keepdims=True)
        hit = erow == ix
        w_rows.append(jnp.sum(jnp.where(hit, scores, 0.0), axis=0, keepdims=True))
        idx_rows.append(ix)
        hits.append(hit)
        masked = jnp.where(hit, -jnp.inf, masked)
    wsum = w_rows[0]
    for w in w_rows[1:]:
        wsum = wsum + w
    pad = 8 - TOP_K
    idx_ref[...] = jnp.concatenate(idx_rows + [jnp.zeros((pad, tm), jnp.int32)], axis=0)
    wt_ref[...] = jnp.concatenate([w / wsum * ROUTED_SCALE for w in w_rows] + [jnp.zeros((pad, tm), F32)], axis=0)

    @pl.when(pl.program_id(0) == 0)
    def _():
        run_ref[...] = c0_ref[...]

    earlier = (lax.broadcasted_iota(jnp.int32, (tm, tm), 0) < lax.broadcasted_iota(jnp.int32, (tm, tm), 1))
    earlier = jnp.where(earlier, 1.0, 0.0).astype(BF16)
    run = run_ref[...]
    pos_rows = []
    for hit in hits:
        onehot = jnp.where(hit, 1.0, 0.0)
        before = _dot(onehot.astype(BF16), earlier).astype(jnp.int32)
        pos_rows.append(jnp.sum(jnp.where(hit, run + before, 0), axis=0, keepdims=True))
        run = run + jnp.sum(onehot, axis=1, keepdims=True).astype(jnp.int32)
    run_ref[...] = run
    cnt_ref[...] = run
    pos_ref[...] = jnp.concatenate(pos_rows + [jnp.zeros((pad, tm), jnp.int32)], axis=0)


def _router(x1, wr_t, rb, counts0, tm, m=None):
    m = m or x1.shape[0]
    tok = lambda: pl.BlockSpec((8, tm), lambda i: (0, i))
    return pl.pallas_call(
        functools.partial(_router_kernel, tm=tm),
        grid=(m // tm,),
        in_specs=[pl.BlockSpec((tm, D_MODEL), lambda i: (i, 0)),
                  pl.BlockSpec((N_EXPERTS, D_MODEL), lambda i: (0, 0)),
                  pl.BlockSpec((N_EXPERTS, 1), lambda i: (0, 0)),
                  pl.BlockSpec((N_EXPERTS, 1), lambda i: (0, 0))],
        out_specs=[tok(), tok(), tok(), pl.BlockSpec((N_EXPERTS, 1), lambda i: (0, 0))],
        out_shape=[jax.ShapeDtypeStruct((8, m), jnp.int32), jax.ShapeDtypeStruct((8, m), F32),
                   jax.ShapeDtypeStruct((8, m), jnp.int32), jax.ShapeDtypeStruct((N_EXPERTS, 1), jnp.int32)],
        scratch_shapes=[pltpu.VMEM((N_EXPERTS, 1), jnp.int32)],
        compiler_params=_cparams("arbitrary"),
        name="router",
    )(x1, wr_t, rb, counts0)


def _expert_kernel(be_ref, nv_ref, x_ref, wg_ref, wu_ref, wd_ref, o_ref, wgb_ref, wub_ref, wdb_ref):
    i = pl.program_id(0)
    valid = i < nv_ref[0]
    new_expert = (i == 0) | (be_ref[i] != be_ref[jnp.maximum(i - 1, 0)])

    @pl.when(valid & new_expert)
    def _():
        wgb_ref[...] = wg_ref[0].astype(BF16)
        wub_ref[...] = wu_ref[0].astype(BF16)
        wdb_ref[...] = wd_ref[0].astype(BF16)

    @pl.when(valid)
    def _():
        x = x_ref[...].astype(BF16)
        h = jax.nn.silu(_dot(x, wgb_ref[...])) * _dot(x, wub_ref[...])
        o_ref[...] = _dot(h.astype(BF16), wdb_ref[...]).astype(o_ref.dtype)

    @pl.when(jnp.logical_not(valid))
    def _():
        o_ref[...] = jnp.zeros_like(o_ref)


def _experts(blk_e, n_valid, xg, wg, wu, wd, tr):
    n_rows = xg.shape[0]
    n_blk = n_rows // tr
    grid_spec = pltpu.PrefetchScalarGridSpec(
        num_scalar_prefetch=2,
        grid=(n_blk,),
        in_specs=[pl.BlockSpec((tr, D_MODEL), lambda i, be, nv: (i, 0)),
                  pl.BlockSpec((1, D_MODEL, EXPERT_FF), lambda i, be, nv: (be[i], 0, 0)),
                  pl.BlockSpec((1, D_MODEL, EXPERT_FF), lambda i, be, nv: (be[i], 0, 0)),
                  pl.BlockSpec((1, EXPERT_FF, D_MODEL), lambda i, be, nv: (be[i], 0, 0))],
        out_specs=pl.BlockSpec((tr, D_MODEL), lambda i, be, nv: (i, 0)),
        scratch_shapes=[pltpu.VMEM((D_MODEL, EXPERT_FF), BF16), pltpu.VMEM((D_MODEL, EXPERT_FF), BF16),
                        pltpu.VMEM((EXPERT_FF, D_MODEL), BF16)],
    )
    return pl.pallas_call(
        _expert_kernel,
        grid_spec=grid_spec,
        out_shape=jax.ShapeDtypeStruct((n_rows, D_MODEL), BF16),
        compiler_params=_cparams("arbitrary"),
        name="experts",
    )(blk_e, n_valid, xg, wg, wu, wd)


def _ffn_out_kernel(x_ref, r_ref, w_ref, sg_ref, su_ref, sd_ref, g_ref, b_ref, o_ref):
    x = x_ref[...]
    xb = x.astype(BF16)
    h = jax.nn.silu(_dot(xb, sg_ref[...])) * _dot(xb, su_ref[...])
    f = _dot(h.astype(BF16), sd_ref[...])
    for k in range(TOP_K):
        f = f + r_ref[k].astype(F32) * w_ref[:, k:k + 1]
    o_ref[...] = _layer_norm(DEEPNORM_ALPHA * x + f, g_ref[...], b_ref[...])


def _ffn_out(x1, rows6, wts, sg, su, sd, g, b, tm, m=None):
    m = m or x1.shape[0]
    row = lambda: pl.BlockSpec((tm, D_MODEL), lambda i: (i, 0))
    full = lambda s: pl.BlockSpec(s, lambda i: (0, 0))
    return pl.pallas_call(
        _ffn_out_kernel,
        grid=(m // tm,),
        in_specs=[row(), pl.BlockSpec((TOP_K, tm, D_MODEL), lambda i: (0, i, 0)), pl.BlockSpec((tm, 8), lambda i: (i, 0)),
                  full((D_MODEL, SHARED_FF)), full((D_MODEL, SHARED_FF)), full((SHARED_FF, D_MODEL)),
                  full((1, D_MODEL)), full((1, D_MODEL))],
        out_specs=row(),
        out_shape=jax.ShapeDtypeStruct((m, D_MODEL), F32),
        compiler_params=_cparams("parallel"),
        name="ffn_out",
    )(x1, rows6, wts, sg, su, sd, g, b)


def _rel_bucket(dist):
    n = jnp.maximum(dist, 0)
    nf = jnp.maximum(n, 1).astype(F32)
    large = MAX_EXACT + (jnp.log(nf / MAX_EXACT) / math.log(MAX_DISTANCE / MAX_EXACT)
                         * (N_BUCKETS - MAX_EXACT)).astype(jnp.int32)
    return jnp.where(n < MAX_EXACT, n, jnp.minimum(large, N_BUCKETS - 1))


def _bias_of(tbl, dist):
    onehot = (_rel_bucket(dist)[..., None] == jnp.arange(N_BUCKETS)).astype(F32)
    return jnp.einsum('...k,kh->...h', onehot, tbl, precision=lax.Precision.HIGHEST)


def _toeplitz_bias(tbl, tq, n_tiles):
    i = jnp.arange(tq)[:, None]
    j = jnp.arange(tq)[None, :]
    dist = jnp.arange(n_tiles)[:, None, None] * tq + (i - j)[None]
    bias = jnp.moveaxis(_bias_of(tbl, dist), -1, 0)
    return bias, dist


PAGE_GROUP = 16
IDX_PAGE_GROUP = 64


def _by_group(hpg, a0, a1):
    row = lax.broadcasted_iota(jnp.int32, (2 * hpg, 1), 0)
    return jnp.where(row < hpg, a0, a1)


def _order_key(x):
    bits = pltpu.bitcast(x, jnp.int32)
    return jnp.where(bits < 0, bits ^ 0x7FFFFFFF, bits)


def _row_topk(key_ref, keep_ref, n_keep, width):
    thr = _topk_mask(key_ref, n_keep, 1, width)
    key = key_ref[...]
    gt = key > thr
    eq = key == thr
    need = n_keep - jnp.sum(jnp.where(gt, 1, 0), axis=-1, keepdims=True)
    n_eq = jnp.sum(jnp.where(eq, 1, 0), axis=-1, keepdims=True)
    keep_ref[...] = jnp.where(gt | eq, 1.0, 0.0)

    @pl.when(jnp.max(n_eq - need) > 0)
    def _():
        r_i = lax.broadcasted_iota(jnp.int32, (128, 128), 0)
        c_i = lax.broadcasted_iota(jnp.int32, (128, 128), 1)
        tri = jnp.where(r_i <= c_i, 1.0, 0.0).astype(BF16)
        need_f = need.astype(F32)

        def body(c, before):
            off = pl.multiple_of(c * 128, 128)
            kc = key_ref[:, pl.ds(off, 128)]
            eq_c = kc == thr
            eq_f = jnp.where(eq_c, 1.0, 0.0)
            pref = _dot(jnp.broadcast_to(eq_f, (8, 128)).astype(BF16), tri)[0:1] + before
            keep_ref[:, pl.ds(off, 128)] = jnp.where((kc > thr) | (eq_c & (pref <= need_f)), 1.0, 0.0)
            return before + jnp.sum(eq_f, axis=-1, keepdims=True)

        lax.fori_loop(0, width // 128, body, jnp.zeros((1, 1), F32))


def _s_cmp_win_kernel(q_ref, kc_ref, vc_ref, cb_ref, wk_ref, wv_ref, nk_ref, nv_ref, wb_ref, b0_ref,
                      ocmp_ref, owin_ref, sel_ref, key_ref, keep_ref, *, nbc, n_blocks, width, n_sel, wlen):
    hpg = NSA_HPG
    qf = q_ref[0]
    q = qf.astype(BF16)
    row = lax.broadcasted_iota(jnp.int32, (NSA_HEADS, 1), 0)
    g0 = row < hpg
    kc = kc_ref[0]
    vc = vc_ref[0]
    cb = cb_ref[...]
    valid = cb > 0.5 * NEG
    s = _by_group(hpg, _dot_nt_full(qf, kc[:, :HEAD_DIM]), _dot_nt_full(qf, kc[:, HEAD_DIM:]))
    s = s * ATTN_SCALE + cb
    m = jnp.max(s, axis=-1, keepdims=True)
    e = jnp.where(valid, jnp.exp(s - m), 0.0)
    den = jnp.sum(e, axis=-1, keepdims=True)
    p = e / jnp.where(den > 0, den, 1.0)
    ocmp_ref[0] = (_dot(jnp.where(g0, p, 0.0).astype(BF16), vc[:, :HEAD_DIM].astype(BF16))
                   + _dot(jnp.where(g0, 0.0, p).astype(BF16), vc[:, HEAD_DIM:].astype(BF16)))
    lane = lax.broadcasted_iota(jnp.int32, (1, width), 1)
    cur = n_blocks - 1
    forced = (lane == 0) | (lane == cur) | (lane == cur - 1)
    for g in range(NSA_KV_GROUPS):
        imp = jnp.sum(jnp.where(g0 if g == 0 else jnp.logical_not(g0), p, 0.0), axis=0, keepdims=True)
        impw = jnp.concatenate([imp, jnp.zeros((1, width - nbc), F32)], axis=1)
        score = jnp.where(forced, jnp.inf, jnp.where(lane <= cur, impw, -jnp.inf))
        key_ref[...] = _order_key(score)
        _row_topk(key_ref, keep_ref, n_sel, width)
        sel_ref[0, g:g + 1, :] = keep_ref[...]
    wk0 = wk_ref[pl.ds(0, wlen, stride=2), :].astype(BF16)
    wk1 = wk_ref[pl.ds(1, wlen, stride=2), :].astype(BF16)
    wv0 = wv_ref[pl.ds(0, wlen, stride=2), :].astype(BF16)
    wv1 = wv_ref[pl.ds(1, wlen, stride=2), :].astype(BF16)
    wb = wb_ref[...]
    s = _by_group(hpg, _dot_nt(q, wk0), _dot_nt(q, wk1)) * ATTN_SCALE + wb
    nk = _by_group(hpg, nk_ref[0][:, :HEAD_DIM], nk_ref[0][:, HEAD_DIM:])
    nv = _by_group(hpg, nv_ref[0][:, :HEAD_DIM], nv_ref[0][:, HEAD_DIM:])
    s_new = jnp.sum(qf * nk, axis=-1, keepdims=True) * ATTN_SCALE + b0_ref[...]
    m = jnp.maximum(jnp.max(s, axis=-1, keepdims=True), s_new)
    e = jnp.where(wb > 0.5 * NEG, jnp.exp(s - m), 0.0)
    e_new = jnp.exp(s_new - m)
    den = jnp.sum(e, axis=-1, keepdims=True) + e_new
    o = (_dot(jnp.where(g0, e, 0.0).astype(BF16), wv0) + _dot(jnp.where(g0, 0.0, e).astype(BF16), wv1)
         + e_new * nv)
    owin_ref[0] = o / den


def _s_cmp_win(q, kcomp, vcomp, cbias, wk, wv, nk, nv, wbias, b0, n_blocks, n_sel):
    db = q.shape[0]
    nbc = kcomp.shape[1]
    wlen = wk.shape[1] // 2
    width = -(-n_blocks // 128) * 128
    one = lambda *s: pl.BlockSpec((1,) + s, lambda bi: (bi,) + (0,) * len(s))
    const = lambda a: pl.BlockSpec(a.shape, lambda bi: (0,) * a.ndim)
    return pl.pallas_call(
        functools.partial(_s_cmp_win_kernel, nbc=nbc, n_blocks=n_blocks, width=width, n_sel=n_sel, wlen=wlen),
        grid=(db,),
        in_specs=[one(NSA_HEADS, HEAD_DIM), one(nbc, KV_W), one(nbc, KV_W), const(cbias),
                  pl.BlockSpec((None, 2 * wlen, HEAD_DIM), lambda bi: (bi, 0, 0)),
                  pl.BlockSpec((None, 2 * wlen, HEAD_DIM), lambda bi: (bi, 0, 0)),
                  one(1, KV_W), one(1, KV_W), const(wbias), const(b0)],
        out_specs=[one(NSA_HEADS, HEAD_DIM), one(NSA_HEADS, HEAD_DIM), one(NSA_KV_GROUPS, width)],
        out_shape=[jax.ShapeDtypeStruct((db, NSA_HEADS, HEAD_DIM), F32),
                   jax.ShapeDtypeStruct((db, NSA_HEADS, HEAD_DIM), F32),
                   jax.ShapeDtypeStruct((db, NSA_KV_GROUPS, width), F32)],
        scratch_shapes=[pltpu.VMEM((1, width), jnp.int32), pltpu.VMEM((1, width), F32)],
        compiler_params=_cparams("parallel"),
        name="sample_cmp_win",
    )(q, kcomp, vcomp, cbias, wk, wv, nk, nv, wbias, b0)


def _s_sel_kernel(idx_ref, hp_ref, q_ref, nk_ref, nv_ref, tb_ref, c_ref, *rest, n_sel, cur, n_near):
    del hp_ref
    k_refs, v_refs, o_ref = rest[:n_sel], rest[n_sel:2 * n_sel], rest[2 * n_sel]
    bi = pl.program_id(0)
    g = pl.program_id(1)
    q = q_ref[0].astype(BF16)
    nk = jnp.where(g == 0, nk_ref[0][:, :HEAD_DIM], nk_ref[0][:, HEAD_DIM:])
    nv = jnp.where(g == 0, nv_ref[0][:, :HEAD_DIM], nv_ref[0][:, HEAD_DIM:])
    rowi = lax.broadcasted_iota(jnp.int32, (SEL_BLOCK, 1), 0)
    ss, vs = [], []
    for r in range(n_sel):
        idx = idx_ref[(bi * NSA_KV_GROUPS + g) * n_sel + r]
        first = (rowi == 0) & (idx >= cur)
        kr = jnp.where(first, nk, k_refs[r][pl.ds(g, SEL_BLOCK, stride=2), :]).astype(BF16)
        vs.append(jnp.where(first, nv, v_refs[r][pl.ds(g, SEL_BLOCK, stride=2), :]).astype(BF16))
        u = jnp.clip(idx - (cur - (n_near - 1)), 0, n_near - 1)
        bias = jnp.where(idx >= cur - (n_near - 1), tb_ref[u], c_ref[...])
        ss.append(_dot_nt(q, kr) * ATTN_SCALE + bias)
    m = ss[0].max(axis=-1, keepdims=True)
    for s in ss[1:]:
        m = jnp.maximum(m, s.max(axis=-1, keepdims=True))
    den = jnp.zeros((NSA_HEADS, 1), F32)
    o = jnp.zeros((NSA_HEADS, HEAD_DIM), F32)
    for s, v in zip(ss, vs):
        e = jnp.where(s > 0.5 * NEG, jnp.exp(s - m), 0.0)
        den = den + e.sum(axis=-1, keepdims=True)
        o = o + _dot(e.astype(BF16), v)
    o_ref[0, 0] = o / jnp.where(den > 0, den, 1.0)


def _s_sel(idx_flat, hp_flat, q, nk, nv, tb, c, cache_k, cache_v, n_sel, cur):
    db = q.shape[0]
    rows = SEL_BLOCK * NSA_KV_GROUPS

    def page(r):
        return pl.BlockSpec((None, rows, HEAD_DIM),
                            lambda bi, g, idx, hp, r=r: (hp[(bi * NSA_KV_GROUPS + g) * n_sel + r], 0, 0))

    grid_spec = pltpu.PrefetchScalarGridSpec(
        num_scalar_prefetch=2,
        grid=(db, NSA_KV_GROUPS),
        in_specs=[pl.BlockSpec((1, NSA_HEADS, HEAD_DIM), lambda bi, g, idx, hp: (bi, 0, 0)),
                  pl.BlockSpec((1, 1, KV_W), lambda bi, g, idx, hp: (bi, 0, 0)),
                  pl.BlockSpec((1, 1, KV_W), lambda bi, g, idx, hp: (bi, 0, 0)),
                  pl.BlockSpec(tb.shape, lambda bi, g, idx, hp: (0, 0, 0)),
                  pl.BlockSpec(c.shape, lambda bi, g, idx, hp: (0, 0))]
        + [page(r) for r in range(n_sel)] + [page(r) for r in range(n_sel)],
        out_specs=pl.BlockSpec((1, 1, NSA_HEADS, HEAD_DIM), lambda bi, g, idx, hp: (bi, g, 0, 0)),
    )
    return pl.pallas_call(
        functools.partial(_s_sel_kernel, n_sel=n_sel, cur=cur, n_near=tb.shape[0]),
        grid_spec=grid_spec,
        out_shape=jax.ShapeDtypeStruct((db, NSA_KV_GROUPS, NSA_HEADS, HEAD_DIM), F32),
        compiler_params=_cparams("parallel", "parallel"),
        name="sample_sel",
    )(idx_flat, hp_flat, q, nk, nv, tb, c, *([cache_k] * n_sel), *([cache_v] * n_sel))


def _s_index_kernel(pt_ref, iq_ref, iw_ref, nik_ref, *rest, pg, past, n_keep, width):
    del pt_ref
    pages, keep_ref, sc_ref, key_ref = rest[:pg], rest[pg], rest[pg + 1], rest[pg + 2]
    j = pl.program_id(1)
    span = pg * PAGE_SIZE
    iqf = iq_ref[0]
    iw = iw_ref[0]
    ik = jnp.concatenate([p[...] for p in pages], axis=0)
    lg = jnp.maximum(_dot_nt_x3(iqf, ik) * IDX_DIM ** -0.5, 0.0)
    sc = jnp.sum(lg * iw, axis=0, keepdims=True) * IDX_HEADS ** -0.5 + 0.0
    sc_ref[:, pl.ds(pl.multiple_of(j * span, span), span)] = sc

    @pl.when(j == pl.num_programs(1) - 1)
    def _():
        lg_new = jnp.maximum(jnp.sum(iqf * nik_ref[0], axis=-1, keepdims=True) * IDX_DIM ** -0.5, 0.0)
        sc_new = jnp.sum(lg_new * iw, axis=0, keepdims=True) * IDX_HEADS ** -0.5 + 0.0
        lane = lax.broadcasted_iota(jnp.int32, (1, width - past), 1)
        sc_ref[:, past:] = jnp.where(lane == 0, sc_new, -jnp.inf)
        key_ref[...] = _order_key(sc_ref[...])
        _row_topk(key_ref, keep_ref.at[0], n_keep, width)


def _s_index(page_table, iq, iw, nik, cache_idx, n_keep):
    db, n_pages = page_table.shape
    pg = math.gcd(n_pages, IDX_PAGE_GROUP)
    past = n_pages * PAGE_SIZE
    width = past + 128

    def page(r):
        return pl.BlockSpec((None, PAGE_SIZE, IDX_DIM), lambda bi, j, pt, r=r: (pt[bi, j * pg + r], 0, 0))

    grid_spec = pltpu.PrefetchScalarGridSpec(
        num_scalar_prefetch=1,
        grid=(db, n_pages // pg),
        in_specs=[pl.BlockSpec((1, IDX_HEADS, IDX_DIM), lambda bi, j, pt: (bi, 0, 0)),
                  pl.BlockSpec((1, IDX_HEADS, 1), lambda bi, j, pt: (bi, 0, 0)),
                  pl.BlockSpec((1, 1, IDX_DIM), lambda bi, j, pt: (bi, 0, 0))]
        + [page(r) for r in range(pg)],
        out_specs=pl.BlockSpec((1, 1, width), lambda bi, j, pt: (bi, 0, 0)),
        scratch_shapes=[pltpu.VMEM((1, width), F32), pltpu.VMEM((1, width), jnp.int32)],
    )
    return pl.pallas_call(
        functools.partial(_s_index_kernel, pg=pg, past=past, n_keep=n_keep, width=width),
        grid_spec=grid_spec,
        out_shape=jax.ShapeDtypeStruct((db, 1, width), F32),
        compiler_params=_cparams("parallel", "arbitrary"),
        name="sample_index",
    )(page_table, iq, iw, nik, *([cache_idx] * pg))


def _s_dsa_kernel(pt_ref, q_ref, keep_ref, bt_ref, c_ref, b0_ref, nk_ref, nv_ref, *rest, pg, past):
    del pt_ref
    kp, vp, o_ref = rest[:pg], rest[pg:2 * pg], rest[2 * pg]
    m_ref, l_ref, acc_ref = rest[2 * pg + 1:]
    hpg = DSA_HPG
    j = pl.program_id(1)
    last = pl.num_programs(1) - 1
    span = pg * PAGE_SIZE

    @pl.when(j == 0)
    def _():
        m_ref[...] = jnp.full_like(m_ref, NEG)
        l_ref[...] = jnp.zeros_like(l_ref)
        acc_ref[...] = jnp.zeros_like(acc_ref)

    qf = q_ref[0]
    q = qf.astype(BF16)
    row = lax.broadcasted_iota(jnp.int32, (DSA_HEADS, 1), 0)
    g0 = row < hpg
    rows = lambda refs, g: jnp.concatenate([p[pl.ds(g, PAGE_SIZE, stride=2), :] for p in refs], axis=0).astype(BF16)
    s = _by_group(hpg, _dot_nt(q, rows(kp, 0)), _dot_nt(q, rows(kp, 1))) * ATTN_SCALE
    keep = keep_ref[0, :, pl.ds(pl.multiple_of(j * span, span), span)] > 0.5
    s = jnp.where(keep, s + jnp.where(j == last, bt_ref[...], c_ref[...]), NEG)
    m_old = m_ref[...]
    m_new = jnp.maximum(m_old, jnp.max(s, axis=-1, keepdims=True))
    alpha = jnp.exp(m_old - m_new)
    e = jnp.where(keep, jnp.exp(s - m_new), 0.0)
    l_ref[...] = alpha * l_ref[...] + jnp.sum(e, axis=-1, keepdims=True)
    acc_ref[...] = (alpha * acc_ref[...] + _dot(jnp.where(g0, e, 0.0).astype(BF16), rows(vp, 0))
                    + _dot(jnp.where(g0, 0.0, e).astype(BF16), rows(vp, 1)))
    m_ref[...] = m_new

    @pl.when(j == last)
    def _():
        keep_new = keep_ref[0, :, past:past + 1] > 0.5
        nk = _by_group(hpg, nk_ref[0][:, :HEAD_DIM], nk_ref[0][:, HEAD_DIM:])
        nv = _by_group(hpg, nv_ref[0][:, :HEAD_DIM], nv_ref[0][:, HEAD_DIM:])
        s_new = jnp.sum(qf * nk, axis=-1, keepdims=True) * ATTN_SCALE + b0_ref[...]
        s_new = jnp.where(keep_new, s_new, NEG)
        m_old2 = m_ref[...]
        m2 = jnp.maximum(m_old2, s_new)
        a2 = jnp.exp(m_old2 - m2)
        e_new = jnp.where(keep_new, jnp.exp(s_new - m2), 0.0)
        l = a2 * l_ref[...] + e_new
        o_ref[0] = (a2 * acc_ref[...] + e_new * nv) / jnp.where(l > 0, l, 1.0)


def _s_dsa(page_table, q, keep, btail, c, b0, nk, nv, cache_k, cache_v):
    db, n_pages = page_table.shape
    pg = math.gcd(n_pages, PAGE_GROUP)
    past = n_pages * PAGE_SIZE
    width = keep.shape[-1]
    rows = PAGE_SIZE * DSA_KV_GROUPS

    def page(r):
        return pl.BlockSpec((None, rows, HEAD_DIM), lambda bi, j, pt, r=r: (pt[bi, j * pg + r], 0, 0))

    grid_spec = pltpu.PrefetchScalarGridSpec(
        num_scalar_prefetch=1,
        grid=(db, n_pages // pg),
        in_specs=[pl.BlockSpec((1, DSA_HEADS, HEAD_DIM), lambda bi, j, pt: (bi, 0, 0)),
                  pl.BlockSpec((1, 1, width), lambda bi, j, pt: (bi, 0, 0)),
                  pl.BlockSpec(btail.shape, lambda bi, j, pt: (0, 0)),
                  pl.BlockSpec(c.shape, lambda bi, j, pt: (0, 0)),
                  pl.BlockSpec(b0.shape, lambda bi, j, pt: (0, 0)),
                  pl.BlockSpec((1, 1, KV_W), lambda bi, j, pt: (bi, 0, 0)),
                  pl.BlockSpec((1, 1, KV_W), lambda bi, j, pt: (bi, 0, 0))]
        + [page(r) for r in range(pg)] + [page(r) for r in range(pg)],
        out_specs=pl.BlockSpec((1, DSA_HEADS, HEAD_DIM), lambda bi, j, pt: (bi, 0, 0)),
        scratch_shapes=[pltpu.VMEM((DSA_HEADS, 1), F32), pltpu.VMEM((DSA_HEADS, 1), F32),
                        pltpu.VMEM((DSA_HEADS, HEAD_DIM), F32)],
    )
    return pl.pallas_call(
        functools.partial(_s_dsa_kernel, pg=pg, past=past),
        grid_spec=grid_spec,
        out_shape=jax.ShapeDtypeStruct((db, DSA_HEADS, HEAD_DIM), F32),
        compiler_params=_cparams("parallel", "arbitrary"),
        name="sample_dsa",
    )(page_table, q, keep, btail, c, b0, nk, nv, *([cache_k] * pg), *([cache_v] * pg))


def _split_w_in(w_in):
    points = np.cumsum(IN_COLS)[:-1].tolist()
    q_a, kv_a, g_a, q_b, kv_b, iq, ik, iw, g_m = jnp.split(w_in, points, axis=-1)
    pad = jnp.zeros((D_MODEL, MISC_W - IDX_DIM - IDX_HEADS - 3 * NSA_HEADS), w_in.dtype)
    w_kv = jnp.concatenate([kv_a, kv_b, ik, iw, g_a, pad], axis=-1)
    w_q = jnp.concatenate([q_a, q_b, iq], axis=-1)
    return w_kv, w_q, g_m[:, :D_MODEL], g_m[:, D_MODEL:]


KV_WIDTHS = (KV_W,) * 8 + (MISC_W,)
Q_WIDTHS = (NSA_HEADS * HEAD_DIM, DSA_HEADS * HEAD_DIM, IDX_HEADS * IDX_DIM)


def _row_tile(m, cap):
    tm = math.gcd(m, cap)
    assert tm % 8 == 0 or tm == m
    return tm


def kernel(x_prompt, x_sample, cache_cmp_k, cache_cmp_v, cache_slc_k, cache_slc_v, state_win_k, state_win_v,
           cache_dsa_k, cache_dsa_v, cache_idx_k, page_table, rel_bias_table, w_in, cmp_pe, cmp_w1, cmp_w2,
           w_up_a, w_up_b, w_o, ln1_g, ln1_b, w_router, router_bias, moe_w_gate, moe_w_up, moe_w_down,
           sh_w_gate, sh_w_up, sh_w_down, ln2_g, ln2_b):
    assert w_in.shape[0] == DEPTH == 1
    b, t, _ = x_prompt.shape
    db, dt, _ = x_sample.shape
    assert dt == 1
    n_pool = cache_cmp_k.shape[1]
    n_pages = page_table.shape[1]
    tq = Q_TILE
    assert t % tq == 0 and t % CMP_BLOCK == 0
    n_p, n_s = b * t, db * dt

    tbl_a = rel_bias_table[:, :NSA_HEADS]
    tbl_b = rel_bias_table[:, NSA_HEADS:]
    mixer_w32 = _split_w_in(w_in[0]) + (w_up_a[0], w_up_b[0], w_o[0])
    w_kv32, w_q32, w_ga32, w_gb32, w_ua32, w_ub32, w_o32 = mixer_w32
    w_kv, w_q, w_ga, w_gb, w_ua, w_ub, w_ob = [w.astype(BF16) for w in mixer_w32]
    cw = [_compress_weights(cmp_pe[0, i], cmp_w1[0, i], cmp_w2[0, i]) for i in range(2)]

    xp = x_prompt.reshape(n_p, D_MODEL)
    xs = x_sample.reshape(n_s, D_MODEL)
    xpb = xp.astype(BF16)
    tm_p = _row_tile(n_p, 512)
    (kc, vc, ks, vs, kw, vw, kb, vb, misc,
     kc_h, vc_h, ks_h, vs_h, kb_h, vb_h) = _project(xpb, w_kv, KV_WIDTHS, tm_p, by_head=(0, 1, 2, 3, 6, 7))
    qa, qb, iq = _project(xpb, w_q, Q_WIDTHS, tm_p)
    s_kc, s_vc, s_ks, s_vs, s_kw, s_vw, s_kb, s_vb, s_misc = _project(xs, w_kv32, KV_WIDTHS, n_s)
    s_qa, s_qb, s_iq = _project(xs, w_q32, Q_WIDTHS, n_s)

    nb = t // CMP_BLOCK
    r3 = lambda a: a.reshape(b, t, -1)
    kcomp = _compress(kc.reshape(b * nb, CMP_BLOCK, KV_W), *cw[0]).reshape(b, nb, KV_W)
    vcomp = _compress(vc.reshape(b * nb, CMP_BLOCK, KV_W), *cw[1]).reshape(b, nb, KV_W)
    cdist = jnp.arange(t)[:, None] - (jnp.arange(nb) * CMP_BLOCK + CMP_BLOCK - 1)[None, :]
    cbias = jnp.where(cdist >= 0, jnp.moveaxis(_bias_of(tbl_a, cdist), -1, 0), NEG).swapaxes(1, 2)
    o_cmp, selmask = _cmp_select(r3(qa), kcomp, vcomp, cbias, tq)

    assert MAX_DISTANCE <= tq
    bias2_a, dist2 = _toeplitz_bias(tbl_a, tq, 2)
    bias2_b, _ = _toeplitz_bias(tbl_b, tq, 2)
    c_a, c_b = tbl_a[N_BUCKETS - 1], tbl_b[N_BUCKETS - 1]
    d_a = jnp.where(dist2 >= 0, (bias2_a - c_a[:, None, None, None]) * LOG2E, NEG)
    d_b = jnp.where(dist2 >= 0, (bias2_b - c_b[:, None, None, None]) * LOG2E, NEG)
    expand = (jnp.arange(t)[None, :] // SEL_BLOCK == jnp.arange(nb)[:, None]).astype(BF16)
    o_sel = _dense_attn(r3(qa), r3(ks), r3(vs), selmask, expand, d_a, tq, "sel")

    n_wchunks = -(-(WINDOW - 1) // tq) + 1
    bias_w, dist_w = _toeplitz_bias(tbl_a, tq, n_wchunks)
    wtiles = jnp.where((dist_w >= 0) & (dist_w < WINDOW), bias_w * LOG2E, NEG)
    o_win = _window_attn(r3(qa), r3(kw), r3(vw), wtiles, tq)

    n_keep = min(DSA_TOPK, t // 4)
    keepmask = _index_select(r3(iq), r3(misc), tq, n_keep)
    o_b = _dense_attn(r3(qb), r3(kb), r3(vb), keepmask, jnp.zeros((8, 128), BF16), d_b, tq, "dsa")

    tm_m = _row_tile(n_p, 256)
    y_p = _merge_up(xpb, o_cmp.reshape(n_p, -1), o_sel.reshape(n_p, -1), o_win.reshape(n_p, -1),
                    o_b.reshape(n_p, -1), misc, w_ga, w_gb, w_ua, w_ub, tm_m, 1024)
    x1 = _merge_out(xp, y_p, w_ob, ln1_g, ln1_b, tm_m, out_rows=n_p + n_s)

    past = n_pages * PAGE_SIZE
    halves = PAGE_SIZE // CMP_BLOCK

    def comp_pool(cache, i):
        c = _compress_pool(cache, cmp_pe[0, i], cmp_w1[0, i], cmp_w2[0, i])
        return c.reshape(n_pool, halves * KV_W)[page_table].reshape(db, n_pages * halves, KV_W)

    s_kcomp = comp_pool(cache_cmp_k[0], 0)
    s_vcomp = comp_pool(cache_cmp_v[0], 1)
    assert past % SEL_BLOCK == 0 and past >= 4 * SEL_BLOCK and PAGE_SIZE == 2 * SEL_BLOCK
    total = past + dt
    nbc = past // CMP_BLOCK
    n_blocks = -(-total // SEL_BLOCK)
    cur = past // SEL_BLOCK
    n_sel = min(N_SEL_BLOCKS, n_blocks)
    col = lambda v: v.reshape(-1, 1)
    new = lambda a: a.reshape(db, 1, KV_W)
    cb_s = _bias_of(tbl_a, past - (jnp.arange(nbc) * CMP_BLOCK + CMP_BLOCK - 1)).T
    w_past = state_win_k.shape[2]
    wdist = w_past - jnp.arange(w_past)
    wb_s = jnp.where(wdist < WINDOW, _bias_of(tbl_a, wdist).T, NEG)
    rows2 = lambda a, n: a.reshape(a.shape[0], n * NSA_KV_GROUPS, HEAD_DIM)
    s_q8 = s_qa.reshape(db, NSA_HEADS, HEAD_DIM)
    so_cmp, so_win, selmask = _s_cmp_win(
        s_q8, s_kcomp, s_vcomp, cb_s, rows2(state_win_k[0], w_past), rows2(state_win_v[0], w_past),
        new(s_kw), new(s_vw), wb_s, col(tbl_a[0]), n_blocks, n_sel)
    kept = selmask[:, :, None, :n_blocks] > 0.5
    nth = jnp.cumsum(kept, axis=-1) == (jnp.arange(n_sel) + 1)[:, None]
    sel_idx = jnp.sum(jnp.where(kept & nth, jnp.arange(n_blocks), 0), axis=-1).astype(jnp.int32)
    sel_page = jnp.take_along_axis(page_table, jnp.minimum(sel_idx // 2, n_pages - 1).reshape(db, -1), axis=1)
    sel_hp = sel_page.reshape(sel_idx.shape) * 2 + sel_idx % 2
    n_near = 4
    ndist = (n_near - 1 - jnp.arange(n_near))[:, None] * SEL_BLOCK - jnp.arange(SEL_BLOCK)[None, :]
    tb_s = jnp.where(ndist[:, None, :] >= 0, jnp.moveaxis(_bias_of(tbl_a, ndist), -1, 1), NEG)
    half_pages = lambda c: c.reshape(n_pool * halves, SEL_BLOCK * NSA_KV_GROUPS, HEAD_DIM)
    so_sel2 = _s_sel(sel_idx.reshape(-1), sel_hp.reshape(-1).astype(jnp.int32), s_q8, new(s_ks), new(s_vs), tb_s,
                     col(tbl_a[N_BUCKETS - 1]), half_pages(cache_slc_k[0]), half_pages(cache_slc_v[0]), n_sel, cur)
    so_sel = jnp.concatenate([so_sel2[:, 0, :NSA_HPG], so_sel2[:, 1, NSA_HPG:]], axis=1)
    keep_s = _s_index(page_table, s_iq.reshape(db, IDX_HEADS, IDX_DIM),
                      s_misc[:, MISC_IW:MISC_IW + IDX_HEADS].reshape(db, IDX_HEADS, 1),
                      s_misc[:, :IDX_DIM].reshape(db, 1, IDX_DIM), cache_idx_k[0], min(DSA_TOPK, total // 4))
    span = math.gcd(n_pages, PAGE_GROUP) * PAGE_SIZE
    assert span >= MAX_DISTANCE
    bt_s = _bias_of(tbl_b, span - jnp.arange(span)).T
    pages2 = lambda c: c.reshape(n_pool, PAGE_SIZE * DSA_KV_GROUPS, HEAD_DIM)
    so_b = _s_dsa(page_table, s_qb.reshape(db, DSA_HEADS, HEAD_DIM), keep_s, bt_s, col(tbl_b[N_BUCKETS - 1]),
                  col(tbl_b[0]), new(s_kb), new(s_vb), pages2(cache_dsa_k[0]), pages2(cache_dsa_v[0]))
    n_win = min(WINDOW, total)
    g4 = lambda a: a.reshape(db, dt, NSA_KV_GROUPS, HEAD_DIM)
    s_wk = jnp.concatenate([state_win_k[0], g4(s_kw)], axis=1)[:, -n_win:]
    s_wv = jnp.concatenate([state_win_v[0], g4(s_vw)], axis=1)[:, -n_win:]
    y_s = _merge_up(xs, so_cmp.reshape(n_s, -1), so_sel.reshape(n_s, -1), so_win.reshape(n_s, -1),
                    so_b.reshape(n_s, -1), s_misc, w_ga32, w_gb32, w_ua32, w_ub32, n_s, 512)
    assert n_p % n_s == 0
    x1 = _merge_out(xs, y_s, w_o32, ln1_g, ln1_b, n_s, out_rows=n_p + n_s, into=x1)
    x1s = x1[n_p:]

    wr_t = w_router[0].T
    rb = router_bias[0].reshape(N_EXPERTS, 1)
    eidx_p, wts_p, pos_p, cnt_p = _router(x1, wr_t, rb, jnp.zeros((N_EXPERTS, 1), jnp.int32), tm_m, m=n_p)
    eidx_s, wts_s, pos_s, cnt = _router(x1s, wr_t, rb, cnt_p, n_s)
    n_tok = n_p + n_s
    eidx = jnp.concatenate([eidx_p[:TOP_K], eidx_s[:TOP_K]], axis=1)
    pos = jnp.concatenate([pos_p[:TOP_K], pos_s[:TOP_K]], axis=1)
    tr = EXPERT_ROWS
    n_asg = n_tok * TOP_K
    counts = cnt[:, 0]
    padded = (counts + tr - 1) // tr * tr
    pend = jnp.cumsum(padded)
    pad_start = pend - padded
    n_rows = -(-n_asg // tr) * tr + N_EXPERTS * tr
    n_blk = n_rows // tr
    dest = pos + jnp.sum(jnp.where(eidx[..., None] == jnp.arange(N_EXPERTS), pad_start, 0), axis=-1)
    blk_e = jnp.minimum(jnp.sum(pend[None, :] <= (jnp.arange(n_blk) * tr)[:, None], axis=1), N_EXPERTS - 1)
    n_valid = (pend[-1] // tr).astype(jnp.int32).reshape(1)
    row_tok = jnp.zeros((n_rows,), jnp.int32).at[dest.reshape(-1)].set(
        jnp.tile(jnp.arange(n_tok, dtype=jnp.int32), TOP_K), unique_indices=True)
    xg = x1[row_tok]
    out_rows = _experts(blk_e.astype(jnp.int32), n_valid, xg, moe_w_gate[0], moe_w_up[0], moe_w_down[0], tr)
    rows6 = out_rows[dest]

    sg, su, sd = sh_w_gate[0].astype(BF16), sh_w_up[0].astype(BF16), sh_w_down[0].astype(BF16)
    y_prompt = _ffn_out(x1, rows6, wts_p.T, sg, su, sd, ln2_g, ln2_b, tm_m, m=n_p).reshape(b, t, D_MODEL)
    y_sample = _ffn_out(x1s, rows6[:, n_p:], wts_s.T, sg, su, sd, ln2_g, ln2_b, n_s).reshape(db, dt, D_MODEL)

    n_win = min(WINDOW, t)
    st = lambda a: a.reshape(1, b, t, NSA_KV_GROUPS, HEAD_DIM)
    ss = lambda a: a.reshape(1, db, dt, NSA_KV_GROUPS, HEAD_DIM)
    return (y_prompt, y_sample,
            st(kc_h), st(vc_h), st(ks_h), st(vs_h), st(kw)[:, :, -n_win:], st(vw)[:, :, -n_win:], st(kb_h), st(vb_h),
            misc[:, :IDX_DIM].reshape(1, b, t, IDX_DIM),
            ss(s_kc), ss(s_vc), ss(s_ks), ss(s_vs), s_wk[None], s_wv[None], ss(s_kb), ss(s_vb),
            s_misc[:, :IDX_DIM].reshape(1, db, dt, IDX_DIM))
```

```python
import functools
import math

import jax
import jax.numpy as jnp
import numpy as np
from jax import lax
from jax.experimental import pallas as pl
from jax.experimental.pallas import tpu as pltpu

D_MODEL = 2048
PAGE_SIZE = 128
HEAD_DIM = 128
NSA_HEADS = 8
NSA_KV_GROUPS = 2
NSA_HPG = NSA_HEADS // NSA_KV_GROUPS
CMP_BLOCK = 64
CMP_HIDDEN = 128
SEL_BLOCK = 64
N_SEL_BLOCKS = 16
WINDOW = 512
DSA_HEADS = 8
DSA_KV_GROUPS = 2
DSA_HPG = DSA_HEADS // DSA_KV_GROUPS
IDX_HEADS = 8
IDX_DIM = 64
DSA_TOPK = 256
N_BUCKETS = 32
MAX_EXACT = 16
MAX_DISTANCE = 128
N_EXPERTS = 64
EXPERT_FF = 512
SHARED_FF = 512
TOP_K = 6
N_EXPERT_GROUPS = 8
TOPK_GROUPS = 4
ROUTED_SCALE = 2.5
LN_EPS = 1e-5
ATTN_SCALE = HEAD_DIM ** -0.5
DEPTH = 1
DEEPNORM_ALPHA = (2 * DEPTH) ** 0.25
IN_COLS = (NSA_HEADS * HEAD_DIM, 6 * NSA_KV_GROUPS * HEAD_DIM, 3 * NSA_HEADS,
           DSA_HEADS * HEAD_DIM, 2 * DSA_KV_GROUPS * HEAD_DIM,
           IDX_HEADS * IDX_DIM, IDX_DIM, IDX_HEADS, 2 * D_MODEL)

KV_W = NSA_KV_GROUPS * HEAD_DIM
MISC_W = 128
MISC_IW = IDX_DIM
MISC_GA = IDX_DIM + IDX_HEADS
NEG = -1e30
Q_TILE = 256
EXPERT_ROWS = 256
VMEM_LIMIT = 56 * 1024 * 1024

BF16 = jnp.bfloat16
F32 = jnp.float32


def _cparams(*sem):
    return pltpu.CompilerParams(dimension_semantics=sem, vmem_limit_bytes=VMEM_LIMIT)


def _dot(a, b):
    return jnp.dot(a, b, preferred_element_type=F32)


def _dot_nt(a, b):
    return lax.dot_general(a, b, (((1,), (1,)), ((), ())), preferred_element_type=F32)


def _dot_nt_full(a, b):
    return lax.dot_general(a.astype(F32), b.astype(F32), (((1,), (1,)), ((), ())),
                           preferred_element_type=F32, precision=lax.Precision.HIGHEST)


def _dot_nt_x3(a, b):
    ah, bh = a.astype(BF16), b.astype(BF16)
    al, bl = (a - ah.astype(F32)).astype(BF16), (b - bh.astype(F32)).astype(BF16)
    return _dot_nt(ah, bh) + (_dot_nt(ah, bl) + _dot_nt(al, bh))


def _mm(a, w):
    if w.dtype == F32:
        return jnp.dot(a.astype(F32), w, preferred_element_type=F32, precision=lax.Precision.HIGHEST)
    return jnp.dot(a.astype(w.dtype), w, preferred_element_type=F32)


def _proj_kernel(x_ref, w_ref, *o_refs, widths, by_head):
    acc = _mm(x_ref[...], w_ref[...])
    tm = acc.shape[0]
    off = 0
    for o_ref, wd in zip(o_refs, widths):
        o_ref[...] = acc[:, off:off + wd]
        off += wd
    for r_ref, i in zip(o_refs[len(widths):], by_head):
        off = sum(widths[:i])
        n_h = widths[i] // HEAD_DIM
        for h in range(n_h):
            r_ref[pl.ds(h, tm, stride=n_h), :] = acc[:, off + h * HEAD_DIM:off + (h + 1) * HEAD_DIM]


def _project(x, w, widths, tm, by_head=()):
    m, k = x.shape
    n = w.shape[1]
    assert sum(widths) == n and m % tm == 0
    heads = [widths[i] // HEAD_DIM for i in by_head]
    return pl.pallas_call(
        functools.partial(_proj_kernel, widths=widths, by_head=by_head),
        grid=(m // tm,),
        in_specs=[pl.BlockSpec((tm, k), lambda i: (i, 0)),
                  pl.BlockSpec((k, n), lambda i: (0, 0))],
        out_specs=[pl.BlockSpec((tm, wd), lambda i: (i, 0)) for wd in widths]
        + [pl.BlockSpec((tm * n_h, HEAD_DIM), lambda i: (i, 0)) for n_h in heads],
        out_shape=[jax.ShapeDtypeStruct((m, wd), F32) for wd in widths]
        + [jax.ShapeDtypeStruct((m * n_h, HEAD_DIM), F32) for n_h in heads],
        compiler_params=_cparams("parallel"),
        name="project",
    )(x, w)


CMP_JCHUNK = 8


def _compress_kernel(x_ref, pe_ref, w1_ref, w2_ref, o_ref, acc_ref):
    jc = pl.program_id(1)

    @pl.when(jc == 0)
    def _():
        acc_ref[...] = jnp.zeros_like(acc_ref)

    acc = acc_ref[...]
    for jj in range(CMP_JCHUNK):
        lhs = (x_ref[:, jj, :] + pe_ref[jj:jj + 1, :]).astype(BF16)
        acc = acc + _dot(lhs, w1_ref[jj])
    acc_ref[...] = acc

    @pl.when(jc == pl.num_programs(1) - 1)
    def _():
        h = jax.nn.gelu(acc_ref[...])
        o_ref[...] = _dot(h.astype(BF16), w2_ref[...])


def _compress(rows, pe2, w1big, w2big):
    r = rows.shape[0]
    tr = math.gcd(r, 1024)
    assert tr % 8 == 0
    return pl.pallas_call(
        _compress_kernel,
        grid=(r // tr, CMP_BLOCK // CMP_JCHUNK),
        in_specs=[pl.BlockSpec((tr, CMP_JCHUNK, KV_W), lambda i, j: (i, j, 0)),
                  pl.BlockSpec((CMP_JCHUNK, KV_W), lambda i, j: (j, 0)),
                  pl.BlockSpec((CMP_JCHUNK, KV_W, KV_W), lambda i, j: (j, 0, 0)),
                  pl.BlockSpec((KV_W, KV_W), lambda i, j: (0, 0))],
        out_specs=pl.BlockSpec((tr, KV_W), lambda i, j: (i, 0)),
        out_shape=jax.ShapeDtypeStruct((r, KV_W), F32),
        scratch_shapes=[pltpu.VMEM((tr, KV_W), F32)],
        compiler_params=_cparams("parallel", "arbitrary"),
        name="compress",
    )(rows, pe2, w1big, w2big)


def _rows_at(x_ref, r):
    n, rows, w = x_ref.shape
    return x_ref.reshape(n * rows, w)[pl.ds(r, n, stride=rows), :]


def _compress_pool_kernel(x_ref, pe_ref, w1_ref, w2_ref, o_ref, acc_ref, *, tr):
    jc = pl.program_id(1)

    @pl.when(jc == 0)
    def _():
        acc_ref[...] = jnp.zeros_like(acc_ref)

    for g in range(NSA_KV_GROUPS):
        acc = acc_ref[g]
        for jp in range(CMP_JCHUNK // 2):
            parts = []
            for u in range(2):
                jj = 2 * jp + u
                xs = _rows_at(x_ref, 2 * jj + g) + pe_ref[jj:jj + 1, :]
                parts.append(xs.astype(BF16))
            acc = acc + _dot(jnp.concatenate(parts, axis=-1), w1_ref[jp])
        acc_ref[g] = acc

    @pl.when(jc == pl.num_programs(1) - 1)
    def _():
        for g in range(NSA_KV_GROUPS):
            h = jax.nn.gelu(acc_ref[g])
            o_ref[:, g * HEAD_DIM:(g + 1) * HEAD_DIM] = _dot(h.astype(BF16), w2_ref[...])


def _compress_pool(cache, pe, w1, w2):
    n_pool = cache.shape[0]
    r = n_pool * (PAGE_SIZE // CMP_BLOCK)
    rows = cache.reshape(r, CMP_BLOCK * NSA_KV_GROUPS, HEAD_DIM)
    tr = math.gcd(r, 1024)
    assert tr % 8 == 0
    rows_per = 2 * CMP_JCHUNK
    w1p = w1.reshape(CMP_BLOCK // 2, 2 * HEAD_DIM, CMP_HIDDEN).astype(BF16)
    return pl.pallas_call(
        functools.partial(_compress_pool_kernel, tr=tr),
        grid=(r // tr, CMP_BLOCK // CMP_JCHUNK),
        in_specs=[pl.BlockSpec((tr, rows_per, HEAD_DIM), lambda i, j: (i, j, 0)),
                  pl.BlockSpec((CMP_JCHUNK, HEAD_DIM), lambda i, j: (j, 0)),
                  pl.BlockSpec((CMP_JCHUNK // 2, 2 * HEAD_DIM, CMP_HIDDEN), lambda i, j: (j, 0, 0)),
                  pl.BlockSpec((CMP_HIDDEN, HEAD_DIM), lambda i, j: (0, 0))],
        out_specs=pl.BlockSpec((tr, KV_W), lambda i, j: (i, 0)),
        out_shape=jax.ShapeDtypeStruct((r, KV_W), F32),
        scratch_shapes=[pltpu.VMEM((NSA_KV_GROUPS, tr, CMP_HIDDEN), F32)],
        compiler_params=_cparams("parallel", "arbitrary"),
        name="compress_pool",
    )(rows, pe, w1p, w2.astype(BF16))


def _compress_weights(pe, w1, w2):
    pe2 = jnp.concatenate([pe, pe], axis=-1)
    z1 = jnp.zeros_like(w1)
    w1big = jnp.concatenate([jnp.concatenate([w1, z1], axis=2), jnp.concatenate([z1, w1], axis=2)], axis=1)
    z2 = jnp.zeros_like(w2)
    w2big = jnp.concatenate([jnp.concatenate([w2, z2], axis=1), jnp.concatenate([z2, w2], axis=1)], axis=0)
    return pe2, w1big.astype(BF16), w2big.astype(BF16)


def _dot_tn(a, b):
    return lax.dot_general(a, b, (((0,), (0,)), ((), ())), preferred_element_type=F32)


def _cmp_select_kernel(q_ref, kc_ref, vc_ref, cb_ref, o_ref, sel_ref, *, tq, nb, n_sel):
    qi = pl.program_id(1)
    t = qi * tq + lax.broadcasted_iota(jnp.int32, (nb, tq), 1)
    j = lax.broadcasted_iota(jnp.int32, (nb, tq), 0)
    cur = t // SEL_BLOCK
    for g in range(NSA_KV_GROUPS):
        kc = kc_ref[0, :, g * HEAD_DIM:(g + 1) * HEAD_DIM].astype(BF16)
        vc = vc_ref[0, :, g * HEAD_DIM:(g + 1) * HEAD_DIM].astype(BF16)
        imp = jnp.zeros((nb, tq), F32)
        for h in range(NSA_HPG):
            hh = g * NSA_HPG + h
            q = q_ref[0, :, hh * HEAD_DIM:(hh + 1) * HEAD_DIM].astype(BF16)
            cb = cb_ref[hh]
            valid = cb > 0.5 * NEG
            s = _dot_nt(kc, q) * ATTN_SCALE + cb
            m = jnp.max(s, axis=0, keepdims=True)
            e = jnp.where(valid, jnp.exp(s - m), 0.0)
            den = jnp.sum(e, axis=0, keepdims=True)
            p = e / jnp.where(den > 0, den, 1.0)
            o_ref[0, :, hh * HEAD_DIM:(hh + 1) * HEAD_DIM] = _dot_tn(p.astype(BF16), vc)
            imp = imp + p
        forced = (j == 0) | (j == cur) | (j == cur - 1)
        score = jnp.where(forced, jnp.inf, jnp.where(j <= cur, imp, -jnp.inf))
        rank = jnp.zeros((nb, tq), jnp.int32)
        for k in range(nb):
            row = score[k:k + 1, :]
            rank = rank + jnp.where(row > score, 1, 0) + jnp.where(row == score, jnp.where(k < j, 1, 0), 0)
        sel_ref[0, g] = jnp.where(rank < n_sel, 1.0, 0.0).astype(F32)


def _cmp_select(qa, kcomp, vcomp, cbias, tq):
    b, t, _ = qa.shape
    nb = kcomp.shape[1]
    n_sel = min(N_SEL_BLOCKS, nb)
    return pl.pallas_call(
        functools.partial(_cmp_select_kernel, tq=tq, nb=nb, n_sel=n_sel),
        grid=(b, t // tq),
        in_specs=[pl.BlockSpec((1, tq, NSA_HEADS * HEAD_DIM), lambda bi, qi: (bi, qi, 0)),
                  pl.BlockSpec((1, nb, KV_W), lambda bi, qi: (bi, 0, 0)),
                  pl.BlockSpec((1, nb, KV_W), lambda bi, qi: (bi, 0, 0)),
                  pl.BlockSpec((NSA_HEADS, nb, tq), lambda bi, qi: (0, 0, qi))],
        out_specs=[pl.BlockSpec((1, tq, NSA_HEADS * HEAD_DIM), lambda bi, qi: (bi, qi, 0)),
                   pl.BlockSpec((1, NSA_KV_GROUPS, nb, tq), lambda bi, qi: (bi, 0, 0, qi))],
        out_shape=[jax.ShapeDtypeStruct((b, t, NSA_HEADS * HEAD_DIM), F32),
                   jax.ShapeDtypeStruct((b, NSA_KV_GROUPS, nb, t), F32)],
        compiler_params=_cparams("parallel", "parallel"),
        name="cmp_select",
    )(qa, kcomp, vcomp, cbias)


LOG2E = math.log2(math.e)


def _dense_attn_kernel(q_ref, k_ref, v_ref, mask_ref, expand_ref, d_ref, o_ref, *, tq, n_tiles, hpg, mode):
    qi = pl.program_id(2)

    def attend(n_ch):
        chunk = lambda c: slice(c * tq, (c + 1) * tq)
        if mode == "sel":
            mbs = [jnp.where(_dot_tn(mask_ref[0, 0].astype(BF16), expand_ref[:, chunk(c)]) > 0.5, 0.0, NEG)
                   for c in range(n_ch)]
        else:
            mbs = [jnp.where(mask_ref[0, :, chunk(c)].astype(F32) > 0.5, 0.0, NEG) for c in range(n_ch)]
        ks = [k_ref[0, chunk(c), :].astype(BF16) for c in range(n_ch)]
        vs = [v_ref[0, chunk(c), :].astype(BF16) for c in range(n_ch)]
        for hh in range(hpg):
            q = (q_ref[0, :, hh * HEAD_DIM:(hh + 1) * HEAD_DIM] * (ATTN_SCALE * LOG2E)).astype(BF16)
            ss = []
            for c in range(n_ch):
                s = _dot_nt(q, ks[c]) + mbs[c]
                if n_ch - 1 - c < 2:
                    s = s + d_ref[hh, n_ch - 1 - c]
                ss.append(s)
            mx = ss[0]
            for s in ss[1:]:
                mx = jnp.maximum(mx, s)
            m = jnp.max(mx, axis=-1, keepdims=True)
            es = [jnp.exp2(s - m) for s in ss]
            tot = es[0]
            for e in es[1:]:
                tot = tot + e
            den = jnp.sum(tot, axis=-1, keepdims=True)
            o = _dot(es[0].astype(BF16), vs[0])
            for c in range(1, n_ch):
                o = o + _dot(es[c].astype(BF16), vs[c])
            o_ref[0, :, hh * HEAD_DIM:(hh + 1) * HEAD_DIM] = jnp.where(m > 0.5 * NEG, o / den, 0.0)

    for tile in range(n_tiles):
        @pl.when(qi == tile)
        def _(tile=tile):
            attend(tile + 1)


def _dense_attn(q, k, v, mask, expand, dtiles, tq, mode):
    b, t, qw = q.shape
    n_heads = qw // HEAD_DIM
    hpg = n_heads // (k.shape[2] // HEAD_DIM)
    n_groups = n_heads // hpg
    gw = hpg * HEAD_DIM
    if mode == "sel":
        nb = mask.shape[2]
        mask_spec = pl.BlockSpec((1, 1, nb, tq), lambda bi, g, qi: (bi, g, 0, qi))
    else:
        mask_spec = pl.BlockSpec((1, tq, t), lambda bi, g, qi: (bi, qi, 0))
    return pl.pallas_call(
        functools.partial(_dense_attn_kernel, tq=tq, n_tiles=t // tq, hpg=hpg, mode=mode),
        grid=(b, n_groups, t // tq),
        in_specs=[pl.BlockSpec((1, tq, gw), lambda bi, g, qi: (bi, qi, g)),
                  pl.BlockSpec((1, t, HEAD_DIM), lambda bi, g, qi: (bi, 0, g)),
                  pl.BlockSpec((1, t, HEAD_DIM), lambda bi, g, qi: (bi, 0, g)),
                  mask_spec,
                  pl.BlockSpec(expand.shape, lambda bi, g, qi: (0, 0)),
                  pl.BlockSpec((hpg,) + dtiles.shape[1:], lambda bi, g, qi: (g, 0, 0, 0))],
        out_specs=pl.BlockSpec((1, tq, gw), lambda bi, g, qi: (bi, qi, g)),
        out_shape=jax.ShapeDtypeStruct((b, t, qw), F32),
        compiler_params=_cparams("parallel", "parallel", "parallel"),
        name="dense_attn_" + mode,
    )(q, k, v, mask, expand, dtiles)


def _window_attn_kernel(q_ref, k_ref, v_ref, w_ref, o_ref, *, tq, n_chunks, n_heads, hpg):
    qi = pl.program_id(1)
    starts, pens = [], []
    for r in range(n_chunks):
        cj = qi - (n_chunks - 1) + r
        starts.append(pl.multiple_of(jnp.maximum(cj, 0) * tq, tq))
        pens.append(jnp.where(cj < 0, NEG, 0.0).astype(F32))
    for hh in range(n_heads):
        g = hh // hpg
        q = (q_ref[0, :, hh * HEAD_DIM:(hh + 1) * HEAD_DIM] * (ATTN_SCALE * LOG2E)).astype(BF16)
        ss = []
        for r in range(n_chunks):
            k = k_ref[0, pl.ds(starts[r], tq), g * HEAD_DIM:(g + 1) * HEAD_DIM].astype(BF16)
            ss.append(_dot_nt(q, k) + (w_ref[hh, n_chunks - 1 - r] + pens[r]))
        mx = ss[0]
        for s in ss[1:]:
            mx = jnp.maximum(mx, s)
        m = jnp.max(mx, axis=-1, keepdims=True)
        es = [jnp.exp2(s - m) for s in ss]
        tot = es[0]
        for e in es[1:]:
            tot = tot + e
        den = jnp.sum(tot, axis=-1, keepdims=True)
        o = jnp.zeros((tq, HEAD_DIM), F32)
        for r in range(n_chunks):
            v = v_ref[0, pl.ds(starts[r], tq), g * HEAD_DIM:(g + 1) * HEAD_DIM].astype(BF16)
            o = o + _dot(es[r].astype(BF16), v)
        o_ref[0, :, hh * HEAD_DIM:(hh + 1) * HEAD_DIM] = jnp.where(m > 0.5 * NEG, o / den, 0.0)


def _window_attn(q, k, v, wtiles, tq):
    b, t, qw = q.shape
    n_heads = qw // HEAD_DIM
    hpg = n_heads // (k.shape[2] // HEAD_DIM)
    n_chunks = wtiles.shape[1]
    return pl.pallas_call(
        functools.partial(_window_attn_kernel, tq=tq, n_chunks=n_chunks, n_heads=n_heads, hpg=hpg),
        grid=(b, t // tq),
        in_specs=[pl.BlockSpec((1, tq, qw), lambda bi, qi: (bi, qi, 0)),
                  pl.BlockSpec((1, t, k.shape[2]), lambda bi, qi: (bi, 0, 0)),
                  pl.BlockSpec((1, t, v.shape[2]), lambda bi, qi: (bi, 0, 0)),
                  pl.BlockSpec(wtiles.shape, lambda bi, qi: (0, 0, 0, 0))],
        out_specs=pl.BlockSpec((1, tq, qw), lambda bi, qi: (bi, qi, 0)),
        out_shape=jax.ShapeDtypeStruct((b, t, qw), F32),
        compiler_params=_cparams("parallel", "parallel"),
        name="window_attn",
    )(q, k, v, wtiles)


INT_MIN = -2 ** 31


def _topk_mask(key_ref, n_keep, tq, s_len):
    def body(i, thr_u):
        cand_u = thr_u | jnp.left_shift(jnp.int32(1), 31 - i)
        below = (cand_u ^ INT_MIN) - 1
        cnt = jnp.sum(jnp.where(key_ref[...] > below, 1, 0), axis=-1, keepdims=True)
        return jnp.where(cnt >= n_keep, cand_u, thr_u)

    thr_u = lax.fori_loop(0, 32, body, jnp.zeros((tq, 1), jnp.int32))
    thr = thr_u ^ INT_MIN
    return thr


def _index_select_kernel(iq_ref, wt_ref, mk_ref, o_ref, key_ref, *, tq, s_len, n_keep):
    qi = pl.program_id(1)
    wts = wt_ref[0] * IDX_DIM ** -0.5
    iqs = [iq_ref[0, :, h * IDX_DIM:(h + 1) * IDX_DIM].astype(BF16) for h in range(IDX_HEADS)]
    kidx = lax.broadcasted_iota(jnp.int32, (tq, tq), 0)
    t = qi * tq + lax.broadcasted_iota(jnp.int32, (tq, tq), 1)
    chunk = lambda c: slice(c * tq, (c + 1) * tq)

    def select(n_ch):
        s_b = n_ch * tq
        for c in range(n_ch):
            ik = mk_ref[0, chunk(c), 0:IDX_DIM].astype(BF16)
            score = jnp.zeros((tq, tq), F32)
            for h in range(IDX_HEADS):
                score = score + jnp.maximum(_dot_nt(ik, iqs[h]), 0.0) * wts[h:h + 1, :]
            score = score * IDX_HEADS ** -0.5 + 0.0
            score = jnp.where(c * tq + kidx <= t, score, -jnp.inf)
            bits = pltpu.bitcast(score, jnp.int32)
            key_ref[chunk(c), :] = jnp.where(bits < 0, bits ^ 0x7FFFFFFF, bits)

        def bit_step(i, thr_u):
            cand_u = thr_u | jnp.left_shift(jnp.int32(1), 31 - i)
            below = (cand_u ^ INT_MIN) - 1
            cnt = jnp.sum(jnp.where(key_ref[:s_b, :] > below, 1, 0), axis=0, keepdims=True)
            return jnp.where(cnt >= n_keep, cand_u, thr_u)

        thr = lax.fori_loop(0, 32, bit_step, jnp.zeros((1, tq), jnp.int32)) ^ INT_MIN
        key = key_ref[:s_b, :]
        gt = key > thr
        eq = key == thr
        need = n_keep - jnp.sum(jnp.where(gt, 1, 0), axis=0, keepdims=True)
        n_eq = jnp.sum(jnp.where(eq, 1, 0), axis=0, keepdims=True)
        keep = jnp.where(gt | eq, 1.0, 0.0)
        for c in range(n_ch):
            o_ref[0, :, chunk(c)] = keep[chunk(c), :].T.astype(o_ref.dtype)
        if s_b < s_len:
            o_ref[0, :, s_b:] = jnp.zeros((tq, s_len - s_b), o_ref.dtype)

        @pl.when(jnp.max(n_eq - need) > 0)
        def _():
            r_i = lax.broadcasted_iota(jnp.int32, (128, 128), 0)
            c_i = lax.broadcasted_iota(jnp.int32, (128, 128), 1)
            tri = jnp.where(c_i <= r_i, 1.0, 0.0).astype(BF16)
            before = jnp.zeros((1, tq), F32)
            need_f = need.astype(F32)
            for c in range(n_ch):
                parts = []
                for u in range(tq // 128):
                    sl = slice(c * tq + u * 128, c * tq + (u + 1) * 128)
                    eq_c = eq[sl, :]
                    eq_f = jnp.where(eq_c, 1.0, 0.0)
                    pref = _dot(tri, eq_f.astype(BF16)) + before
                    parts.append(jnp.where(gt[sl, :] | (eq_c & (pref <= need_f)), 1.0, 0.0))
                    before = before + jnp.sum(eq_f, axis=0, keepdims=True)
                o_ref[0, :, chunk(c)] = jnp.concatenate(parts, axis=0).T.astype(o_ref.dtype)

    for tile in range(s_len // tq):
        @pl.when(qi == tile)
        def _(tile=tile):
            select(tile + 1)


def _index_select(iq, iw_t, misc, tq, n_keep):
    b, t, _ = iq.shape
    return pl.pallas_call(
        functools.partial(_index_select_kernel, tq=tq, s_len=t, n_keep=n_keep),
        grid=(b, t // tq),
        in_specs=[pl.BlockSpec((1, tq, IDX_HEADS * IDX_DIM), lambda bi, qi: (bi, qi, 0)),
                  pl.BlockSpec((1, IDX_HEADS, tq), lambda bi, qi: (bi, 0, qi)),
                  pl.BlockSpec((1, t, MISC_W), lambda bi, qi: (bi, 0, 0))],
        out_specs=pl.BlockSpec((1, tq, t), lambda bi, qi: (bi, qi, 0)),
        out_shape=jax.ShapeDtypeStruct((b, t, t), BF16),
        scratch_shapes=[pltpu.VMEM((t, tq), jnp.int32)],
        compiler_params=_cparams("parallel", "parallel"),
        name="index_select",
    )(iq, iw_t, misc)


def _merge_up_kernel(x_ref, oc_ref, os_ref, ow_ref, ob_ref, misc_ref, wga_ref, wgb_ref, wua_ref, wub_ref,
                     y_ref, oa_ref):
    ga = jax.nn.sigmoid(misc_ref[:, MISC_GA:MISC_GA + 3 * NSA_HEADS])
    for hh in range(NSA_HEADS):
        sl = slice(hh * HEAD_DIM, (hh + 1) * HEAD_DIM)
        oa = (ga[:, hh:hh + 1] * oc_ref[:, sl] + ga[:, NSA_HEADS + hh:NSA_HEADS + hh + 1] * os_ref[:, sl]
              + ga[:, 2 * NSA_HEADS + hh:2 * NSA_HEADS + hh + 1] * ow_ref[:, sl])
        oa_ref[:, sl] = oa.astype(oa_ref.dtype)
    x = x_ref[...]
    ya = _mm(oa_ref[...], wua_ref[...])
    yb = _mm(ob_ref[...], wub_ref[...])
    g_a = jax.nn.sigmoid(_mm(x, wga_ref[...]))
    g_b = jax.nn.sigmoid(_mm(x, wgb_ref[...]))
    y_ref[...] = (g_a * ya + g_b * yb).astype(y_ref.dtype)


def _merge_up(xb, o_cmp, o_sel, o_win, o_b, misc, wga, wgb, wua, wub, tm, tn):
    m = xb.shape[0]
    mx = wua.dtype
    aw = NSA_HEADS * HEAD_DIM
    bw = DSA_HEADS * HEAD_DIM
    row = lambda w: pl.BlockSpec((tm, w), lambda j, i: (i, 0))
    wcol = lambda k: pl.BlockSpec((k, tn), lambda j, i: (0, j))
    return pl.pallas_call(
        _merge_up_kernel,
        grid=(D_MODEL // tn, m // tm),
        in_specs=[row(D_MODEL), row(aw), row(aw), row(aw), row(bw), row(MISC_W),
                  wcol(D_MODEL), wcol(D_MODEL), wcol(aw), wcol(bw)],
        out_specs=pl.BlockSpec((tm, tn), lambda j, i: (i, j)),
        out_shape=jax.ShapeDtypeStruct((m, D_MODEL), mx),
        scratch_shapes=[pltpu.VMEM((tm, aw), mx)],
        compiler_params=_cparams("parallel", "parallel"),
        name="merge_up",
    )(xb, o_cmp, o_sel, o_win, o_b, misc, wga, wgb, wua, wub)


def _layer_norm(z, g, b):
    mu = jnp.mean(z, axis=-1, keepdims=True)
    zc = z - mu
    var = jnp.mean(zc * zc, axis=-1, keepdims=True)
    return zc * lax.rsqrt(var + LN_EPS) * g + b


def _merge_out_kernel(x_ref, y_ref, wo_ref, g_ref, b_ref, *rest):
    o_ref = rest[-1]
    z = DEEPNORM_ALPHA * x_ref[...] + _mm(y_ref[...], wo_ref[...])
    o_ref[...] = _layer_norm(z, g_ref[...], b_ref[...])


def _merge_out(x, y, wo, g, b, tm, out_rows=None, into=None):
    m = x.shape[0]
    out_rows = out_rows or m
    row = lambda: pl.BlockSpec((tm, D_MODEL), lambda i: (i, 0))
    args = [x, y, wo, g, b]
    in_specs = [row(), row(), pl.BlockSpec((D_MODEL, D_MODEL), lambda i: (0, 0)),
                pl.BlockSpec((1, D_MODEL), lambda i: (0, 0)), pl.BlockSpec((1, D_MODEL), lambda i: (0, 0))]
    first = 0
    aliases = {}
    if into is not None:
        assert into.shape == (out_rows, D_MODEL) and (out_rows - m) % tm == 0
        first = (out_rows - m) // tm
        args.append(into)
        in_specs.append(pl.BlockSpec(memory_space=pl.ANY))
        aliases = {len(args) - 1: 0}
    return pl.pallas_call(
        _merge_out_kernel,
        grid=(m // tm,),
        in_specs=in_specs,
        out_specs=pl.BlockSpec((tm, D_MODEL), lambda i: (i + first, 0)),
        out_shape=jax.ShapeDtypeStruct((out_rows, D_MODEL), F32),
        input_output_aliases=aliases,
        compiler_params=_cparams("parallel"),
        name="merge_out",
    )(*args)


def _router_kernel(x_ref, wr_ref, rb_ref, c0_ref, idx_ref, wt_ref, pos_ref, cnt_ref, run_ref, *, tm):
    epg = N_EXPERTS // N_EXPERT_GROUPS
    logits = lax.dot_general(wr_ref[...], x_ref[...], (((1,), (1,)), ((), ())),
                             preferred_element_type=F32, precision=lax.Precision.HIGHEST)
    scores = jax.nn.sigmoid(logits)
    biased = scores + rb_ref[...]
    sub = lax.broadcasted_iota(jnp.int32, (epg, tm), 0)
    gs_rows = []
    for r in range(N_EXPERT_GROUPS):
        bg = biased[r * epg:(r + 1) * epg, :]
        m1 = jnp.max(bg, axis=0, keepdims=True)
        i1 = jnp.min(jnp.where(bg == m1, sub, epg), axis=0, keepdims=True)
        m2 = jnp.max(jnp.where(sub == i1, -jnp.inf, bg), axis=0, keepdims=True)
        gs_rows.append(m1 + m2)
    gs = jnp.concatenate(gs_rows, axis=0)
    grow = lax.broadcasted_iota(jnp.int32, (N_EXPERT_GROUPS, tm), 0)
    rank = jnp.zeros((N_EXPERT_GROUPS, tm), jnp.int32)
    for k in range(N_EXPERT_GROUPS):
        rk = gs[k:k + 1, :]
        rank = rank + jnp.where((rk > gs) | ((rk == gs) & (k < grow)), 1, 0)
    gkeep = rank < TOPK_GROUPS
    masked = jnp.concatenate(
        [jnp.where(gkeep[r:r + 1, :], biased[r * epg:(r + 1) * epg, :], -jnp.inf) for r in range(N_EXPERT_GROUPS)],
        axis=0)
    erow = lax.broadcasted_iota(jnp.int32, (N_EXPERTS, tm), 0)
    idx_rows, w_rows, hits = [], [], []
    for _ in range(TOP_K):
        m = jnp.max(masked, axis=0, keepdims=True)
        ix = jnp.min(jnp.where(masked == m, erow, N_EXPERTS), axis=0, keepdims=True)
        hit = erow == ix
        w_rows.append(jnp.sum(jnp.where(hit, scores, 0.0), axis=0, keepdims=True))
        idx_rows.append(ix)
        hits.append(hit)
        masked = jnp.where(hit, -jnp.inf, masked)
    wsum = w_rows[0]
    for w in w_rows[1:]:
        wsum = wsum + w
    pad = 8 - TOP_K
    idx_ref[...] = jnp.concatenate(idx_rows + [jnp.zeros((pad, tm), jnp.int32)], axis=0)
    wt_ref[...] = jnp.concatenate([w / wsum * ROUTED_SCALE for w in w_rows] + [jnp.zeros((pad, tm), F32)], axis=0)

    @pl.when(pl.program_id(0) == 0)
    def _():
        run_ref[...] = c0_ref[...]

    earlier = (lax.broadcasted_iota(jnp.int32, (tm, tm), 0) < lax.broadcasted_iota(jnp.int32, (tm, tm), 1))
    earlier = jnp.where(earlier, 1.0, 0.0).astype(BF16)
    run = run_ref[...]
    pos_rows = []
    for hit in hits:
        onehot = jnp.where(hit, 1.0, 0.0)
        before = _dot(onehot.astype(BF16), earlier).astype(jnp.int32)
        pos_rows.append(jnp.sum(jnp.where(hit, run + before, 0), axis=0, keepdims=True))
        run = run + jnp.sum(onehot, axis=1, keepdims=True).astype(jnp.int32)
    run_ref[...] = run
    cnt_ref[...] = run
    pos_ref[...] = jnp.concatenate(pos_rows + [jnp.zeros((pad, tm), jnp.int32)], axis=0)


def _router(x1, wr_t, rb, counts0, tm, m=None):
    m = m or x1.shape[0]
    tok = lambda: pl.BlockSpec((8, tm), lambda i: (0, i))
    return pl.pallas_call(
        functools.partial(_router_kernel, tm=tm),
        grid=(m // tm,),
        in_specs=[pl.BlockSpec((tm, D_MODEL), lambda i: (i, 0)),
                  pl.BlockSpec((N_EXPERTS, D_MODEL), lambda i: (0, 0)),
                  pl.BlockSpec((N_EXPERTS, 1), lambda i: (0, 0)),
                  pl.BlockSpec((N_EXPERTS, 1), lambda i: (0, 0))],
        out_specs=[tok(), tok(), tok(), pl.BlockSpec((N_EXPERTS, 1), lambda i: (0, 0))],
        out_shape=[jax.ShapeDtypeStruct((8, m), jnp.int32), jax.ShapeDtypeStruct((8, m), F32),
                   jax.ShapeDtypeStruct((8, m), jnp.int32), jax.ShapeDtypeStruct((N_EXPERTS, 1), jnp.int32)],
        scratch_shapes=[pltpu.VMEM((N_EXPERTS, 1), jnp.int32)],
        compiler_params=_cparams("arbitrary"),
        name="router",
    )(x1, wr_t, rb, counts0)


def _expert_kernel(be_ref, nv_ref, x_ref, wg_ref, wu_ref, wd_ref, o_ref, wgb_ref, wub_ref, wdb_ref):
    i = pl.program_id(0)
    valid = i < nv_ref[0]
    new_expert = (i == 0) | (be_ref[i] != be_ref[jnp.maximum(i - 1, 0)])

    @pl.when(valid & new_expert)
    def _():
        wgb_ref[...] = wg_ref[0].astype(BF16)
        wub_ref[...] = wu_ref[0].astype(BF16)
        wdb_ref[...] = wd_ref[0].astype(BF16)

    @pl.when(valid)
    def _():
        x = x_ref[...].astype(BF16)
        h = jax.nn.silu(_dot(x, wgb_ref[...])) * _dot(x, wub_ref[...])
        o_ref[...] = _dot(h.astype(BF16), wdb_ref[...]).astype(o_ref.dtype)

    @pl.when(jnp.logical_not(valid))
    def _():
        o_ref[...] = jnp.zeros_like(o_ref)


def _experts(blk_e, n_valid, xg, wg, wu, wd, tr):
    n_rows = xg.shape[0]
    n_blk = n_rows // tr
    grid_spec = pltpu.PrefetchScalarGridSpec(
        num_scalar_prefetch=2,
        grid=(n_blk,),
        in_specs=[pl.BlockSpec((tr, D_MODEL), lambda i, be, nv: (i, 0)),
                  pl.BlockSpec((1, D_MODEL, EXPERT_FF), lambda i, be, nv: (be[i], 0, 0)),
                  pl.BlockSpec((1, D_MODEL, EXPERT_FF), lambda i, be, nv: (be[i], 0, 0)),
                  pl.BlockSpec((1, EXPERT_FF, D_MODEL), lambda i, be, nv: (be[i], 0, 0))],
        out_specs=pl.BlockSpec((tr, D_MODEL), lambda i, be, nv: (i, 0)),
        scratch_shapes=[pltpu.VMEM((D_MODEL, EXPERT_FF), BF16), pltpu.VMEM((D_MODEL, EXPERT_FF), BF16),
                        pltpu.VMEM((EXPERT_FF, D_MODEL), BF16)],
    )
    return pl.pallas_call(
        _expert_kernel,
        grid_spec=grid_spec,
        out_shape=jax.ShapeDtypeStruct((n_rows, D_MODEL), BF16),
        compiler_params=_cparams("arbitrary"),
        name="experts",
    )(blk_e, n_valid, xg, wg, wu, wd)


def _ffn_out_kernel(x_ref, r_ref, w_ref, sg_ref, su_ref, sd_ref, g_ref, b_ref, o_ref):
    x = x_ref[...]
    xb = x.astype(BF16)
    h = jax.nn.silu(_dot(xb, sg_ref[...])) * _dot(xb, su_ref[...])
    f = _dot(h.astype(BF16), sd_ref[...])
    for k in range(TOP_K):
        f = f + r_ref[k].astype(F32) * w_ref[:, k:k + 1]
    o_ref[...] = _layer_norm(DEEPNORM_ALPHA * x + f, g_ref[...], b_ref[...])


def _ffn_out(x1, rows6, wts, sg, su, sd, g, b, tm, m=None):
    m = m or x1.shape[0]
    row = lambda: pl.BlockSpec((tm, D_MODEL), lambda i: (i, 0))
    full = lambda s: pl.BlockSpec(s, lambda i: (0, 0))
    return pl.pallas_call(
        _ffn_out_kernel,
        grid=(m // tm,),
        in_specs=[row(), pl.BlockSpec((TOP_K, tm, D_MODEL), lambda i: (0, i, 0)), pl.BlockSpec((tm, 8), lambda i: (i, 0)),
                  full((D_MODEL, SHARED_FF)), full((D_MODEL, SHARED_FF)), full((SHARED_FF, D_MODEL)),
                  full((1, D_MODEL)), full((1, D_MODEL))],
        out_specs=row(),
        out_shape=jax.ShapeDtypeStruct((m, D_MODEL), F32),
        compiler_params=_cparams("parallel"),
        name="ffn_out",
    )(x1, rows6, wts, sg, su, sd, g, b)


def _rel_bucket(dist):
    n = jnp.maximum(dist, 0)
    nf = jnp.maximum(n, 1).astype(F32)
    large = MAX_EXACT + (jnp.log(nf / MAX_EXACT) / math.log(MAX_DISTANCE / MAX_EXACT)
                         * (N_BUCKETS - MAX_EXACT)).astype(jnp.int32)
    return jnp.where(n < MAX_EXACT, n, jnp.minimum(large, N_BUCKETS - 1))


def _bias_of(tbl, dist):
    onehot = (_rel_bucket(dist)[..., None] == jnp.arange(N_BUCKETS)).astype(F32)
    return jnp.einsum('...k,kh->...h', onehot, tbl, precision=lax.Precision.HIGHEST)


def _toeplitz_bias(tbl, tq, n_tiles):
    i = jnp.arange(tq)[:, None]
    j = jnp.arange(tq)[None, :]
    dist = jnp.arange(n_tiles)[:, None, None] * tq + (i - j)[None]
    bias = jnp.moveaxis(_bias_of(tbl, dist), -1, 0)
    return bias, dist


PAGE_GROUP = 16
IDX_PAGE_GROUP = 64


def _by_group(hpg, a0, a1):
    row = lax.broadcasted_iota(jnp.int32, (2 * hpg, 1), 0)
    return jnp.where(row < hpg, a0, a1)


def _order_key(x):
    bits = pltpu.bitcast(x, jnp.int32)
    return jnp.where(bits < 0, bits ^ 0x7FFFFFFF, bits)


def _row_topk(key_ref, keep_ref, n_keep, width):
    thr = _topk_mask(key_ref, n_keep, 1, width)
    key = key_ref[...]
    gt = key > thr
    eq = key == thr
    need = n_keep - jnp.sum(jnp.where(gt, 1, 0), axis=-1, keepdims=True)
    n_eq = jnp.sum(jnp.where(eq, 1, 0), axis=-1, keepdims=True)
    keep_ref[...] = jnp.where(gt | eq, 1.0, 0.0)

    @pl.when(jnp.max(n_eq - need) > 0)
    def _():
        r_i = lax.broadcasted_iota(jnp.int32, (128, 128), 0)
        c_i = lax.broadcasted_iota(jnp.int32, (128, 128), 1)
        tri = jnp.where(r_i <= c_i, 1.0, 0.0).astype(BF16)
        need_f = need.astype(F32)

        def body(c, before):
            off = pl.multiple_of(c * 128, 128)
            kc = key_ref[:, pl.ds(off, 128)]
            eq_c = kc == thr
            eq_f = jnp.where(eq_c, 1.0, 0.0)
            pref = _dot(jnp.broadcast_to(eq_f, (8, 128)).astype(BF16), tri)[0:1] + before
            keep_ref[:, pl.ds(off, 128)] = jnp.where((kc > thr) | (eq_c & (pref <= need_f)), 1.0, 0.0)
            return before + jnp.sum(eq_f, axis=-1, keepdims=True)

        lax.fori_loop(0, width // 128, body, jnp.zeros((1, 1), F32))


def _s_cmp_win_kernel(q_ref, kc_ref, vc_ref, cb_ref, wk_ref, wv_ref, nk_ref, nv_ref, wb_ref, b0_ref,
                      ocmp_ref, owin_ref, sel_ref, key_ref, keep_ref, *, nbc, n_blocks, width, n_sel, wlen):
    hpg = NSA_HPG
    qf = q_ref[0]
    q = qf.astype(BF16)
    row = lax.broadcasted_iota(jnp.int32, (NSA_HEADS, 1), 0)
    g0 = row < hpg
    kc = kc_ref[0]
    vc = vc_ref[0]
    cb = cb_ref[...]
    valid = cb > 0.5 * NEG
    s = _by_group(hpg, _dot_nt_full(qf, kc[:, :HEAD_DIM]), _dot_nt_full(qf, kc[:, HEAD_DIM:]))
    s = s * ATTN_SCALE + cb
    m = jnp.max(s, axis=-1, keepdims=True)
    e = jnp.where(valid, jnp.exp(s - m), 0.0)
    den = jnp.sum(e, axis=-1, keepdims=True)
    p = e / jnp.where(den > 0, den, 1.0)
    ocmp_ref[0] = (_dot(jnp.where(g0, p, 0.0).astype(BF16), vc[:, :HEAD_DIM].astype(BF16))
                   + _dot(jnp.where(g0, 0.0, p).astype(BF16), vc[:, HEAD_DIM:].astype(BF16)))
    lane = lax.broadcasted_iota(jnp.int32, (1, width), 1)
    cur = n_blocks - 1
    forced = (lane == 0) | (lane == cur) | (lane == cur - 1)
    for g in range(NSA_KV_GROUPS):
        imp = jnp.sum(jnp.where(g0 if g == 0 else jnp.logical_not(g0), p, 0.0), axis=0, keepdims=True)
        impw = jnp.concatenate([imp, jnp.zeros((1, width - nbc), F32)], axis=1)
        score = jnp.where(forced, jnp.inf, jnp.where(lane <= cur, impw, -jnp.inf))
        key_ref[...] = _order_key(score)
        _row_topk(key_ref, keep_ref, n_sel, width)
        sel_ref[0, g:g + 1, :] = keep_ref[...]
    wk0 = wk_ref[pl.ds(0, wlen, stride=2), :].astype(BF16)
    wk1 = wk_ref[pl.ds(1, wlen, stride=2), :].astype(BF16)
    wv0 = wv_ref[pl.ds(0, wlen, stride=2), :].astype(BF16)
    wv1 = wv_ref[pl.ds(1, wlen, stride=2), :].astype(BF16)
    wb = wb_ref[...]
    s = _by_group(hpg, _dot_nt(q, wk0), _dot_nt(q, wk1)) * ATTN_SCALE + wb
    nk = _by_group(hpg, nk_ref[0][:, :HEAD_DIM], nk_ref[0][:, HEAD_DIM:])
    nv = _by_group(hpg, nv_ref[0][:, :HEAD_DIM], nv_ref[0][:, HEAD_DIM:])
    s_new = jnp.sum(qf * nk, axis=-1, keepdims=True) * ATTN_SCALE + b0_ref[...]
    m = jnp.maximum(jnp.max(s, axis=-1, keepdims=True), s_new)
    e = jnp.where(wb > 0.5 * NEG, jnp.exp(s - m), 0.0)
    e_new = jnp.exp(s_new - m)
    den = jnp.sum(e, axis=-1, keepdims=True) + e_new
    o = (_dot(jnp.where(g0, e, 0.0).astype(BF16), wv0) + _dot(jnp.where(g0, 0.0, e).astype(BF16), wv1)
         + e_new * nv)
    owin_ref[0] = o / den


def _s_cmp_win(q, kcomp, vcomp, cbias, wk, wv, nk, nv, wbias, b0, n_blocks, n_sel):
    db = q.shape[0]
    nbc = kcomp.shape[1]
    wlen = wk.shape[1] // 2
    width = -(-n_blocks // 128) * 128
    one = lambda *s: pl.BlockSpec((1,) + s, lambda bi: (bi,) + (0,) * len(s))
    const = lambda a: pl.BlockSpec(a.shape, lambda bi: (0,) * a.ndim)
    return pl.pallas_call(
        functools.partial(_s_cmp_win_kernel, nbc=nbc, n_blocks=n_blocks, width=width, n_sel=n_sel, wlen=wlen),
        grid=(db,),
        in_specs=[one(NSA_HEADS, HEAD_DIM), one(nbc, KV_W), one(nbc, KV_W), const(cbias),
                  pl.BlockSpec((None, 2 * wlen, HEAD_DIM), lambda bi: (bi, 0, 0)),
                  pl.BlockSpec((None, 2 * wlen, HEAD_DIM), lambda bi: (bi, 0, 0)),
                  one(1, KV_W), one(1, KV_W), const(wbias), const(b0)],
        out_specs=[one(NSA_HEADS, HEAD_DIM), one(NSA_HEADS, HEAD_DIM), one(NSA_KV_GROUPS, width)],
        out_shape=[jax.ShapeDtypeStruct((db, NSA_HEADS, HEAD_DIM), F32),
                   jax.ShapeDtypeStruct((db, NSA_HEADS, HEAD_DIM), F32),
                   jax.ShapeDtypeStruct((db, NSA_KV_GROUPS, width), F32)],
        scratch_shapes=[pltpu.VMEM((1, width), jnp.int32), pltpu.VMEM((1, width), F32)],
        compiler_params=_cparams("parallel"),
        name="sample_cmp_win",
    )(q, kcomp, vcomp, cbias, wk, wv, nk, nv, wbias, b0)


def _s_sel_kernel(idx_ref, hp_ref, q_ref, nk_ref, nv_ref, tb_ref, c_ref, *rest, n_sel, cur, n_near):
    del hp_ref
    k_refs, v_refs, o_ref = rest[:n_sel], rest[n_sel:2 * n_sel], rest[2 * n_sel]
    bi = pl.program_id(0)
    g = pl.program_id(1)
    q = q_ref[0].astype(BF16)
    nk = jnp.where(g == 0, nk_ref[0][:, :HEAD_DIM], nk_ref[0][:, HEAD_DIM:])
    nv = jnp.where(g == 0, nv_ref[0][:, :HEAD_DIM], nv_ref[0][:, HEAD_DIM:])
    rowi = lax.broadcasted_iota(jnp.int32, (SEL_BLOCK, 1), 0)
    ss, vs = [], []
    for r in range(n_sel):
        idx = idx_ref[(bi * NSA_KV_GROUPS + g) * n_sel + r]
        first = (rowi == 0) & (idx >= cur)
        kr = jnp.where(first, nk, k_refs[r][pl.ds(g, SEL_BLOCK, stride=2), :]).astype(BF16)
        vs.append(jnp.where(first, nv, v_refs[r][pl.ds(g, SEL_BLOCK, stride=2), :]).astype(BF16))
        u = jnp.clip(idx - (cur - (n_near - 1)), 0, n_near - 1)
        bias = jnp.where(idx >= cur - (n_near - 1), tb_ref[u], c_ref[...])
        ss.append(_dot_nt(q, kr) * ATTN_SCALE + bias)
    m = ss[0].max(axis=-1, keepdims=True)
    for s in ss[1:]:
        m = jnp.maximum(m, s.max(axis=-1, keepdims=True))
    den = jnp.zeros((NSA_HEADS, 1), F32)
    o = jnp.zeros((NSA_HEADS, HEAD_DIM), F32)
    for s, v in zip(ss, vs):
        e = jnp.where(s > 0.5 * NEG, jnp.exp(s - m), 0.0)
        den = den + e.sum(axis=-1, keepdims=True)
        o = o + _dot(e.astype(BF16), v)
    o_ref[0, 0] = o / jnp.where(den > 0, den, 1.0)


def _s_sel(idx_flat, hp_flat, q, nk, nv, tb, c, cache_k, cache_v, n_sel, cur):
    db = q.shape[0]
    rows = SEL_BLOCK * NSA_KV_GROUPS

    def page(r):
        return pl.BlockSpec((None, rows, HEAD_DIM),
                            lambda bi, g, idx, hp, r=r: (hp[(bi * NSA_KV_GROUPS + g) * n_sel + r], 0, 0))

    grid_spec = pltpu.PrefetchScalarGridSpec(
        num_scalar_prefetch=2,
        grid=(db, NSA_KV_GROUPS),
        in_specs=[pl.BlockSpec((1, NSA_HEADS, HEAD_DIM), lambda bi, g, idx, hp: (bi, 0, 0)),
                  pl.BlockSpec((1, 1, KV_W), lambda bi, g, idx, hp: (bi, 0, 0)),
                  pl.BlockSpec((1, 1, KV_W), lambda bi, g, idx, hp: (bi, 0, 0)),
                  pl.BlockSpec(tb.shape, lambda bi, g, idx, hp: (0, 0, 0)),
                  pl.BlockSpec(c.shape, lambda bi, g, idx, hp: (0, 0))]
        + [page(r) for r in range(n_sel)] + [page(r) for r in range(n_sel)],
        out_specs=pl.BlockSpec((1, 1, NSA_HEADS, HEAD_DIM), lambda bi, g, idx, hp: (bi, g, 0, 0)),
    )
    return pl.pallas_call(
        functools.partial(_s_sel_kernel, n_sel=n_sel, cur=cur, n_near=tb.shape[0]),
        grid_spec=grid_spec,
        out_shape=jax.ShapeDtypeStruct((db, NSA_KV_GROUPS, NSA_HEADS, HEAD_DIM), F32),
        compiler_params=_cparams("parallel", "parallel"),
        name="sample_sel",
    )(idx_flat, hp_flat, q, nk, nv, tb, c, *([cache_k] * n_sel), *([cache_v] * n_sel))


def _s_index_kernel(pt_ref, iq_ref, iw_ref, nik_ref, *rest, pg, past, n_keep, width):
    del pt_ref
    pages, keep_ref, sc_ref, key_ref = rest[:pg], rest[pg], rest[pg + 1], rest[pg + 2]
    j = pl.program_id(1)
    span = pg * PAGE_SIZE
    iqf = iq_ref[0]
    iw = iw_ref[0]
    ik = jnp.concatenate([p[...] for p in pages], axis=0)
    lg = jnp.maximum(_dot_nt_x3(iqf, ik) * IDX_DIM ** -0.5, 0.0)
    sc = jnp.sum(lg * iw, axis=0, keepdims=True) * IDX_HEADS ** -0.5 + 0.0
    sc_ref[:, pl.ds(pl.multiple_of(j * span, span), span)] = sc

    @pl.when(j == pl.num_programs(1) - 1)
    def _():
        lg_new = jnp.maximum(jnp.sum(iqf * nik_ref[0], axis=-1, keepdims=True) * IDX_DIM ** -0.5, 0.0)
        sc_new = jnp.sum(lg_new * iw, axis=0, keepdims=True) * IDX_HEADS ** -0.5 + 0.0
        lane = lax.broadcasted_iota(jnp.int32, (1, width - past), 1)
        sc_ref[:, past:] = jnp.where(lane == 0, sc_new, -jnp.inf)
        key_ref[...] = _order_key(sc_ref[...])
        _row_topk(key_ref, keep_ref.at[0], n_keep, width)


def _s_index(page_table, iq, iw, nik, cache_idx, n_keep):
    db, n_pages = page_table.shape
    pg = math.gcd(n_pages, IDX_PAGE_GROUP)
    past = n_pages * PAGE_SIZE
    width = past + 128

    def page(r):
        return pl.BlockSpec((None, PAGE_SIZE, IDX_DIM), lambda bi, j, pt, r=r: (pt[bi, j * pg + r], 0, 0))

    grid_spec = pltpu.PrefetchScalarGridSpec(
        num_scalar_prefetch=1,
        grid=(db, n_pages // pg),
        in_specs=[pl.BlockSpec((1, IDX_HEADS, IDX_DIM), lambda bi, j, pt: (bi, 0, 0)),
                  pl.BlockSpec((1, IDX_HEADS, 1), lambda bi, j, pt: (bi, 0, 0)),
                  pl.BlockSpec((1, 1, IDX_DIM), lambda bi, j, pt: (bi, 0, 0))]
        + [page(r) for r in range(pg)],
        out_specs=pl.BlockSpec((1, 1, width), lambda bi, j, pt: (bi, 0, 0)),
        scratch_shapes=[pltpu.VMEM((1, width), F32), pltpu.VMEM((1, width), jnp.int32)],
    )
    return pl.pallas_call(
        functools.partial(_s_index_kernel, pg=pg, past=past, n_keep=n_keep, width=width),
        grid_spec=grid_spec,
        out_shape=jax.ShapeDtypeStruct((db, 1, width), F32),
        compiler_params=_cparams("parallel", "arbitrary"),
        name="sample_index",
    )(page_table, iq, iw, nik, *([cache_idx] * pg))


def _s_dsa_kernel(pt_ref, q_ref, keep_ref, bt_ref, c_ref, b0_ref, nk_ref, nv_ref, *rest, pg, past):
    del pt_ref
    kp, vp, o_ref = rest[:pg], rest[pg:2 * pg], rest[2 * pg]
    m_ref, l_ref, acc_ref = rest[2 * pg + 1:]
    hpg = DSA_HPG
    j = pl.program_id(1)
    last = pl.num_programs(1) - 1
    span = pg * PAGE_SIZE

    @pl.when(j == 0)
    def _():
        m_ref[...] = jnp.full_like(m_ref, NEG)
        l_ref[...] = jnp.zeros_like(l_ref)
        acc_ref[...] = jnp.zeros_like(acc_ref)

    qf = q_ref[0]
    q = qf.astype(BF16)
    row = lax.broadcasted_iota(jnp.int32, (DSA_HEADS, 1), 0)
    g0 = row < hpg
    rows = lambda refs, g: jnp.concatenate([p[pl.ds(g, PAGE_SIZE, stride=2), :] for p in refs], axis=0).astype(BF16)
    s = _by_group(hpg, _dot_nt(q, rows(kp, 0)), _dot_nt(q, rows(kp, 1))) * ATTN_SCALE
    keep = keep_ref[0, :, pl.ds(pl.multiple_of(j * span, span), span)] > 0.5
    s = jnp.where(keep, s + jnp.where(j == last, bt_ref[...], c_ref[...]), NEG)
    m_old = m_ref[...]
    m_new = jnp.maximum(m_old, jnp.max(s, axis=-1, keepdims=True))
    alpha = jnp.exp(m_old - m_new)
    e = jnp.where(keep, jnp.exp(s - m_new), 0.0)
    l_ref[...] = alpha * l_ref[...] + jnp.sum(e, axis=-1, keepdims=True)
    acc_ref[...] = (alpha * acc_ref[...] + _dot(jnp.where(g0, e, 0.0).astype(BF16), rows(vp, 0))
                    + _dot(jnp.where(g0, 0.0, e).astype(BF16), rows(vp, 1)))
    m_ref[...] = m_new

    @pl.when(j == last)
    def _():
        keep_new = keep_ref[0, :, past:past + 1] > 0.5
        nk = _by_group(hpg, nk_ref[0][:, :HEAD_DIM], nk_ref[0][:, HEAD_DIM:])
        nv = _by_group(hpg, nv_ref[0][:, :HEAD_DIM], nv_ref[0][:, HEAD_DIM:])
        s_new = jnp.sum(qf * nk, axis=-1, keepdims=True) * ATTN_SCALE + b0_ref[...]
        s_new = jnp.where(keep_new, s_new, NEG)
        m_old2 = m_ref[...]
        m2 = jnp.maximum(m_old2, s_new)
        a2 = jnp.exp(m_old2 - m2)
        e_new = jnp.where(keep_new, jnp.exp(s_new - m2), 0.0)
        l = a2 * l_ref[...] + e_new
        o_ref[0] = (a2 * acc_ref[...] + e_new * nv) / jnp.where(l > 0, l, 1.0)


def _s_dsa(page_table, q, keep, btail, c, b0, nk, nv, cache_k, cache_v):
    db, n_pages = page_table.shape
    pg = math.gcd(n_pages, PAGE_GROUP)
    past = n_pages * PAGE_SIZE
    width = keep.shape[-1]
    rows = PAGE_SIZE * DSA_KV_GROUPS

    def page(r):
        return pl.BlockSpec((None, rows, HEAD_DIM), lambda bi, j, pt, r=r: (pt[bi, j * pg + r], 0, 0))

    grid_spec = pltpu.PrefetchScalarGridSpec(
        num_scalar_prefetch=1,
        grid=(db, n_pages // pg),
        in_specs=[pl.BlockSpec((1, DSA_HEADS, HEAD_DIM), lambda bi, j, pt: (bi, 0, 0)),
                  pl.BlockSpec((1, 1, width), lambda bi, j, pt: (bi, 0, 0)),
                  pl.BlockSpec(btail.shape, lambda bi, j, pt: (0, 0)),
                  pl.BlockSpec(c.shape, lambda bi, j, pt: (0, 0)),
                  pl.BlockSpec(b0.shape, lambda bi, j, pt: (0, 0)),
                  pl.BlockSpec((1, 1, KV_W), lambda bi, j, pt: (bi, 0, 0)),
                  pl.BlockSpec((1, 1, KV_W), lambda bi, j, pt: (bi, 0, 0))]
        + [page(r) for r in range(pg)] + [page(r) for r in range(pg)],
        out_specs=pl.BlockSpec((1, DSA_HEADS, HEAD_DIM), lambda bi, j, pt: (bi, 0, 0)),
        scratch_shapes=[pltpu.VMEM((DSA_HEADS, 1), F32), pltpu.VMEM((DSA_HEADS, 1), F32),
                        pltpu.VMEM((DSA_HEADS, HEAD_DIM), F32)],
    )
    return pl.pallas_call(
        functools.partial(_s_dsa_kernel, pg=pg, past=past),
        grid_spec=grid_spec,
        out_shape=jax.ShapeDtypeStruct((db, DSA_HEADS, HEAD_DIM), F32),
        compiler_params=_cparams("parallel", "arbitrary"),
        name="sample_dsa",
    )(page_table, q, keep, btail, c, b0, nk, nv, *([cache_k] * pg), *([cache_v] * pg))


def _split_w_in(w_in):
    points = np.cumsum(IN_COLS)[:-1].tolist()
    q_a, kv_a, g_a, q_b, kv_b, iq, ik, iw, g_m = jnp.split(w_in, points, axis=-1)
    pad = jnp.zeros((D_MODEL, MISC_W - IDX_DIM - IDX_HEADS - 3 * NSA_HEADS), w_in.dtype)
    w_kv = jnp.concatenate([kv_a, kv_b, ik, iw, g_a, pad], axis=-1)
    w_q = jnp.concatenate([q_a, q_b, iq], axis=-1)
    return w_kv, w_q, g_m[:, :D_MODEL], g_m[:, D_MODEL:]


KV_WIDTHS = (KV_W,) * 8 + (MISC_W,)
Q_WIDTHS = (NSA_HEADS * HEAD_DIM, DSA_HEADS * HEAD_DIM, IDX_HEADS * IDX_DIM)


def _row_tile(m, cap):
    tm = math.gcd(m, cap)
    assert tm % 8 == 0 or tm == m
    return tm


def kernel(x_prompt, x_sample, cache_cmp_k, cache_cmp_v, cache_slc_k, cache_slc_v, state_win_k, state_win_v,
           cache_dsa_k, cache_dsa_v, cache_idx_k, page_table, rel_bias_table, w_in, cmp_pe, cmp_w1, cmp_w2,
           w_up_a, w_up_b, w_o, ln1_g, ln1_b, w_router, router_bias, moe_w_gate, moe_w_up, moe_w_down,
           sh_w_gate, sh_w_up, sh_w_down, ln2_g, ln2_b):
    assert w_in.shape[0] == DEPTH == 1
    b, t, _ = x_prompt.shape
    db, dt, _ = x_sample.shape
    assert dt == 1
    n_pool = cache_cmp_k.shape[1]
    n_pages = page_table.shape[1]
    tq = Q_TILE
    assert t % tq == 0 and t % CMP_BLOCK == 0
    n_p, n_s = b * t, db * dt

    tbl_a = rel_bias_table[:, :NSA_HEADS]
    tbl_b = rel_bias_table[:, NSA_HEADS:]
    mixer_w32 = _split_w_in(w_in[0]) + (w_up_a[0], w_up_b[0], w_o[0])
    w_kv32, w_q32, w_ga32, w_gb32, w_ua32, w_ub32, w_o32 = mixer_w32
    w_kv, w_q, w_ga, w_gb, w_ua, w_ub, w_ob = [w.astype(BF16) for w in mixer_w32]
    cw = [_compress_weights(cmp_pe[0, i], cmp_w1[0, i], cmp_w2[0, i]) for i in range(2)]

    xp = x_prompt.reshape(n_p, D_MODEL)
    xs = x_sample.reshape(n_s, D_MODEL)
    xpb = xp.astype(BF16)
    tm_p = _row_tile(n_p, 512)
    (kc, vc, ks, vs, kw, vw, kb, vb, misc,
     kc_h, vc_h, ks_h, vs_h, kb_h, vb_h) = _project(xpb, w_kv, KV_WIDTHS, tm_p, by_head=(0, 1, 2, 3, 6, 7))
    qa, qb, iq = _project(xpb, w_q, Q_WIDTHS, tm_p)
    s_kc, s_vc, s_ks, s_vs, s_kw, s_vw, s_kb, s_vb, s_misc = _project(xs, w_kv32, KV_WIDTHS, n_s)
    s_qa, s_qb, s_iq = _project(xs, w_q32, Q_WIDTHS, n_s)

    nb = t // CMP_BLOCK
    r3 = lambda a: a.reshape(b, t, -1)
    kcomp = _compress(kc.reshape(b * nb, CMP_BLOCK, KV_W), *cw[0]).reshape(b, nb, KV_W)
    vcomp = _compress(vc.reshape(b * nb, CMP_BLOCK, KV_W), *cw[1]).reshape(b, nb, KV_W)
    cdist = jnp.arange(t)[:, None] - (jnp.arange(nb) * CMP_BLOCK + CMP_BLOCK - 1)[None, :]
    cbias = jnp.where(cdist >= 0, jnp.moveaxis(_bias_of(tbl_a, cdist), -1, 0), NEG).swapaxes(1, 2)
    o_cmp, selmask = _cmp_select(r3(qa), kcomp, vcomp, cbias, tq)

    assert MAX_DISTANCE <= tq
    bias2_a, dist2 = _toeplitz_bias(tbl_a, tq, 2)
    bias2_b, _ = _toeplitz_bias(tbl_b, tq, 2)
    c_a, c_b = tbl_a[N_BUCKETS - 1], tbl_b[N_BUCKETS - 1]
    d_a = jnp.where(dist2 >= 0, (bias2_a - c_a[:, None, None, None]) * LOG2E, NEG)
    d_b = jnp.where(dist2 >= 0, (bias2_b - c_b[:, None, None, None]) * LOG2E, NEG)
    expand = (jnp.arange(t)[None, :] // SEL_BLOCK == jnp.arange(nb)[:, None]).astype(BF16)
    o_sel = _dense_attn(r3(qa), r3(ks), r3(vs), selmask, expand, d_a, tq, "sel")

    n_wchunks = -(-(WINDOW - 1) // tq) + 1
    bias_w, dist_w = _toeplitz_bias(tbl_a, tq, n_wchunks)
    wtiles = jnp.where((dist_w >= 0) & (dist_w < WINDOW), bias_w * LOG2E, NEG)
    o_win = _window_attn(r3(qa), r3(kw), r3(vw), wtiles, tq)

    n_keep = min(DSA_TOPK, t // 4)
    iw_t = r3(misc)[:, :, MISC_IW:MISC_IW + IDX_HEADS].swapaxes(1, 2)
    keepmask = _index_select(r3(iq), iw_t, r3(misc), tq, n_keep)
    o_b = _dense_attn(r3(qb), r3(kb), r3(vb), keepmask, jnp.zeros((8, 128), BF16), d_b, tq, "dsa")

    tm_m = _row_tile(n_p, 256)
    y_p = _merge_up(xpb, o_cmp.reshape(n_p, -1), o_sel.reshape(n_p, -1), o_win.reshape(n_p, -1),
                    o_b.reshape(n_p, -1), misc, w_ga, w_gb, w_ua, w_ub, tm_m, 1024)
    x1 = _merge_out(xp, y_p, w_ob, ln1_g, ln1_b, tm_m, out_rows=n_p + n_s)

    past = n_pages * PAGE_SIZE
    halves = PAGE_SIZE // CMP_BLOCK

    def comp_pool(cache, i):
        c = _compress_pool(cache, cmp_pe[0, i], cmp_w1[0, i], cmp_w2[0, i])
        return c.reshape(n_pool, halves * KV_W)[page_table].reshape(db, n_pages * halves, KV_W)

    s_kcomp = comp_pool(cache_cmp_k[0], 0)
    s_vcomp = comp_pool(cache_cmp_v[0], 1)
    assert past % SEL_BLOCK == 0 and past >= 4 * SEL_BLOCK and PAGE_SIZE == 2 * SEL_BLOCK
    total = past + dt
    nbc = past // CMP_BLOCK
    n_blocks = -(-total // SEL_BLOCK)
    cur = past // SEL_BLOCK
    n_sel = min(N_SEL_BLOCKS, n_blocks)
    col = lambda v: v.reshape(-1, 1)
    new = lambda a: a.reshape(db, 1, KV_W)
    cb_s = _bias_of(tbl_a, past - (jnp.arange(nbc) * CMP_BLOCK + CMP_BLOCK - 1)).T
    w_past = state_win_k.shape[2]
    wdist = w_past - jnp.arange(w_past)
    wb_s = jnp.where(wdist < WINDOW, _bias_of(tbl_a, wdist).T, NEG)
    rows2 = lambda a, n: a.reshape(a.shape[0], n * NSA_KV_GROUPS, HEAD_DIM)
    s_q8 = s_qa.reshape(db, NSA_HEADS, HEAD_DIM)
    so_cmp, so_win, selmask = _s_cmp_win(
        s_q8, s_kcomp, s_vcomp, cb_s, rows2(state_win_k[0], w_past), rows2(state_win_v[0], w_past),
        new(s_kw), new(s_vw), wb_s, col(tbl_a[0]), n_blocks, n_sel)
    kept = selmask[:, :, None, :n_blocks] > 0.5
    nth = jnp.cumsum(kept, axis=-1) == (jnp.arange(n_sel) + 1)[:, None]
    sel_idx = jnp.sum(jnp.where(kept & nth, jnp.arange(n_blocks), 0), axis=-1).astype(jnp.int32)
    sel_page = jnp.take_along_axis(page_table, jnp.minimum(sel_idx // 2, n_pages - 1).reshape(db, -1), axis=1)
    sel_hp = sel_page.reshape(sel_idx.shape) * 2 + sel_idx % 2
    n_near = 4
    ndist = (n_near - 1 - jnp.arange(n_near))[:, None] * SEL_BLOCK - jnp.arange(SEL_BLOCK)[None, :]
    tb_s = jnp.where(ndist[:, None, :] >= 0, jnp.moveaxis(_bias_of(tbl_a, ndist), -1, 1), NEG)
    half_pages = lambda c: c.reshape(n_pool * halves, SEL_BLOCK * NSA_KV_GROUPS, HEAD_DIM)
    so_sel2 = _s_sel(sel_idx.reshape(-1), sel_hp.reshape(-1).astype(jnp.int32), s_q8, new(s_ks), new(s_vs), tb_s,
                     col(tbl_a[N_BUCKETS - 1]), half_pages(cache_slc_k[0]), half_pages(cache_slc_v[0]), n_sel, cur)
    so_sel = jnp.concatenate([so_sel2[:, 0, :NSA_HPG], so_sel2[:, 1, NSA_HPG:]], axis=1)
    keep_s = _s_index(page_table, s_iq.reshape(db, IDX_HEADS, IDX_DIM),
                      s_misc[:, MISC_IW:MISC_IW + IDX_HEADS].reshape(db, IDX_HEADS, 1),
                      s_misc[:, :IDX_DIM].reshape(db, 1, IDX_DIM), cache_idx_k[0], min(DSA_TOPK, total // 4))
    span = math.gcd(n_pages, PAGE_GROUP) * PAGE_SIZE
    assert span >= MAX_DISTANCE
    bt_s = _bias_of(tbl_b, span - jnp.arange(span)).T
    pages2 = lambda c: c.reshape(n_pool, PAGE_SIZE * DSA_KV_GROUPS, HEAD_DIM)
    so_b = _s_dsa(page_table, s_qb.reshape(db, DSA_HEADS, HEAD_DIM), keep_s, bt_s, col(tbl_b[N_BUCKETS - 1]),
                  col(tbl_b[0]), new(s_kb), new(s_vb), pages2(cache_dsa_k[0]), pages2(cache_dsa_v[0]))
    n_win = min(WINDOW, total)
    g4 = lambda a: a.reshape(db, dt, NSA_KV_GROUPS, HEAD_DIM)
    s_wk = jnp.concatenate([state_win_k[0], g4(s_kw)], axis=1)[:, -n_win:]
    s_wv = jnp.concatenate([state_win_v[0], g4(s_vw)], axis=1)[:, -n_win:]
    y_s = _merge_up(xs, so_cmp.reshape(n_s, -1), so_sel.reshape(n_s, -1), so_win.reshape(n_s, -1),
                    so_b.reshape(n_s, -1), s_misc, w_ga32, w_gb32, w_ua32, w_ub32, n_s, 512)
    assert n_p % n_s == 0
    x1 = _merge_out(xs, y_s, w_o32, ln1_g, ln1_b, n_s, out_rows=n_p + n_s, into=x1)
    x1s = x1[n_p:]

    wr_t = w_router[0].T
    rb = router_bias[0].reshape(N_EXPERTS, 1)
    eidx_p, wts_p, pos_p, cnt_p = _router(x1, wr_t, rb, jnp.zeros((N_EXPERTS, 1), jnp.int32), tm_m, m=n_p)
    eidx_s, wts_s, pos_s, cnt = _router(x1s, wr_t, rb, cnt_p, n_s)
    n_tok = n_p + n_s
    eidx = jnp.concatenate([eidx_p[:TOP_K], eidx_s[:TOP_K]], axis=1)
    pos = jnp.concatenate([pos_p[:TOP_K], pos_s[:TOP_K]], axis=1)
    tr = EXPERT_ROWS
    n_asg = n_tok * TOP_K
    counts = cnt[:, 0]
    padded = (counts + tr - 1) // tr * tr
    pend = jnp.cumsum(padded)
    pad_start = pend - padded
    n_rows = -(-n_asg // tr) * tr + N_EXPERTS * tr
    n_blk = n_rows // tr
    dest = pos + jnp.sum(jnp.where(eidx[..., None] == jnp.arange(N_EXPERTS), pad_start, 0), axis=-1)
    blk_e = jnp.minimum(jnp.sum(pend[None, :] <= (jnp.arange(n_blk) * tr)[:, None], axis=1), N_EXPERTS - 1)
    n_valid = (pend[-1] // tr).astype(jnp.int32).reshape(1)
    row_tok = jnp.zeros((n_rows,), jnp.int32).at[dest.reshape(-1)].set(
        jnp.tile(jnp.arange(n_tok, dtype=jnp.int32), TOP_K), unique_indices=True)
    xg = x1[row_tok]
    out_rows = _experts(blk_e.astype(jnp.int32), n_valid, xg, moe_w_gate[0], moe_w_up[0], moe_w_down[0], tr)
    rows6 = out_rows[dest]

    sg, su, sd = sh_w_gate[0].astype(BF16), sh_w_up[0].astype(BF16), sh_w_down[0].astype(BF16)
    y_prompt = _ffn_out(x1, rows6, wts_p.T, sg, su, sd, ln2_g, ln2_b, tm_m, m=n_p).reshape(b, t, D_MODEL)
    y_sample = _ffn_out(x1s, rows6[:, n_p:], wts_s.T, sg, su, sd, ln2_g, ln2_b, n_s).reshape(db, dt, D_MODEL)

    n_win = min(WINDOW, t)
    st = lambda a: a.reshape(1, b, t, NSA_KV_GROUPS, HEAD_DIM)
    ss = lambda a: a.reshape(1, db, dt, NSA_KV_GROUPS, HEAD_DIM)
    return (y_prompt, y_sample,
            st(kc_h), st(vc_h), st(ks_h), st(vs_h), st(kw)[:, :, -n_win:], st(vw)[:, :, -n_win:], st(kb_h), st(vb_h),
            misc[:, :IDX_DIM].reshape(1, b, t, IDX_DIM),
            ss(s_kc), ss(s_vc), ss(s_ks), ss(s_vs), s_wk[None], s_wv[None], ss(s_kb), ss(s_vb),
            s_misc[:, :IDX_DIM].reshape(1, db, dt, IDX_DIM))
```

```python
import functools
import math

import jax
import jax.numpy as jnp
import numpy as np
from jax import lax
from jax.experimental import pallas as pl
from jax.experimental.pallas import tpu as pltpu

D_MODEL = 2048
PAGE_SIZE = 128
HEAD_DIM = 128
NSA_HEADS = 8
NSA_KV_GROUPS = 2
NSA_HPG = NSA_HEADS // NSA_KV_GROUPS
CMP_BLOCK = 64
CMP_HIDDEN = 128
SEL_BLOCK = 64
N_SEL_BLOCKS = 16
WINDOW = 512
DSA_HEADS = 8
DSA_KV_GROUPS = 2
DSA_HPG = DSA_HEADS // DSA_KV_GROUPS
IDX_HEADS = 8
IDX_DIM = 64
DSA_TOPK = 256
N_BUCKETS = 32
MAX_EXACT = 16
MAX_DISTANCE = 128
N_EXPERTS = 64
EXPERT_FF = 512
SHARED_FF = 512
TOP_K = 6
N_EXPERT_GROUPS = 8
TOPK_GROUPS = 4
ROUTED_SCALE = 2.5
LN_EPS = 1e-5
ATTN_SCALE = HEAD_DIM ** -0.5
DEPTH = 1
DEEPNORM_ALPHA = (2 * DEPTH) ** 0.25
IN_COLS = (NSA_HEADS * HEAD_DIM, 6 * NSA_KV_GROUPS * HEAD_DIM, 3 * NSA_HEADS,
           DSA_HEADS * HEAD_DIM, 2 * DSA_KV_GROUPS * HEAD_DIM,
           IDX_HEADS * IDX_DIM, IDX_DIM, IDX_HEADS, 2 * D_MODEL)

KV_W = NSA_KV_GROUPS * HEAD_DIM
MISC_W = 128
MISC_IW = IDX_DIM
MISC_GA = IDX_DIM + IDX_HEADS
NEG = -1e30
Q_TILE = 256
EXPERT_ROWS = 256
VMEM_LIMIT = 56 * 1024 * 1024

BF16 = jnp.bfloat16
F32 = jnp.float32


def _cparams(*sem):
    return pltpu.CompilerParams(dimension_semantics=sem, vmem_limit_bytes=VMEM_LIMIT)


def _dot(a, b):
    return jnp.dot(a, b, preferred_element_type=F32)


def _dot_nt(a, b):
    return lax.dot_general(a, b, (((1,), (1,)), ((), ())), preferred_element_type=F32)


def _dot_nt_full(a, b):
    return lax.dot_general(a.astype(F32), b.astype(F32), (((1,), (1,)), ((), ())),
                           preferred_element_type=F32, precision=lax.Precision.HIGHEST)


def _dot_nt_x3(a, b):
    ah, bh = a.astype(BF16), b.astype(BF16)
    al, bl = (a - ah.astype(F32)).astype(BF16), (b - bh.astype(F32)).astype(BF16)
    return _dot_nt(ah, bh) + (_dot_nt(ah, bl) + _dot_nt(al, bh))


def _mm(a, w):
    if w.dtype == F32:
        return jnp.dot(a.astype(F32), w, preferred_element_type=F32, precision=lax.Precision.HIGHEST)
    return jnp.dot(a.astype(w.dtype), w, preferred_element_type=F32)


def _proj_kernel(x_ref, w_ref, *o_refs, widths, by_head):
    acc = _mm(x_ref[...], w_ref[...])
    tm = acc.shape[0]
    off = 0
    for o_ref, wd in zip(o_refs, widths):
        o_ref[...] = acc[:, off:off + wd]
        off += wd
    for r_ref, i in zip(o_refs[len(widths):], by_head):
        off = sum(widths[:i])
        n_h = widths[i] // HEAD_DIM
        for h in range(n_h):
            r_ref[pl.ds(h, tm, stride=n_h), :] = acc[:, off + h * HEAD_DIM:off + (h + 1) * HEAD_DIM]


def _project(x, w, widths, tm, by_head=()):
    m, k = x.shape
    n = w.shape[1]
    assert sum(widths) == n and m % tm == 0
    heads = [widths[i] // HEAD_DIM for i in by_head]
    return pl.pallas_call(
        functools.partial(_proj_kernel, widths=widths, by_head=by_head),
        grid=(m // tm,),
        in_specs=[pl.BlockSpec((tm, k), lambda i: (i, 0)),
                  pl.BlockSpec((k, n), lambda i: (0, 0))],
        out_specs=[pl.BlockSpec((tm, wd), lambda i: (i, 0)) for wd in widths]
        + [pl.BlockSpec((tm * n_h, HEAD_DIM), lambda i: (i, 0)) for n_h in heads],
        out_shape=[jax.ShapeDtypeStruct((m, wd), F32) for wd in widths]
        + [jax.ShapeDtypeStruct((m * n_h, HEAD_DIM), F32) for n_h in heads],
        compiler_params=_cparams("parallel"),
        name="project",
    )(x, w)


CMP_JCHUNK = 8


def _compress_kernel(x_ref, pe_ref, w1_ref, w2_ref, o_ref, acc_ref):
    jc = pl.program_id(1)

    @pl.when(jc == 0)
    def _():
        acc_ref[...] = jnp.zeros_like(acc_ref)

    acc = acc_ref[...]
    for jj in range(CMP_JCHUNK):
        lhs = (x_ref[:, jj, :] + pe_ref[jj:jj + 1, :]).astype(BF16)
        acc = acc + _dot(lhs, w1_ref[jj])
    acc_ref[...] = acc

    @pl.when(jc == pl.num_programs(1) - 1)
    def _():
        h = jax.nn.gelu(acc_ref[...])
        o_ref[...] = _dot(h.astype(BF16), w2_ref[...])


def _compress(rows, pe2, w1big, w2big):
    r = rows.shape[0]
    tr = math.gcd(r, 1024)
    assert tr % 8 == 0
    return pl.pallas_call(
        _compress_kernel,
        grid=(r // tr, CMP_BLOCK // CMP_JCHUNK),
        in_specs=[pl.BlockSpec((tr, CMP_JCHUNK, KV_W), lambda i, j: (i, j, 0)),
                  pl.BlockSpec((CMP_JCHUNK, KV_W), lambda i, j: (j, 0)),
                  pl.BlockSpec((CMP_JCHUNK, KV_W, KV_W), lambda i, j: (j, 0, 0)),
                  pl.BlockSpec((KV_W, KV_W), lambda i, j: (0, 0))],
        out_specs=pl.BlockSpec((tr, KV_W), lambda i, j: (i, 0)),
        out_shape=jax.ShapeDtypeStruct((r, KV_W), F32),
        scratch_shapes=[pltpu.VMEM((tr, KV_W), F32)],
        compiler_params=_cparams("parallel", "arbitrary"),
        name="compress",
    )(rows, pe2, w1big, w2big)


def _rows_at(x_ref, r):
    n, rows, w = x_ref.shape
    return x_ref.reshape(n * rows, w)[pl.ds(r, n, stride=rows), :]


def _compress_pool_kernel(x_ref, pe_ref, w1_ref, w2_ref, o_ref, acc_ref, *, tr):
    jc = pl.program_id(1)

    @pl.when(jc == 0)
    def _():
        acc_ref[...] = jnp.zeros_like(acc_ref)

    for g in range(NSA_KV_GROUPS):
        acc = acc_ref[g]
        for jp in range(CMP_JCHUNK // 2):
            parts = []
            for u in range(2):
                jj = 2 * jp + u
                xs = _rows_at(x_ref, 2 * jj + g) + pe_ref[jj:jj + 1, :]
                parts.append(xs.astype(BF16))
            acc = acc + _dot(jnp.concatenate(parts, axis=-1), w1_ref[jp])
        acc_ref[g] = acc

    @pl.when(jc == pl.num_programs(1) - 1)
    def _():
        for g in range(NSA_KV_GROUPS):
            h = jax.nn.gelu(acc_ref[g])
            o_ref[:, g * HEAD_DIM:(g + 1) * HEAD_DIM] = _dot(h.astype(BF16), w2_ref[...])


def _compress_pool(cache, pe, w1, w2):
    n_pool = cache.shape[0]
    r = n_pool * (PAGE_SIZE // CMP_BLOCK)
    rows = cache.reshape(r, CMP_BLOCK * NSA_KV_GROUPS, HEAD_DIM)
    tr = math.gcd(r, 1024)
    assert tr % 8 == 0
    rows_per = 2 * CMP_JCHUNK
    w1p = w1.reshape(CMP_BLOCK // 2, 2 * HEAD_DIM, CMP_HIDDEN).astype(BF16)
    return pl.pallas_call(
        functools.partial(_compress_pool_kernel, tr=tr),
        grid=(r // tr, CMP_BLOCK // CMP_JCHUNK),
        in_specs=[pl.BlockSpec((tr, rows_per, HEAD_DIM), lambda i, j: (i, j, 0)),
                  pl.BlockSpec((CMP_JCHUNK, HEAD_DIM), lambda i, j: (j, 0)),
                  pl.BlockSpec((CMP_JCHUNK // 2, 2 * HEAD_DIM, CMP_HIDDEN), lambda i, j: (j, 0, 0)),
                  pl.BlockSpec((CMP_HIDDEN, HEAD_DIM), lambda i, j: (0, 0))],
        out_specs=pl.BlockSpec((tr, KV_W), lambda i, j: (i, 0)),
        out_shape=jax.ShapeDtypeStruct((r, KV_W), F32),
        scratch_shapes=[pltpu.VMEM((NSA_KV_GROUPS, tr, CMP_HIDDEN), F32)],
        compiler_params=_cparams("parallel", "arbitrary"),
        name="compress_pool",
    )(rows, pe, w1p, w2.astype(BF16))


def _compress_weights(pe, w1, w2):
    pe2 = jnp.concatenate([pe, pe], axis=-1)
    z1 = jnp.zeros_like(w1)
    w1big = jnp.concatenate([jnp.concatenate([w1, z1], axis=2), jnp.concatenate([z1, w1], axis=2)], axis=1)
    z2 = jnp.zeros_like(w2)
    w2big = jnp.concatenate([jnp.concatenate([w2, z2], axis=1), jnp.concatenate([z2, w2], axis=1)], axis=0)
    return pe2, w1big.astype(BF16), w2big.astype(BF16)


def _dot_tn(a, b):
    return lax.dot_general(a, b, (((0,), (0,)), ((), ())), preferred_element_type=F32)


def _cmp_select_kernel(q_ref, kc_ref, vc_ref, cb_ref, o_ref, sel_ref, *, tq, nb, n_sel):
    qi = pl.program_id(1)
    t = qi * tq + lax.broadcasted_iota(jnp.int32, (nb, tq), 1)
    j = lax.broadcasted_iota(jnp.int32, (nb, tq), 0)
    cur = t // SEL_BLOCK
    for g in range(NSA_KV_GROUPS):
        kc = kc_ref[0, :, g * HEAD_DIM:(g + 1) * HEAD_DIM].astype(BF16)
        vc = vc_ref[0, :, g * HEAD_DIM:(g + 1) * HEAD_DIM].astype(BF16)
        imp = jnp.zeros((nb, tq), F32)
        for h in range(NSA_HPG):
            hh = g * NSA_HPG + h
            q = q_ref[0, :, hh * HEAD_DIM:(hh + 1) * HEAD_DIM].astype(BF16)
            cb = cb_ref[hh]
            valid = cb > 0.5 * NEG
            s = _dot_nt(kc, q) * ATTN_SCALE + cb
            m = jnp.max(s, axis=0, keepdims=True)
            e = jnp.where(valid, jnp.exp(s - m), 0.0)
            den = jnp.sum(e, axis=0, keepdims=True)
            p = e / jnp.where(den > 0, den, 1.0)
            o_ref[0, :, hh * HEAD_DIM:(hh + 1) * HEAD_DIM] = _dot_tn(p.astype(BF16), vc)
            imp = imp + p
        forced = (j == 0) | (j == cur) | (j == cur - 1)
        score = jnp.where(forced, jnp.inf, jnp.where(j <= cur, imp, -jnp.inf))
        rank = jnp.zeros((nb, tq), jnp.int32)
        for k in range(nb):
            row = score[k:k + 1, :]
            rank = rank + jnp.where(row > score, 1, 0) + jnp.where(row == score, jnp.where(k < j, 1, 0), 0)
        sel_ref[0, g] = jnp.where(rank < n_sel, 1.0, 0.0).astype(F32)


def _cmp_select(qa, kcomp, vcomp, cbias, tq):
    b, t, _ = qa.shape
    nb = kcomp.shape[1]
    n_sel = min(N_SEL_BLOCKS, nb)
    return pl.pallas_call(
        functools.partial(_cmp_select_kernel, tq=tq, nb=nb, n_sel=n_sel),
        grid=(b, t // tq),
        in_specs=[pl.BlockSpec((1, tq, NSA_HEADS * HEAD_DIM), lambda bi, qi: (bi, qi, 0)),
                  pl.BlockSpec((1, nb, KV_W), lambda bi, qi: (bi, 0, 0)),
                  pl.BlockSpec((1, nb, KV_W), lambda bi, qi: (bi, 0, 0)),
                  pl.BlockSpec((NSA_HEADS, nb, tq), lambda bi, qi: (0, 0, qi))],
        out_specs=[pl.BlockSpec((1, tq, NSA_HEADS * HEAD_DIM), lambda bi, qi: (bi, qi, 0)),
                   pl.BlockSpec((1, NSA_KV_GROUPS, nb, tq), lambda bi, qi: (bi, 0, 0, qi))],
        out_shape=[jax.ShapeDtypeStruct((b, t, NSA_HEADS * HEAD_DIM), F32),
                   jax.ShapeDtypeStruct((b, NSA_KV_GROUPS, nb, t), F32)],
        compiler_params=_cparams("parallel", "parallel"),
        name="cmp_select",
    )(qa, kcomp, vcomp, cbias)


LOG2E = math.log2(math.e)


def _dense_attn_kernel(q_ref, k_ref, v_ref, mask_ref, expand_ref, d_ref, o_ref, *, tq, n_tiles, hpg, mode):
    qi = pl.program_id(2)

    def attend(n_ch):
        chunk = lambda c: slice(c * tq, (c + 1) * tq)
        if mode == "sel":
            mbs = [jnp.where(_dot_tn(mask_ref[0, 0].astype(BF16), expand_ref[:, chunk(c)]) > 0.5, 0.0, NEG)
                   for c in range(n_ch)]
        else:
            mbs = [jnp.where(mask_ref[0, :, chunk(c)].astype(F32) > 0.5, 0.0, NEG) for c in range(n_ch)]
        ks = [k_ref[0, chunk(c), :].astype(BF16) for c in range(n_ch)]
        vs = [v_ref[0, chunk(c), :].astype(BF16) for c in range(n_ch)]
        for hh in range(hpg):
            q = (q_ref[0, :, hh * HEAD_DIM:(hh + 1) * HEAD_DIM] * (ATTN_SCALE * LOG2E)).astype(BF16)
            ss = []
            for c in range(n_ch):
                s = _dot_nt(q, ks[c]) + mbs[c]
                if n_ch - 1 - c < 2:
                    s = s + d_ref[hh, n_ch - 1 - c]
                ss.append(s)
            mx = ss[0]
            for s in ss[1:]:
                mx = jnp.maximum(mx, s)
            m = jnp.max(mx, axis=-1, keepdims=True)
            es = [jnp.exp2(s - m) for s in ss]
            tot = es[0]
            for e in es[1:]:
                tot = tot + e
            den = jnp.sum(tot, axis=-1, keepdims=True)
            o = _dot(es[0].astype(BF16), vs[0])
            for c in range(1, n_ch):
                o = o + _dot(es[c].astype(BF16), vs[c])
            o_ref[0, :, hh * HEAD_DIM:(hh + 1) * HEAD_DIM] = jnp.where(m > 0.5 * NEG, o / den, 0.0)

    for tile in range(n_tiles):
        @pl.when(qi == tile)
        def _(tile=tile):
            attend(tile + 1)


def _dense_attn(q, k, v, mask, expand, dtiles, tq, mode):
    b, t, qw = q.shape
    n_heads = qw // HEAD_DIM
    hpg = n_heads // (k.shape[2] // HEAD_DIM)
    n_groups = n_heads // hpg
    gw = hpg * HEAD_DIM
    if mode == "sel":
        nb = mask.shape[2]
        mask_spec = pl.BlockSpec((1, 1, nb, tq), lambda bi, g, qi: (bi, g, 0, qi))
    else:
        mask_spec = pl.BlockSpec((1, tq, t), lambda bi, g, qi: (bi, qi, 0))
    return pl.pallas_call(
        functools.partial(_dense_attn_kernel, tq=tq, n_tiles=t // tq, hpg=hpg, mode=mode),
        grid=(b, n_groups, t // tq),
        in_specs=[pl.BlockSpec((1, tq, gw), lambda bi, g, qi: (bi, qi, g)),
                  pl.BlockSpec((1, t, HEAD_DIM), lambda bi, g, qi: (bi, 0, g)),
                  pl.BlockSpec((1, t, HEAD_DIM), lambda bi, g, qi: (bi, 0, g)),
                  mask_spec,
                  pl.BlockSpec(expand.shape, lambda bi, g, qi: (0, 0)),
                  pl.BlockSpec((hpg,) + dtiles.shape[1:], lambda bi, g, qi: (g, 0, 0, 0))],
        out_specs=pl.BlockSpec((1, tq, gw), lambda bi, g, qi: (bi, qi, g)),
        out_shape=jax.ShapeDtypeStruct((b, t, qw), F32),
        compiler_params=_cparams("parallel", "parallel", "parallel"),
        name="dense_attn_" + mode,
    )(q, k, v, mask, expand, dtiles)


def _window_attn_kernel(q_ref, k_ref, v_ref, w_ref, o_ref, *, tq, n_chunks, n_heads, hpg):
    qi = pl.program_id(1)
    starts, pens = [], []
    for r in range(n_chunks):
        cj = qi - (n_chunks - 1) + r
        starts.append(pl.multiple_of(jnp.maximum(cj, 0) * tq, tq))
        pens.append(jnp.where(cj < 0, NEG, 0.0).astype(F32))
    for hh in range(n_heads):
        g = hh // hpg
        q = (q_ref[0, :, hh * HEAD_DIM:(hh + 1) * HEAD_DIM] * (ATTN_SCALE * LOG2E)).astype(BF16)
        ss = []
        for r in range(n_chunks):
            k = k_ref[0, pl.ds(starts[r], tq), g * HEAD_DIM:(g + 1) * HEAD_DIM].astype(BF16)
            ss.append(_dot_nt(q, k) + (w_ref[hh, n_chunks - 1 - r] + pens[r]))
        mx = ss[0]
        for s in ss[1:]:
            mx = jnp.maximum(mx, s)
        m = jnp.max(mx, axis=-1, keepdims=True)
        es = [jnp.exp2(s - m) for s in ss]
        tot = es[0]
        for e in es[1:]:
            tot = tot + e
        den = jnp.sum(tot, axis=-1, keepdims=True)
        o = jnp.zeros((tq, HEAD_DIM), F32)
        for r in range(n_chunks):
            v = v_ref[0, pl.ds(starts[r], tq), g * HEAD_DIM:(g + 1) * HEAD_DIM].astype(BF16)
            o = o + _dot(es[r].astype(BF16), v)
        o_ref[0, :, hh * HEAD_DIM:(hh + 1) * HEAD_DIM] = jnp.where(m > 0.5 * NEG, o / den, 0.0)


def _window_attn(q, k, v, wtiles, tq):
    b, t, qw = q.shape
    n_heads = qw // HEAD_DIM
    hpg = n_heads // (k.shape[2] // HEAD_DIM)
    n_chunks = wtiles.shape[1]
    return pl.pallas_call(
        functools.partial(_window_attn_kernel, tq=tq, n_chunks=n_chunks, n_heads=n_heads, hpg=hpg),
        grid=(b, t // tq),
        in_specs=[pl.BlockSpec((1, tq, qw), lambda bi, qi: (bi, qi, 0)),
                  pl.BlockSpec((1, t, k.shape[2]), lambda bi, qi: (bi, 0, 0)),
                  pl.BlockSpec((1, t, v.shape[2]), lambda bi, qi: (bi, 0, 0)),
                  pl.BlockSpec(wtiles.shape, lambda bi, qi: (0, 0, 0, 0))],
        out_specs=pl.BlockSpec((1, tq, qw), lambda bi, qi: (bi, qi, 0)),
        out_shape=jax.ShapeDtypeStruct((b, t, qw), F32),
        compiler_params=_cparams("parallel", "parallel"),
        name="window_attn",
    )(q, k, v, wtiles)


INT_MIN = -2 ** 31


def _topk_mask(key_ref, n_keep, tq, s_len):
    def body(i, thr_u):
        cand_u = thr_u | jnp.left_shift(jnp.int32(1), 31 - i)
        below = (cand_u ^ INT_MIN) - 1
        cnt = jnp.sum(jnp.where(key_ref[...] > below, 1, 0), axis=-1, keepdims=True)
        return jnp.where(cnt >= n_keep, cand_u, thr_u)

    thr_u = lax.fori_loop(0, 32, body, jnp.zeros((tq, 1), jnp.int32))
    thr = thr_u ^ INT_MIN
    return thr


def _index_select_kernel(iq_ref, wt_ref, mk_ref, o_ref, key_ref, *, tq, s_len, n_keep):
    qi = pl.program_id(1)
    wts = wt_ref[0] * IDX_DIM ** -0.5
    iqs = [iq_ref[0, :, h * IDX_DIM:(h + 1) * IDX_DIM].astype(BF16) for h in range(IDX_HEADS)]
    kidx = lax.broadcasted_iota(jnp.int32, (tq, tq), 0)
    t = qi * tq + lax.broadcasted_iota(jnp.int32, (tq, tq), 1)
    chunk = lambda c: slice(c * tq, (c + 1) * tq)

    def select(n_ch):
        s_b = n_ch * tq
        for c in range(n_ch):
            ik = mk_ref[0, chunk(c), 0:IDX_DIM].astype(BF16)
            score = jnp.zeros((tq, tq), F32)
            for h in range(IDX_HEADS):
                score = score + jnp.maximum(_dot_nt(ik, iqs[h]), 0.0) * wts[h:h + 1, :]
            score = score * IDX_HEADS ** -0.5 + 0.0
            score = jnp.where(c * tq + kidx <= t, score, -jnp.inf)
            bits = pltpu.bitcast(score, jnp.int32)
            key_ref[chunk(c), :] = jnp.where(bits < 0, bits ^ 0x7FFFFFFF, bits)

        def bit_step(i, thr_u):
            cand_u = thr_u | jnp.left_shift(jnp.int32(1), 31 - i)
            below = (cand_u ^ INT_MIN) - 1
            cnt = jnp.sum(jnp.where(key_ref[:s_b, :] > below, 1, 0), axis=0, keepdims=True)
            return jnp.where(cnt >= n_keep, cand_u, thr_u)

        thr = lax.fori_loop(0, 32, bit_step, jnp.zeros((1, tq), jnp.int32)) ^ INT_MIN
        key = key_ref[:s_b, :]
        gt = key > thr
        eq = key == thr
        need = n_keep - jnp.sum(jnp.where(gt, 1, 0), axis=0, keepdims=True)
        n_eq = jnp.sum(jnp.where(eq, 1, 0), axis=0, keepdims=True)
        keep = jnp.where(gt | eq, 1.0, 0.0)
        for c in range(n_ch):
            o_ref[0, :, chunk(c)] = keep[chunk(c), :].T.astype(o_ref.dtype)
        if s_b < s_len:
            o_ref[0, :, s_b:] = jnp.zeros((tq, s_len - s_b), o_ref.dtype)

        @pl.when(jnp.max(n_eq - need) > 0)
        def _():
            r_i = lax.broadcasted_iota(jnp.int32, (128, 128), 0)
            c_i = lax.broadcasted_iota(jnp.int32, (128, 128), 1)
            tri = jnp.where(c_i <= r_i, 1.0, 0.0).astype(BF16)
            before = jnp.zeros((1, tq), F32)
            need_f = need.astype(F32)
            for c in range(n_ch):
                parts = []
                for u in range(tq // 128):
                    sl = slice(c * tq + u * 128, c * tq + (u + 1) * 128)
                    eq_c = eq[sl, :]
                    eq_f = jnp.where(eq_c, 1.0, 0.0)
                    pref = _dot(tri, eq_f.astype(BF16)) + before
                    parts.append(jnp.where(gt[sl, :] | (eq_c & (pref <= need_f)), 1.0, 0.0))
                    before = before + jnp.sum(eq_f, axis=0, keepdims=True)
                o_ref[0, :, chunk(c)] = jnp.concatenate(parts, axis=0).T.astype(o_ref.dtype)

    for tile in range(s_len // tq):
        @pl.when(qi == tile)
        def _(tile=tile):
            select(tile + 1)


def _index_select(iq, iw_t, misc, tq, n_keep):
    b, t, _ = iq.shape
    return pl.pallas_call(
        functools.partial(_index_select_kernel, tq=tq, s_len=t, n_keep=n_keep),
        grid=(b, t // tq),
        in_specs=[pl.BlockSpec((1, tq, IDX_HEADS * IDX_DIM), lambda bi, qi: (bi, qi, 0)),
                  pl.BlockSpec((1, IDX_HEADS, tq), lambda bi, qi: (bi, 0, qi)),
                  pl.BlockSpec((1, t, MISC_W), lambda bi, qi: (bi, 0, 0))],
        out_specs=pl.BlockSpec((1, tq, t), lambda bi, qi: (bi, qi, 0)),
        out_shape=jax.ShapeDtypeStruct((b, t, t), BF16),
        scratch_shapes=[pltpu.VMEM((t, tq), jnp.int32)],
        compiler_params=_cparams("parallel", "parallel"),
        name="index_select",
    )(iq, iw_t, misc)


def _merge_up_kernel(x_ref, oc_ref, os_ref, ow_ref, ob_ref, misc_ref, wga_ref, wgb_ref, wua_ref, wub_ref,
                     y_ref, oa_ref):
    ga = jax.nn.sigmoid(misc_ref[:, MISC_GA:MISC_GA + 3 * NSA_HEADS])
    for hh in range(NSA_HEADS):
        sl = slice(hh * HEAD_DIM, (hh + 1) * HEAD_DIM)
        oa = (ga[:, hh:hh + 1] * oc_ref[:, sl] + ga[:, NSA_HEADS + hh:NSA_HEADS + hh + 1] * os_ref[:, sl]
              + ga[:, 2 * NSA_HEADS + hh:2 * NSA_HEADS + hh + 1] * ow_ref[:, sl])
        oa_ref[:, sl] = oa.astype(oa_ref.dtype)
    x = x_ref[...]
    ya = _mm(oa_ref[...], wua_ref[...])
    yb = _mm(ob_ref[...], wub_ref[...])
    g_a = jax.nn.sigmoid(_mm(x, wga_ref[...]))
    g_b = jax.nn.sigmoid(_mm(x, wgb_ref[...]))
    y_ref[...] = (g_a * ya + g_b * yb).astype(y_ref.dtype)


def _merge_up(xb, o_cmp, o_sel, o_win, o_b, misc, wga, wgb, wua, wub, tm, tn):
    m = xb.shape[0]
    mx = wua.dtype
    aw = NSA_HEADS * HEAD_DIM
    bw = DSA_HEADS * HEAD_DIM
    row = lambda w: pl.BlockSpec((tm, w), lambda j, i: (i, 0))
    wcol = lambda k: pl.BlockSpec((k, tn), lambda j, i: (0, j))
    return pl.pallas_call(
        _merge_up_kernel,
        grid=(D_MODEL // tn, m // tm),
        in_specs=[row(D_MODEL), row(aw), row(aw), row(aw), row(bw), row(MISC_W),
                  wcol(D_MODEL), wcol(D_MODEL), wcol(aw), wcol(bw)],
        out_specs=pl.BlockSpec((tm, tn), lambda j, i: (i, j)),
        out_shape=jax.ShapeDtypeStruct((m, D_MODEL), mx),
        scratch_shapes=[pltpu.VMEM((tm, aw), mx)],
        compiler_params=_cparams("parallel", "parallel"),
        name="merge_up",
    )(xb, o_cmp, o_sel, o_win, o_b, misc, wga, wgb, wua, wub)


def _layer_norm(z, g, b):
    mu = jnp.mean(z, axis=-1, keepdims=True)
    zc = z - mu
    var = jnp.mean(zc * zc, axis=-1, keepdims=True)
    return zc * lax.rsqrt(var + LN_EPS) * g + b


def _merge_out_kernel(x_ref, y_ref, wo_ref, g_ref, b_ref, *rest):
    o_ref = rest[-1]
    z = DEEPNORM_ALPHA * x_ref[...] + _mm(y_ref[...], wo_ref[...])
    o_ref[...] = _layer_norm(z, g_ref[...], b_ref[...])


def _merge_out(x, y, wo, g, b, tm, out_rows=None, into=None):
    m = x.shape[0]
    out_rows = out_rows or m
    row = lambda: pl.BlockSpec((tm, D_MODEL), lambda i: (i, 0))
    args = [x, y, wo, g, b]
    in_specs = [row(), row(), pl.BlockSpec((D_MODEL, D_MODEL), lambda i: (0, 0)),
                pl.BlockSpec((1, D_MODEL), lambda i: (0, 0)), pl.BlockSpec((1, D_MODEL), lambda i: (0, 0))]
    first = 0
    aliases = {}
    if into is not None:
        assert into.shape == (out_rows, D_MODEL) and (out_rows - m) % tm == 0
        first = (out_rows - m) // tm
        args.append(into)
        in_specs.append(pl.BlockSpec(memory_space=pl.ANY))
        aliases = {len(args) - 1: 0}
    return pl.pallas_call(
        _merge_out_kernel,
        grid=(m // tm,),
        in_specs=in_specs,
        out_specs=pl.BlockSpec((tm, D_MODEL), lambda i: (i + first, 0)),
        out_shape=jax.ShapeDtypeStruct((out_rows, D_MODEL), F32),
        input_output_aliases=aliases,
        compiler_params=_cparams("parallel"),
        name="merge_out",
    )(*args)


def _router_kernel(x_ref, wr_ref, rb_ref, c0_ref, idx_ref, wt_ref, pos_ref, cnt_ref, run_ref, *, tm):
    epg = N_EXPERTS // N_EXPERT_GROUPS
    logits = lax.dot_general(wr_ref[...], x_ref[...], (((1,), (1,)), ((), ())),
                             preferred_element_type=F32, precision=lax.Precision.HIGHEST)
    scores = jax.nn.sigmoid(logits)
    biased = scores + rb_ref[...]
    sub = lax.broadcasted_iota(jnp.int32, (epg, tm), 0)
    gs_rows = []
    for r in range(N_EXPERT_GROUPS):
        bg = biased[r * epg:(r + 1) * epg, :]
        m1 = jnp.max(bg, axis=0, keepdims=True)
        i1 = jnp.min(jnp.where(bg == m1, sub, epg), axis=0, keepdims=True)
        m2 = jnp.max(jnp.where(sub == i1, -jnp.inf, bg), axis=0, keepdims=True)
        gs_rows.append(m1 + m2)
    gs = jnp.concatenate(gs_rows, axis=0)
    grow = lax.broadcasted_iota(jnp.int32, (N_EXPERT_GROUPS, tm), 0)
    rank = jnp.zeros((N_EXPERT_GROUPS, tm), jnp.int32)
    for k in range(N_EXPERT_GROUPS):
        rk = gs[k:k + 1, :]
        rank = rank + jnp.where((rk > gs) | ((rk == gs) & (k < grow)), 1, 0)
    gkeep = rank < TOPK_GROUPS
    masked = jnp.concatenate(
        [jnp.where(gkeep[r:r + 1, :], biased[r * epg:(r + 1) * epg, :], -jnp.inf) for r in range(N_EXPERT_GROUPS)],
        axis=0)
    erow = lax.broadcasted_iota(jnp.int32, (N_EXPERTS, tm), 0)
    idx_rows, w_rows, hits = [], [], []
    for _ in range(TOP_K):
        m = jnp.max(masked, axis=0, keepdims=True)
        ix = jnp.min(jnp.where(masked == m, erow, N_EXPERTS), axis=0, keepdims=True)
        hit = erow == ix
        w_rows.append(jnp.sum(jnp.where(hit, scores, 0.0), axis=0, keepdims=True))
        idx_rows.append(ix)
        hits.append(hit)
        masked = jnp.where(hit, -jnp.inf, masked)
    wsum = w_rows[0]
    for w in w_rows[1:]:
        wsum = wsum + w
    pad = 8 - TOP_K
    idx_ref[...] = jnp.concatenate(idx_rows + [jnp.zeros((pad, tm), jnp.int32)], axis=0)
    wt_ref[...] = jnp.concatenate([w / wsum * ROUTED_SCALE for w in w_rows] + [jnp.zeros((pad, tm), F32)], axis=0)

    @pl.when(pl.program_id(0) == 0)
    def _():
        run_ref[...] = c0_ref[...]

    earlier = (lax.broadcasted_iota(jnp.int32, (tm, tm), 0) < lax.broadcasted_iota(jnp.int32, (tm, tm), 1))
    earlier = jnp.where(earlier, 1.0, 0.0).astype(BF16)
    run = run_ref[...]
    pos_rows = []
    for hit in hits:
        onehot = jnp.where(hit, 1.0, 0.0)
        before = _dot(onehot.astype(BF16), earlier).astype(jnp.int32)
        pos_rows.append(jnp.sum(jnp.where(hit, run + before, 0), axis=0, keepdims=True))
        run = run + jnp.sum(onehot, axis=1, keepdims=True).astype(jnp.int32)
    run_ref[...] = run
    cnt_ref[...] = run
    pos_ref[...] = jnp.concatenate(pos_rows + [jnp.zeros((pad, tm), jnp.int32)], axis=0)


def _router(x1, wr_t, rb, counts0, tm, m=None):
    m = m or x1.shape[0]
    tok = lambda: pl.BlockSpec((8, tm), lambda i: (0, i))
    return pl.pallas_call(
        functools.partial(_router_kernel, tm=tm),
        grid=(m // tm,),
        in_specs=[pl.BlockSpec((tm, D_MODEL), lambda i: (i, 0)),
                  pl.BlockSpec((N_EXPERTS, D_MODEL), lambda i: (0, 0)),
                  pl.BlockSpec((N_EXPERTS, 1), lambda i: (0, 0)),
                  pl.BlockSpec((N_EXPERTS, 1), lambda i: (0, 0))],
        out_specs=[tok(), tok(), tok(), pl.BlockSpec((N_EXPERTS, 1), lambda i: (0, 0))],
        out_shape=[jax.ShapeDtypeStruct((8, m), jnp.int32), jax.ShapeDtypeStruct((8, m), F32),
                   jax.ShapeDtypeStruct((8, m), jnp.int32), jax.ShapeDtypeStruct((N_EXPERTS, 1), jnp.int32)],
        scratch_shapes=[pltpu.VMEM((N_EXPERTS, 1), jnp.int32)],
        compiler_params=_cparams("arbitrary"),
        name="router",
    )(x1, wr_t, rb, counts0)


def _expert_kernel(be_ref, nv_ref, x_ref, wg_ref, wu_ref, wd_ref, o_ref, wgb_ref, wub_ref, wdb_ref):
    i = pl.program_id(0)
    valid = i < nv_ref[0]
    new_expert = (i == 0) | (be_ref[i] != be_ref[jnp.maximum(i - 1, 0)])

    @pl.when(valid & new_expert)
    def _():
        wgb_ref[...] = wg_ref[0].astype(BF16)
        wub_ref[...] = wu_ref[0].astype(BF16)
        wdb_ref[...] = wd_ref[0].astype(BF16)

    @pl.when(valid)
    def _():
        x = x_ref[...].astype(BF16)
        h = jax.nn.silu(_dot(x, wgb_ref[...])) * _dot(x, wub_ref[...])
        o_ref[...] = _dot(h.astype(BF16), wdb_ref[...]).astype(o_ref.dtype)

    @pl.when(jnp.logical_not(valid))
    def _():
        o_ref[...] = jnp.zeros_like(o_ref)


def _experts(blk_e, n_valid, xg, wg, wu, wd, tr):
    n_rows = xg.shape[0]
    n_blk = n_rows // tr
    grid_spec = pltpu.PrefetchScalarGridSpec(
        num_scalar_prefetch=2,
        grid=(n_blk,),
        in_specs=[pl.BlockSpec((tr, D_MODEL), lambda i, be, nv: (i, 0)),
                  pl.BlockSpec((1, D_MODEL, EXPERT_FF), lambda i, be, nv: (be[i], 0, 0)),
                  pl.BlockSpec((1, D_MODEL, EXPERT_FF), lambda i, be, nv: (be[i], 0, 0)),
                  pl.BlockSpec((1, EXPERT_FF, D_MODEL), lambda i, be, nv: (be[i], 0, 0))],
        out_specs=pl.BlockSpec((tr, D_MODEL), lambda i, be, nv: (i, 0)),
        scratch_shapes=[pltpu.VMEM((D_MODEL, EXPERT_FF), BF16), pltpu.VMEM((D_MODEL, EXPERT_FF), BF16),
                        pltpu.VMEM((EXPERT_FF, D_MODEL), BF16)],
    )
    return pl.pallas_call(
        _expert_kernel,
        grid_spec=grid_spec,
        out_shape=jax.ShapeDtypeStruct((n_rows, D_MODEL), BF16),
        compiler_params=_cparams("arbitrary"),
        name="experts",
    )(blk_e, n_valid, xg, wg, wu, wd)


def _ffn_out_kernel(x_ref, r_ref, w_ref, sg_ref, su_ref, sd_ref, g_ref, b_ref, o_ref):
    x = x_ref[...]
    xb = x.astype(BF16)
    h = jax.nn.silu(_dot(xb, sg_ref[...])) * _dot(xb, su_ref[...])
    f = _dot(h.astype(BF16), sd_ref[...])
    for k in range(TOP_K):
        f = f + r_ref[k].astype(F32) * w_ref[:, k:k + 1]
    o_ref[...] = _layer_norm(DEEPNORM_ALPHA * x + f, g_ref[...], b_ref[...])


def _ffn_out(x1, rows6, wts, sg, su, sd, g, b, tm, m=None):
    m = m or x1.shape[0]
    row = lambda: pl.BlockSpec((tm, D_MODEL), lambda i: (i, 0))
    full = lambda s: pl.BlockSpec(s, lambda i: (0, 0))
    return pl.pallas_call(
        _ffn_out_kernel,
        grid=(m // tm,),
        in_specs=[row(), pl.BlockSpec((TOP_K, tm, D_MODEL), lambda i: (0, i, 0)), pl.BlockSpec((tm, 8), lambda i: (i, 0)),
                  full((D_MODEL, SHARED_FF)), full((D_MODEL, SHARED_FF)), full((SHARED_FF, D_MODEL)),
                  full((1, D_MODEL)), full((1, D_MODEL))],
        out_specs=row(),
        out_shape=jax.ShapeDtypeStruct((m, D_MODEL), F32),
        compiler_params=_cparams("parallel"),
        name="ffn_out",
    )(x1, rows6, wts, sg, su, sd, g, b)


def _rel_bucket(dist):
    n = jnp.maximum(dist, 0)
    nf = jnp.maximum(n, 1).astype(F32)
    large = MAX_EXACT + (jnp.log(nf / MAX_EXACT) / math.log(MAX_DISTANCE / MAX_EXACT)
                         * (N_BUCKETS - MAX_EXACT)).astype(jnp.int32)
    return jnp.where(n < MAX_EXACT, n, jnp.minimum(large, N_BUCKETS - 1))


def _bias_of(tbl, dist):
    onehot = (_rel_bucket(dist)[..., None] == jnp.arange(N_BUCKETS)).astype(F32)
    return jnp.einsum('...k,kh->...h', onehot, tbl, precision=lax.Precision.HIGHEST)


def _toeplitz_bias(tbl, tq, n_tiles):
    i = jnp.arange(tq)[:, None]
    j = jnp.arange(tq)[None, :]
    dist = jnp.arange(n_tiles)[:, None, None] * tq + (i - j)[None]
    bias = jnp.moveaxis(_bias_of(tbl, dist), -1, 0)
    return bias, dist


PAGE_GROUP = 16
IDX_PAGE_GROUP = 64


def _by_group(hpg, a0, a1):
    row = lax.broadcasted_iota(jnp.int32, (2 * hpg, 1), 0)
    return jnp.where(row < hpg, a0, a1)


def _order_key(x):
    bits = pltpu.bitcast(x, jnp.int32)
    return jnp.where(bits < 0, bits ^ 0x7FFFFFFF, bits)


def _row_topk(key_ref, keep_ref, n_keep, width):
    thr = _topk_mask(key_ref, n_keep, 1, width)
    key = key_ref[...]
    gt = key > thr
    eq = key == thr
    need = n_keep - jnp.sum(jnp.where(gt, 1, 0), axis=-1, keepdims=True)
    n_eq = jnp.sum(jnp.where(eq, 1, 0), axis=-1, keepdims=True)
    keep_ref[...] = jnp.where(gt | eq, 1.0, 0.0)

    @pl.when(jnp.max(n_eq - need) > 0)
    def _():
        r_i = lax.broadcasted_iota(jnp.int32, (128, 128), 0)
        c_i = lax.broadcasted_iota(jnp.int32, (128, 128), 1)
        tri = jnp.where(r_i <= c_i, 1.0, 0.0).astype(BF16)
        need_f = need.astype(F32)

        def body(c, before):
            off = pl.multiple_of(c * 128, 128)
            kc = key_ref[:, pl.ds(off, 128)]
            eq_c = kc == thr
            eq_f = jnp.where(eq_c, 1.0, 0.0)
            pref = _dot(jnp.broadcast_to(eq_f, (8, 128)).astype(BF16), tri)[0:1] + before
            keep_ref[:, pl.ds(off, 128)] = jnp.where((kc > thr) | (eq_c & (pref <= need_f)), 1.0, 0.0)
            return before + jnp.sum(eq_f, axis=-1, keepdims=True)

        lax.fori_loop(0, width // 128, body, jnp.zeros((1, 1), F32))


def _s_cmp_win_kernel(q_ref, kc_ref, vc_ref, cb_ref, wk_ref, wv_ref, nk_ref, nv_ref, wb_ref, b0_ref,
                      ocmp_ref, owin_ref, sel_ref, key_ref, keep_ref, *, nbc, n_blocks, width, n_sel, wlen):
    hpg = NSA_HPG
    qf = q_ref[0]
    q = qf.astype(BF16)
    row = lax.broadcasted_iota(jnp.int32, (NSA_HEADS, 1), 0)
    g0 = row < hpg
    kc = kc_ref[0]
    vc = vc_ref[0]
    cb = cb_ref[...]
    valid = cb > 0.5 * NEG
    s = _by_group(hpg, _dot_nt_full(qf, kc[:, :HEAD_DIM]), _dot_nt_full(qf, kc[:, HEAD_DIM:]))
    s = s * ATTN_SCALE + cb
    m = jnp.max(s, axis=-1, keepdims=True)
    e = jnp.where(valid, jnp.exp(s - m), 0.0)
    den = jnp.sum(e, axis=-1, keepdims=True)
    p = e / jnp.where(den > 0, den, 1.0)
    ocmp_ref[0] = (_dot(jnp.where(g0, p, 0.0).astype(BF16), vc[:, :HEAD_DIM].astype(BF16))
                   + _dot(jnp.where(g0, 0.0, p).astype(BF16), vc[:, HEAD_DIM:].astype(BF16)))
    lane = lax.broadcasted_iota(jnp.int32, (1, width), 1)
    cur = n_blocks - 1
    forced = (lane == 0) | (lane == cur) | (lane == cur - 1)
    for g in range(NSA_KV_GROUPS):
        imp = jnp.sum(jnp.where(g0 if g == 0 else jnp.logical_not(g0), p, 0.0), axis=0, keepdims=True)
        impw = jnp.concatenate([imp, jnp.zeros((1, width - nbc), F32)], axis=1)
        score = jnp.where(forced, jnp.inf, jnp.where(lane <= cur, impw, -jnp.inf))
        key_ref[...] = _order_key(score)
        _row_topk(key_ref, keep_ref, n_sel, width)
        sel_ref[0, g:g + 1, :] = keep_ref[...]
    wk0 = wk_ref[pl.ds(0, wlen, stride=2), :].astype(BF16)
    wk1 = wk_ref[pl.ds(1, wlen, stride=2), :].astype(BF16)
    wv0 = wv_ref[pl.ds(0, wlen, stride=2), :].astype(BF16)
    wv1 = wv_ref[pl.ds(1, wlen, stride=2), :].astype(BF16)
    wb = wb_ref[...]
    s = _by_group(hpg, _dot_nt(q, wk0), _dot_nt(q, wk1)) * ATTN_SCALE + wb
    nk = _by_group(hpg, nk_ref[0][:, :HEAD_DIM], nk_ref[0][:, HEAD_DIM:])
    nv = _by_group(hpg, nv_ref[0][:, :HEAD_DIM], nv_ref[0][:, HEAD_DIM:])
    s_new = jnp.sum(qf * nk, axis=-1, keepdims=True) * ATTN_SCALE + b0_ref[...]
    m = jnp.maximum(jnp.max(s, axis=-1, keepdims=True), s_new)
    e = jnp.where(wb > 0.5 * NEG, jnp.exp(s - m), 0.0)
    e_new = jnp.exp(s_new - m)
    den = jnp.sum(e, axis=-1, keepdims=True) + e_new
    o = (_dot(jnp.where(g0, e, 0.0).astype(BF16), wv0) + _dot(jnp.where(g0, 0.0, e).astype(BF16), wv1)
         + e_new * nv)
    owin_ref[0] = o / den


def _s_cmp_win(q, kcomp, vcomp, cbias, wk, wv, nk, nv, wbias, b0, n_blocks, n_sel):
    db = q.shape[0]
    nbc = kcomp.shape[1]
    wlen = wk.shape[1] // 2
    width = -(-n_blocks // 128) * 128
    one = lambda *s: pl.BlockSpec((1,) + s, lambda bi: (bi,) + (0,) * len(s))
    const = lambda a: pl.BlockSpec(a.shape, lambda bi: (0,) * a.ndim)
    return pl.pallas_call(
        functools.partial(_s_cmp_win_kernel, nbc=nbc, n_blocks=n_blocks, width=width, n_sel=n_sel, wlen=wlen),
        grid=(db,),
        in_specs=[one(NSA_HEADS, HEAD_DIM), one(nbc, KV_W), one(nbc, KV_W), const(cbias),
                  pl.BlockSpec((None, 2 * wlen, HEAD_DIM), lambda bi: (bi, 0, 0)),
                  pl.BlockSpec((None, 2 * wlen, HEAD_DIM), lambda bi: (bi, 0, 0)),
                  one(1, KV_W), one(1, KV_W), const(wbias), const(b0)],
        out_specs=[one(NSA_HEADS, HEAD_DIM), one(NSA_HEADS, HEAD_DIM), one(NSA_KV_GROUPS, width)],
        out_shape=[jax.ShapeDtypeStruct((db, NSA_HEADS, HEAD_DIM), F32),
                   jax.ShapeDtypeStruct((db, NSA_HEADS, HEAD_DIM), F32),
                   jax.ShapeDtypeStruct((db, NSA_KV_GROUPS, width), F32)],
        scratch_shapes=[pltpu.VMEM((1, width), jnp.int32), pltpu.VMEM((1, width), F32)],
        compiler_params=_cparams("parallel"),
        name="sample_cmp_win",
    )(q, kcomp, vcomp, cbias, wk, wv, nk, nv, wbias, b0)


def _s_sel_kernel(idx_ref, hp_ref, q_ref, nk_ref, nv_ref, tb_ref, c_ref, *rest, n_sel, cur, n_near):
    del hp_ref
    k_refs, v_refs, o_ref = rest[:n_sel], rest[n_sel:2 * n_sel], rest[2 * n_sel]
    bi = pl.program_id(0)
    g = pl.program_id(1)
    q = q_ref[0].astype(BF16)
    nk = jnp.where(g == 0, nk_ref[0][:, :HEAD_DIM], nk_ref[0][:, HEAD_DIM:])
    nv = jnp.where(g == 0, nv_ref[0][:, :HEAD_DIM], nv_ref[0][:, HEAD_DIM:])
    rowi = lax.broadcasted_iota(jnp.int32, (SEL_BLOCK, 1), 0)
    ss, vs = [], []
    for r in range(n_sel):
        idx = idx_ref[(bi * NSA_KV_GROUPS + g) * n_sel + r]
        first = (rowi == 0) & (idx >= cur)
        kr = jnp.where(first, nk, k_refs[r][pl.ds(g, SEL_BLOCK, stride=2), :]).astype(BF16)
        vs.append(jnp.where(first, nv, v_refs[r][pl.ds(g, SEL_BLOCK, stride=2), :]).astype(BF16))
        u = jnp.clip(idx - (cur - (n_near - 1)), 0, n_near - 1)
        bias = jnp.where(idx >= cur - (n_near - 1), tb_ref[u], c_ref[...])
        ss.append(_dot_nt(q, kr) * ATTN_SCALE + bias)
    m = ss[0].max(axis=-1, keepdims=True)
    for s in ss[1:]:
        m = jnp.maximum(m, s.max(axis=-1, keepdims=True))
    den = jnp.zeros((NSA_HEADS, 1), F32)
    o = jnp.zeros((NSA_HEADS, HEAD_DIM), F32)
    for s, v in zip(ss, vs):
        e = jnp.where(s > 0.5 * NEG, jnp.exp(s - m), 0.0)
        den = den + e.sum(axis=-1, keepdims=True)
        o = o + _dot(e.astype(BF16), v)
    o_ref[0, 0] = o / jnp.where(den > 0, den, 1.0)


def _s_sel(idx_flat, hp_flat, q, nk, nv, tb, c, cache_k, cache_v, n_sel, cur):
    db = q.shape[0]
    rows = SEL_BLOCK * NSA_KV_GROUPS

    def page(r):
        return pl.BlockSpec((None, rows, HEAD_DIM),
                            lambda bi, g, idx, hp, r=r: (hp[(bi * NSA_KV_GROUPS + g) * n_sel + r], 0, 0))

    grid_spec = pltpu.PrefetchScalarGridSpec(
        num_scalar_prefetch=2,
        grid=(db, NSA_KV_GROUPS),
        in_specs=[pl.BlockSpec((1, NSA_HEADS, HEAD_DIM), lambda bi, g, idx, hp: (bi, 0, 0)),
                  pl.BlockSpec((1, 1, KV_W), lambda bi, g, idx, hp: (bi, 0, 0)),
                  pl.BlockSpec((1, 1, KV_W), lambda bi, g, idx, hp: (bi, 0, 0)),
                  pl.BlockSpec(tb.shape, lambda bi, g, idx, hp: (0, 0, 0)),
                  pl.BlockSpec(c.shape, lambda bi, g, idx, hp: (0, 0))]
        + [page(r) for r in range(n_sel)] + [page(r) for r in range(n_sel)],
        out_specs=pl.BlockSpec((1, 1, NSA_HEADS, HEAD_DIM), lambda bi, g, idx, hp: (bi, g, 0, 0)),
    )
    return pl.pallas_call(
        functools.partial(_s_sel_kernel, n_sel=n_sel, cur=cur, n_near=tb.shape[0]),
        grid_spec=grid_spec,
        out_shape=jax.ShapeDtypeStruct((db, NSA_KV_GROUPS, NSA_HEADS, HEAD_DIM), F32),
        compiler_params=_cparams("parallel", "parallel"),
        name="sample_sel",
    )(idx_flat, hp_flat, q, nk, nv, tb, c, *([cache_k] * n_sel), *([cache_v] * n_sel))


def _s_index_kernel(pt_ref, iq_ref, iw_ref, nik_ref, *rest, pg, past, n_keep, width):
    del pt_ref
    pages, keep_ref, sc_ref, key_ref = rest[:pg], rest[pg], rest[pg + 1], rest[pg + 2]
    j = pl.program_id(1)
    span = pg * PAGE_SIZE
    iqf = iq_ref[0]
    iw = iw_ref[0]
    ik = jnp.concatenate([p[...] for p in pages], axis=0)
    lg = jnp.maximum(_dot_nt_x3(iqf, ik) * IDX_DIM ** -0.5, 0.0)
    sc = jnp.sum(lg * iw, axis=0, keepdims=True) * IDX_HEADS ** -0.5 + 0.0
    sc_ref[:, pl.ds(pl.multiple_of(j * span, span), span)] = sc

    @pl.when(j == pl.num_programs(1) - 1)
    def _():
        lg_new = jnp.maximum(jnp.sum(iqf * nik_ref[0], axis=-1, keepdims=True) * IDX_DIM ** -0.5, 0.0)
        sc_new = jnp.sum(lg_new * iw, axis=0, keepdims=True) * IDX_HEADS ** -0.5 + 0.0
        lane = lax.broadcasted_iota(jnp.int32, (1, width - past), 1)
        sc_ref[:, past:] = jnp.where(lane == 0, sc_new, -jnp.inf)
        key_ref[...] = _order_key(sc_ref[...])
        _row_topk(key_ref, keep_ref.at[0], n_keep, width)


def _s_index(page_table, iq, iw, nik, cache_idx, n_keep):
    db, n_pages = page_table.shape
    pg = math.gcd(n_pages, IDX_PAGE_GROUP)
    past = n_pages * PAGE_SIZE
    width = past + 128

    def page(r):
        return pl.BlockSpec((None, PAGE_SIZE, IDX_DIM), lambda bi, j, pt, r=r: (pt[bi, j * pg + r], 0, 0))

    grid_spec = pltpu.PrefetchScalarGridSpec(
        num_scalar_prefetch=1,
        grid=(db, n_pages // pg),
        in_specs=[pl.BlockSpec((1, IDX_HEADS, IDX_DIM), lambda bi, j, pt: (bi, 0, 0)),
                  pl.BlockSpec((1, IDX_HEADS, 1), lambda bi, j, pt: (bi, 0, 0)),
                  pl.BlockSpec((1, 1, IDX_DIM), lambda bi, j, pt: (bi, 0, 0))]
        + [page(r) for r in range(pg)],
        out_specs=pl.BlockSpec((1, 1, width), lambda bi, j, pt: (bi, 0, 0)),
        scratch_shapes=[pltpu.VMEM((1, width), F32), pltpu.VMEM((1, width), jnp.int32)],
    )
    return pl.pallas_call(
        functools.partial(_s_index_kernel, pg=pg, past=past, n_keep=n_keep, width=width),
        grid_spec=grid_spec,
        out_shape=jax.ShapeDtypeStruct((db, 1, width), F32),
        compiler_params=_cparams("parallel", "arbitrary"),
        name="sample_index",
    )(page_table, iq, iw, nik, *([cache_idx] * pg))


def _s_dsa_kernel(pt_ref, q_ref, keep_ref, bt_ref, c_ref, b0_ref, nk_ref, nv_ref, *rest, pg, past):
    del pt_ref
    kp, vp, o_ref = rest[:pg], rest[pg:2 * pg], rest[2 * pg]
    m_ref, l_ref, acc_ref = rest[2 * pg + 1:]
    hpg = DSA_HPG
    j = pl.program_id(1)
    last = pl.num_programs(1) - 1
    span = pg * PAGE_SIZE

    @pl.when(j == 0)
    def _():
        m_ref[...] = jnp.full_like(m_ref, NEG)
        l_ref[...] = jnp.zeros_like(l_ref)
        acc_ref[...] = jnp.zeros_like(acc_ref)

    qf = q_ref[0]
    q = qf.astype(BF16)
    row = lax.broadcasted_iota(jnp.int32, (DSA_HEADS, 1), 0)
    g0 = row < hpg
    rows = lambda refs, g: jnp.concatenate([p[pl.ds(g, PAGE_SIZE, stride=2), :] for p in refs], axis=0).astype(BF16)
    s = _by_group(hpg, _dot_nt(q, rows(kp, 0)), _dot_nt(q, rows(kp, 1))) * ATTN_SCALE
    keep = keep_ref[0, :, pl.ds(pl.multiple_of(j * span, span), span)] > 0.5
    s = jnp.where(keep, s + jnp.where(j == last, bt_ref[...], c_ref[...]), NEG)
    m_old = m_ref[...]
    m_new = jnp.maximum(m_old, jnp.max(s, axis=-1, keepdims=True))
    alpha = jnp.exp(m_old - m_new)
    e = jnp.where(keep, jnp.exp(s - m_new), 0.0)
    l_ref[...] = alpha * l_ref[...] + jnp.sum(e, axis=-1, keepdims=True)
    acc_ref[...] = (alpha * acc_ref[...] + _dot(jnp.where(g0, e, 0.0).astype(BF16), rows(vp, 0))
                    + _dot(jnp.where(g0, 0.0, e).astype(BF16), rows(vp, 1)))
    m_ref[...] = m_new

    @pl.when(j == last)
    def _():
        keep_new = keep_ref[0, :, past:past + 1] > 0.5
        nk = _by_group(hpg, nk_ref[0][:, :HEAD_DIM], nk_ref[0][:, HEAD_DIM:])
        nv = _by_group(hpg, nv_ref[0][:, :HEAD_DIM], nv_ref[0][:, HEAD_DIM:])
        s_new = jnp.sum(qf * nk, axis=-1, keepdims=True) * ATTN_SCALE + b0_ref[...]
        s_new = jnp.where(keep_new, s_new, NEG)
        m_old2 = m_ref[...]
        m2 = jnp.maximum(m_old2, s_new)
        a2 = jnp.exp(m_old2 - m2)
        e_new = jnp.where(keep_new, jnp.exp(s_new - m2), 0.0)
        l = a2 * l_ref[...] + e_new
        o_ref[0] = (a2 * acc_ref[...] + e_new * nv) / jnp.where(l > 0, l, 1.0)


def _s_dsa(page_table, q, keep, btail, c, b0, nk, nv, cache_k, cache_v):
    db, n_pages = page_table.shape
    pg = math.gcd(n_pages, PAGE_GROUP)
    past = n_pages * PAGE_SIZE
    width = keep.shape[-1]
    rows = PAGE_SIZE * DSA_KV_GROUPS

    def page(r):
        return pl.BlockSpec((None, rows, HEAD_DIM), lambda bi, j, pt, r=r: (pt[bi, j * pg + r], 0, 0))

    grid_spec = pltpu.PrefetchScalarGridSpec(
        num_scalar_prefetch=1,
        grid=(db, n_pages // pg),
        in_specs=[pl.BlockSpec((1, DSA_HEADS, HEAD_DIM), lambda bi, j, pt: (bi, 0, 0)),
                  pl.BlockSpec((1, 1, width), lambda bi, j, pt: (bi, 0, 0)),
                  pl.BlockSpec(btail.shape, lambda bi, j, pt: (0, 0)),
                  pl.BlockSpec(c.shape, lambda bi, j, pt: (0, 0)),
                  pl.BlockSpec(b0.shape, lambda bi, j, pt: (0, 0)),
                  pl.BlockSpec((1, 1, KV_W), lambda bi, j, pt: (bi, 0, 0)),
                  pl.BlockSpec((1, 1, KV_W), lambda bi, j, pt: (bi, 0, 0))]
        + [page(r) for r in range(pg)] + [page(r) for r in range(pg)],
        out_specs=pl.BlockSpec((1, DSA_HEADS, HEAD_DIM), lambda bi, j, pt: (bi, 0, 0)),
        scratch_shapes=[pltpu.VMEM((DSA_HEADS, 1), F32), pltpu.VMEM((DSA_HEADS, 1), F32),
                        pltpu.VMEM((DSA_HEADS, HEAD_DIM), F32)],
    )
    return pl.pallas_call(
        functools.partial(_s_dsa_kernel, pg=pg, past=past),
        grid_spec=grid_spec,
        out_shape=jax.ShapeDtypeStruct((db, DSA_HEADS, HEAD_DIM), F32),
        compiler_params=_cparams("parallel", "arbitrary"),
        name="sample_dsa",
    )(page_table, q, keep, btail, c, b0, nk, nv, *([cache_k] * pg), *([cache_v] * pg))


def _split_w_in(w_in):
    points = np.cumsum(IN_COLS)[:-1].tolist()
    q_a, kv_a, g_a, q_b, kv_b, iq, ik, iw, g_m = jnp.split(w_in, points, axis=-1)
    pad = jnp.zeros((D_MODEL, MISC_W - IDX_DIM - IDX_HEADS - 3 * NSA_HEADS), w_in.dtype)
    w_kv = jnp.concatenate([kv_a, kv_b, ik, iw, g_a, pad], axis=-1)
    w_q = jnp.concatenate([q_a, q_b, iq], axis=-1)
    return w_kv, w_q, g_m[:, :D_MODEL], g_m[:, D_MODEL:]


KV_WIDTHS = (KV_W,) * 8 + (MISC_W,)
Q_WIDTHS = (NSA_HEADS * HEAD_DIM, DSA_HEADS * HEAD_DIM, IDX_HEADS * IDX_DIM)


def _row_tile(m, cap):
    tm = math.gcd(m, cap)
    assert tm % 8 == 0 or tm == m
    return tm


def kernel(x_prompt, x_sample, cache_cmp_k, cache_cmp_v, cache_slc_k, cache_slc_v, state_win_k, state_win_v,
           cache_dsa_k, cache_dsa_v, cache_idx_k, page_table, rel_bias_table, w_in, cmp_pe, cmp_w1, cmp_w2,
           w_up_a, w_up_b, w_o, ln1_g, ln1_b, w_router, router_bias, moe_w_gate, moe_w_up, moe_w_down,
           sh_w_gate, sh_w_up, sh_w_down, ln2_g, ln2_b):
    assert w_in.shape[0] == DEPTH == 1
    b, t, _ = x_prompt.shape
    db, dt, _ = x_sample.shape
    assert dt == 1
    n_pool = cache_cmp_k.shape[1]
    n_pages = page_table.shape[1]
    tq = Q_TILE
    assert t % tq == 0 and t % CMP_BLOCK == 0
    n_p, n_s = b * t, db * dt

    tbl_a = rel_bias_table[:, :NSA_HEADS]
    tbl_b = rel_bias_table[:, NSA_HEADS:]
    mixer_w32 = _split_w_in(w_in[0]) + (w_up_a[0], w_up_b[0], w_o[0])
    w_kv32, w_q32, w_ga32, w_gb32, w_ua32, w_ub32, w_o32 = mixer_w32
    w_kv, w_q, w_ga, w_gb, w_ua, w_ub, w_ob = [w.astype(BF16) for w in mixer_w32]
    cw = [_compress_weights(cmp_pe[0, i], cmp_w1[0, i], cmp_w2[0, i]) for i in range(2)]

    xp = x_prompt.reshape(n_p, D_MODEL)
    xs = x_sample.reshape(n_s, D_MODEL)
    xpb = xp.astype(BF16)
    tm_p = _row_tile(n_p, 512)
    (kc, vc, ks, vs, kw, vw, kb, vb, misc,
     kc_h, vc_h, ks_h, vs_h, kb_h, vb_h) = _project(xpb, w_kv, KV_WIDTHS, tm_p, by_head=(0, 1, 2, 3, 6, 7))
    qa, qb, iq = _project(xpb, w_q, Q_WIDTHS, tm_p)
    s_kc, s_vc, s_ks, s_vs, s_kw, s_vw, s_kb, s_vb, s_misc = _project(xs, w_kv32, KV_WIDTHS, n_s)
    s_qa, s_qb, s_iq = _project(xs, w_q32, Q_WIDTHS, n_s)

    nb = t // CMP_BLOCK
    r3 = lambda a: a.reshape(b, t, -1)
    kcomp = _compress(kc.reshape(b * nb, CMP_BLOCK, KV_W), *cw[0]).reshape(b, nb, KV_W)
    vcomp = _compress(vc.reshape(b * nb, CMP_BLOCK, KV_W), *cw[1]).reshape(b, nb, KV_W)
    cdist = jnp.arange(t)[:, None] - (jnp.arange(nb) * CMP_BLOCK + CMP_BLOCK - 1)[None, :]
    cbias = jnp.where(cdist >= 0, jnp.moveaxis(_bias_of(tbl_a, cdist), -1, 0), NEG).swapaxes(1, 2)
    o_cmp, selmask = _cmp_select(r3(qa), kcomp, vcomp, cbias, tq)

    assert MAX_DISTANCE <= tq
    bias2_a, dist2 = _toeplitz_bias(tbl_a, tq, 2)
    bias2_b, _ = _toeplitz_bias(tbl_b, tq, 2)
    c_a, c_b = tbl_a[N_BUCKETS - 1], tbl_b[N_BUCKETS - 1]
    d_a = jnp.where(dist2 >= 0, (bias2_a - c_a[:, None, None, None]) * LOG2E, NEG)
    d_b = jnp.where(dist2 >= 0, (bias2_b - c_b[:, None, None, None]) * LOG2E, NEG)
    expand = (jnp.arange(t)[None, :] // SEL_BLOCK == jnp.arange(nb)[:, None]).astype(BF16)
    o_sel = _dense_attn(r3(qa), r3(ks), r3(vs), selmask, expand, d_a, tq, "sel")

    n_wchunks = -(-(WINDOW - 1) // tq) + 1
    bias_w, dist_w = _toeplitz_bias(tbl_a, tq, n_wchunks)
    wtiles = jnp.where((dist_w >= 0) & (dist_w < WINDOW), bias_w * LOG2E, NEG)
    o_win = _window_attn(r3(qa), r3(kw), r3(vw), wtiles, tq)

    n_keep = min(DSA_TOPK, t // 4)
    iw_t = r3(misc)[:, :, MISC_IW:MISC_IW + IDX_HEADS].swapaxes(1, 2)
    keepmask = _index_select(r3(iq), iw_t, r3(misc), tq, n_keep)
    o_b = _dense_attn(r3(qb), r3(kb), r3(vb), keepmask, jnp.zeros((8, 128), BF16), d_b, tq, "dsa")

    tm_m = _row_tile(n_p, 256)
    y_p = _merge_up(xpb, o_cmp.reshape(n_p, -1), o_sel.reshape(n_p, -1), o_win.reshape(n_p, -1),
                    o_b.reshape(n_p, -1), misc, w_ga, w_gb, w_ua, w_ub, tm_m, 1024)
    x1 = _merge_out(xp, y_p, w_ob, ln1_g, ln1_b, tm_m, out_rows=n_p + n_s)

    past = n_pages * PAGE_SIZE
    halves = PAGE_SIZE // CMP_BLOCK

    def comp_pool(cache, i):
        c = _compress_pool(cache, cmp_pe[0, i], cmp_w1[0, i], cmp_w2[0, i])
        return c.reshape(n_pool, halves * KV_W)[page_table].reshape(db, n_pages * halves, KV_W)

    s_kcomp = comp_pool(cache_cmp_k[0], 0)
    s_vcomp = comp_pool(cache_cmp_v[0], 1)
    assert past % SEL_BLOCK == 0 and past >= 4 * SEL_BLOCK and PAGE_SIZE == 2 * SEL_BLOCK
    total = past + dt
    nbc = past // CMP_BLOCK
    n_blocks = -(-total // SEL_BLOCK)
    cur = past // SEL_BLOCK
    n_sel = min(N_SEL_BLOCKS, n_blocks)
    col = lambda v: v.reshape(-1, 1)
    new = lambda a: a.reshape(db, 1, KV_W)
    cb_s = _bias_of(tbl_a, past - (jnp.arange(nbc) * CMP_BLOCK + CMP_BLOCK - 1)).T
    w_past = state_win_k.shape[2]
    wdist = w_past - jnp.arange(w_past)
    wb_s = jnp.where(wdist < WINDOW, _bias_of(tbl_a, wdist).T, NEG)
    rows2 = lambda a, n: a.reshape(a.shape[0], n * NSA_KV_GROUPS, HEAD_DIM)
    s_q8 = s_qa.reshape(db, NSA_HEADS, HEAD_DIM)
    so_cmp, so_win, selmask = _s_cmp_win(
        s_q8, s_kcomp, s_vcomp, cb_s, rows2(state_win_k[0], w_past), rows2(state_win_v[0], w_past),
        new(s_kw), new(s_vw), wb_s, col(tbl_a[0]), n_blocks, n_sel)
    kept = selmask[:, :, None, :n_blocks] > 0.5
    nth = jnp.cumsum(kept, axis=-1) == (jnp.arange(n_sel) + 1)[:, None]
    sel_idx = jnp.sum(jnp.where(kept & nth, jnp.arange(n_blocks), 0), axis=-1).astype(jnp.int32)
    sel_page = jnp.take_along_axis(page_table, jnp.minimum(sel_idx // 2, n_pages - 1).reshape(db, -1), axis=1)
    sel_hp = sel_page.reshape(sel_idx.shape) * 2 + sel_idx % 2
    n_near = 4
    ndist = (n_near - 1 - jnp.arange(n_near))[:, None] * SEL_BLOCK - jnp.arange(SEL_BLOCK)[None, :]
    tb_s = jnp.where(ndist[:, None, :] >= 0, jnp.moveaxis(_bias_of(tbl_a, ndist), -1, 1), NEG)
    half_pages = lambda c: c.reshape(n_pool * halves, SEL_BLOCK * NSA_KV_GROUPS, HEAD_DIM)
    so_sel2 = _s_sel(sel_idx.reshape(-1), sel_hp.reshape(-1).astype(jnp.int32), s_q8, new(s_ks), new(s_vs), tb_s,
                     col(tbl_a[N_BUCKETS - 1]), half_pages(cache_slc_k[0]), half_pages(cache_slc_v[0]), n_sel, cur)
    so_sel = jnp.concatenate([so_sel2[:, 0, :NSA_HPG], so_sel2[:, 1, NSA_HPG:]], axis=1)
    keep_s = _s_index(page_table, s_iq.reshape(db, IDX_HEADS, IDX_DIM),
                      s_misc[:, MISC_IW:MISC_IW + IDX_HEADS].reshape(db, IDX_HEADS, 1),
                      s_misc[:, :IDX_DIM].reshape(db, 1, IDX_DIM), cache_idx_k[0], min(DSA_TOPK, total // 4))
    span = math.gcd(n_pages, PAGE_GROUP) * PAGE_SIZE
    assert span >= MAX_DISTANCE
    bt_s = _bias_of(tbl_b, span - jnp.arange(span)).T
    pages2 = lambda c: c.reshape(n_pool, PAGE_SIZE * DSA_KV_GROUPS, HEAD_DIM)
    so_b = _s_dsa(page_table, s_qb.reshape(db, DSA_HEADS, HEAD_DIM), keep_s, bt_s, col(tbl_b[N_BUCKETS - 1]),
                  col(tbl_b[0]), new(s_kb), new(s_vb), pages2(cache_dsa_k[0]), pages2(cache_dsa_v[0]))
    n_win = min(WINDOW, total)
    g4 = lambda a: a.reshape(db, dt, NSA_KV_GROUPS, HEAD_DIM)
    s_wk = jnp.concatenate([state_win_k[0], g4(s_kw)], axis=1)[:, -n_win:]
    s_wv = jnp.concatenate([state_win_v[0], g4(s_vw)], axis=1)[:, -n_win:]
    y_s = _merge_up(xs, so_cmp.reshape(n_s, -1), so_sel.reshape(n_s, -1), so_win.reshape(n_s, -1),
                    so_b.reshape(n_s, -1), s_misc, w_ga32, w_gb32, w_ua32, w_ub32, n_s, 512)
    assert n_p % n_s == 0
    x1 = _merge_out(xs, y_s, w_o32, ln1_g, ln1_b, n_s, out_rows=n_p + n_s, into=x1)
    x1s = x1[n_p:]

    wr_t = w_router[0].T
    rb = router_bias[0].reshape(N_EXPERTS, 1)
    eidx_p, wts_p, pos_p, cnt_p = _router(x1, wr_t, rb, jnp.zeros((N_EXPERTS, 1), jnp.int32), tm_m, m=n_p)
    eidx_s, wts_s, pos_s, cnt = _router(x1s, wr_t, rb, cnt_p, n_s)
    n_tok = n_p + n_s
    eidx = jnp.concatenate([eidx_p[:TOP_K], eidx_s[:TOP_K]], axis=1)
    pos = jnp.concatenate([pos_p[:TOP_K], pos_s[:TOP_K]], axis=1)
    tr = EXPERT_ROWS
    n_asg = n_tok * TOP_K
    counts = cnt[:, 0]
    padded = (counts + tr - 1) // tr * tr
    pend = jnp.cumsum(padded)
    pad_start = pend - padded
    n_rows = -(-n_asg // tr) * tr + N_EXPERTS * tr
    n_blk = n_rows // tr
    dest = pos + jnp.sum(jnp.where(eidx[..., None] == jnp.arange(N_EXPERTS), pad_start, 0), axis=-1)
    blk_e = jnp.minimum(jnp.sum(pend[None, :] <= (jnp.arange(n_blk) * tr)[:, None], axis=1), N_EXPERTS - 1)
    n_valid = (pend[-1] // tr).astype(jnp.int32).reshape(1)
    fill = pad_start[:, None] + counts[:, None] + jnp.arange(tr)[None, :]
    fill = jnp.where(fill < pend[:, None], fill, n_rows)
    n_extra = n_rows - n_asg - N_EXPERTS * tr
    slot_keys = jnp.concatenate([dest.reshape(-1), fill.reshape(-1),
                                 jnp.full((n_extra,), n_rows)]).astype(jnp.int32)
    slot_toks = jnp.concatenate([jnp.tile(jnp.arange(n_tok, dtype=jnp.int32), TOP_K),
                                 jnp.zeros((N_EXPERTS * tr + n_extra,), jnp.int32)])
    row_tok = lax.sort((slot_keys, slot_toks), num_keys=1)[1]
    xg = x1[row_tok]
    out_rows = _experts(blk_e.astype(jnp.int32), n_valid, xg, moe_w_gate[0], moe_w_up[0], moe_w_down[0], tr)
    rows6 = out_rows[dest]

    sg, su, sd = sh_w_gate[0].astype(BF16), sh_w_up[0].astype(BF16), sh_w_down[0].astype(BF16)
    y_prompt = _ffn_out(x1, rows6, wts_p.T, sg, su, sd, ln2_g, ln2_b, tm_m, m=n_p).reshape(b, t, D_MODEL)
    y_sample = _ffn_out(x1s, rows6[:, n_p:], wts_s.T, sg, su, sd, ln2_g, ln2_b, n_s).reshape(db, dt, D_MODEL)

    n_win = min(WINDOW, t)
    st = lambda a: a.reshape(1, b, t, NSA_KV_GROUPS, HEAD_DIM)
    ss = lambda a: a.reshape(1, db, dt, NSA_KV_GROUPS, HEAD_DIM)
    return (y_prompt, y_sample,
            st(kc_h), st(vc_h), st(ks_h), st(vs_h), st(kw)[:, :, -n_win:], st(vw)[:, :, -n_win:], st(kb_h), st(vb_h),
            misc[:, :IDX_DIM].reshape(1, b, t, IDX_DIM),
            ss(s_kc), ss(s_vc), ss(s_ks), ss(s_vs), s_wk[None], s_wv[None], ss(s_kb), ss(s_vb),
            s_misc[:, :IDX_DIM].reshape(1, db, dt, IDX_DIM))
```

```python
import functools
import math

import jax
import jax.numpy as jnp
import numpy as np
from jax import lax
from jax.experimental import pallas as pl
from jax.experimental.pallas import tpu as pltpu

D_MODEL = 2048
PAGE_SIZE = 128
HEAD_DIM = 128
NSA_HEADS = 8
NSA_KV_GROUPS = 2
NSA_HPG = NSA_HEADS // NSA_KV_GROUPS
CMP_BLOCK = 64
CMP_HIDDEN = 128
SEL_BLOCK = 64
N_SEL_BLOCKS = 16
WINDOW = 512
DSA_HEADS = 8
DSA_KV_GROUPS = 2
DSA_HPG = DSA_HEADS // DSA_KV_GROUPS
IDX_HEADS = 8
IDX_DIM = 64
DSA_TOPK = 256
N_BUCKETS = 32
MAX_EXACT = 16
MAX_DISTANCE = 128
N_EXPERTS = 64
EXPERT_FF = 512
SHARED_FF = 512
TOP_K = 6
N_EXPERT_GROUPS = 8
TOPK_GROUPS = 4
ROUTED_SCALE = 2.5
LN_EPS = 1e-5
ATTN_SCALE = HEAD_DIM ** -0.5
DEPTH = 1
DEEPNORM_ALPHA = (2 * DEPTH) ** 0.25
IN_COLS = (NSA_HEADS * HEAD_DIM, 6 * NSA_KV_GROUPS * HEAD_DIM, 3 * NSA_HEADS,
           DSA_HEADS * HEAD_DIM, 2 * DSA_KV_GROUPS * HEAD_DIM,
           IDX_HEADS * IDX_DIM, IDX_DIM, IDX_HEADS, 2 * D_MODEL)

KV_W = NSA_KV_GROUPS * HEAD_DIM
MISC_W = 128
MISC_IW = IDX_DIM
MISC_GA = IDX_DIM + IDX_HEADS
NEG = -1e30
Q_TILE = 256
EXPERT_ROWS = 512
VMEM_LIMIT = 56 * 1024 * 1024

BF16 = jnp.bfloat16
F32 = jnp.float32


def _cparams(*sem):
    return pltpu.CompilerParams(dimension_semantics=sem, vmem_limit_bytes=VMEM_LIMIT)


def _dot(a, b):
    return jnp.dot(a, b, preferred_element_type=F32)


def _dot_nt(a, b):
    return lax.dot_general(a, b, (((1,), (1,)), ((), ())), preferred_element_type=F32)


def _dot_nt_full(a, b):
    return lax.dot_general(a.astype(F32), b.astype(F32), (((1,), (1,)), ((), ())),
                           preferred_element_type=F32, precision=lax.Precision.HIGHEST)


def _dot_nt_x3(a, b):
    ah, bh = a.astype(BF16), b.astype(BF16)
    al, bl = (a - ah.astype(F32)).astype(BF16), (b - bh.astype(F32)).astype(BF16)
    return _dot_nt(ah, bh) + (_dot_nt(ah, bl) + _dot_nt(al, bh))


def _mm(a, w):
    if w.dtype == F32:
        return jnp.dot(a.astype(F32), w, preferred_element_type=F32, precision=lax.Precision.HIGHEST)
    return jnp.dot(a.astype(w.dtype), w, preferred_element_type=F32)


def _proj_kernel(x_ref, w_ref, *o_refs, widths, by_head):
    acc = _mm(x_ref[...], w_ref[...])
    tm = acc.shape[0]
    off = 0
    for o_ref, wd in zip(o_refs, widths):
        o_ref[...] = acc[:, off:off + wd]
        off += wd
    for r_ref, i in zip(o_refs[len(widths):], by_head):
        off = sum(widths[:i])
        n_h = widths[i] // HEAD_DIM
        for h in range(n_h):
            r_ref[pl.ds(h, tm, stride=n_h), :] = acc[:, off + h * HEAD_DIM:off + (h + 1) * HEAD_DIM]


def _project(x, w, widths, tm, by_head=()):
    m, k = x.shape
    n = w.shape[1]
    assert sum(widths) == n and m % tm == 0
    heads = [widths[i] // HEAD_DIM for i in by_head]
    return pl.pallas_call(
        functools.partial(_proj_kernel, widths=widths, by_head=by_head),
        grid=(m // tm,),
        in_specs=[pl.BlockSpec((tm, k), lambda i: (i, 0)),
                  pl.BlockSpec((k, n), lambda i: (0, 0))],
        out_specs=[pl.BlockSpec((tm, wd), lambda i: (i, 0)) for wd in widths]
        + [pl.BlockSpec((tm * n_h, HEAD_DIM), lambda i: (i, 0)) for n_h in heads],
        out_shape=[jax.ShapeDtypeStruct((m, wd), F32) for wd in widths]
        + [jax.ShapeDtypeStruct((m * n_h, HEAD_DIM), F32) for n_h in heads],
        compiler_params=_cparams("parallel"),
        name="project",
    )(x, w)


CMP_JCHUNK = 8


def _compress_kernel(x_ref, pe_ref, w1_ref, w2_ref, o_ref, acc_ref):
    jc = pl.program_id(1)

    @pl.when(jc == 0)
    def _():
        acc_ref[...] = jnp.zeros_like(acc_ref)

    acc = acc_ref[...]
    for jj in range(CMP_JCHUNK):
        lhs = (x_ref[:, jj, :] + pe_ref[jj:jj + 1, :]).astype(BF16)
        acc = acc + _dot(lhs, w1_ref[jj])
    acc_ref[...] = acc

    @pl.when(jc == pl.num_programs(1) - 1)
    def _():
        h = jax.nn.gelu(acc_ref[...])
        o_ref[...] = _dot(h.astype(BF16), w2_ref[...])


def _compress(rows, pe2, w1big, w2big):
    r = rows.shape[0]
    tr = math.gcd(r, 1024)
    assert tr % 8 == 0
    return pl.pallas_call(
        _compress_kernel,
        grid=(r // tr, CMP_BLOCK // CMP_JCHUNK),
        in_specs=[pl.BlockSpec((tr, CMP_JCHUNK, KV_W), lambda i, j: (i, j, 0)),
                  pl.BlockSpec((CMP_JCHUNK, KV_W), lambda i, j: (j, 0)),
                  pl.BlockSpec((CMP_JCHUNK, KV_W, KV_W), lambda i, j: (j, 0, 0)),
                  pl.BlockSpec((KV_W, KV_W), lambda i, j: (0, 0))],
        out_specs=pl.BlockSpec((tr, KV_W), lambda i, j: (i, 0)),
        out_shape=jax.ShapeDtypeStruct((r, KV_W), F32),
        scratch_shapes=[pltpu.VMEM((tr, KV_W), F32)],
        compiler_params=_cparams("parallel", "arbitrary"),
        name="compress",
    )(rows, pe2, w1big, w2big)


def _rows_at(x_ref, r):
    n, rows, w = x_ref.shape
    return x_ref.reshape(n * rows, w)[pl.ds(r, n, stride=rows), :]


def _compress_pool_kernel(x_ref, pe_ref, w1_ref, w2_ref, o_ref, acc_ref, *, tr):
    jc = pl.program_id(1)

    @pl.when(jc == 0)
    def _():
        acc_ref[...] = jnp.zeros_like(acc_ref)

    for g in range(NSA_KV_GROUPS):
        acc = acc_ref[g]
        for jp in range(CMP_JCHUNK // 2):
            parts = []
            for u in range(2):
                jj = 2 * jp + u
                xs = _rows_at(x_ref, 2 * jj + g) + pe_ref[jj:jj + 1, :]
                parts.append(xs.astype(BF16))
            acc = acc + _dot(jnp.concatenate(parts, axis=-1), w1_ref[jp])
        acc_ref[g] = acc

    @pl.when(jc == pl.num_programs(1) - 1)
    def _():
        for g in range(NSA_KV_GROUPS):
            h = jax.nn.gelu(acc_ref[g])
            o_ref[:, g * HEAD_DIM:(g + 1) * HEAD_DIM] = _dot(h.astype(BF16), w2_ref[...])


def _compress_pool(cache, pe, w1, w2):
    n_pool = cache.shape[0]
    r = n_pool * (PAGE_SIZE // CMP_BLOCK)
    rows = cache.reshape(r, CMP_BLOCK * NSA_KV_GROUPS, HEAD_DIM)
    tr = math.gcd(r, 1024)
    assert tr % 8 == 0
    rows_per = 2 * CMP_JCHUNK
    w1p = w1.reshape(CMP_BLOCK // 2, 2 * HEAD_DIM, CMP_HIDDEN).astype(BF16)
    return pl.pallas_call(
        functools.partial(_compress_pool_kernel, tr=tr),
        grid=(r // tr, CMP_BLOCK // CMP_JCHUNK),
        in_specs=[pl.BlockSpec((tr, rows_per, HEAD_DIM), lambda i, j: (i, j, 0)),
                  pl.BlockSpec((CMP_JCHUNK, HEAD_DIM), lambda i, j: (j, 0)),
                  pl.BlockSpec((CMP_JCHUNK // 2, 2 * HEAD_DIM, CMP_HIDDEN), lambda i, j: (j, 0, 0)),
                  pl.BlockSpec((CMP_HIDDEN, HEAD_DIM), lambda i, j: (0, 0))],
        out_specs=pl.BlockSpec((tr, KV_W), lambda i, j: (i, 0)),
        out_shape=jax.ShapeDtypeStruct((r, KV_W), F32),
        scratch_shapes=[pltpu.VMEM((NSA_KV_GROUPS, tr, CMP_HIDDEN), F32)],
        compiler_params=_cparams("parallel", "arbitrary"),
        name="compress_pool",
    )(rows, pe, w1p, w2.astype(BF16))


def _compress_weights(pe, w1, w2):
    pe2 = jnp.concatenate([pe, pe], axis=-1)
    z1 = jnp.zeros_like(w1)
    w1big = jnp.concatenate([jnp.concatenate([w1, z1], axis=2), jnp.concatenate([z1, w1], axis=2)], axis=1)
    z2 = jnp.zeros_like(w2)
    w2big = jnp.concatenate([jnp.concatenate([w2, z2], axis=1), jnp.concatenate([z2, w2], axis=1)], axis=0)
    return pe2, w1big.astype(BF16), w2big.astype(BF16)


def _dot_tn(a, b):
    return lax.dot_general(a, b, (((0,), (0,)), ((), ())), preferred_element_type=F32)


def _cmp_select_kernel(q_ref, kc_ref, vc_ref, cb_ref, o_ref, sel_ref, *, tq, nb, n_sel):
    qi = pl.program_id(1)
    t = qi * tq + lax.broadcasted_iota(jnp.int32, (nb, tq), 1)
    j = lax.broadcasted_iota(jnp.int32, (nb, tq), 0)
    cur = t // SEL_BLOCK
    for g in range(NSA_KV_GROUPS):
        kc = kc_ref[0, :, g * HEAD_DIM:(g + 1) * HEAD_DIM].astype(BF16)
        vc = vc_ref[0, :, g * HEAD_DIM:(g + 1) * HEAD_DIM].astype(BF16)
        imp = jnp.zeros((nb, tq), F32)
        for h in range(NSA_HPG):
            hh = g * NSA_HPG + h
            q = q_ref[0, :, hh * HEAD_DIM:(hh + 1) * HEAD_DIM].astype(BF16)
            cb = cb_ref[hh]
            valid = cb > 0.5 * NEG
            s = _dot_nt(kc, q) * ATTN_SCALE + cb
            m = jnp.max(s, axis=0, keepdims=True)
            e = jnp.where(valid, jnp.exp(s - m), 0.0)
            den = jnp.sum(e, axis=0, keepdims=True)
            p = e / jnp.where(den > 0, den, 1.0)
            o_ref[0, :, hh * HEAD_DIM:(hh + 1) * HEAD_DIM] = _dot_tn(p.astype(BF16), vc)
            imp = imp + p
        forced = (j == 0) | (j == cur) | (j == cur - 1)
        score = jnp.where(forced, jnp.inf, jnp.where(j <= cur, imp, -jnp.inf))
        rank = jnp.zeros((nb, tq), jnp.int32)
        for k in range(nb):
            row = score[k:k + 1, :]
            rank = rank + jnp.where(row > score, 1, 0) + jnp.where(row == score, jnp.where(k < j, 1, 0), 0)
        sel_ref[0, g] = jnp.where(rank < n_sel, 1.0, 0.0).astype(F32)


def _cmp_select(qa, kcomp, vcomp, cbias, tq):
    b, t, _ = qa.shape
    nb = kcomp.shape[1]
    n_sel = min(N_SEL_BLOCKS, nb)
    return pl.pallas_call(
        functools.partial(_cmp_select_kernel, tq=tq, nb=nb, n_sel=n_sel),
        grid=(b, t // tq),
        in_specs=[pl.BlockSpec((1, tq, NSA_HEADS * HEAD_DIM), lambda bi, qi: (bi, qi, 0)),
                  pl.BlockSpec((1, nb, KV_W), lambda bi, qi: (bi, 0, 0)),
                  pl.BlockSpec((1, nb, KV_W), lambda bi, qi: (bi, 0, 0)),
                  pl.BlockSpec((NSA_HEADS, nb, tq), lambda bi, qi: (0, 0, qi))],
        out_specs=[pl.BlockSpec((1, tq, NSA_HEADS * HEAD_DIM), lambda bi, qi: (bi, qi, 0)),
                   pl.BlockSpec((1, NSA_KV_GROUPS, nb, tq), lambda bi, qi: (bi, 0, 0, qi))],
        out_shape=[jax.ShapeDtypeStruct((b, t, NSA_HEADS * HEAD_DIM), F32),
                   jax.ShapeDtypeStruct((b, NSA_KV_GROUPS, nb, t), F32)],
        compiler_params=_cparams("parallel", "parallel"),
        name="cmp_select",
    )(qa, kcomp, vcomp, cbias)


LOG2E = math.log2(math.e)


def _dense_attn_kernel(q_ref, k_ref, v_ref, mask_ref, expand_ref, d_ref, o_ref, *, tq, n_tiles, hpg, mode):
    qi = pl.program_id(2)

    def attend(n_ch):
        chunk = lambda c: slice(c * tq, (c + 1) * tq)
        if mode == "sel":
            mbs = [jnp.where(_dot_tn(mask_ref[0, 0].astype(BF16), expand_ref[:, chunk(c)]) > 0.5, 0.0, NEG)
                   for c in range(n_ch)]
        else:
            mbs = [jnp.where(mask_ref[0, :, chunk(c)].astype(F32) > 0.5, 0.0, NEG) for c in range(n_ch)]
        ks = [k_ref[0, chunk(c), :].astype(BF16) for c in range(n_ch)]
        vs = [v_ref[0, chunk(c), :].astype(BF16) for c in range(n_ch)]
        for hh in range(hpg):
            q = (q_ref[0, :, hh * HEAD_DIM:(hh + 1) * HEAD_DIM] * (ATTN_SCALE * LOG2E)).astype(BF16)
            ss = []
            for c in range(n_ch):
                s = _dot_nt(q, ks[c]) + mbs[c]
                if n_ch - 1 - c < 2:
                    s = s + d_ref[hh, n_ch - 1 - c]
                ss.append(s)
            mx = ss[0]
            for s in ss[1:]:
                mx = jnp.maximum(mx, s)
            m = jnp.max(mx, axis=-1, keepdims=True)
            es = [jnp.exp2(s - m) for s in ss]
            tot = es[0]
            for e in es[1:]:
                tot = tot + e
            den = jnp.sum(tot, axis=-1, keepdims=True)
            o = _dot(es[0].astype(BF16), vs[0])
            for c in range(1, n_ch):
                o = o + _dot(es[c].astype(BF16), vs[c])
            o_ref[0, :, hh * HEAD_DIM:(hh + 1) * HEAD_DIM] = jnp.where(m > 0.5 * NEG, o / den, 0.0)

    for tile in range(n_tiles):
        @pl.when(qi == tile)
        def _(tile=tile):
            attend(tile + 1)


def _dense_attn(q, k, v, mask, expand, dtiles, tq, mode):
    b, t, qw = q.shape
    n_heads = qw // HEAD_DIM
    hpg = n_heads // (k.shape[2] // HEAD_DIM)
    n_groups = n_heads // hpg
    gw = hpg * HEAD_DIM
    if mode == "sel":
        nb = mask.shape[2]
        mask_spec = pl.BlockSpec((1, 1, nb, tq), lambda bi, g, qi: (bi, g, 0, qi))
    else:
        mask_spec = pl.BlockSpec((1, tq, t), lambda bi, g, qi: (bi, qi, 0))
    return pl.pallas_call(
        functools.partial(_dense_attn_kernel, tq=tq, n_tiles=t // tq, hpg=hpg, mode=mode),
        grid=(b, n_groups, t // tq),
        in_specs=[pl.BlockSpec((1, tq, gw), lambda bi, g, qi: (bi, qi, g)),
                  pl.BlockSpec((1, t, HEAD_DIM), lambda bi, g, qi: (bi, 0, g)),
                  pl.BlockSpec((1, t, HEAD_DIM), lambda bi, g, qi: (bi, 0, g)),
                  mask_spec,
                  pl.BlockSpec(expand.shape, lambda bi, g, qi: (0, 0)),
                  pl.BlockSpec((hpg,) + dtiles.shape[1:], lambda bi, g, qi: (g, 0, 0, 0))],
        out_specs=pl.BlockSpec((1, tq, gw), lambda bi, g, qi: (bi, qi, g)),
        out_shape=jax.ShapeDtypeStruct((b, t, qw), F32),
        compiler_params=_cparams("parallel", "parallel", "parallel"),
        name="dense_attn_" + mode,
    )(q, k, v, mask, expand, dtiles)


def _window_attn_kernel(q_ref, k_ref, v_ref, w_ref, o_ref, *, tq, n_chunks, n_heads, hpg):
    qi = pl.program_id(1)
    starts, pens = [], []
    for r in range(n_chunks):
        cj = qi - (n_chunks - 1) + r
        starts.append(pl.multiple_of(jnp.maximum(cj, 0) * tq, tq))
        pens.append(jnp.where(cj < 0, NEG, 0.0).astype(F32))
    for hh in range(n_heads):
        g = hh // hpg
        q = (q_ref[0, :, hh * HEAD_DIM:(hh + 1) * HEAD_DIM] * (ATTN_SCALE * LOG2E)).astype(BF16)
        ss = []
        for r in range(n_chunks):
            k = k_ref[0, pl.ds(starts[r], tq), g * HEAD_DIM:(g + 1) * HEAD_DIM].astype(BF16)
            ss.append(_dot_nt(q, k) + (w_ref[hh, n_chunks - 1 - r] + pens[r]))
        mx = ss[0]
        for s in ss[1:]:
            mx = jnp.maximum(mx, s)
        m = jnp.max(mx, axis=-1, keepdims=True)
        es = [jnp.exp2(s - m) for s in ss]
        tot = es[0]
        for e in es[1:]:
            tot = tot + e
        den = jnp.sum(tot, axis=-1, keepdims=True)
        o = jnp.zeros((tq, HEAD_DIM), F32)
        for r in range(n_chunks):
            v = v_ref[0, pl.ds(starts[r], tq), g * HEAD_DIM:(g + 1) * HEAD_DIM].astype(BF16)
            o = o + _dot(es[r].astype(BF16), v)
        o_ref[0, :, hh * HEAD_DIM:(hh + 1) * HEAD_DIM] = jnp.where(m > 0.5 * NEG, o / den, 0.0)


def _window_attn(q, k, v, wtiles, tq):
    b, t, qw = q.shape
    n_heads = qw // HEAD_DIM
    hpg = n_heads // (k.shape[2] // HEAD_DIM)
    n_chunks = wtiles.shape[1]
    return pl.pallas_call(
        functools.partial(_window_attn_kernel, tq=tq, n_chunks=n_chunks, n_heads=n_heads, hpg=hpg),
        grid=(b, t // tq),
        in_specs=[pl.BlockSpec((1, tq, qw), lambda bi, qi: (bi, qi, 0)),
                  pl.BlockSpec((1, t, k.shape[2]), lambda bi, qi: (bi, 0, 0)),
                  pl.BlockSpec((1, t, v.shape[2]), lambda bi, qi: (bi, 0, 0)),
                  pl.BlockSpec(wtiles.shape, lambda bi, qi: (0, 0, 0, 0))],
        out_specs=pl.BlockSpec((1, tq, qw), lambda bi, qi: (bi, qi, 0)),
        out_shape=jax.ShapeDtypeStruct((b, t, qw), F32),
        compiler_params=_cparams("parallel", "parallel"),
        name="window_attn",
    )(q, k, v, wtiles)


INT_MIN = -2 ** 31


def _topk_mask(key_ref, n_keep, tq, s_len):
    def body(i, thr_u):
        cand_u = thr_u | jnp.left_shift(jnp.int32(1), 31 - i)
        below = (cand_u ^ INT_MIN) - 1
        cnt = jnp.sum(jnp.where(key_ref[...] > below, 1, 0), axis=-1, keepdims=True)
        return jnp.where(cnt >= n_keep, cand_u, thr_u)

    thr_u = lax.fori_loop(0, 32, body, jnp.zeros((tq, 1), jnp.int32))
    thr = thr_u ^ INT_MIN
    return thr


def _index_select_kernel(iq_ref, wt_ref, mk_ref, o_ref, key_ref, *, tq, s_len, n_keep):
    qi = pl.program_id(1)
    wts = wt_ref[0] * IDX_DIM ** -0.5
    iqs = [iq_ref[0, :, h * IDX_DIM:(h + 1) * IDX_DIM].astype(BF16) for h in range(IDX_HEADS)]
    kidx = lax.broadcasted_iota(jnp.int32, (tq, tq), 0)
    t = qi * tq + lax.broadcasted_iota(jnp.int32, (tq, tq), 1)
    chunk = lambda c: slice(c * tq, (c + 1) * tq)

    def select(n_ch):
        s_b = n_ch * tq
        for c in range(n_ch):
            ik = mk_ref[0, chunk(c), 0:IDX_DIM].astype(BF16)
            score = jnp.zeros((tq, tq), F32)
            for h in range(IDX_HEADS):
                score = score + jnp.maximum(_dot_nt(ik, iqs[h]), 0.0) * wts[h:h + 1, :]
            score = score * IDX_HEADS ** -0.5 + 0.0
            score = jnp.where(c * tq + kidx <= t, score, -jnp.inf)
            bits = pltpu.bitcast(score, jnp.int32)
            key_ref[chunk(c), :] = jnp.where(bits < 0, bits ^ 0x7FFFFFFF, bits)

        def bit_step(i, thr_u):
            cand_u = thr_u | jnp.left_shift(jnp.int32(1), 31 - i)
            below = (cand_u ^ INT_MIN) - 1
            cnt = jnp.sum(jnp.where(key_ref[:s_b, :] > below, 1, 0), axis=0, keepdims=True)
            return jnp.where(cnt >= n_keep, cand_u, thr_u)

        thr = lax.fori_loop(0, 32, bit_step, jnp.zeros((1, tq), jnp.int32)) ^ INT_MIN
        key = key_ref[:s_b, :]
        gt = key > thr
        eq = key == thr
        need = n_keep - jnp.sum(jnp.where(gt, 1, 0), axis=0, keepdims=True)
        n_eq = jnp.sum(jnp.where(eq, 1, 0), axis=0, keepdims=True)
        keep = jnp.where(gt | eq, 1.0, 0.0)
        for c in range(n_ch):
            o_ref[0, :, chunk(c)] = keep[chunk(c), :].T.astype(o_ref.dtype)
        if s_b < s_len:
            o_ref[0, :, s_b:] = jnp.zeros((tq, s_len - s_b), o_ref.dtype)

        @pl.when(jnp.max(n_eq - need) > 0)
        def _():
            r_i = lax.broadcasted_iota(jnp.int32, (128, 128), 0)
            c_i = lax.broadcasted_iota(jnp.int32, (128, 128), 1)
            tri = jnp.where(c_i <= r_i, 1.0, 0.0).astype(BF16)
            before = jnp.zeros((1, tq), F32)
            need_f = need.astype(F32)
            for c in range(n_ch):
                parts = []
                for u in range(tq // 128):
                    sl = slice(c * tq + u * 128, c * tq + (u + 1) * 128)
                    eq_c = eq[sl, :]
                    eq_f = jnp.where(eq_c, 1.0, 0.0)
                    pref = _dot(tri, eq_f.astype(BF16)) + before
                    parts.append(jnp.where(gt[sl, :] | (eq_c & (pref <= need_f)), 1.0, 0.0))
                    before = before + jnp.sum(eq_f, axis=0, keepdims=True)
                o_ref[0, :, chunk(c)] = jnp.concatenate(parts, axis=0).T.astype(o_ref.dtype)

    for tile in range(s_len // tq):
        @pl.when(qi == tile)
        def _(tile=tile):
            select(tile + 1)


def _index_select(iq, iw_t, misc, tq, n_keep):
    b, t, _ = iq.shape
    return pl.pallas_call(
        functools.partial(_index_select_kernel, tq=tq, s_len=t, n_keep=n_keep),
        grid=(b, t // tq),
        in_specs=[pl.BlockSpec((1, tq, IDX_HEADS * IDX_DIM), lambda bi, qi: (bi, qi, 0)),
                  pl.BlockSpec((1, IDX_HEADS, tq), lambda bi, qi: (bi, 0, qi)),
                  pl.BlockSpec((1, t, MISC_W), lambda bi, qi: (bi, 0, 0))],
        out_specs=pl.BlockSpec((1, tq, t), lambda bi, qi: (bi, qi, 0)),
        out_shape=jax.ShapeDtypeStruct((b, t, t), BF16),
        scratch_shapes=[pltpu.VMEM((t, tq), jnp.int32)],
        compiler_params=_cparams("parallel", "parallel"),
        name="index_select",
    )(iq, iw_t, misc)


def _merge_up_kernel(x_ref, oc_ref, os_ref, ow_ref, ob_ref, misc_ref, wga_ref, wgb_ref, wua_ref, wub_ref,
                     y_ref, oa_ref):
    ga = jax.nn.sigmoid(misc_ref[:, MISC_GA:MISC_GA + 3 * NSA_HEADS])
    for hh in range(NSA_HEADS):
        sl = slice(hh * HEAD_DIM, (hh + 1) * HEAD_DIM)
        oa = (ga[:, hh:hh + 1] * oc_ref[:, sl] + ga[:, NSA_HEADS + hh:NSA_HEADS + hh + 1] * os_ref[:, sl]
              + ga[:, 2 * NSA_HEADS + hh:2 * NSA_HEADS + hh + 1] * ow_ref[:, sl])
        oa_ref[:, sl] = oa.astype(oa_ref.dtype)
    x = x_ref[...]
    ya = _mm(oa_ref[...], wua_ref[...])
    yb = _mm(ob_ref[...], wub_ref[...])
    g_a = jax.nn.sigmoid(_mm(x, wga_ref[...]))
    g_b = jax.nn.sigmoid(_mm(x, wgb_ref[...]))
    y_ref[...] = (g_a * ya + g_b * yb).astype(y_ref.dtype)


def _merge_up(xb, o_cmp, o_sel, o_win, o_b, misc, wga, wgb, wua, wub, tm, tn):
    m = xb.shape[0]
    mx = wua.dtype
    aw = NSA_HEADS * HEAD_DIM
    bw = DSA_HEADS * HEAD_DIM
    row = lambda w: pl.BlockSpec((tm, w), lambda j, i: (i, 0))
    wcol = lambda k: pl.BlockSpec((k, tn), lambda j, i: (0, j))
    return pl.pallas_call(
        _merge_up_kernel,
        grid=(D_MODEL // tn, m // tm),
        in_specs=[row(D_MODEL), row(aw), row(aw), row(aw), row(bw), row(MISC_W),
                  wcol(D_MODEL), wcol(D_MODEL), wcol(aw), wcol(bw)],
        out_specs=pl.BlockSpec((tm, tn), lambda j, i: (i, j)),
        out_shape=jax.ShapeDtypeStruct((m, D_MODEL), mx),
        scratch_shapes=[pltpu.VMEM((tm, aw), mx)],
        compiler_params=_cparams("parallel", "parallel"),
        name="merge_up",
    )(xb, o_cmp, o_sel, o_win, o_b, misc, wga, wgb, wua, wub)


def _layer_norm(z, g, b):
    mu = jnp.mean(z, axis=-1, keepdims=True)
    zc = z - mu
    var = jnp.mean(zc * zc, axis=-1, keepdims=True)
    return zc * lax.rsqrt(var + LN_EPS) * g + b


def _merge_out_kernel(x_ref, y_ref, wo_ref, g_ref, b_ref, *rest):
    o_ref = rest[-1]
    z = DEEPNORM_ALPHA * x_ref[...] + _mm(y_ref[...], wo_ref[...])
    o_ref[...] = _layer_norm(z, g_ref[...], b_ref[...])


def _merge_out(x, y, wo, g, b, tm, out_rows=None, into=None):
    m = x.shape[0]
    out_rows = out_rows or m
    row = lambda: pl.BlockSpec((tm, D_MODEL), lambda i: (i, 0))
    args = [x, y, wo, g, b]
    in_specs = [row(), row(), pl.BlockSpec((D_MODEL, D_MODEL), lambda i: (0, 0)),
                pl.BlockSpec((1, D_MODEL), lambda i: (0, 0)), pl.BlockSpec((1, D_MODEL), lambda i: (0, 0))]
    first = 0
    aliases = {}
    if into is not None:
        assert into.shape == (out_rows, D_MODEL) and (out_rows - m) % tm == 0
        first = (out_rows - m) // tm
        args.append(into)
        in_specs.append(pl.BlockSpec(memory_space=pl.ANY))
        aliases = {len(args) - 1: 0}
    return pl.pallas_call(
        _merge_out_kernel,
        grid=(m // tm,),
        in_specs=in_specs,
        out_specs=pl.BlockSpec((tm, D_MODEL), lambda i: (i + first, 0)),
        out_shape=jax.ShapeDtypeStruct((out_rows, D_MODEL), F32),
        input_output_aliases=aliases,
        compiler_params=_cparams("parallel"),
        name="merge_out",
    )(*args)


def _router_kernel(x_ref, wr_ref, rb_ref, c0_ref, idx_ref, wt_ref, pos_ref, cnt_ref, run_ref, *, tm):
    epg = N_EXPERTS // N_EXPERT_GROUPS
    logits = lax.dot_general(wr_ref[...], x_ref[...], (((1,), (1,)), ((), ())),
                             preferred_element_type=F32, precision=lax.Precision.HIGHEST)
    scores = jax.nn.sigmoid(logits)
    biased = scores + rb_ref[...]
    sub = lax.broadcasted_iota(jnp.int32, (epg, tm), 0)
    gs_rows = []
    for r in range(N_EXPERT_GROUPS):
        bg = biased[r * epg:(r + 1) * epg, :]
        m1 = jnp.max(bg, axis=0, keepdims=True)
        i1 = jnp.min(jnp.where(bg == m1, sub, epg), axis=0, keepdims=True)
        m2 = jnp.max(jnp.where(sub == i1, -jnp.inf, bg), axis=0, keepdims=True)
        gs_rows.append(m1 + m2)
    gs = jnp.concatenate(gs_rows, axis=0)
    grow = lax.broadcasted_iota(jnp.int32, (N_EXPERT_GROUPS, tm), 0)
    rank = jnp.zeros((N_EXPERT_GROUPS, tm), jnp.int32)
    for k in range(N_EXPERT_GROUPS):
        rk = gs[k:k + 1, :]
        rank = rank + jnp.where((rk > gs) | ((rk == gs) & (k < grow)), 1, 0)
    gkeep = rank < TOPK_GROUPS
    masked = jnp.concatenate(
        [jnp.where(gkeep[r:r + 1, :], biased[r * epg:(r + 1) * epg, :], -jnp.inf) for r in range(N_EXPERT_GROUPS)],
        axis=0)
    erow = lax.broadcasted_iota(jnp.int32, (N_EXPERTS, tm), 0)
    idx_rows, w_rows, hits = [], [], []
    for _ in range(TOP_K):
        m = jnp.max(masked, axis=0, keepdims=True)
        ix = jnp.min(jnp.where(masked == m, erow, N_EXPERTS), axis=0, keepdims=True)
        hit = erow == ix
        w_rows.append(jnp.sum(jnp.where(hit, scores, 0.0), axis=0, keepdims=True))
        idx_rows.append(ix)
        hits.append(hit)
        masked = jnp.where(hit, -jnp.inf, masked)
    wsum = w_rows[0]
    for w in w_rows[1:]:
        wsum = wsum + w
    pad = 8 - TOP_K
    idx_ref[...] = jnp.concatenate(idx_rows + [jnp.zeros((pad, tm), jnp.int32)], axis=0)
    wt_ref[...] = jnp.concatenate([w / wsum * ROUTED_SCALE for w in w_rows] + [jnp.zeros((pad, tm), F32)], axis=0)

    @pl.when(pl.program_id(0) == 0)
    def _():
        run_ref[...] = c0_ref[...]

    earlier = (lax.broadcasted_iota(jnp.int32, (tm, tm), 0) < lax.broadcasted_iota(jnp.int32, (tm, tm), 1))
    earlier = jnp.where(earlier, 1.0, 0.0).astype(BF16)
    run = run_ref[...]
    pos_rows = []
    for hit in hits:
        onehot = jnp.where(hit, 1.0, 0.0)
        before = _dot(onehot.astype(BF16), earlier).astype(jnp.int32)
        pos_rows.append(jnp.sum(jnp.where(hit, run + before, 0), axis=0, keepdims=True))
        run = run + jnp.sum(onehot, axis=1, keepdims=True).astype(jnp.int32)
    run_ref[...] = run
    cnt_ref[...] = run
    pos_ref[...] = jnp.concatenate(pos_rows + [jnp.zeros((pad, tm), jnp.int32)], axis=0)


def _router(x1, wr_t, rb, counts0, tm, m=None):
    m = m or x1.shape[0]
    tok = lambda: pl.BlockSpec((8, tm), lambda i: (0, i))
    return pl.pallas_call(
        functools.partial(_router_kernel, tm=tm),
        grid=(m // tm,),
        in_specs=[pl.BlockSpec((tm, D_MODEL), lambda i: (i, 0)),
                  pl.BlockSpec((N_EXPERTS, D_MODEL), lambda i: (0, 0)),
                  pl.BlockSpec((N_EXPERTS, 1), lambda i: (0, 0)),
                  pl.BlockSpec((N_EXPERTS, 1), lambda i: (0, 0))],
        out_specs=[tok(), tok(), tok(), pl.BlockSpec((N_EXPERTS, 1), lambda i: (0, 0))],
        out_shape=[jax.ShapeDtypeStruct((8, m), jnp.int32), jax.ShapeDtypeStruct((8, m), F32),
                   jax.ShapeDtypeStruct((8, m), jnp.int32), jax.ShapeDtypeStruct((N_EXPERTS, 1), jnp.int32)],
        scratch_shapes=[pltpu.VMEM((N_EXPERTS, 1), jnp.int32)],
        compiler_params=_cparams("arbitrary"),
        name="router",
    )(x1, wr_t, rb, counts0)


def _expert_kernel(be_ref, nv_ref, x_ref, wg_ref, wu_ref, wd_ref, o_ref, wgb_ref, wub_ref, wdb_ref):
    i = pl.program_id(0)
    valid = i < nv_ref[0]
    new_expert = (i == 0) | (be_ref[i] != be_ref[jnp.maximum(i - 1, 0)])

    @pl.when(valid & new_expert)
    def _():
        wgb_ref[...] = wg_ref[0].astype(BF16)
        wub_ref[...] = wu_ref[0].astype(BF16)
        wdb_ref[...] = wd_ref[0].astype(BF16)

    @pl.when(valid)
    def _():
        x = x_ref[...].astype(BF16)
        h = jax.nn.silu(_dot(x, wgb_ref[...])) * _dot(x, wub_ref[...])
        o_ref[...] = _dot(h.astype(BF16), wdb_ref[...]).astype(o_ref.dtype)

    @pl.when(jnp.logical_not(valid))
    def _():
        o_ref[...] = jnp.zeros_like(o_ref)


def _experts(blk_e, n_valid, xg, wg, wu, wd, tr):
    n_rows = xg.shape[0]
    n_blk = n_rows // tr
    grid_spec = pltpu.PrefetchScalarGridSpec(
        num_scalar_prefetch=2,
        grid=(n_blk,),
        in_specs=[pl.BlockSpec((tr, D_MODEL), lambda i, be, nv: (i, 0)),
                  pl.BlockSpec((1, D_MODEL, EXPERT_FF), lambda i, be, nv: (be[i], 0, 0)),
                  pl.BlockSpec((1, D_MODEL, EXPERT_FF), lambda i, be, nv: (be[i], 0, 0)),
                  pl.BlockSpec((1, EXPERT_FF, D_MODEL), lambda i, be, nv: (be[i], 0, 0))],
        out_specs=pl.BlockSpec((tr, D_MODEL), lambda i, be, nv: (i, 0)),
        scratch_shapes=[pltpu.VMEM((D_MODEL, EXPERT_FF), BF16), pltpu.VMEM((D_MODEL, EXPERT_FF), BF16),
                        pltpu.VMEM((EXPERT_FF, D_MODEL), BF16)],
    )
    return pl.pallas_call(
        _expert_kernel,
        grid_spec=grid_spec,
        out_shape=jax.ShapeDtypeStruct((n_rows, D_MODEL), BF16),
        compiler_params=_cparams("arbitrary"),
        name="experts",
    )(blk_e, n_valid, xg, wg, wu, wd)


def _ffn_out_kernel(x_ref, r_ref, w_ref, sg_ref, su_ref, sd_ref, g_ref, b_ref, o_ref):
    x = x_ref[...]
    xb = x.astype(BF16)
    h = jax.nn.silu(_dot(xb, sg_ref[...])) * _dot(xb, su_ref[...])
    f = _dot(h.astype(BF16), sd_ref[...])
    for k in range(TOP_K):
        f = f + r_ref[k].astype(F32) * w_ref[:, k:k + 1]
    o_ref[...] = _layer_norm(DEEPNORM_ALPHA * x + f, g_ref[...], b_ref[...])


def _ffn_out(x1, rows6, wts, sg, su, sd, g, b, tm, m=None):
    m = m or x1.shape[0]
    row = lambda: pl.BlockSpec((tm, D_MODEL), lambda i: (i, 0))
    full = lambda s: pl.BlockSpec(s, lambda i: (0, 0))
    return pl.pallas_call(
        _ffn_out_kernel,
        grid=(m // tm,),
        in_specs=[row(), pl.BlockSpec((TOP_K, tm, D_MODEL), lambda i: (0, i, 0)), pl.BlockSpec((tm, 8), lambda i: (i, 0)),
                  full((D_MODEL, SHARED_FF)), full((D_MODEL, SHARED_FF)), full((SHARED_FF, D_MODEL)),
                  full((1, D_MODEL)), full((1, D_MODEL))],
        out_specs=row(),
        out_shape=jax.ShapeDtypeStruct((m, D_MODEL), F32),
        compiler_params=_cparams("parallel"),
        name="ffn_out",
    )(x1, rows6, wts, sg, su, sd, g, b)


def _rel_bucket(dist):
    n = jnp.maximum(dist, 0)
    nf = jnp.maximum(n, 1).astype(F32)
    large = MAX_EXACT + (jnp.log(nf / MAX_EXACT) / math.log(MAX_DISTANCE / MAX_EXACT)
                         * (N_BUCKETS - MAX_EXACT)).astype(jnp.int32)
    return jnp.where(n < MAX_EXACT, n, jnp.minimum(large, N_BUCKETS - 1))


def _bias_of(tbl, dist):
    onehot = (_rel_bucket(dist)[..., None] == jnp.arange(N_BUCKETS)).astype(F32)
    return jnp.einsum('...k,kh->...h', onehot, tbl, precision=lax.Precision.HIGHEST)


def _toeplitz_bias(tbl, tq, n_tiles):
    i = jnp.arange(tq)[:, None]
    j = jnp.arange(tq)[None, :]
    dist = jnp.arange(n_tiles)[:, None, None] * tq + (i - j)[None]
    bias = jnp.moveaxis(_bias_of(tbl, dist), -1, 0)
    return bias, dist


PAGE_GROUP = 16
IDX_PAGE_GROUP = 64


def _by_group(hpg, a0, a1):
    row = lax.broadcasted_iota(jnp.int32, (2 * hpg, 1), 0)
    return jnp.where(row < hpg, a0, a1)


def _order_key(x):
    bits = pltpu.bitcast(x, jnp.int32)
    return jnp.where(bits < 0, bits ^ 0x7FFFFFFF, bits)


def _row_topk(key_ref, keep_ref, n_keep, width):
    thr = _topk_mask(key_ref, n_keep, 1, width)
    key = key_ref[...]
    gt = key > thr
    eq = key == thr
    need = n_keep - jnp.sum(jnp.where(gt, 1, 0), axis=-1, keepdims=True)
    n_eq = jnp.sum(jnp.where(eq, 1, 0), axis=-1, keepdims=True)
    keep_ref[...] = jnp.where(gt | eq, 1.0, 0.0)

    @pl.when(jnp.max(n_eq - need) > 0)
    def _():
        r_i = lax.broadcasted_iota(jnp.int32, (128, 128), 0)
        c_i = lax.broadcasted_iota(jnp.int32, (128, 128), 1)
        tri = jnp.where(r_i <= c_i, 1.0, 0.0).astype(BF16)
        need_f = need.astype(F32)

        def body(c, before):
            off = pl.multiple_of(c * 128, 128)
            kc = key_ref[:, pl.ds(off, 128)]
            eq_c = kc == thr
            eq_f = jnp.where(eq_c, 1.0, 0.0)
            pref = _dot(jnp.broadcast_to(eq_f, (8, 128)).astype(BF16), tri)[0:1] + before
            keep_ref[:, pl.ds(off, 128)] = jnp.where((kc > thr) | (eq_c & (pref <= need_f)), 1.0, 0.0)
            return before + jnp.sum(eq_f, axis=-1, keepdims=True)

        lax.fori_loop(0, width // 128, body, jnp.zeros((1, 1), F32))


def _s_cmp_win_kernel(q_ref, kc_ref, vc_ref, cb_ref, wk_ref, wv_ref, nk_ref, nv_ref, wb_ref, b0_ref,
                      ocmp_ref, owin_ref, sel_ref, key_ref, keep_ref, *, nbc, n_blocks, width, n_sel, wlen):
    hpg = NSA_HPG
    qf = q_ref[0]
    q = qf.astype(BF16)
    row = lax.broadcasted_iota(jnp.int32, (NSA_HEADS, 1), 0)
    g0 = row < hpg
    kc = kc_ref[0]
    vc = vc_ref[0]
    cb = cb_ref[...]
    valid = cb > 0.5 * NEG
    s = _by_group(hpg, _dot_nt_full(qf, kc[:, :HEAD_DIM]), _dot_nt_full(qf, kc[:, HEAD_DIM:]))
    s = s * ATTN_SCALE + cb
    m = jnp.max(s, axis=-1, keepdims=True)
    e = jnp.where(valid, jnp.exp(s - m), 0.0)
    den = jnp.sum(e, axis=-1, keepdims=True)
    p = e / jnp.where(den > 0, den, 1.0)
    ocmp_ref[0] = (_dot(jnp.where(g0, p, 0.0).astype(BF16), vc[:, :HEAD_DIM].astype(BF16))
                   + _dot(jnp.where(g0, 0.0, p).astype(BF16), vc[:, HEAD_DIM:].astype(BF16)))
    lane = lax.broadcasted_iota(jnp.int32, (1, width), 1)
    cur = n_blocks - 1
    forced = (lane == 0) | (lane == cur) | (lane == cur - 1)
    for g in range(NSA_KV_GROUPS):
        imp = jnp.sum(jnp.where(g0 if g == 0 else jnp.logical_not(g0), p, 0.0), axis=0, keepdims=True)
        impw = jnp.concatenate([imp, jnp.zeros((1, width - nbc), F32)], axis=1)
        score = jnp.where(forced, jnp.inf, jnp.where(lane <= cur, impw, -jnp.inf))
        key_ref[...] = _order_key(score)
        _row_topk(key_ref, keep_ref, n_sel, width)
        sel_ref[0, g:g + 1, :] = keep_ref[...]
    wk0 = wk_ref[pl.ds(0, wlen, stride=2), :].astype(BF16)
    wk1 = wk_ref[pl.ds(1, wlen, stride=2), :].astype(BF16)
    wv0 = wv_ref[pl.ds(0, wlen, stride=2), :].astype(BF16)
    wv1 = wv_ref[pl.ds(1, wlen, stride=2), :].astype(BF16)
    wb = wb_ref[...]
    s = _by_group(hpg, _dot_nt(q, wk0), _dot_nt(q, wk1)) * ATTN_SCALE + wb
    nk = _by_group(hpg, nk_ref[0][:, :HEAD_DIM], nk_ref[0][:, HEAD_DIM:])
    nv = _by_group(hpg, nv_ref[0][:, :HEAD_DIM], nv_ref[0][:, HEAD_DIM:])
    s_new = jnp.sum(qf * nk, axis=-1, keepdims=True) * ATTN_SCALE + b0_ref[...]
    m = jnp.maximum(jnp.max(s, axis=-1, keepdims=True), s_new)
    e = jnp.where(wb > 0.5 * NEG, jnp.exp(s - m), 0.0)
    e_new = jnp.exp(s_new - m)
    den = jnp.sum(e, axis=-1, keepdims=True) + e_new
    o = (_dot(jnp.where(g0, e, 0.0).astype(BF16), wv0) + _dot(jnp.where(g0, 0.0, e).astype(BF16), wv1)
         + e_new * nv)
    owin_ref[0] = o / den


def _s_cmp_win(q, kcomp, vcomp, cbias, wk, wv, nk, nv, wbias, b0, n_blocks, n_sel):
    db = q.shape[0]
    nbc = kcomp.shape[1]
    wlen = wk.shape[1] // 2
    width = -(-n_blocks // 128) * 128
    one = lambda *s: pl.BlockSpec((1,) + s, lambda bi: (bi,) + (0,) * len(s))
    const = lambda a: pl.BlockSpec(a.shape, lambda bi: (0,) * a.ndim)
    return pl.pallas_call(
        functools.partial(_s_cmp_win_kernel, nbc=nbc, n_blocks=n_blocks, width=width, n_sel=n_sel, wlen=wlen),
        grid=(db,),
        in_specs=[one(NSA_HEADS, HEAD_DIM), one(nbc, KV_W), one(nbc, KV_W), const(cbias),
                  pl.BlockSpec((None, 2 * wlen, HEAD_DIM), lambda bi: (bi, 0, 0)),
                  pl.BlockSpec((None, 2 * wlen, HEAD_DIM), lambda bi: (bi, 0, 0)),
                  one(1, KV_W), one(1, KV_W), const(wbias), const(b0)],
        out_specs=[one(NSA_HEADS, HEAD_DIM), one(NSA_HEADS, HEAD_DIM), one(NSA_KV_GROUPS, width)],
        out_shape=[jax.ShapeDtypeStruct((db, NSA_HEADS, HEAD_DIM), F32),
                   jax.ShapeDtypeStruct((db, NSA_HEADS, HEAD_DIM), F32),
                   jax.ShapeDtypeStruct((db, NSA_KV_GROUPS, width), F32)],
        scratch_shapes=[pltpu.VMEM((1, width), jnp.int32), pltpu.VMEM((1, width), F32)],
        compiler_params=_cparams("parallel"),
        name="sample_cmp_win",
    )(q, kcomp, vcomp, cbias, wk, wv, nk, nv, wbias, b0)


def _s_sel_kernel(idx_ref, hp_ref, q_ref, nk_ref, nv_ref, tb_ref, c_ref, *rest, n_sel, cur, n_near):
    del hp_ref
    k_refs, v_refs, o_ref = rest[:n_sel], rest[n_sel:2 * n_sel], rest[2 * n_sel]
    bi = pl.program_id(0)
    g = pl.program_id(1)
    q = q_ref[0].astype(BF16)
    nk = jnp.where(g == 0, nk_ref[0][:, :HEAD_DIM], nk_ref[0][:, HEAD_DIM:])
    nv = jnp.where(g == 0, nv_ref[0][:, :HEAD_DIM], nv_ref[0][:, HEAD_DIM:])
    rowi = lax.broadcasted_iota(jnp.int32, (SEL_BLOCK, 1), 0)
    ss, vs = [], []
    for r in range(n_sel):
        idx = idx_ref[(bi * NSA_KV_GROUPS + g) * n_sel + r]
        first = (rowi == 0) & (idx >= cur)
        kr = jnp.where(first, nk, k_refs[r][pl.ds(g, SEL_BLOCK, stride=2), :]).astype(BF16)
        vs.append(jnp.where(first, nv, v_refs[r][pl.ds(g, SEL_BLOCK, stride=2), :]).astype(BF16))
        u = jnp.clip(idx - (cur - (n_near - 1)), 0, n_near - 1)
        bias = jnp.where(idx >= cur - (n_near - 1), tb_ref[u], c_ref[...])
        ss.append(_dot_nt(q, kr) * ATTN_SCALE + bias)
    m = ss[0].max(axis=-1, keepdims=True)
    for s in ss[1:]:
        m = jnp.maximum(m, s.max(axis=-1, keepdims=True))
    den = jnp.zeros((NSA_HEADS, 1), F32)
    o = jnp.zeros((NSA_HEADS, HEAD_DIM), F32)
    for s, v in zip(ss, vs):
        e = jnp.where(s > 0.5 * NEG, jnp.exp(s - m), 0.0)
        den = den + e.sum(axis=-1, keepdims=True)
        o = o + _dot(e.astype(BF16), v)
    o_ref[0, 0] = o / jnp.where(den > 0, den, 1.0)


def _s_sel(idx_flat, hp_flat, q, nk, nv, tb, c, cache_k, cache_v, n_sel, cur):
    db = q.shape[0]
    rows = SEL_BLOCK * NSA_KV_GROUPS

    def page(r):
        return pl.BlockSpec((None, rows, HEAD_DIM),
                            lambda bi, g, idx, hp, r=r: (hp[(bi * NSA_KV_GROUPS + g) * n_sel + r], 0, 0))

    grid_spec = pltpu.PrefetchScalarGridSpec(
        num_scalar_prefetch=2,
        grid=(db, NSA_KV_GROUPS),
        in_specs=[pl.BlockSpec((1, NSA_HEADS, HEAD_DIM), lambda bi, g, idx, hp: (bi, 0, 0)),
                  pl.BlockSpec((1, 1, KV_W), lambda bi, g, idx, hp: (bi, 0, 0)),
                  pl.BlockSpec((1, 1, KV_W), lambda bi, g, idx, hp: (bi, 0, 0)),
                  pl.BlockSpec(tb.shape, lambda bi, g, idx, hp: (0, 0, 0)),
                  pl.BlockSpec(c.shape, lambda bi, g, idx, hp: (0, 0))]
        + [page(r) for r in range(n_sel)] + [page(r) for r in range(n_sel)],
        out_specs=pl.BlockSpec((1, 1, NSA_HEADS, HEAD_DIM), lambda bi, g, idx, hp: (bi, g, 0, 0)),
    )
    return pl.pallas_call(
        functools.partial(_s_sel_kernel, n_sel=n_sel, cur=cur, n_near=tb.shape[0]),
        grid_spec=grid_spec,
        out_shape=jax.ShapeDtypeStruct((db, NSA_KV_GROUPS, NSA_HEADS, HEAD_DIM), F32),
        compiler_params=_cparams("parallel", "parallel"),
        name="sample_sel",
    )(idx_flat, hp_flat, q, nk, nv, tb, c, *([cache_k] * n_sel), *([cache_v] * n_sel))


def _s_index_kernel(pt_ref, iq_ref, iw_ref, nik_ref, *rest, pg, past, n_keep, width):
    del pt_ref
    pages, keep_ref, sc_ref, key_ref = rest[:pg], rest[pg], rest[pg + 1], rest[pg + 2]
    j = pl.program_id(1)
    span = pg * PAGE_SIZE
    iqf = iq_ref[0]
    iw = iw_ref[0]
    ik = jnp.concatenate([p[...] for p in pages], axis=0)
    lg = jnp.maximum(_dot_nt_x3(iqf, ik) * IDX_DIM ** -0.5, 0.0)
    sc = jnp.sum(lg * iw, axis=0, keepdims=True) * IDX_HEADS ** -0.5 + 0.0
    sc_ref[:, pl.ds(pl.multiple_of(j * span, span), span)] = sc

    @pl.when(j == pl.num_programs(1) - 1)
    def _():
        lg_new = jnp.maximum(jnp.sum(iqf * nik_ref[0], axis=-1, keepdims=True) * IDX_DIM ** -0.5, 0.0)
        sc_new = jnp.sum(lg_new * iw, axis=0, keepdims=True) * IDX_HEADS ** -0.5 + 0.0
        lane = lax.broadcasted_iota(jnp.int32, (1, width - past), 1)
        sc_ref[:, past:] = jnp.where(lane == 0, sc_new, -jnp.inf)
        key_ref[...] = _order_key(sc_ref[...])
        _row_topk(key_ref, keep_ref.at[0], n_keep, width)


def _s_index(page_table, iq, iw, nik, cache_idx, n_keep):
    db, n_pages = page_table.shape
    pg = math.gcd(n_pages, IDX_PAGE_GROUP)
    past = n_pages * PAGE_SIZE
    width = past + 128

    def page(r):
        return pl.BlockSpec((None, PAGE_SIZE, IDX_DIM), lambda bi, j, pt, r=r: (pt[bi, j * pg + r], 0, 0))

    grid_spec = pltpu.PrefetchScalarGridSpec(
        num_scalar_prefetch=1,
        grid=(db, n_pages // pg),
        in_specs=[pl.BlockSpec((1, IDX_HEADS, IDX_DIM), lambda bi, j, pt: (bi, 0, 0)),
                  pl.BlockSpec((1, IDX_HEADS, 1), lambda bi, j, pt: (bi, 0, 0)),
                  pl.BlockSpec((1, 1, IDX_DIM), lambda bi, j, pt: (bi, 0, 0))]
        + [page(r) for r in range(pg)],
        out_specs=pl.BlockSpec((1, 1, width), lambda bi, j, pt: (bi, 0, 0)),
        scratch_shapes=[pltpu.VMEM((1, width), F32), pltpu.VMEM((1, width), jnp.int32)],
    )
    return pl.pallas_call(
        functools.partial(_s_index_kernel, pg=pg, past=past, n_keep=n_keep, width=width),
        grid_spec=grid_spec,
        out_shape=jax.ShapeDtypeStruct((db, 1, width), F32),
        compiler_params=_cparams("parallel", "arbitrary"),
        name="sample_index",
    )(page_table, iq, iw, nik, *([cache_idx] * pg))


def _s_dsa_kernel(pt_ref, q_ref, keep_ref, bt_ref, c_ref, b0_ref, nk_ref, nv_ref, *rest, pg, past):
    del pt_ref
    kp, vp, o_ref = rest[:pg], rest[pg:2 * pg], rest[2 * pg]
    m_ref, l_ref, acc_ref = rest[2 * pg + 1:]
    hpg = DSA_HPG
    j = pl.program_id(1)
    last = pl.num_programs(1) - 1
    span = pg * PAGE_SIZE

    @pl.when(j == 0)
    def _():
        m_ref[...] = jnp.full_like(m_ref, NEG)
        l_ref[...] = jnp.zeros_like(l_ref)
        acc_ref[...] = jnp.zeros_like(acc_ref)

    qf = q_ref[0]
    q = qf.astype(BF16)
    row = lax.broadcasted_iota(jnp.int32, (DSA_HEADS, 1), 0)
    g0 = row < hpg
    rows = lambda refs, g: jnp.concatenate([p[pl.ds(g, PAGE_SIZE, stride=2), :] for p in refs], axis=0).astype(BF16)
    s = _by_group(hpg, _dot_nt(q, rows(kp, 0)), _dot_nt(q, rows(kp, 1))) * ATTN_SCALE
    keep = keep_ref[0, :, pl.ds(pl.multiple_of(j * span, span), span)] > 0.5
    s = jnp.where(keep, s + jnp.where(j == last, bt_ref[...], c_ref[...]), NEG)
    m_old = m_ref[...]
    m_new = jnp.maximum(m_old, jnp.max(s, axis=-1, keepdims=True))
    alpha = jnp.exp(m_old - m_new)
    e = jnp.where(keep, jnp.exp(s - m_new), 0.0)
    l_ref[...] = alpha * l_ref[...] + jnp.sum(e, axis=-1, keepdims=True)
    acc_ref[...] = (alpha * acc_ref[...] + _dot(jnp.where(g0, e, 0.0).astype(BF16), rows(vp, 0))
                    + _dot(jnp.where(g0, 0.0, e).astype(BF16), rows(vp, 1)))
    m_ref[...] = m_new

    @pl.when(j == last)
    def _():
        keep_new = keep_ref[0, :, past:past + 1] > 0.5
        nk = _by_group(hpg, nk_ref[0][:, :HEAD_DIM], nk_ref[0][:, HEAD_DIM:])
        nv = _by_group(hpg, nv_ref[0][:, :HEAD_DIM], nv_ref[0][:, HEAD_DIM:])
        s_new = jnp.sum(qf * nk, axis=-1, keepdims=True) * ATTN_SCALE + b0_ref[...]
        s_new = jnp.where(keep_new, s_new, NEG)
        m_old2 = m_ref[...]
        m2 = jnp.maximum(m_old2, s_new)
        a2 = jnp.exp(m_old2 - m2)
        e_new = jnp.where(keep_new, jnp.exp(s_new - m2), 0.0)
        l = a2 * l_ref[...] + e_new
        o_ref[0] = (a2 * acc_ref[...] + e_new * nv) / jnp.where(l > 0, l, 1.0)


def _s_dsa(page_table, q, keep, btail, c, b0, nk, nv, cache_k, cache_v):
    db, n_pages = page_table.shape
    pg = math.gcd(n_pages, PAGE_GROUP)
    past = n_pages * PAGE_SIZE
    width = keep.shape[-1]
    rows = PAGE_SIZE * DSA_KV_GROUPS

    def page(r):
        return pl.BlockSpec((None, rows, HEAD_DIM), lambda bi, j, pt, r=r: (pt[bi, j * pg + r], 0, 0))

    grid_spec = pltpu.PrefetchScalarGridSpec(
        num_scalar_prefetch=1,
        grid=(db, n_pages // pg),
        in_specs=[pl.BlockSpec((1, DSA_HEADS, HEAD_DIM), lambda bi, j, pt: (bi, 0, 0)),
                  pl.BlockSpec((1, 1, width), lambda bi, j, pt: (bi, 0, 0)),
                  pl.BlockSpec(btail.shape, lambda bi, j, pt: (0, 0)),
                  pl.BlockSpec(c.shape, lambda bi, j, pt: (0, 0)),
                  pl.BlockSpec(b0.shape, lambda bi, j, pt: (0, 0)),
                  pl.BlockSpec((1, 1, KV_W), lambda bi, j, pt: (bi, 0, 0)),
                  pl.BlockSpec((1, 1, KV_W), lambda bi, j, pt: (bi, 0, 0))]
        + [page(r) for r in range(pg)] + [page(r) for r in range(pg)],
        out_specs=pl.BlockSpec((1, DSA_HEADS, HEAD_DIM), lambda bi, j, pt: (bi, 0, 0)),
        scratch_shapes=[pltpu.VMEM((DSA_HEADS, 1), F32), pltpu.VMEM((DSA_HEADS, 1), F32),
                        pltpu.VMEM((DSA_HEADS, HEAD_DIM), F32)],
    )
    return pl.pallas_call(
        functools.partial(_s_dsa_kernel, pg=pg, past=past),
        grid_spec=grid_spec,
        out_shape=jax.ShapeDtypeStruct((db, DSA_HEADS, HEAD_DIM), F32),
        compiler_params=_cparams("parallel", "arbitrary"),
        name="sample_dsa",
    )(page_table, q, keep, btail, c, b0, nk, nv, *([cache_k] * pg), *([cache_v] * pg))


def _split_w_in(w_in):
    points = np.cumsum(IN_COLS)[:-1].tolist()
    q_a, kv_a, g_a, q_b, kv_b, iq, ik, iw, g_m = jnp.split(w_in, points, axis=-1)
    pad = jnp.zeros((D_MODEL, MISC_W - IDX_DIM - IDX_HEADS - 3 * NSA_HEADS), w_in.dtype)
    w_kv = jnp.concatenate([kv_a, kv_b, ik, iw, g_a, pad], axis=-1)
    w_q = jnp.concatenate([q_a, q_b, iq], axis=-1)
    return w_kv, w_q, g_m[:, :D_MODEL], g_m[:, D_MODEL:]


KV_WIDTHS = (KV_W,) * 8 + (MISC_W,)
Q_WIDTHS = (NSA_HEADS * HEAD_DIM, DSA_HEADS * HEAD_DIM, IDX_HEADS * IDX_DIM)


def _row_tile(m, cap):
    tm = math.gcd(m, cap)
    assert tm % 8 == 0 or tm == m
    return tm


def kernel(x_prompt, x_sample, cache_cmp_k, cache_cmp_v, cache_slc_k, cache_slc_v, state_win_k, state_win_v,
           cache_dsa_k, cache_dsa_v, cache_idx_k, page_table, rel_bias_table, w_in, cmp_pe, cmp_w1, cmp_w2,
           w_up_a, w_up_b, w_o, ln1_g, ln1_b, w_router, router_bias, moe_w_gate, moe_w_up, moe_w_down,
           sh_w_gate, sh_w_up, sh_w_down, ln2_g, ln2_b):
    assert w_in.shape[0] == DEPTH == 1
    b, t, _ = x_prompt.shape
    db, dt, _ = x_sample.shape
    assert dt == 1
    n_pool = cache_cmp_k.shape[1]
    n_pages = page_table.shape[1]
    tq = Q_TILE
    assert t % tq == 0 and t % CMP_BLOCK == 0
    n_p, n_s = b * t, db * dt

    tbl_a = rel_bias_table[:, :NSA_HEADS]
    tbl_b = rel_bias_table[:, NSA_HEADS:]
    mixer_w32 = _split_w_in(w_in[0]) + (w_up_a[0], w_up_b[0], w_o[0])
    w_kv32, w_q32, w_ga32, w_gb32, w_ua32, w_ub32, w_o32 = mixer_w32
    w_kv, w_q, w_ga, w_gb, w_ua, w_ub, w_ob = [w.astype(BF16) for w in mixer_w32]
    cw = [_compress_weights(cmp_pe[0, i], cmp_w1[0, i], cmp_w2[0, i]) for i in range(2)]

    xp = x_prompt.reshape(n_p, D_MODEL)
    xs = x_sample.reshape(n_s, D_MODEL)
    xpb = xp.astype(BF16)
    tm_p = _row_tile(n_p, 512)
    (kc, vc, ks, vs, kw, vw, kb, vb, misc,
     kc_h, vc_h, ks_h, vs_h, kb_h, vb_h) = _project(xpb, w_kv, KV_WIDTHS, tm_p, by_head=(0, 1, 2, 3, 6, 7))
    qa, qb, iq = _project(xpb, w_q, Q_WIDTHS, tm_p)
    s_kc, s_vc, s_ks, s_vs, s_kw, s_vw, s_kb, s_vb, s_misc = _project(xs, w_kv32, KV_WIDTHS, n_s)
    s_qa, s_qb, s_iq = _project(xs, w_q32, Q_WIDTHS, n_s)

    nb = t // CMP_BLOCK
    r3 = lambda a: a.reshape(b, t, -1)
    kcomp = _compress(kc.reshape(b * nb, CMP_BLOCK, KV_W), *cw[0]).reshape(b, nb, KV_W)
    vcomp = _compress(vc.reshape(b * nb, CMP_BLOCK, KV_W), *cw[1]).reshape(b, nb, KV_W)
    cdist = jnp.arange(t)[:, None] - (jnp.arange(nb) * CMP_BLOCK + CMP_BLOCK - 1)[None, :]
    cbias = jnp.where(cdist >= 0, jnp.moveaxis(_bias_of(tbl_a, cdist), -1, 0), NEG).swapaxes(1, 2)
    o_cmp, selmask = _cmp_select(r3(qa), kcomp, vcomp, cbias, tq)

    assert MAX_DISTANCE <= tq
    bias2_a, dist2 = _toeplitz_bias(tbl_a, tq, 2)
    bias2_b, _ = _toeplitz_bias(tbl_b, tq, 2)
    c_a, c_b = tbl_a[N_BUCKETS - 1], tbl_b[N_BUCKETS - 1]
    d_a = jnp.where(dist2 >= 0, (bias2_a - c_a[:, None, None, None]) * LOG2E, NEG)
    d_b = jnp.where(dist2 >= 0, (bias2_b - c_b[:, None, None, None]) * LOG2E, NEG)
    expand = (jnp.arange(t)[None, :] // SEL_BLOCK == jnp.arange(nb)[:, None]).astype(BF16)
    o_sel = _dense_attn(r3(qa), r3(ks), r3(vs), selmask, expand, d_a, tq, "sel")

    n_wchunks = -(-(WINDOW - 1) // tq) + 1
    bias_w, dist_w = _toeplitz_bias(tbl_a, tq, n_wchunks)
    wtiles = jnp.where((dist_w >= 0) & (dist_w < WINDOW), bias_w * LOG2E, NEG)
    o_win = _window_attn(r3(qa), r3(kw), r3(vw), wtiles, tq)

    n_keep = min(DSA_TOPK, t // 4)
    iw_t = r3(misc)[:, :, MISC_IW:MISC_IW + IDX_HEADS].swapaxes(1, 2)
    keepmask = _index_select(r3(iq), iw_t, r3(misc), tq, n_keep)
    o_b = _dense_attn(r3(qb), r3(kb), r3(vb), keepmask, jnp.zeros((8, 128), BF16), d_b, tq, "dsa")

    tm_m = _row_tile(n_p, 256)
    y_p = _merge_up(xpb, o_cmp.reshape(n_p, -1), o_sel.reshape(n_p, -1), o_win.reshape(n_p, -1),
                    o_b.reshape(n_p, -1), misc, w_ga, w_gb, w_ua, w_ub, tm_m, 1024)
    x1 = _merge_out(xp, y_p, w_ob, ln1_g, ln1_b, tm_m, out_rows=n_p + n_s)

    past = n_pages * PAGE_SIZE
    halves = PAGE_SIZE // CMP_BLOCK

    def comp_pool(cache, i):
        c = _compress_pool(cache, cmp_pe[0, i], cmp_w1[0, i], cmp_w2[0, i])
        return c.reshape(n_pool, halves * KV_W)[page_table].reshape(db, n_pages * halves, KV_W)

    s_kcomp = comp_pool(cache_cmp_k[0], 0)
    s_vcomp = comp_pool(cache_cmp_v[0], 1)
    assert past % SEL_BLOCK == 0 and past >= 4 * SEL_BLOCK and PAGE_SIZE == 2 * SEL_BLOCK
    total = past + dt
    nbc = past // CMP_BLOCK
    n_blocks = -(-total // SEL_BLOCK)
    cur = past // SEL_BLOCK
    n_sel = min(N_SEL_BLOCKS, n_blocks)
    col = lambda v: v.reshape(-1, 1)
    new = lambda a: a.reshape(db, 1, KV_W)
    cb_s = _bias_of(tbl_a, past - (jnp.arange(nbc) * CMP_BLOCK + CMP_BLOCK - 1)).T
    w_past = state_win_k.shape[2]
    wdist = w_past - jnp.arange(w_past)
    wb_s = jnp.where(wdist < WINDOW, _bias_of(tbl_a, wdist).T, NEG)
    rows2 = lambda a, n: a.reshape(a.shape[0], n * NSA_KV_GROUPS, HEAD_DIM)
    s_q8 = s_qa.reshape(db, NSA_HEADS, HEAD_DIM)
    so_cmp, so_win, selmask = _s_cmp_win(
        s_q8, s_kcomp, s_vcomp, cb_s, rows2(state_win_k[0], w_past), rows2(state_win_v[0], w_past),
        new(s_kw), new(s_vw), wb_s, col(tbl_a[0]), n_blocks, n_sel)
    kept = selmask[:, :, None, :n_blocks] > 0.5
    nth = jnp.cumsum(kept, axis=-1) == (jnp.arange(n_sel) + 1)[:, None]
    sel_idx = jnp.sum(jnp.where(kept & nth, jnp.arange(n_blocks), 0), axis=-1).astype(jnp.int32)
    sel_page = jnp.take_along_axis(page_table, jnp.minimum(sel_idx // 2, n_pages - 1).reshape(db, -1), axis=1)
    sel_hp = sel_page.reshape(sel_idx.shape) * 2 + sel_idx % 2
    n_near = 4
    ndist = (n_near - 1 - jnp.arange(n_near))[:, None] * SEL_BLOCK - jnp.arange(SEL_BLOCK)[None, :]
    tb_s = jnp.where(ndist[:, None, :] >= 0, jnp.moveaxis(_bias_of(tbl_a, ndist), -1, 1), NEG)
    half_pages = lambda c: c.reshape(n_pool * halves, SEL_BLOCK * NSA_KV_GROUPS, HEAD_DIM)
    so_sel2 = _s_sel(sel_idx.reshape(-1), sel_hp.reshape(-1).astype(jnp.int32), s_q8, new(s_ks), new(s_vs), tb_s,
                     col(tbl_a[N_BUCKETS - 1]), half_pages(cache_slc_k[0]), half_pages(cache_slc_v[0]), n_sel, cur)
    so_sel = jnp.concatenate([so_sel2[:, 0, :NSA_HPG], so_sel2[:, 1, NSA_HPG:]], axis=1)
    keep_s = _s_index(page_table, s_iq.reshape(db, IDX_HEADS, IDX_DIM),
                      s_misc[:, MISC_IW:MISC_IW + IDX_HEADS].reshape(db, IDX_HEADS, 1),
                      s_misc[:, :IDX_DIM].reshape(db, 1, IDX_DIM), cache_idx_k[0], min(DSA_TOPK, total // 4))
    span = math.gcd(n_pages, PAGE_GROUP) * PAGE_SIZE
    assert span >= MAX_DISTANCE
    bt_s = _bias_of(tbl_b, span - jnp.arange(span)).T
    pages2 = lambda c: c.reshape(n_pool, PAGE_SIZE * DSA_KV_GROUPS, HEAD_DIM)
    so_b = _s_dsa(page_table, s_qb.reshape(db, DSA_HEADS, HEAD_DIM), keep_s, bt_s, col(tbl_b[N_BUCKETS - 1]),
                  col(tbl_b[0]), new(s_kb), new(s_vb), pages2(cache_dsa_k[0]), pages2(cache_dsa_v[0]))
    n_win = min(WINDOW, total)
    g4 = lambda a: a.reshape(db, dt, NSA_KV_GROUPS, HEAD_DIM)
    s_wk = jnp.concatenate([state_win_k[0], g4(s_kw)], axis=1)[:, -n_win:]
    s_wv = jnp.concatenate([state_win_v[0], g4(s_vw)], axis=1)[:, -n_win:]
    y_s = _merge_up(xs, so_cmp.reshape(n_s, -1), so_sel.reshape(n_s, -1), so_win.reshape(n_s, -1),
                    so_b.reshape(n_s, -1), s_misc, w_ga32, w_gb32, w_ua32, w_ub32, n_s, 512)
    assert n_p % n_s == 0
    x1 = _merge_out(xs, y_s, w_o32, ln1_g, ln1_b, n_s, out_rows=n_p + n_s, into=x1)
    x1s = x1[n_p:]

    wr_t = w_router[0].T
    rb = router_bias[0].reshape(N_EXPERTS, 1)
    eidx_p, wts_p, pos_p, cnt_p = _router(x1, wr_t, rb, jnp.zeros((N_EXPERTS, 1), jnp.int32), tm_m, m=n_p)
    eidx_s, wts_s, pos_s, cnt = _router(x1s, wr_t, rb, cnt_p, n_s)
    n_tok = n_p + n_s
    eidx = jnp.concatenate([eidx_p[:TOP_K], eidx_s[:TOP_K]], axis=1)
    pos = jnp.concatenate([pos_p[:TOP_K], pos_s[:TOP_K]], axis=1)
    tr = EXPERT_ROWS
    n_asg = n_tok * TOP_K
    counts = cnt[:, 0]
    padded = (counts + tr - 1) // tr * tr
    pend = jnp.cumsum(padded)
    pad_start = pend - padded
    n_rows = -(-n_asg // tr) * tr + N_EXPERTS * tr
    n_blk = n_rows // tr
    dest = pos + jnp.sum(jnp.where(eidx[..., None] == jnp.arange(N_EXPERTS), pad_start, 0), axis=-1)
    blk_e = jnp.minimum(jnp.sum(pend[None, :] <= (jnp.arange(n_blk) * tr)[:, None], axis=1), N_EXPERTS - 1)
    n_valid = (pend[-1] // tr).astype(jnp.int32).reshape(1)
    fill = pad_start[:, None] + counts[:, None] + jnp.arange(tr)[None, :]
    fill = jnp.where(fill < pend[:, None], fill, n_rows)
    n_extra = n_rows - n_asg - N_EXPERTS * tr
    slot_keys = jnp.concatenate([dest.reshape(-1), fill.reshape(-1),
                                 jnp.full((n_extra,), n_rows)]).astype(jnp.int32)
    slot_toks = jnp.concatenate([jnp.tile(jnp.arange(n_tok, dtype=jnp.int32), TOP_K),
                                 jnp.zeros((N_EXPERTS * tr + n_extra,), jnp.int32)])
    row_tok = lax.sort((slot_keys, slot_toks), num_keys=1)[1]
    xg = x1[row_tok]
    out_rows = _experts(blk_e.astype(jnp.int32), n_valid, xg, moe_w_gate[0], moe_w_up[0], moe_w_down[0], tr)
    rows6 = out_rows[dest]

    sg, su, sd = sh_w_gate[0].astype(BF16), sh_w_up[0].astype(BF16), sh_w_down[0].astype(BF16)
    y_prompt = _ffn_out(x1, rows6, wts_p.T, sg, su, sd, ln2_g, ln2_b, tm_m, m=n_p).reshape(b, t, D_MODEL)
    y_sample = _ffn_out(x1s, rows6[:, n_p:], wts_s.T, sg, su, sd, ln2_g, ln2_b, n_s).reshape(db, dt, D_MODEL)

    n_win = min(WINDOW, t)
    st = lambda a: a.reshape(1, b, t, NSA_KV_GROUPS, HEAD_DIM)
    ss = lambda a: a.reshape(1, db, dt, NSA_KV_GROUPS, HEAD_DIM)
    return (y_prompt, y_sample,
            st(kc_h), st(vc_h), st(ks_h), st(vs_h), st(kw)[:, :, -n_win:], st(vw)[:, :, -n_win:], st(kb_h), st(vb_h),
            misc[:, :IDX_DIM].reshape(1, b, t, IDX_DIM),
            ss(s_kc), ss(s_vc), ss(s_ks), ss(s_vs), s_wk[None], s_wv[None], ss(s_kb), ss(s_vb),
            s_misc[:, :IDX_DIM].reshape(1, db, dt, IDX_DIM))
```
